```python
import math
import jax, jax.numpy as jnp
from jax import lax
import numpy as np

D_MODEL = 4096
BATCH = 4
SEQ = 4096
DEPTH = 2

A_HEAD_DIM = 128
A_HEADS = D_MODEL // 2 // A_HEAD_DIM
A_KV_HEADS = 4
IDX_HEADS = 16
IDX_DIM = 64
DSA_TOPK = 256
RET_KEY_DIM = 256
RET_VAL_DIM = 256
RET_HEADS = D_MODEL // 2 // RET_VAL_DIM
RET_CHUNK = 128
RET_THETA = 10000.0
MLA_V = 128
MLA_HEADS = D_MODEL // MLA_V
MLA_Q_RANK = 1024
MLA_KV_RANK = 512
MLA_NOPE = 128
MLA_ROPE = 64
FFN_DIM = 14336
N_EXPERTS = 8
TOP_K_EXPERTS = 2
EXPERT_DIM = 5120
ROPE_THETA = 500000.0
Q_BLOCK = 128
LN_EPS = 1e-5
RMS_EPS = 1e-6
DEEPNORM_ALPHA = (2.0 * DEPTH) ** 0.25
DEEPNORM_BETA = (8.0 * DEPTH) ** -0.25

L0_COLS = (
    A_HEADS * A_HEAD_DIM,
    A_KV_HEADS * A_HEAD_DIM,
    A_KV_HEADS * A_HEAD_DIM,
    IDX_HEADS * IDX_DIM,
    IDX_DIM,
    IDX_HEADS,
    RET_HEADS * RET_KEY_DIM,
    RET_HEADS * RET_KEY_DIM,
    RET_HEADS * RET_VAL_DIM,
    RET_HEADS * RET_VAL_DIM,
)
L0_IN = sum(L0_COLS)
L0_MIX = A_HEADS * A_HEAD_DIM + RET_HEADS * RET_VAL_DIM
L1_DOWN = MLA_Q_RANK + MLA_KV_RANK + MLA_ROPE

kernel_name = "hybrid_dsa_retention_mla_moe_block"

F32 = jnp.float32


def layer_norm(x, g, b):
    xf = x.astype(F32)
    mu = jnp.mean(xf, -1, keepdims=True)
    var = jnp.mean(jnp.square(xf - mu), -1, keepdims=True)
    return ((xf - mu) * lax.rsqrt(var + LN_EPS) * g + b).astype(x.dtype)


def rms_norm(x, g):
    xf = x.astype(F32)
    return (xf * lax.rsqrt(jnp.mean(xf * xf, -1, keepdims=True) + RMS_EPS) * g).astype(x.dtype)


def rope_tables(T, rot_dim, theta):
    inv = theta ** (-jnp.arange(0, rot_dim, 2, dtype=F32) / rot_dim)
    ang = jnp.arange(T, dtype=F32)[:, None] * inv[None, :]
    return jnp.cos(ang), jnp.sin(ang)


def retention_tables(T):
    inv = 1.0 / (RET_THETA ** jnp.linspace(0.0, 1.0, RET_KEY_DIM // 2, dtype=F32))
    ang = jnp.arange(T, dtype=F32)[:, None] * inv[None, :]
    return jnp.cos(ang), jnp.sin(ang)


def apply_rotary(x, cos, sin):
    half = x.shape[-1] // 2
    x1 = x[..., :half].astype(F32)
    x2 = x[..., half:].astype(F32)
    c = cos[None, :, None, :]
    s = sin[None, :, None, :]
    return jnp.concatenate([x1 * c - x2 * s, x2 * c + x1 * s], -1).astype(x.dtype)


def partial_rotary(x, cos, sin):
    r = 2 * cos.shape[-1]
    return jnp.concatenate([apply_rotary(x[..., :r], cos, sin), x[..., r:]], -1)


def split_cols(a, sizes):
    offs, o = [], 0
    for s in sizes[:-1]:
        o += s
        offs.append(o)
    return jnp.split(a, offs, axis=-1)


def to_blocks(a):
    B, T = a.shape[:2]
    return jnp.swapaxes(a.reshape((B, T // Q_BLOCK, Q_BLOCK) + a.shape[2:]), 0, 1)


def from_blocks(a):
    a = jnp.swapaxes(a, 0, 1)
    return a.reshape((a.shape[0], a.shape[1] * a.shape[2]) + a.shape[3:])


def dsa_attention(q, k, v, qi, ki, wi):
    B, T = q.shape[:2]
    topk = min(DSA_TOPK, T // 4)
    n_rep = A_HEADS // A_KV_HEADS
    key_pos = jnp.arange(T)

    def block(args):
        qb, qib, wib, blk = args
        q_pos = blk * Q_BLOCK + jnp.arange(Q_BLOCK)
        causal = key_pos[None, :] <= q_pos[:, None]
        logits = jnp.einsum('bqhd,bsd->bqhs', qib, ki, preferred_element_type=F32) * IDX_DIM ** -0.5
        index_score = jnp.einsum('bqh,bqhs->bqs', wib.astype(F32) * IDX_HEADS ** -0.5, jax.nn.relu(logits))
        index_score = jnp.where(causal[None], index_score, -jnp.inf)
        _, sel = lax.top_k(index_score, topk)
        valid = sel <= q_pos[None, :, None]
        k_sel = jax.vmap(lambda kk, ii: kk[ii])(k, sel)
        v_sel = jax.vmap(lambda vv, ii: vv[ii])(v, sel)
        qg = qb.reshape(B, Q_BLOCK, A_KV_HEADS, n_rep, A_HEAD_DIM)
        s = jnp.einsum('bqgrd,bqkgd->bqgrk', qg, k_sel, preferred_element_type=F32) * A_HEAD_DIM ** -0.5
        s = jnp.where(valid[:, :, None, None, :], s, -jnp.inf)
        p = jax.nn.softmax(s, axis=-1).astype(v.dtype)
        o = jnp.einsum('bqgrk,bqkgd->bqgrd', p, v_sel)
        return o.reshape(B, Q_BLOCK, A_HEADS * A_HEAD_DIM)

    nb = T // Q_BLOCK
    out = lax.map(block, (to_blocks(q), to_blocks(qi), to_blocks(wi), jnp.arange(nb)))
    return from_blocks(out)


def retention_chunkwise(q, k, v, log_gamma):
    B, T, H, DK = q.shape
    DV = v.shape[-1]
    C = RET_CHUNK
    pos = jnp.arange(C, dtype=F32)
    diff = pos[:, None] - pos[None, :]
    decay_in = jnp.exp(jnp.where(diff[None] >= 0, log_gamma[:, None, None] * diff[None], -jnp.inf))
    q_decay = jnp.exp(log_gamma[:, None] * (pos[None] + 1.0))
    k_decay = jnp.exp(log_gamma[:, None] * (C - 1.0 - pos[None]))
    chunk_decay = jnp.exp(log_gamma * C)

    def chunks(a):
        return jnp.moveaxis(a.astype(F32).reshape(B, T // C, C, H, a.shape[-1]), (1, 3), (0, 2))

    def step(state, inp):
        qc, kc, vc = inp
        inner = jnp.einsum('bhid,bhjd->bhij', qc, kc) * decay_in[None]
        o = (jnp.einsum('bhij,bhjv->bhiv', inner, vc)
             + jnp.einsum('bhid,bhdv->bhiv', qc * q_decay[None, :, :, None], state))
        state = (state * chunk_decay[None, :, None, None]
                 + jnp.einsum('bhjd,bhjv->bhdv', kc * k_decay[None, :, :, None], vc))
        return state, o

    state0 = jnp.zeros((B, H, DK, DV), F32)
    _, o = lax.scan(step, state0, (chunks(q), chunks(k), chunks(v)))
    return jnp.moveaxis(o, (0, 2), (1, 3)).reshape(B, T, H, DV)


def retention_mixer(q, k, v, g, gn_g, cos, sin):
    B, T = q.shape[:2]
    log_gamma = jnp.log(1.0 - 2.0 ** (-5.0 - jnp.arange(RET_HEADS, dtype=F32)))
    q = apply_rotary(q, cos, sin)
    k = apply_rotary(k, cos, sin) * RET_KEY_DIM ** -0.5
    o = retention_chunkwise(q, k, v, log_gamma)
    mu = jnp.mean(o, -1, keepdims=True)
    var = jnp.mean(jnp.square(o - mu), -1, keepdims=True)
    o = ((o - mu) * lax.rsqrt(var + LN_EPS)).reshape(B, T, RET_HEADS * RET_VAL_DIM) * gn_g
    return (jax.nn.silu(g.astype(F32)) * o).astype(g.dtype)


def mla_attention(x, w_dq_dkv, q_norm_g, w_uq, kv_norm_g, w_ukv, cos, sin):
    B, T, _ = x.shape
    cq, ckv, kr = split_cols(x @ w_dq_dkv, (MLA_Q_RANK, MLA_KV_RANK, MLA_ROPE))
    q = (rms_norm(cq, q_norm_g) @ w_uq).reshape(B, T, MLA_HEADS, MLA_NOPE + MLA_ROPE)
    q_nope = q[..., :MLA_NOPE]
    q_rope = apply_rotary(q[..., MLA_NOPE:], cos, sin)
    k_rope = apply_rotary(kr[:, :, None, :], cos, sin)[:, :, 0]
    kv = (rms_norm(ckv, kv_norm_g) @ w_ukv).reshape(B, T, MLA_HEADS, MLA_NOPE + MLA_V)
    k_nope = kv[..., :MLA_NOPE]
    v = kv[..., MLA_NOPE:]
    scale = (MLA_NOPE + MLA_ROPE) ** -0.5
    key_pos = jnp.arange(T)

    def block(args):
        qn, qr, blk = args
        q_pos = blk * Q_BLOCK + jnp.arange(Q_BLOCK)
        s = (jnp.einsum('bqhd,bshd->bhqs', qn, k_nope, preferred_element_type=F32)
             + jnp.einsum('bqhd,bsd->bhqs', qr, k_rope, preferred_element_type=F32)) * scale
        s = jnp.where((key_pos[None, :] <= q_pos[:, None])[None, None], s, -jnp.inf)
        p = jax.nn.softmax(s, axis=-1).astype(v.dtype)
        return jnp.einsum('bhqs,bshd->bqhd', p, v).reshape(B, Q_BLOCK, MLA_HEADS * MLA_V)

    out = lax.map(block, (to_blocks(q_nope), to_blocks(q_rope), jnp.arange(T // Q_BLOCK)))
    return from_blocks(out)


def swiglu(x, w1, w3, w2):
    return (jax.nn.silu(x @ w1) * (x @ w3)) @ w2


def moe_swiglu(x, router, w1, w3, w2):
    B, T, D = x.shape
    xt = x.reshape(B * T, D)
    logits = jnp.dot(xt, router, preferred_element_type=F32)
    top_val, top_idx = lax.top_k(logits, TOP_K_EXPERTS)
    gates = jax.nn.softmax(top_val, axis=-1)
    flat_e = top_idx.reshape(-1)
    order = jnp.argsort(flat_e)
    tok = order // TOP_K_EXPERTS
    group_sizes = jnp.bincount(flat_e, length=N_EXPERTS).astype(jnp.int32)
    xs = xt[tok]
    h = jax.nn.silu(lax.ragged_dot(xs, w1, group_sizes)) * lax.ragged_dot(xs, w3, group_sizes)
    ys = lax.ragged_dot(h, w2, group_sizes) * gates.reshape(-1)[order][:, None].astype(h.dtype)
    y = jnp.zeros_like(xt).at[tok].add(ys.astype(xt.dtype))
    return y.reshape(B, T, D)


def even_layer(x, w_in, ret_gn_g, w_out, ln1_g, ln1_b, w1, w3, w2, ln2_g, ln2_b, rope_a, rope_i, rope_r):
    B, T, _ = x.shape
    qa, ka, va, qi, ki, wi, qb, kb, vb, gb = split_cols(x @ w_in, L0_COLS)
    qa = partial_rotary(qa.reshape(B, T, A_HEADS, A_HEAD_DIM), *rope_a)
    ka = partial_rotary(ka.reshape(B, T, A_KV_HEADS, A_HEAD_DIM), *rope_a)
    va = va.reshape(B, T, A_KV_HEADS, A_HEAD_DIM)
    qi = partial_rotary(qi.reshape(B, T, IDX_HEADS, IDX_DIM), *rope_i)
    ki = partial_rotary(ki.reshape(B, T, 1, IDX_DIM), *rope_i)[:, :, 0]
    ya = dsa_attention(qa, ka, va, qi, ki, wi)
    yb = retention_mixer(qb.reshape(B, T, RET_HEADS, RET_KEY_DIM),
                         kb.reshape(B, T, RET_HEADS, RET_KEY_DIM),
                         vb.reshape(B, T, RET_HEADS, RET_VAL_DIM),
                         gb, ret_gn_g, *rope_r)
    y = jnp.concatenate([ya, yb.astype(ya.dtype)], axis=-1) @ w_out
    x = layer_norm(DEEPNORM_ALPHA * x + y, ln1_g, ln1_b)
    return layer_norm(DEEPNORM_ALPHA * x + swiglu(x, w1, w3, w2), ln2_g, ln2_b)


def odd_layer(x, w_dq_dkv, q_norm_g, w_uq, kv_norm_g, w_ukv, w_out, ln1_g, ln1_b,
              router, we1, we3, we2, ln2_g, ln2_b, rope_c):
    y = mla_attention(x, w_dq_dkv, q_norm_g, w_uq, kv_norm_g, w_ukv, *rope_c) @ w_out
    x = layer_norm(DEEPNORM_ALPHA * x + y, ln1_g, ln1_b)
    return layer_norm(DEEPNORM_ALPHA * x + moe_swiglu(x, router, we1, we3, we2), ln2_g, ln2_b)


def setup_inputs(seed: int = 0) -> dict:
    key = jax.random.key(seed)
    ks = jax.random.split(key, 25)

    def w(k, shape, fan_in, scale=1.0):
        return jax.random.normal(k, shape, F32) * (scale * fan_in ** -0.5)

    def gain(k, n):
        return 1.0 + 0.02 * jax.random.normal(k, (n,), F32)

    def bias(k, n):
        return 0.02 * jax.random.normal(k, (n,), F32)

    D = D_MODEL
    b = DEEPNORM_BETA
    return {
        "x": jax.random.normal(ks[0], (BATCH, SEQ, D), F32),
        "l0_w_in": w(ks[1], (D, L0_IN), D),
        "l0_ret_gn_g": gain(ks[2], RET_HEADS * RET_VAL_DIM),
        "l0_w_out": w(ks[3], (L0_MIX, D), L0_MIX, b),
        "l0_ln1_g": gain(ks[4], D),
        "l0_ln1_b": bias(ks[5], D),
        "l0_ffn_w1": w(ks[6], (D, FFN_DIM), D),
        "l0_ffn_w3": w(ks[7], (D, FFN_DIM), D),
        "l0_ffn_w2": w(ks[8], (FFN_DIM, D), FFN_DIM, b),
        "l0_ln2_g": gain(ks[9], D),
        "l0_ln2_b": bias(ks[10], D),
        "l1_w_dq_dkv": w(ks[11], (D, L1_DOWN), D),
        "l1_q_norm_g": gain(ks[12], MLA_Q_RANK),
        "l1_w_uq": w(ks[13], (MLA_Q_RANK, MLA_HEADS * (MLA_NOPE + MLA_ROPE)), MLA_Q_RANK),
        "l1_kv_norm_g": gain(ks[14], MLA_KV_RANK),
        "l1_w_ukv": w(ks[15], (MLA_KV_RANK, MLA_HEADS * (MLA_NOPE + MLA_V)), MLA_KV_RANK),
        "l1_w_out": w(ks[16], (MLA_HEADS * MLA_V, D), MLA_HEADS * MLA_V, b),
        "l1_ln1_g": gain(ks[17], D),
        "l1_ln1_b": bias(ks[18], D),
        "l1_router": w(ks[19], (D, N_EXPERTS), D),
        "l1_moe_w1": w(ks[20], (N_EXPERTS, D, EXPERT_DIM), D),
        "l1_moe_w3": w(ks[21], (N_EXPERTS, D, EXPERT_DIM), D),
        "l1_moe_w2": w(ks[22], (N_EXPERTS, EXPERT_DIM, D), EXPERT_DIM, b),
        "l1_ln2_g": gain(ks[23], D),
        "l1_ln2_b": bias(ks[24], D),
    }


def reference(x, l0_w_in, l0_ret_gn_g, l0_w_out, l0_ln1_g, l0_ln1_b, l0_ffn_w1, l0_ffn_w3, l0_ffn_w2,
              l0_ln2_g, l0_ln2_b, l1_w_dq_dkv, l1_q_norm_g, l1_w_uq, l1_kv_norm_g, l1_w_ukv, l1_w_out,
              l1_ln1_g, l1_ln1_b, l1_router, l1_moe_w1, l1_moe_w3, l1_moe_w2, l1_ln2_g, l1_ln2_b):
    T = x.shape[1]
    rope_a = rope_tables(T, A_HEAD_DIM // 4, ROPE_THETA)
    rope_i = rope_tables(T, IDX_DIM // 4, ROPE_THETA)
    rope_r = retention_tables(T)
    rope_c = rope_tables(T, MLA_ROPE, ROPE_THETA)
    layer_params = (
        (l0_w_in, l0_ret_gn_g, l0_w_out, l0_ln1_g, l0_ln1_b, l0_ffn_w1, l0_ffn_w3, l0_ffn_w2, l0_ln2_g, l0_ln2_b),
        (l1_w_dq_dkv, l1_q_norm_g, l1_w_uq, l1_kv_norm_g, l1_w_ukv, l1_w_out, l1_ln1_g, l1_ln1_b,
         l1_router, l1_moe_w1, l1_moe_w3, l1_moe_w2, l1_ln2_g, l1_ln2_b),
    )
    for layer in range(DEPTH):
        p = layer_params[layer]
        if layer % 2 == 0:
            x = even_layer(x, *p, rope_a, rope_i, rope_r)
        else:
            x = odd_layer(x, *p, rope_c)
    return x
```

```python
import functools

import jax
import jax.numpy as jnp
from jax import lax
from jax.experimental import pallas as pl
from jax.experimental.pallas import tpu as pltpu

F32 = jnp.float32
BF16 = jnp.bfloat16
I32 = jnp.int32

A_HEAD_DIM = 128
A_KV_HEADS = 4
IDX_HEADS = 16
IDX_DIM = 64
DSA_TOPK_MAX = 256
RET_KEY_DIM = 256
RET_VAL_DIM = 256
RET_CHUNK = 128
RET_THETA = 10000.0
MLA_V = 128
MLA_NOPE = 128
MLA_ROPE = 64
ROPE_THETA = 500000.0
Q_BLOCK = 128
LN_EPS = 1e-5
RMS_EPS = 1e-6
DEPTH = 2
ALPHA = (2.0 * DEPTH) ** 0.25

LANE = 128
V7X_VMEM_BYTES = 64 * 1024 * 1024
VMEM_LIMIT = V7X_VMEM_BYTES - 8 * 1024 * 1024
MASKED = -1e30
INT_MIN = -(2 ** 31)

NT_DIMS = (((1,), (1,)), ((), ()))
TN_DIMS = (((0,), (0,)), ((), ()))


def _tile(n, pref, mult=LANE):
    if n <= pref:
        return n
    t = (pref // mult) * mult
    while t > mult and n % t:
        t -= mult
    assert n % t == 0, (n, pref, mult)
    return t


def _params(*sem):
    return pltpu.CompilerParams(dimension_semantics=sem, vmem_limit_bytes=VMEM_LIMIT)


def _rope_cos_sin(T, rot_dim, theta):
    inv = theta ** (-jnp.arange(0, rot_dim, 2, dtype=F32) / rot_dim)
    ang = jnp.arange(T, dtype=F32)[:, None] * inv[None, :]
    return jnp.cos(ang), jnp.sin(ang)


def _lane_tables(cos, sin, head_dim, scale=1.0):
    T, half = cos.shape
    rest = head_dim - 2 * half
    zh = jnp.zeros((T, half), F32)
    c = jnp.concatenate([cos, cos, jnp.ones((T, rest), F32)], 1)
    sa = jnp.concatenate([-sin, zh, jnp.zeros((T, rest), F32)], 1)
    sb = jnp.concatenate([zh, sin, jnp.zeros((T, rest), F32)], 1)
    reps = LANE // head_dim
    return tuple(jnp.tile(t * scale, (1, reps)) for t in (c, sa, sb))


def _proj_kernel(*refs, slab_pat, mode, half, scale, with_tab):
    if with_tab:
        x_ref, w_ref, c_ref, sa_ref, sb_ref, o_ref = refs
    else:
        x_ref, w_ref, o_ref = refs
    acc = jnp.dot(x_ref[...], w_ref[...], preferred_element_type=F32)
    for s, p in enumerate(slab_pat):
        a = acc[:, s * LANE:(s + 1) * LANE]
        if p < 0:
            out = a if scale == 1.0 else a * scale
        else:
            c = c_ref[:, p * LANE:(p + 1) * LANE]
            sa = sa_ref[:, p * LANE:(p + 1) * LANE]
            if mode == "lane":
                sb = sb_ref[:, p * LANE:(p + 1) * LANE]
                out = a * c + pltpu.roll(a, LANE - half, 1) * sa + pltpu.roll(a, half, 1) * sb
            else:
                q = s ^ 1
                out = a * c + acc[:, q * LANE:(q + 1) * LANE] * sa
        o_ref[:, s * LANE:(s + 1) * LANE] = out.astype(o_ref.dtype)


def _proj(x, w, *, out_dtype, bm=1024, bn=1024, tabs=None, slab_pat=None, mode="lane", half=0,
          scale=1.0, seq_len=None, name="proj"):
    M, K = x.shape
    N = w.shape[1]
    bm = _tile(M, bm) if seq_len is None else _tile(seq_len, bm)
    bn = _tile(N, bn)
    if slab_pat is None:
        slab_pat = (-1,) * (bn // LANE)
    assert len(slab_pat) == bn // LANE
    in_specs = [pl.BlockSpec((bm, K), lambda i, j: (i, 0)),
                pl.BlockSpec((K, bn), lambda i, j: (0, j))]
    args = [x, w]
    if tabs is not None:
        tb = seq_len // bm
        tw = tabs[0].shape[1]
        in_specs += [pl.BlockSpec((bm, tw), lambda i, j: (i % tb, 0))] * 3
        args += list(tabs)
    kern = functools.partial(_proj_kernel, slab_pat=tuple(slab_pat), mode=mode, half=half,
                             scale=scale, with_tab=tabs is not None)
    return pl.pallas_call(
        kern,
        grid=(M // bm, N // bn),
        in_specs=in_specs,
        out_specs=pl.BlockSpec((bm, bn), lambda i, j: (i, j)),
        out_shape=jax.ShapeDtypeStruct((M, N), out_dtype),
        compiler_params=_params("parallel", "parallel"),
        name=name,
    )(*args)


def _mm_ksplit_kernel(x_ref, w_ref, o_ref):
    part = jnp.dot(x_ref[...], w_ref[...], preferred_element_type=F32)

    @pl.when(pl.program_id(2) == 0)
    def _():
        o_ref[...] = part

    @pl.when(pl.program_id(2) > 0)
    def _():
        o_ref[...] += part


def _mm_ksplit(x, w, *, bm=1024, bn=1024, bk=2048, name="mm_ksplit"):
    M, K = x.shape
    N = w.shape[1]
    bm, bn, bk = _tile(M, bm), _tile(N, bn), _tile(K, bk)
    return pl.pallas_call(
        _mm_ksplit_kernel,
        grid=(M // bm, N // bn, K // bk),
        in_specs=[pl.BlockSpec((bm, bk), lambda i, j, k: (i, k)),
                  pl.BlockSpec((bk, bn), lambda i, j, k: (k, j))],
        out_specs=pl.BlockSpec((bm, bn), lambda i, j, k: (i, j)),
        out_shape=jax.ShapeDtypeStruct((M, N), F32),
        compiler_params=_params("parallel", "parallel", "arbitrary"),
        name=name,
    )(x, w)


def _swiglu_up_kernel(x_ref, w1_ref, w3_ref, o_ref):
    x = x_ref[...]
    a = jnp.dot(x, w1_ref[...], preferred_element_type=F32)
    b = jnp.dot(x, w3_ref[...], preferred_element_type=F32)
    o_ref[...] = (a * jax.nn.sigmoid(a) * b).astype(o_ref.dtype)


def _swiglu_up(x, w1, w3, *, bm=1024, bn=512):
    M, K = x.shape
    N = w1.shape[1]
    bm, bn = _tile(M, bm), _tile(N, bn)
    return pl.pallas_call(
        _swiglu_up_kernel,
        grid=(M // bm, N // bn),
        in_specs=[pl.BlockSpec((bm, K), lambda i, j: (i, 0)),
                  pl.BlockSpec((K, bn), lambda i, j: (0, j)),
                  pl.BlockSpec((K, bn), lambda i, j: (0, j))],
        out_specs=pl.BlockSpec((bm, bn), lambda i, j: (i, j)),
        out_shape=jax.ShapeDtypeStruct((M, N), BF16),
        compiler_params=_params("parallel", "parallel"),
        name="swiglu_up",
    )(x, w1, w3)


def _layer_norm_rows(z, g, b):
    mu = jnp.mean(z, axis=-1, keepdims=True)
    zc = z - mu
    var = jnp.mean(zc * zc, axis=-1, keepdims=True)
    return zc * lax.rsqrt(var + LN_EPS) * g + b


def _add_ln_kernel(x_ref, y_ref, g_ref, b_ref, of_ref, ob_ref):
    out = _layer_norm_rows(ALPHA * x_ref[...] + y_ref[...], g_ref[...], b_ref[...])
    of_ref[...] = out
    ob_ref[...] = out.astype(ob_ref.dtype)


def _add_ln(x, y, g, b, *, bm=256):
    M, D = x.shape
    bm = _tile(M, bm, 8)
    row = pl.BlockSpec((bm, D), lambda i: (i, 0))
    vec = pl.BlockSpec((1, D), lambda i: (0, 0))
    return pl.pallas_call(
        _add_ln_kernel,
        grid=(M // bm,),
        in_specs=[row, row, vec, vec],
        out_specs=[row, row],
        out_shape=[jax.ShapeDtypeStruct((M, D), F32), jax.ShapeDtypeStruct((M, D), BF16)],
        compiler_params=_params("parallel"),
        name="add_ln",
    )(x, y, g.reshape(1, D), b.reshape(1, D))


def _dsa_kernel(q_ref, k_ref, v_ref, iq_ref, ik_ref, o_ref, keys_ref, bias_ref, *, topk, ck, n_rep, idx_bits):
    blk = pl.program_id(1)
    n_chunks = (blk * Q_BLOCK + Q_BLOCK + ck - 1) // ck
    row = lax.broadcasted_iota(I32, (Q_BLOCK, ck), 0) + blk * Q_BLOCK
    lane = lax.broadcasted_iota(I32, (Q_BLOCK, ck), 1)
    w_off = IDX_HEADS * IDX_DIM + IDX_DIM
    iq = iq_ref[0]
    wi = iq[:, w_off:w_off + IDX_HEADS]
    q_idx = [iq[:, h * IDX_DIM:(h + 1) * IDX_DIM].astype(BF16) for h in range(IDX_HEADS)]

    def score_body(c, carry):
        off = pl.multiple_of(c * ck, ck)
        kc = ik_ref[0, pl.ds(off, ck), :][:, :IDX_DIM].astype(BF16)
        s = jnp.zeros((Q_BLOCK, ck), F32)
        for h in range(IDX_HEADS):
            lg = lax.dot_general(q_idx[h], kc, NT_DIMS, preferred_element_type=F32)
            s = s + wi[:, h:h + 1] * jnp.maximum(lg, 0.0)
        bits = pltpu.bitcast(s, I32)
        key = bits ^ ((bits >> 31) & 0x7FFFFFFF)
        keys_ref[c] = jnp.where(lane + off <= row, key, INT_MIN)
        return carry

    lax.fori_loop(0, n_chunks, score_body, 0)

    def count(indicator):
        def body(c, acc):
            part = indicator(keys_ref[c], lane + c * ck)
            for j in range(ck // LANE):
                acc = acc + part[:, j * LANE:(j + 1) * LANE]
            return acc
        acc = lax.fori_loop(0, n_chunks, body, jnp.zeros((Q_BLOCK, LANE), F32))
        return jnp.sum(acc, axis=1, keepdims=True)

    kf = float(topk)
    ok = count(lambda kc, idx: jnp.where(kc >= 0, 1.0, 0.0)) >= kf
    thr = jnp.where(ok, 0, INT_MIN).astype(I32)

    def bit_body(i, thr):
        cand = thr + lax.shift_left(jnp.int32(1), 30 - i)
        ok = count(lambda kc, idx: jnp.where(kc >= cand, 1.0, 0.0)) >= kf
        return jnp.where(ok, cand, thr)

    thr = lax.fori_loop(0, 31, bit_body, thr)

    need = kf - count(lambda kc, idx: jnp.where(kc > thr, 1.0, 0.0))

    def tie_body(i, last):
        cand = last + lax.shift_left(jnp.int32(1), idx_bits - 1 - i)
        ok = count(lambda kc, idx: jnp.where(kc == thr, jnp.where(idx < cand, 1.0, 0.0), 0.0)) < need
        return jnp.where(ok, cand, last)

    last = lax.fori_loop(0, idx_bits, tie_body, jnp.zeros((Q_BLOCK, 1), I32))
    last = jnp.where(thr > INT_MIN, last, -1)

    def bias_body(c, carry):
        kc = keys_ref[c]
        tie_bias = jnp.where(lane + c * ck <= last, 0.0, MASKED)
        bias_ref[c] = jnp.where(kc == thr, tie_bias, jnp.where(kc > thr, 0.0, MASKED))
        return carry

    lax.fori_loop(0, n_chunks, bias_body, 0)

    q = q_ref[0]
    rows = n_rep * Q_BLOCK
    for g in range(A_KV_HEADS):
        qg = jnp.concatenate(
            [q[:, (g * n_rep + r) * A_HEAD_DIM:(g * n_rep + r + 1) * A_HEAD_DIM] for r in range(n_rep)], axis=0)

        def att_body(c, carry, g=g, qg=qg):
            m, l, acc = carry
            off = pl.multiple_of(c * ck, ck)
            kc = k_ref[0, pl.ds(off, ck), g * A_HEAD_DIM:(g + 1) * A_HEAD_DIM]
            vc = v_ref[0, pl.ds(off, ck), g * A_HEAD_DIM:(g + 1) * A_HEAD_DIM]
            s = lax.dot_general(qg, kc, NT_DIMS, preferred_element_type=F32)
            b = bias_ref[c]
            s = s + jnp.concatenate([b] * n_rep, axis=0)
            m_new = jnp.maximum(m, jnp.max(s, axis=1, keepdims=True))
            a = jnp.exp(m - m_new)
            p = jnp.exp(s - m_new)
            l = a * l + jnp.sum(p, axis=1, keepdims=True)
            acc = a * acc + jnp.dot(p.astype(BF16), vc, preferred_element_type=F32)
            return m_new, l, acc

        init = (jnp.full((rows, 1), MASKED, F32), jnp.zeros((rows, 1), F32), jnp.zeros((rows, A_HEAD_DIM), F32))
        _, l, acc = lax.fori_loop(0, n_chunks, att_body, init)
        o = acc / l
        for r in range(n_rep):
            col = (g * n_rep + r) * A_HEAD_DIM
            o_ref[0, :, col:col + A_HEAD_DIM] = o[r * Q_BLOCK:(r + 1) * Q_BLOCK].astype(o_ref.dtype)


def _dsa(q, k, v, idx, *, B, T):
    a_heads = q.shape[-1] // A_HEAD_DIM
    n_rep = a_heads // A_KV_HEADS
    topk = min(DSA_TOPK_MAX, T // 4)
    ck = _tile(T, 512)
    idx_w = idx.shape[-1]
    kv_w = k.shape[-1]
    kern = functools.partial(_dsa_kernel, topk=topk, ck=ck, n_rep=n_rep, idx_bits=max(1, (T - 1).bit_length()))
    return pl.pallas_call(
        kern,
        grid=(B, T // Q_BLOCK),
        in_specs=[pl.BlockSpec((1, Q_BLOCK, q.shape[-1]), lambda b, i: (b, i, 0)),
                  pl.BlockSpec((1, T, kv_w), lambda b, i: (b, 0, 0)),
                  pl.BlockSpec((1, T, kv_w), lambda b, i: (b, 0, 0)),
                  pl.BlockSpec((1, Q_BLOCK, idx_w), lambda b, i: (b, i, 0)),
                  pl.BlockSpec((1, T, LANE), lambda b, i: (b, 0, IDX_HEADS * IDX_DIM // LANE))],
        out_specs=pl.BlockSpec((1, Q_BLOCK, q.shape[-1]), lambda b, i: (b, i, 0)),
        out_shape=jax.ShapeDtypeStruct(q.shape, BF16),
        scratch_shapes=[pltpu.VMEM((T // ck, Q_BLOCK, ck), I32), pltpu.VMEM((T // ck, Q_BLOCK, ck), F32)],
        compiler_params=_params("parallel", "parallel"),
        name="dsa",
    )(q, k, v, idx, idx)


def _retention_kernel(q_ref, k_ref, v_ref, g_ref, gn_ref, din_ref, qd_ref, kd_ref, cd_ref, o_ref, state_ref, *, n_sub):
    @pl.when(pl.program_id(2) == 0)
    def _():
        state_ref[...] = jnp.zeros_like(state_ref)

    din = din_ref[0]
    qd = qd_ref[0]
    kd = kd_ref[0]
    cd = cd_ref[0]
    gn = gn_ref[...]
    for s in range(n_sub):
        sl = pl.ds(s * RET_CHUNK, RET_CHUNK)
        qc = q_ref[sl, :]
        kc = k_ref[sl, :]
        vc = v_ref[sl, :]
        st = state_ref[...]
        inner = lax.dot_general(qc, kc, NT_DIMS, preferred_element_type=F32) * din
        o = (jnp.dot(inner.astype(BF16), vc, preferred_element_type=F32)
             + jnp.dot(qc, st.astype(BF16), preferred_element_type=F32) * qd)
        vk = (vc.astype(F32) * kd).astype(BF16)
        state_ref[...] = st * cd + lax.dot_general(kc, vk, TN_DIMS, preferred_element_type=F32)
        mu = jnp.mean(o, axis=-1, keepdims=True)
        oc = o - mu
        var = jnp.mean(oc * oc, axis=-1, keepdims=True)
        gate = g_ref[sl, :].astype(F32)
        o_ref[sl, :] = (gate * jax.nn.sigmoid(gate) * (oc * lax.rsqrt(var + LN_EPS) * gn)).astype(o_ref.dtype)


def _retention(qk, pv, gn_g, *, B, T, heads, v_blk0):
    N = qk.shape[0]
    C = RET_CHUNK
    rb = _tile(T, 512)
    n_sub = rb // C
    nr = T // rb
    log_gamma = jnp.log(1.0 - 2.0 ** (-5.0 - jnp.arange(heads, dtype=F32)))
    pos = jnp.arange(C, dtype=F32)
    diff = pos[:, None] - pos[None, :]
    din = jnp.exp(jnp.where(diff[None] >= 0, log_gamma[:, None, None] * diff[None], -jnp.inf))
    qd = jnp.exp(log_gamma[:, None] * (pos[None] + 1.0))[:, :, None]
    kd = jnp.exp(log_gamma[:, None] * (C - 1.0 - pos[None]))[:, :, None]
    cd = jnp.exp(log_gamma * C)[:, None, None]
    W = RET_VAL_DIM
    blk = lambda off: pl.BlockSpec((rb, W), lambda b, h, r: (b * nr + r, off + h))
    per_head = lambda shape: pl.BlockSpec((1,) + shape, lambda b, h, r: (h, 0, 0))
    return pl.pallas_call(
        functools.partial(_retention_kernel, n_sub=n_sub),
        grid=(B, heads, nr),
        in_specs=[blk(0), blk(heads), blk(v_blk0), blk(v_blk0 + heads),
                  pl.BlockSpec((1, W), lambda b, h, r: (0, h)),
                  per_head((C, C)), per_head((C, 1)), per_head((C, 1)), per_head((1, 1))],
        out_specs=pl.BlockSpec((rb, W), lambda b, h, r: (b * nr + r, h)),
        out_shape=jax.ShapeDtypeStruct((N, heads * W), BF16),
        scratch_shapes=[pltpu.VMEM((RET_KEY_DIM, RET_VAL_DIM), F32)],
        compiler_params=_params("parallel", "parallel", "arbitrary"),
        name="retention",
    )(qk, qk, pv, pv, gn_g.reshape(1, heads * W), din, qd, kd, cd)


def _mla_down_kernel(x_ref, w_ref, qg_ref, kvg_ref, c_ref, sa_ref, sb_ref, cq_ref, ckv_ref, kr_ref, *, q_rank, kv_rank):
    acc = jnp.dot(x_ref[...], w_ref[...], preferred_element_type=F32)

    def rms(a, g):
        return a * lax.rsqrt(jnp.mean(a * a, axis=-1, keepdims=True) + RMS_EPS) * g

    cq_ref[...] = rms(acc[:, :q_rank], qg_ref[...]).astype(cq_ref.dtype)
    ckv_ref[...] = rms(acc[:, q_rank:q_rank + kv_rank], kvg_ref[...]).astype(ckv_ref.dtype)
    kr = acc[:, q_rank + kv_rank:]
    half = MLA_ROPE // 2
    kr = kr * c_ref[...] + pltpu.roll(kr, LANE - half, 1) * sa_ref[...] + pltpu.roll(kr, half, 1) * sb_ref[...]
    kr_ref[...] = kr.astype(kr_ref.dtype)


def _mla_down(x, w, q_g, kv_g, tabs, *, T, q_rank, kv_rank, bm=512):
    M, K = x.shape
    Nw = w.shape[1]
    bm = _tile(T, bm)
    tb = T // bm
    row = lambda n: pl.BlockSpec((bm, n), lambda i: (i, 0))
    tab = pl.BlockSpec((bm, LANE), lambda i: (i % tb, 0))
    return pl.pallas_call(
        functools.partial(_mla_down_kernel, q_rank=q_rank, kv_rank=kv_rank),
        grid=(M // bm,),
        in_specs=[row(K), pl.BlockSpec((K, Nw), lambda i: (0, 0)),
                  pl.BlockSpec((1, q_rank), lambda i: (0, 0)), pl.BlockSpec((1, kv_rank), lambda i: (0, 0)),
                  tab, tab, tab],
        out_specs=[row(q_rank), row(kv_rank), row(LANE)],
        out_shape=[jax.ShapeDtypeStruct((M, q_rank), BF16), jax.ShapeDtypeStruct((M, kv_rank), BF16),
                   jax.ShapeDtypeStruct((M, LANE), BF16)],
        compiler_params=_params("parallel"),
        name="mla_down",
    )(x, w, q_g.reshape(1, q_rank), kv_g.reshape(1, kv_rank), *tabs)


def _mla_attn_kernel(qn_ref, qr_ref, kn_ref, kr_ref, v_ref, o_ref, *, tq):
    h = pl.program_id(1)
    i = pl.program_id(2)
    lane = lax.broadcasted_iota(I32, (tq, LANE), 1)
    lo = (h % 2) * MLA_ROPE
    own = jnp.where((lane >= lo) & (lane < lo + MLA_ROPE), 1.0, 0.0)
    qr = (qr_ref[...].astype(F32) * own).astype(BF16)
    q = jnp.concatenate([qn_ref[...], qr], axis=1)

    def step(c, carry, masked):
        m, l, acc = carry
        off = pl.multiple_of(c * tq, tq)
        k = jnp.concatenate([kn_ref[pl.ds(off, tq), :], kr_ref[pl.ds(off, tq), :]], axis=1)
        s = lax.dot_general(q, k, NT_DIMS, preferred_element_type=F32)
        if masked:
            r_io = lax.broadcasted_iota(I32, (tq, tq), 0)
            c_io = lax.broadcasted_iota(I32, (tq, tq), 1)
            s = jnp.where(c_io <= r_io, s, MASKED)
        m_new = jnp.maximum(m, jnp.max(s, axis=1, keepdims=True))
        a = jnp.exp(m - m_new)
        p = jnp.exp(s - m_new)
        l = a * l + jnp.sum(p, axis=1, keepdims=True)
        acc = a * acc + jnp.dot(p.astype(BF16), v_ref[pl.ds(off, tq), :], preferred_element_type=F32)
        return m_new, l, acc

    init = (jnp.full((tq, 1), MASKED, F32), jnp.zeros((tq, 1), F32), jnp.zeros((tq, MLA_V), F32))
    carry = lax.fori_loop(0, i, functools.partial(step, masked=False), init)
    _, l, acc = step(i, carry, True)
    o_ref[...] = (acc / l).astype(o_ref.dtype)


def _mla_attn(qn, qr, kv, kr, *, B, T, heads):
    N = qn.shape[0]
    tq = _tile(T, 512)
    nq = T // tq
    return pl.pallas_call(
        functools.partial(_mla_attn_kernel, tq=tq),
        grid=(B, heads, nq),
        in_specs=[pl.BlockSpec((tq, LANE), lambda b, h, i: (b * nq + i, h)),
                  pl.BlockSpec((tq, LANE), lambda b, h, i: (b * nq + i, h // 2)),
                  pl.BlockSpec((T, LANE), lambda b, h, i: (b, h)),
                  pl.BlockSpec((T, LANE), lambda b, h, i: (b, 0)),
                  pl.BlockSpec((T, LANE), lambda b, h, i: (b, heads + h))],
        out_specs=pl.BlockSpec((tq, LANE), lambda b, h, i: (b * nq + i, h)),
        out_shape=jax.ShapeDtypeStruct((N, heads * MLA_V), BF16),
        compiler_params=_params("parallel", "parallel", "parallel"),
        name="mla_attn",
    )(qn, qr, kv, kr, kv)


def _router_kernel(x_ref, r_ref, meta_ref, cnt_ref, carry_ref, *, n_exp):
    @pl.when(pl.program_id(0) == 0)
    def _():
        carry_ref[...] = jnp.zeros_like(carry_ref)

    bm = x_ref.shape[0]
    logits = jnp.dot(x_ref[...], r_ref[...], preferred_element_type=F32, precision=lax.Precision.HIGHEST)
    lane = lax.broadcasted_iota(I32, (bm, LANE), 1).astype(F32)
    logits = jnp.where(lane < n_exp, logits, -jnp.inf)
    m1 = jnp.max(logits, axis=1, keepdims=True)
    i1 = jnp.min(jnp.where(logits == m1, lane, float(LANE)), axis=1, keepdims=True)
    rest = jnp.where(lane == i1, -jnp.inf, logits)
    m2 = jnp.max(rest, axis=1, keepdims=True)
    i2 = jnp.min(jnp.where(rest == m2, lane, float(LANE)), axis=1, keepdims=True)
    e = jnp.exp(m2 - m1)
    g1 = 1.0 / (1.0 + e)
    g2 = e / (1.0 + e)
    sel = jnp.where(lane == i1, 1.0, jnp.where(lane == i2, 1.0, 0.0))
    r_io = lax.broadcasted_iota(I32, (bm, bm), 0)
    c_io = lax.broadcasted_iota(I32, (bm, bm), 1)
    below = jnp.where(c_io < r_io, 1.0, 0.0).astype(BF16)
    carry = carry_ref[0:1, :]
    rank = jnp.dot(below, sel.astype(BF16), preferred_element_type=F32) + carry
    r1 = jnp.sum(jnp.where(lane == i1, rank, 0.0), axis=1, keepdims=True)
    r2 = jnp.sum(jnp.where(lane == i2, rank, 0.0), axis=1, keepdims=True)
    meta = jnp.where(lane == 0, i1, 0.0)
    meta = jnp.where(lane == 1, i2, meta)
    meta = jnp.where(lane == 2, g1, meta)
    meta = jnp.where(lane == 3, g2, meta)
    meta = jnp.where(lane == 4, r1, meta)
    meta = jnp.where(lane == 5, r2, meta)
    meta_ref[...] = meta
    total = carry + jnp.sum(sel, axis=0, keepdims=True)
    carry_ref[...] = jnp.broadcast_to(total, carry_ref.shape)
    cnt_ref[...] = jnp.broadcast_to(total, cnt_ref.shape)


def _router(x, router, *, bm=512):
    M, D = x.shape
    n_exp = router.shape[1]
    bm = _tile(M, bm)
    r_pad = jnp.zeros((D, LANE), F32).at[:, :n_exp].set(router)
    return pl.pallas_call(
        functools.partial(_router_kernel, n_exp=n_exp),
        grid=(M // bm,),
        in_specs=[pl.BlockSpec((bm, D), lambda i: (i, 0)), pl.BlockSpec((D, LANE), lambda i: (0, 0))],
        out_specs=[pl.BlockSpec((bm, LANE), lambda i: (i, 0)), pl.BlockSpec((8, LANE), lambda i: (0, 0))],
        out_shape=[jax.ShapeDtypeStruct((M, LANE), F32), jax.ShapeDtypeStruct((8, LANE), F32)],
        scratch_shapes=[pltpu.VMEM((8, LANE), F32)],
        compiler_params=_params("arbitrary"),
        name="router",
    )(x, r_pad)


DISPATCH_WINDOW = 32


def _dispatch_kernel(dest_ref, x_hbm, xs_in_hbm, xs_hbm, sem, *, n_tok):
    del xs_in_hbm

    def copies(t):
        return [pltpu.make_async_copy(x_hbm.at[pl.ds(t, 1)], xs_hbm.at[pl.ds(dest_ref[2 * t + k], 1)], sem)
                for k in range(2)]

    def body(t, carry):
        @pl.when(t >= DISPATCH_WINDOW)
        def _():
            for cp in copies(t - DISPATCH_WINDOW):
                cp.wait()
        for cp in copies(t):
            cp.start()
        return carry

    lax.fori_loop(0, n_tok, body, 0)

    def drain(t, carry):
        for cp in copies(t):
            cp.wait()
        return carry

    lax.fori_loop(max(0, n_tok - DISPATCH_WINDOW), n_tok, drain, 0)


def _dispatch(x, dest, n_rows):
    M, D = x.shape
    zeros = jnp.zeros((n_rows, D), x.dtype)
    return pl.pallas_call(
        functools.partial(_dispatch_kernel, n_tok=M),
        grid_spec=pltpu.PrefetchScalarGridSpec(
            num_scalar_prefetch=1, grid=(1,),
            in_specs=[pl.BlockSpec(memory_space=pl.ANY), pl.BlockSpec(memory_space=pl.ANY)],
            out_specs=pl.BlockSpec(memory_space=pl.ANY),
            scratch_shapes=[pltpu.SemaphoreType.DMA(())]),
        out_shape=jax.ShapeDtypeStruct((n_rows, D), x.dtype),
        input_output_aliases={2: 0},
        compiler_params=pltpu.CompilerParams(dimension_semantics=("arbitrary",), has_side_effects=True),
        name="moe_dispatch",
    )(dest, x, zeros)


def _experts_kernel(te_ref, nv_ref, xs_ref, w1_ref, w3_ref, w2_ref, o_ref, xb_ref):
    r = pl.program_id(0)
    f = pl.program_id(1)

    @pl.when(r < nv_ref[0])
    def _():
        @pl.when(f == 0)
        def _():
            xb_ref[...] = xs_ref[...].astype(BF16)

        xb = xb_ref[...]
        a = jnp.dot(xb, w1_ref[0, 0], preferred_element_type=F32)
        b = jnp.dot(xb, w3_ref[0, 0], preferred_element_type=F32)
        hid = (a * jax.nn.sigmoid(a) * b).astype(BF16)
        part = jnp.dot(hid, w2_ref[0], preferred_element_type=F32)

        @pl.when(f == 0)
        def _():
            o_ref[...] = part

        @pl.when(f > 0)
        def _():
            o_ref[...] += part


def _experts(xs, w1r, w3r, w2, tile_expert, n_valid, *, tm):
    P, D = xs.shape
    E, nf, _, tf = w1r.shape
    n_tiles = P // tm

    def row_map(r, f, te, nv):
        return (jnp.minimum(r, nv[0] - 1), 0)

    def f_idx(r, f, nv):
        return jnp.where(r < nv[0], f, nf - 1)

    return pl.pallas_call(
        _experts_kernel,
        grid_spec=pltpu.PrefetchScalarGridSpec(
            num_scalar_prefetch=2, grid=(n_tiles, nf),
            in_specs=[pl.BlockSpec((tm, D), row_map, pipeline_mode=pl.Buffered(1)),
                      pl.BlockSpec((1, 1, D, tf), lambda r, f, te, nv: (te[r], f_idx(r, f, nv), 0, 0)),
                      pl.BlockSpec((1, 1, D, tf), lambda r, f, te, nv: (te[r], f_idx(r, f, nv), 0, 0)),
                      pl.BlockSpec((1, tf, D), lambda r, f, te, nv: (te[r], f_idx(r, f, nv), 0))],
            out_specs=pl.BlockSpec((tm, D), row_map),
            scratch_shapes=[pltpu.VMEM((tm, D), BF16)]),
        out_shape=jax.ShapeDtypeStruct((P, D), F32),
        compiler_params=_params("arbitrary", "arbitrary"),
        name="moe_experts",
    )(tile_expert, n_valid, xs, w1r, w3r, w2)


def _combine_kernel(dest_ref, x_ref, meta_ref, g_ref, b_ref, ys_hbm, o_ref, buf_ref, sem):
    bm = x_ref.shape[0]
    t0 = pl.program_id(0) * bm

    def copy(j, k, d):
        return pltpu.make_async_copy(ys_hbm.at[pl.ds(d, 1)], buf_ref.at[k, pl.ds(j, 1)], sem.at[k])

    def issue(j, carry):
        for k in range(2):
            copy(j, k, dest_ref[2 * (t0 + j) + k]).start()
        return carry

    lax.fori_loop(0, bm, issue, 0)

    def wait(j, carry):
        for k in range(2):
            copy(j, k, 0).wait()
        return carry

    lax.fori_loop(0, bm, wait, 0)
    meta = meta_ref[...]
    y = meta[:, 2:3] * buf_ref[0] + meta[:, 3:4] * buf_ref[1]
    o_ref[...] = _layer_norm_rows(ALPHA * x_ref[...] + y, g_ref[...], b_ref[...])


def _combine(x, meta, ys, dest, g, b, *, bm=256):
    M, D = x.shape
    bm = _tile(M, bm, 8)
    row = lambda n: pl.BlockSpec((bm, n), lambda i, d: (i, 0))
    vec = pl.BlockSpec((1, D), lambda i, d: (0, 0))
    return pl.pallas_call(
        _combine_kernel,
        grid_spec=pltpu.PrefetchScalarGridSpec(
            num_scalar_prefetch=1, grid=(M // bm,),
            in_specs=[row(D), row(LANE), vec, vec, pl.BlockSpec(memory_space=pl.ANY)],
            out_specs=row(D),
            scratch_shapes=[pltpu.VMEM((2, bm, D), F32), pltpu.SemaphoreType.DMA((2,))]),
        out_shape=jax.ShapeDtypeStruct((M, D), F32),
        compiler_params=_params("arbitrary"),
        name="moe_combine",
    )(dest, x, meta, g.reshape(1, D), b.reshape(1, D), ys)


def _even_layer(x, xb, w_in, ret_gn_g, w_out, ln1_g, ln1_b, w1, w3, w2, ln2_g, ln2_b, *, B, T):
    N, D = x.shape
    a_heads = D // 2 // A_HEAD_DIM
    r_heads = D // 2 // RET_VAL_DIM
    qa_w, kv_w = a_heads * A_HEAD_DIM, A_KV_HEADS * A_HEAD_DIM
    qi_w = IDX_HEADS * IDX_DIM
    rk_w, rv_w = r_heads * RET_KEY_DIM, r_heads * RET_VAL_DIM
    sizes = (qa_w, kv_w, kv_w, qi_w, IDX_DIM, IDX_HEADS, rk_w, rk_w, rv_w, rv_w)
    offs = [0]
    for s in sizes:
        offs.append(offs[-1] + s)
    col = lambda a, b_: w_in[:, offs[a]:offs[b_]]
    w_qa, w_ka, w_va = col(0, 1), col(1, 2), col(2, 3)
    w_qi, w_ki, w_wi = col(3, 4), col(4, 5), col(5, 6)
    w_qb, w_kb, w_vb, w_gb = col(6, 7), col(7, 8), col(8, 9), col(9, 10)

    cos_a, sin_a = _rope_cos_sin(T, A_HEAD_DIM // 4, ROPE_THETA)
    tab_q = _lane_tables(cos_a, sin_a, A_HEAD_DIM, A_HEAD_DIM ** -0.5)
    tab_k = _lane_tables(cos_a, sin_a, A_HEAD_DIM)
    cos_i, sin_i = _rope_cos_sin(T, IDX_DIM // 4, ROPE_THETA)
    tab_i = _lane_tables(cos_i, sin_i, IDX_DIM)
    pass_c = jnp.ones((T, LANE - IDX_DIM), F32)
    pass_s = jnp.zeros((T, LANE - IDX_DIM), F32)
    tab_idx = tuple(jnp.concatenate([t, t[:, :IDX_DIM], p], 1)
                    for t, p in zip(tab_i, (pass_c, pass_s, pass_s)))
    inv = 1.0 / (RET_THETA ** jnp.linspace(0.0, 1.0, RET_KEY_DIM // 2, dtype=F32))
    ang = jnp.arange(T, dtype=F32)[:, None] * inv[None, :]
    cos_r, sin_r = jnp.cos(ang), jnp.sin(ang)
    sin_pair = jnp.concatenate([-sin_r, sin_r], 1)
    tab_r = (jnp.concatenate([cos_r, cos_r], 1), sin_pair, sin_pair)

    idx_pad = LANE - IDX_DIM - IDX_HEADS
    w_idx = jnp.concatenate([w_qi, w_ki, w_wi * (IDX_DIM ** -0.5 * IDX_HEADS ** -0.5),
                             jnp.zeros((D, idx_pad), F32)], 1).astype(BF16)
    w_rqk = jnp.concatenate([w_qb, w_kb * RET_KEY_DIM ** -0.5], 1).astype(BF16)
    w_pv = jnp.concatenate([w_va, w_vb, w_gb], 1).astype(BF16)
    qa = _proj(xb, w_qa.astype(BF16), out_dtype=BF16, tabs=tab_q, half=A_HEAD_DIM // 8,
               seq_len=T, name="proj_qa", **_pat(qa_w, 1024, 0))
    ka = _proj(xb, w_ka.astype(BF16), out_dtype=BF16, tabs=tab_k, half=A_HEAD_DIM // 8,
               seq_len=T, name="proj_ka", **_pat(kv_w, 1024, 0))
    n_idx = w_idx.shape[1]
    idx = _proj(xb, w_idx, out_dtype=F32, tabs=tab_idx, half=IDX_DIM // 8, seq_len=T, name="proj_idx",
                bm=512, bn=n_idx, slab_pat=(0,) * (qi_w // LANE) + (1,))
    rqk = _proj(xb, w_rqk, out_dtype=BF16, tabs=tab_r, mode="pair", seq_len=T, name="proj_ret_qk",
                bn=1024, slab_pat=(0, 1) * (_tile(2 * rk_w, 1024) // (2 * LANE)))
    pv = _proj(xb, w_pv, out_dtype=BF16, bn=768, seq_len=T, name="proj_v")

    ya = _dsa(qa.reshape(B, T, qa_w), ka.reshape(B, T, kv_w), pv.reshape(B, T, -1), idx.reshape(B, T, n_idx),
              B=B, T=T)
    yb = _retention(rqk, pv, ret_gn_g, B=B, T=T, heads=r_heads, v_blk0=kv_w // RET_VAL_DIM)
    mix = jnp.concatenate([ya.reshape(N, qa_w), yb], axis=1)
    y = _proj(mix, w_out.astype(BF16), out_dtype=F32, name="proj_out0")
    x1, x1b = _add_ln(x, y, ln1_g, ln1_b)
    hid = _swiglu_up(x1b, w1.astype(BF16), w3.astype(BF16))
    y = _mm_ksplit(hid, w2.astype(BF16), name="ffn_down")
    return _add_ln(x1, y, ln2_g, ln2_b)


def _pat(width, bn, p):
    bn = _tile(width, bn)
    return dict(bn=bn, slab_pat=(p,) * (bn // LANE))


def _odd_layer(x, xb, w_dq_dkv, q_norm_g, w_uq, kv_norm_g, w_ukv, w_out, ln1_g, ln1_b,
               router, we1, we3, we2, ln2_g, ln2_b, *, B, T):
    N, D = x.shape
    heads = D // MLA_V
    q_rank, kv_rank = q_norm_g.shape[0], kv_norm_g.shape[0]
    scale = (MLA_NOPE + MLA_ROPE) ** -0.5
    cos_c, sin_c = _rope_cos_sin(T, MLA_ROPE, ROPE_THETA)
    tab_kr = _lane_tables(cos_c, sin_c, MLA_ROPE)
    tab_qr = _lane_tables(cos_c, sin_c, MLA_ROPE, scale)

    w_kr = w_dq_dkv[:, q_rank + kv_rank:]
    w_down = jnp.concatenate([w_dq_dkv[:, :q_rank + kv_rank], w_kr, w_kr], 1).astype(BF16)
    cq, ckv, kr = _mla_down(xb, w_down, q_norm_g, kv_norm_g, tab_kr, T=T, q_rank=q_rank, kv_rank=kv_rank)
    w_uq3 = w_uq.reshape(q_rank, heads, MLA_NOPE + MLA_ROPE)
    w_qn = w_uq3[:, :, :MLA_NOPE].reshape(q_rank, heads * MLA_NOPE).astype(BF16)
    w_qr = w_uq3[:, :, MLA_NOPE:].reshape(q_rank, heads * MLA_ROPE).astype(BF16)
    w_kv3 = w_ukv.reshape(kv_rank, heads, MLA_NOPE + MLA_V)
    w_kv = jnp.concatenate([w_kv3[:, :, :MLA_NOPE].reshape(kv_rank, heads * MLA_NOPE),
                            w_kv3[:, :, MLA_NOPE:].reshape(kv_rank, heads * MLA_V)], 1).astype(BF16)
    qn = _proj(cq, w_qn, out_dtype=BF16, scale=scale, seq_len=T, name="proj_q_nope")
    qr = _proj(cq, w_qr, out_dtype=BF16, tabs=tab_qr, half=MLA_ROPE // 2, seq_len=T, name="proj_q_rope",
               **_pat(heads * MLA_ROPE, 1024, 0))
    kv = _proj(ckv, w_kv, out_dtype=BF16, seq_len=T, name="proj_kv")
    att = _mla_attn(qn, qr, kv, kr, B=B, T=T, heads=heads)
    y = _proj(att, w_out.astype(BF16), out_dtype=F32, name="proj_out1")
    x1, _ = _add_ln(x, y, ln1_g, ln1_b)

    E = router.shape[1]
    F = we1.shape[2]
    tm = _tile(N, 512)
    tf = _tile(F, 256)
    meta, cnt = _router(x1, router)
    counts = cnt[0, :E].astype(I32)
    padded = (counts + tm - 1) // tm * tm
    ends = jnp.cumsum(padded)
    starts = ends - padded
    i1, i2 = meta[:, 0].astype(I32), meta[:, 1].astype(I32)
    dest = jnp.stack([starts[i1] + meta[:, 4].astype(I32), starts[i2] + meta[:, 5].astype(I32)], 1).reshape(-1)
    n_rows = 2 * N + E * tm
    n_tiles = n_rows // tm
    n_valid = (ends[-1] // tm).astype(I32).reshape(1)
    tile_start = jnp.arange(n_tiles, dtype=I32) * tm
    tile_expert = jnp.minimum(jnp.sum(tile_start[:, None] >= ends[None, :], axis=1), E - 1).astype(I32)
    tile_expert = jnp.where(jnp.arange(n_tiles) < n_valid[0], tile_expert, tile_expert[jnp.maximum(n_valid[0] - 1, 0)])
    xs = _dispatch(x1, dest, n_rows)
    w1r = we1.reshape(E, D, F // tf, tf).transpose(0, 2, 1, 3).astype(BF16)
    w3r = we3.reshape(E, D, F // tf, tf).transpose(0, 2, 1, 3).astype(BF16)
    ys = _experts(xs, w1r, w3r, we2.astype(BF16), tile_expert, n_valid, tm=tm)
    return _combine(x1, meta, ys, dest, ln2_g, ln2_b)


def kernel(x, l0_w_in, l0_ret_gn_g, l0_w_out, l0_ln1_g, l0_ln1_b, l0_ffn_w1, l0_ffn_w3, l0_ffn_w2, l0_ln2_g, l0_ln2_b, l1_w_dq_dkv, l1_q_norm_g, l1_w_uq, l1_kv_norm_g, l1_w_ukv, l1_w_out, l1_ln1_g, l1_ln1_b, l1_router, l1_moe_w1, l1_moe_w3, l1_moe_w2, l1_ln2_g, l1_ln2_b):
    B, T, D = x.shape
    x2 = x.reshape(B * T, D)
    h, hb = _even_layer(x2, x2.astype(BF16), l0_w_in, l0_ret_gn_g, l0_w_out, l0_ln1_g, l0_ln1_b,
                        l0_ffn_w1, l0_ffn_w3, l0_ffn_w2, l0_ln2_g, l0_ln2_b, B=B, T=T)
    out = _odd_layer(h, hb, l1_w_dq_dkv, l1_q_norm_g, l1_w_uq, l1_kv_norm_g, l1_w_ukv, l1_w_out,
                     l1_ln1_g, l1_ln1_b, l1_router, l1_moe_w1, l1_moe_w3, l1_moe_w2, l1_ln2_g, l1_ln2_b, B=B, T=T)
    return out.reshape(B, T, D)
```

```python
import functools

import jax
import jax.numpy as jnp
from jax import lax
from jax.experimental import pallas as pl
from jax.experimental.pallas import tpu as pltpu

F32 = jnp.float32
BF16 = jnp.bfloat16
I32 = jnp.int32

A_HEAD_DIM = 128
A_KV_HEADS = 4
IDX_HEADS = 16
IDX_DIM = 64
DSA_TOPK_MAX = 256
RET_KEY_DIM = 256
RET_VAL_DIM = 256
RET_CHUNK = 128
RET_THETA = 10000.0
MLA_V = 128
MLA_NOPE = 128
MLA_ROPE = 64
ROPE_THETA = 500000.0
Q_BLOCK = 128
LN_EPS = 1e-5
RMS_EPS = 1e-6
DEPTH = 2
ALPHA = (2.0 * DEPTH) ** 0.25

LANE = 128
V7X_VMEM_BYTES = 64 * 1024 * 1024
VMEM_LIMIT = V7X_VMEM_BYTES - 8 * 1024 * 1024
MASKED = -1e30
INT_MIN = -(2 ** 31)

NT_DIMS = (((1,), (1,)), ((), ()))
TN_DIMS = (((0,), (0,)), ((), ()))


def _tile(n, pref, mult=LANE):
    if n <= pref:
        return n
    t = (pref // mult) * mult
    while t > mult and n % t:
        t -= mult
    assert n % t == 0, (n, pref, mult)
    return t


def _params(*sem):
    return pltpu.CompilerParams(dimension_semantics=sem, vmem_limit_bytes=VMEM_LIMIT)


def _rope_cos_sin(T, rot_dim, theta):
    inv = theta ** (-jnp.arange(0, rot_dim, 2, dtype=F32) / rot_dim)
    ang = jnp.arange(T, dtype=F32)[:, None] * inv[None, :]
    return jnp.cos(ang), jnp.sin(ang)


def _lane_tables(cos, sin, head_dim, scale=1.0):
    T, half = cos.shape
    rest = head_dim - 2 * half
    zh = jnp.zeros((T, half), F32)
    c = jnp.concatenate([cos, cos, jnp.ones((T, rest), F32)], 1)
    sa = jnp.concatenate([-sin, zh, jnp.zeros((T, rest), F32)], 1)
    sb = jnp.concatenate([zh, sin, jnp.zeros((T, rest), F32)], 1)
    reps = LANE // head_dim
    return tuple(jnp.tile(t * scale, (1, reps)) for t in (c, sa, sb))


def _proj_kernel(*refs, slab_pat, mode, half, scale, with_tab):
    if with_tab:
        x_ref, w_ref, c_ref, sa_ref, sb_ref, o_ref = refs
    else:
        x_ref, w_ref, o_ref = refs
    acc = jnp.dot(x_ref[...], w_ref[...], preferred_element_type=F32)
    for s, p in enumerate(slab_pat):
        a = acc[:, s * LANE:(s + 1) * LANE]
        if p < 0:
            out = a if scale == 1.0 else a * scale
        else:
            c = c_ref[:, p * LANE:(p + 1) * LANE]
            sa = sa_ref[:, p * LANE:(p + 1) * LANE]
            if mode == "lane":
                sb = sb_ref[:, p * LANE:(p + 1) * LANE]
                out = a * c + pltpu.roll(a, LANE - half, 1) * sa + pltpu.roll(a, half, 1) * sb
            else:
                q = s ^ 1
                out = a * c + acc[:, q * LANE:(q + 1) * LANE] * sa
        o_ref[:, s * LANE:(s + 1) * LANE] = out.astype(o_ref.dtype)


def _proj(x, w, *, out_dtype, bm=1024, bn=1024, tabs=None, slab_pat=None, mode="lane", half=0,
          scale=1.0, seq_len=None, name="proj"):
    M, K = x.shape
    N = w.shape[1]
    bm = _tile(M, bm) if seq_len is None else _tile(seq_len, bm)
    bn = _tile(N, bn)
    if slab_pat is None:
        slab_pat = (-1,) * (bn // LANE)
    assert len(slab_pat) == bn // LANE
    in_specs = [pl.BlockSpec((bm, K), lambda i, j: (i, 0)),
                pl.BlockSpec((K, bn), lambda i, j: (0, j))]
    args = [x, w]
    if tabs is not None:
        tb = seq_len // bm
        tw = tabs[0].shape[1]
        in_specs += [pl.BlockSpec((bm, tw), lambda i, j: (i % tb, 0))] * 3
        args += list(tabs)
    kern = functools.partial(_proj_kernel, slab_pat=tuple(slab_pat), mode=mode, half=half,
                             scale=scale, with_tab=tabs is not None)
    return pl.pallas_call(
        kern,
        grid=(M // bm, N // bn),
        in_specs=in_specs,
        out_specs=pl.BlockSpec((bm, bn), lambda i, j: (i, j)),
        out_shape=jax.ShapeDtypeStruct((M, N), out_dtype),
        compiler_params=_params("parallel", "parallel"),
        name=name,
    )(*args)


def _mm_ksplit_kernel(x_ref, w_ref, o_ref):
    part = jnp.dot(x_ref[...], w_ref[...], preferred_element_type=F32)

    @pl.when(pl.program_id(2) == 0)
    def _():
        o_ref[...] = part

    @pl.when(pl.program_id(2) > 0)
    def _():
        o_ref[...] += part


def _mm_ksplit(x, w, *, bm=1024, bn=1024, bk=2048, name="mm_ksplit"):
    M, K = x.shape
    N = w.shape[1]
    bm, bn, bk = _tile(M, bm), _tile(N, bn), _tile(K, bk)
    return pl.pallas_call(
        _mm_ksplit_kernel,
        grid=(M // bm, N // bn, K // bk),
        in_specs=[pl.BlockSpec((bm, bk), lambda i, j, k: (i, k)),
                  pl.BlockSpec((bk, bn), lambda i, j, k: (k, j))],
        out_specs=pl.BlockSpec((bm, bn), lambda i, j, k: (i, j)),
        out_shape=jax.ShapeDtypeStruct((M, N), F32),
        compiler_params=_params("parallel", "parallel", "arbitrary"),
        name=name,
    )(x, w)


def _swiglu_up_kernel(x_ref, w1_ref, w3_ref, o_ref):
    x = x_ref[...]
    a = jnp.dot(x, w1_ref[...], preferred_element_type=F32)
    b = jnp.dot(x, w3_ref[...], preferred_element_type=F32)
    o_ref[...] = (a * jax.nn.sigmoid(a) * b).astype(o_ref.dtype)


def _swiglu_up(x, w1, w3, *, bm=1024, bn=512):
    M, K = x.shape
    N = w1.shape[1]
    bm, bn = _tile(M, bm), _tile(N, bn)
    return pl.pallas_call(
        _swiglu_up_kernel,
        grid=(M // bm, N // bn),
        in_specs=[pl.BlockSpec((bm, K), lambda i, j: (i, 0)),
                  pl.BlockSpec((K, bn), lambda i, j: (0, j)),
                  pl.BlockSpec((K, bn), lambda i, j: (0, j))],
        out_specs=pl.BlockSpec((bm, bn), lambda i, j: (i, j)),
        out_shape=jax.ShapeDtypeStruct((M, N), BF16),
        compiler_params=_params("parallel", "parallel"),
        name="swiglu_up",
    )(x, w1, w3)


def _layer_norm_rows(z, g, b):
    mu = jnp.mean(z, axis=-1, keepdims=True)
    zc = z - mu
    var = jnp.mean(zc * zc, axis=-1, keepdims=True)
    return zc * lax.rsqrt(var + LN_EPS) * g + b


def _add_ln_kernel(x_ref, y_ref, g_ref, b_ref, of_ref, ob_ref):
    out = _layer_norm_rows(ALPHA * x_ref[...] + y_ref[...], g_ref[...], b_ref[...])
    of_ref[...] = out
    ob_ref[...] = out.astype(ob_ref.dtype)


def _add_ln(x, y, g, b, *, bm=256):
    M, D = x.shape
    bm = _tile(M, bm, 8)
    row = pl.BlockSpec((bm, D), lambda i: (i, 0))
    vec = pl.BlockSpec((1, D), lambda i: (0, 0))
    return pl.pallas_call(
        _add_ln_kernel,
        grid=(M // bm,),
        in_specs=[row, row, vec, vec],
        out_specs=[row, row],
        out_shape=[jax.ShapeDtypeStruct((M, D), F32), jax.ShapeDtypeStruct((M, D), BF16)],
        compiler_params=_params("parallel"),
        name="add_ln",
    )(x, y, g.reshape(1, D), b.reshape(1, D))


def _dsa_kernel(q_ref, k_ref, v_ref, iq_ref, ik_ref, o_ref, keys_ref, bias_ref, *, topk, ck, n_rep, idx_bits):
    blk = pl.program_id(1)
    n_chunks = (blk * Q_BLOCK + Q_BLOCK + ck - 1) // ck
    row = lax.broadcasted_iota(I32, (Q_BLOCK, ck), 0) + blk * Q_BLOCK
    lane = lax.broadcasted_iota(I32, (Q_BLOCK, ck), 1)
    w_off = IDX_HEADS * IDX_DIM + IDX_DIM
    iq = iq_ref[0]
    wi = iq[:, w_off:w_off + IDX_HEADS]
    q_idx = [iq[:, h * IDX_DIM:(h + 1) * IDX_DIM].astype(BF16) for h in range(IDX_HEADS)]

    def score_body(c, carry):
        off = pl.multiple_of(c * ck, ck)
        kc = ik_ref[0, pl.ds(off, ck), :][:, :IDX_DIM].astype(BF16)
        s = jnp.zeros((Q_BLOCK, ck), F32)
        for h in range(IDX_HEADS):
            lg = lax.dot_general(q_idx[h], kc, NT_DIMS, preferred_element_type=F32)
            s = s + wi[:, h:h + 1] * jnp.maximum(lg, 0.0)
        bits = pltpu.bitcast(s, I32)
        key = bits ^ ((bits >> 31) & 0x7FFFFFFF)
        keys_ref[c] = jnp.where(lane + off <= row, key, INT_MIN)
        return carry

    lax.fori_loop(0, n_chunks, score_body, 0)

    def count(indicator):
        def body(c, acc):
            part = indicator(keys_ref[c], lane + c * ck)
            for j in range(ck // LANE):
                acc = acc + part[:, j * LANE:(j + 1) * LANE]
            return acc
        acc = lax.fori_loop(0, n_chunks, body, jnp.zeros((Q_BLOCK, LANE), F32))
        return jnp.sum(acc, axis=1, keepdims=True)

    kf = float(topk)
    ok = count(lambda kc, idx: jnp.where(kc >= 0, 1.0, 0.0)) >= kf
    thr = jnp.where(ok, 0, INT_MIN).astype(I32)

    def bit_body(i, thr):
        cand = thr + lax.shift_left(jnp.int32(1), 30 - i)
        ok = count(lambda kc, idx: jnp.where(kc >= cand, 1.0, 0.0)) >= kf
        return jnp.where(ok, cand, thr)

    thr = lax.fori_loop(0, 31, bit_body, thr)

    need = kf - count(lambda kc, idx: jnp.where(kc > thr, 1.0, 0.0))

    def tie_body(i, last):
        cand = last + lax.shift_left(jnp.int32(1), idx_bits - 1 - i)
        ok = count(lambda kc, idx: jnp.where(kc == thr, jnp.where(idx < cand, 1.0, 0.0), 0.0)) < need
        return jnp.where(ok, cand, last)

    last = lax.fori_loop(0, idx_bits, tie_body, jnp.zeros((Q_BLOCK, 1), I32))
    last = jnp.where(thr > INT_MIN, last, -1)

    def bias_body(c, carry):
        kc = keys_ref[c]
        tie_bias = jnp.where(lane + c * ck <= last, 0.0, MASKED)
        bias_ref[c] = jnp.where(kc == thr, tie_bias, jnp.where(kc > thr, 0.0, MASKED))
        return carry

    lax.fori_loop(0, n_chunks, bias_body, 0)

    q = q_ref[0]
    rows = n_rep * Q_BLOCK
    for g in range(A_KV_HEADS):
        qg = jnp.concatenate(
            [q[:, (g * n_rep + r) * A_HEAD_DIM:(g * n_rep + r + 1) * A_HEAD_DIM] for r in range(n_rep)], axis=0)

        def att_body(c, carry, g=g, qg=qg):
            m, l, acc = carry
            off = pl.multiple_of(c * ck, ck)
            kc = k_ref[0, pl.ds(off, ck), g * A_HEAD_DIM:(g + 1) * A_HEAD_DIM]
            vc = v_ref[0, pl.ds(off, ck), g * A_HEAD_DIM:(g + 1) * A_HEAD_DIM]
            s = lax.dot_general(qg, kc, NT_DIMS, preferred_element_type=F32)
            b = bias_ref[c]
            s = s + jnp.concatenate([b] * n_rep, axis=0)
            m_new = jnp.maximum(m, jnp.max(s, axis=1, keepdims=True))
            a = jnp.exp(m - m_new)
            p = jnp.exp(s - m_new)
            l = a * l + jnp.sum(p, axis=1, keepdims=True)
            acc = a * acc + jnp.dot(p.astype(BF16), vc, preferred_element_type=F32)
            return m_new, l, acc

        init = (jnp.full((rows, 1), MASKED, F32), jnp.zeros((rows, 1), F32), jnp.zeros((rows, A_HEAD_DIM), F32))
        _, l, acc = lax.fori_loop(0, n_chunks, att_body, init)
        o = acc / l
        for r in range(n_rep):
            col = (g * n_rep + r) * A_HEAD_DIM
            o_ref[0, :, col:col + A_HEAD_DIM] = o[r * Q_BLOCK:(r + 1) * Q_BLOCK].astype(o_ref.dtype)


def _dsa(q, k, v, idx, *, B, T):
    a_heads = q.shape[-1] // A_HEAD_DIM
    n_rep = a_heads // A_KV_HEADS
    topk = min(DSA_TOPK_MAX, T // 4)
    ck = _tile(T, 512)
    idx_w = idx.shape[-1]
    kv_w = k.shape[-1]
    kern = functools.partial(_dsa_kernel, topk=topk, ck=ck, n_rep=n_rep, idx_bits=max(1, (T - 1).bit_length()))
    return pl.pallas_call(
        kern,
        grid=(B, T // Q_BLOCK),
        in_specs=[pl.BlockSpec((1, Q_BLOCK, q.shape[-1]), lambda b, i: (b, i, 0)),
                  pl.BlockSpec((1, T, kv_w), lambda b, i: (b, 0, 0)),
                  pl.BlockSpec((1, T, kv_w), lambda b, i: (b, 0, 0)),
                  pl.BlockSpec((1, Q_BLOCK, idx_w), lambda b, i: (b, i, 0)),
                  pl.BlockSpec((1, T, LANE), lambda b, i: (b, 0, IDX_HEADS * IDX_DIM // LANE))],
        out_specs=pl.BlockSpec((1, Q_BLOCK, q.shape[-1]), lambda b, i: (b, i, 0)),
        out_shape=jax.ShapeDtypeStruct(q.shape, BF16),
        scratch_shapes=[pltpu.VMEM((T // ck, Q_BLOCK, ck), I32), pltpu.VMEM((T // ck, Q_BLOCK, ck), F32)],
        compiler_params=_params("parallel", "parallel"),
        name="dsa",
    )(q, k, v, idx, idx)


def _retention_kernel(q_ref, k_ref, v_ref, g_ref, gn_ref, din_ref, qd_ref, kd_ref, cd_ref, o_ref, state_ref, *, n_sub):
    @pl.when(pl.program_id(2) == 0)
    def _():
        state_ref[...] = jnp.zeros_like(state_ref)

    din = din_ref[0]
    qd = qd_ref[0]
    kd = kd_ref[0]
    cd = cd_ref[0]
    gn = gn_ref[...]
    for s in range(n_sub):
        sl = pl.ds(s * RET_CHUNK, RET_CHUNK)
        qc = q_ref[sl, :]
        kc = k_ref[sl, :]
        vc = v_ref[sl, :]
        st = state_ref[...]
        inner = lax.dot_general(qc, kc, NT_DIMS, preferred_element_type=F32) * din
        o = (jnp.dot(inner.astype(BF16), vc, preferred_element_type=F32)
             + jnp.dot(qc, st.astype(BF16), preferred_element_type=F32) * qd)
        vk = (vc.astype(F32) * kd).astype(BF16)
        state_ref[...] = st * cd + lax.dot_general(kc, vk, TN_DIMS, preferred_element_type=F32)
        mu = jnp.mean(o, axis=-1, keepdims=True)
        oc = o - mu
        var = jnp.mean(oc * oc, axis=-1, keepdims=True)
        gate = g_ref[sl, :].astype(F32)
        o_ref[sl, :] = (gate * jax.nn.sigmoid(gate) * (oc * lax.rsqrt(var + LN_EPS) * gn)).astype(o_ref.dtype)


def _retention(qk, pv, gn_g, *, B, T, heads, v_blk0):
    N = qk.shape[0]
    C = RET_CHUNK
    rb = _tile(T, 512)
    n_sub = rb // C
    nr = T // rb
    log_gamma = jnp.log(1.0 - 2.0 ** (-5.0 - jnp.arange(heads, dtype=F32)))
    pos = jnp.arange(C, dtype=F32)
    diff = pos[:, None] - pos[None, :]
    din = jnp.exp(jnp.where(diff[None] >= 0, log_gamma[:, None, None] * diff[None], -jnp.inf))
    qd = jnp.exp(log_gamma[:, None] * (pos[None] + 1.0))[:, :, None]
    kd = jnp.exp(log_gamma[:, None] * (C - 1.0 - pos[None]))[:, :, None]
    cd = jnp.exp(log_gamma * C)[:, None, None]
    W = RET_VAL_DIM
    blk = lambda off: pl.BlockSpec((rb, W), lambda b, h, r: (b * nr + r, off + h))
    per_head = lambda shape: pl.BlockSpec((1,) + shape, lambda b, h, r: (h, 0, 0))
    return pl.pallas_call(
        functools.partial(_retention_kernel, n_sub=n_sub),
        grid=(B, heads, nr),
        in_specs=[blk(0), blk(heads), blk(v_blk0), blk(v_blk0 + heads),
                  pl.BlockSpec((1, W), lambda b, h, r: (0, h)),
                  per_head((C, C)), per_head((C, 1)), per_head((C, 1)), per_head((1, 1))],
        out_specs=pl.BlockSpec((rb, W), lambda b, h, r: (b * nr + r, h)),
        out_shape=jax.ShapeDtypeStruct((N, heads * W), BF16),
        scratch_shapes=[pltpu.VMEM((RET_KEY_DIM, RET_VAL_DIM), F32)],
        compiler_params=_params("parallel", "parallel", "arbitrary"),
        name="retention",
    )(qk, qk, pv, pv, gn_g.reshape(1, heads * W), din, qd, kd, cd)


def _mla_down_kernel(x_ref, w_ref, qg_ref, kvg_ref, c_ref, sa_ref, sb_ref, cq_ref, ckv_ref, kr_ref, *, q_rank, kv_rank):
    acc = jnp.dot(x_ref[...], w_ref[...], preferred_element_type=F32)

    def rms(a, g):
        return a * lax.rsqrt(jnp.mean(a * a, axis=-1, keepdims=True) + RMS_EPS) * g

    cq_ref[...] = rms(acc[:, :q_rank], qg_ref[...]).astype(cq_ref.dtype)
    ckv_ref[...] = rms(acc[:, q_rank:q_rank + kv_rank], kvg_ref[...]).astype(ckv_ref.dtype)
    kr = acc[:, q_rank + kv_rank:]
    half = MLA_ROPE // 2
    kr = kr * c_ref[...] + pltpu.roll(kr, LANE - half, 1) * sa_ref[...] + pltpu.roll(kr, half, 1) * sb_ref[...]
    kr_ref[...] = kr.astype(kr_ref.dtype)


def _mla_down(x, w, q_g, kv_g, tabs, *, T, q_rank, kv_rank, bm=512):
    M, K = x.shape
    Nw = w.shape[1]
    bm = _tile(T, bm)
    tb = T // bm
    row = lambda n: pl.BlockSpec((bm, n), lambda i: (i, 0))
    tab = pl.BlockSpec((bm, LANE), lambda i: (i % tb, 0))
    return pl.pallas_call(
        functools.partial(_mla_down_kernel, q_rank=q_rank, kv_rank=kv_rank),
        grid=(M // bm,),
        in_specs=[row(K), pl.BlockSpec((K, Nw), lambda i: (0, 0)),
                  pl.BlockSpec((1, q_rank), lambda i: (0, 0)), pl.BlockSpec((1, kv_rank), lambda i: (0, 0)),
                  tab, tab, tab],
        out_specs=[row(q_rank), row(kv_rank), row(LANE)],
        out_shape=[jax.ShapeDtypeStruct((M, q_rank), BF16), jax.ShapeDtypeStruct((M, kv_rank), BF16),
                   jax.ShapeDtypeStruct((M, LANE), BF16)],
        compiler_params=_params("parallel"),
        name="mla_down",
    )(x, w, q_g.reshape(1, q_rank), kv_g.reshape(1, kv_rank), *tabs)


def _mla_attn_kernel(qn_ref, qr_ref, kn_ref, kr_ref, v_ref, o_ref, *, tq):
    h = pl.program_id(1)
    i = pl.program_id(2)
    lane = lax.broadcasted_iota(I32, (tq, LANE), 1)
    lo = (h % 2) * MLA_ROPE
    own = jnp.where((lane >= lo) & (lane < lo + MLA_ROPE), 1.0, 0.0)
    qr = (qr_ref[...].astype(F32) * own).astype(BF16)
    q = jnp.concatenate([qn_ref[...], qr], axis=1)

    def step(c, carry, masked):
        m, l, acc = carry
        off = pl.multiple_of(c * tq, tq)
        k = jnp.concatenate([kn_ref[pl.ds(off, tq), :], kr_ref[pl.ds(off, tq), :]], axis=1)
        s = lax.dot_general(q, k, NT_DIMS, preferred_element_type=F32)
        if masked:
            r_io = lax.broadcasted_iota(I32, (tq, tq), 0)
            c_io = lax.broadcasted_iota(I32, (tq, tq), 1)
            s = jnp.where(c_io <= r_io, s, MASKED)
        m_new = jnp.maximum(m, jnp.max(s, axis=1, keepdims=True))
        a = jnp.exp(m - m_new)
        p = jnp.exp(s - m_new)
        l = a * l + jnp.sum(p, axis=1, keepdims=True)
        acc = a * acc + jnp.dot(p.astype(BF16), v_ref[pl.ds(off, tq), :], preferred_element_type=F32)
        return m_new, l, acc

    init = (jnp.full((tq, 1), MASKED, F32), jnp.zeros((tq, 1), F32), jnp.zeros((tq, MLA_V), F32))
    carry = lax.fori_loop(0, i, functools.partial(step, masked=False), init)
    _, l, acc = step(i, carry, True)
    o_ref[...] = (acc / l).astype(o_ref.dtype)


def _mla_attn(qn, qr, kv, kr, *, B, T, heads):
    N = qn.shape[0]
    tq = _tile(T, 512)
    nq = T // tq
    return pl.pallas_call(
        functools.partial(_mla_attn_kernel, tq=tq),
        grid=(B, heads, nq),
        in_specs=[pl.BlockSpec((tq, LANE), lambda b, h, i: (b * nq + i, h)),
                  pl.BlockSpec((tq, LANE), lambda b, h, i: (b * nq + i, h // 2)),
                  pl.BlockSpec((T, LANE), lambda b, h, i: (b, h)),
                  pl.BlockSpec((T, LANE), lambda b, h, i: (b, 0)),
                  pl.BlockSpec((T, LANE), lambda b, h, i: (b, heads + h))],
        out_specs=pl.BlockSpec((tq, LANE), lambda b, h, i: (b * nq + i, h)),
        out_shape=jax.ShapeDtypeStruct((N, heads * MLA_V), BF16),
        compiler_params=_params("parallel", "parallel", "parallel"),
        name="mla_attn",
    )(qn, qr, kv, kr, kv)


def _router_kernel(x_ref, r_ref, meta_ref, cnt_ref, carry_ref, *, n_exp):
    @pl.when(pl.program_id(0) == 0)
    def _():
        carry_ref[...] = jnp.zeros_like(carry_ref)

    bm = x_ref.shape[0]
    logits = jnp.dot(x_ref[...], r_ref[...], preferred_element_type=F32, precision=lax.Precision.HIGHEST)
    lane = lax.broadcasted_iota(I32, (bm, LANE), 1).astype(F32)
    logits = jnp.where(lane < n_exp, logits, -jnp.inf)
    m1 = jnp.max(logits, axis=1, keepdims=True)
    i1 = jnp.min(jnp.where(logits == m1, lane, float(LANE)), axis=1, keepdims=True)
    rest = jnp.where(lane == i1, -jnp.inf, logits)
    m2 = jnp.max(rest, axis=1, keepdims=True)
    i2 = jnp.min(jnp.where(rest == m2, lane, float(LANE)), axis=1, keepdims=True)
    e = jnp.exp(m2 - m1)
    g1 = 1.0 / (1.0 + e)
    g2 = e / (1.0 + e)
    sel = jnp.where(lane == i1, 1.0, jnp.where(lane == i2, 1.0, 0.0))
    r_io = lax.broadcasted_iota(I32, (bm, bm), 0)
    c_io = lax.broadcasted_iota(I32, (bm, bm), 1)
    below = jnp.where(c_io < r_io, 1.0, 0.0).astype(BF16)
    carry = carry_ref[0:1, :]
    rank = jnp.dot(below, sel.astype(BF16), preferred_element_type=F32) + carry
    r1 = jnp.sum(jnp.where(lane == i1, rank, 0.0), axis=1, keepdims=True)
    r2 = jnp.sum(jnp.where(lane == i2, rank, 0.0), axis=1, keepdims=True)
    meta = jnp.where(lane == 0, i1, 0.0)
    meta = jnp.where(lane == 1, i2, meta)
    meta = jnp.where(lane == 2, g1, meta)
    meta = jnp.where(lane == 3, g2, meta)
    meta = jnp.where(lane == 4, r1, meta)
    meta = jnp.where(lane == 5, r2, meta)
    meta_ref[...] = meta
    total = carry + jnp.sum(sel, axis=0, keepdims=True)
    carry_ref[...] = jnp.broadcast_to(total, carry_ref.shape)
    cnt_ref[...] = jnp.broadcast_to(total, cnt_ref.shape)


def _router(x, router, *, bm=512):
    M, D = x.shape
    n_exp = router.shape[1]
    bm = _tile(M, bm)
    r_pad = jnp.zeros((D, LANE), F32).at[:, :n_exp].set(router)
    return pl.pallas_call(
        functools.partial(_router_kernel, n_exp=n_exp),
        grid=(M // bm,),
        in_specs=[pl.BlockSpec((bm, D), lambda i: (i, 0)), pl.BlockSpec((D, LANE), lambda i: (0, 0))],
        out_specs=[pl.BlockSpec((bm, LANE), lambda i: (i, 0)), pl.BlockSpec((8, LANE), lambda i: (0, 0))],
        out_shape=[jax.ShapeDtypeStruct((M, LANE), F32), jax.ShapeDtypeStruct((8, LANE), F32)],
        scratch_shapes=[pltpu.VMEM((8, LANE), F32)],
        compiler_params=_params("arbitrary"),
        name="router",
    )(x, r_pad)


def _experts_kernel(te_ref, nv_ref, tok_ref, x_hbm, w1_ref, w3_ref, w2_ref, o_ref, stage_ref, xb_ref, hid_ref, sem,
                    *, tm, tf, n_a):
    r = pl.program_id(0)
    s = pl.program_id(1)
    nv = nv_ref[0]

    def row_copy(j, tok):
        return pltpu.make_async_copy(x_hbm.at[pl.ds(tok, 1)], stage_ref.at[pl.ds(j, 1)], sem)

    def gather_start(tile):
        def body(j, carry):
            row_copy(j, tok_ref[tile * tm + j]).start()
            return carry
        lax.fori_loop(0, tm, body, 0)

    def gather_wait():
        def body(j, carry):
            row_copy(j, 0).wait()
            return carry
        lax.fori_loop(0, tm, body, 0)

    @pl.when(r < nv)
    def _():
        @pl.when(s == 0)
        def _():
            @pl.when(r == 0)
            def _():
                gather_start(0)

            gather_wait()
            xb_ref[...] = stage_ref[...].astype(BF16)

            @pl.when(r + 1 < nv)
            def _():
                gather_start(r + 1)

        @pl.when(s < n_a)
        def _():
            xb = xb_ref[...]
            a = jnp.dot(xb, w1_ref[0], preferred_element_type=F32)
            b = jnp.dot(xb, w3_ref[0], preferred_element_type=F32)
            hid = (a * jax.nn.sigmoid(a) * b).astype(BF16)
            for f in range(n_a):
                @pl.when(s == f)
                def _(f=f):
                    hid_ref[:, f * tf:(f + 1) * tf] = hid

        @pl.when(s >= n_a)
        def _():
            o_ref[...] = jnp.dot(hid_ref[...], w2_ref[0], preferred_element_type=F32)


def _experts(x, w1, w3, w2, tile_expert, n_valid, row_tok, *, tm, tf=512, tn=512):
    D = x.shape[1]
    P = row_tok.shape[0]
    E, _, F = w1.shape
    tf, tn = _tile(F, tf), _tile(D, tn)
    n_a, n_b = F // tf, D // tn
    n_tiles = P // tm

    def a_idx(r, s, nv):
        return jnp.where(r < nv[0], jnp.minimum(s, n_a - 1), n_a - 1)

    def b_idx(r, s, nv):
        return jnp.where(r < nv[0], jnp.maximum(s - n_a, 0), n_b - 1)

    return pl.pallas_call(
        functools.partial(_experts_kernel, tm=tm, tf=tf, n_a=n_a),
        grid_spec=pltpu.PrefetchScalarGridSpec(
            num_scalar_prefetch=3, grid=(n_tiles, n_a + n_b),
            in_specs=[pl.BlockSpec(memory_space=pl.ANY),
                      pl.BlockSpec((1, D, tf), lambda r, s, te, nv, tok: (te[r], 0, a_idx(r, s, nv))),
                      pl.BlockSpec((1, D, tf), lambda r, s, te, nv, tok: (te[r], 0, a_idx(r, s, nv))),
                      pl.BlockSpec((1, F, tn), lambda r, s, te, nv, tok: (te[r], 0, b_idx(r, s, nv)))],
            out_specs=pl.BlockSpec((tm, tn), lambda r, s, te, nv, tok: (jnp.minimum(r, nv[0] - 1), b_idx(r, s, nv))),
            scratch_shapes=[pltpu.VMEM((tm, D), F32), pltpu.VMEM((tm, D), BF16), pltpu.VMEM((tm, F), BF16),
                            pltpu.SemaphoreType.DMA(())]),
        out_shape=jax.ShapeDtypeStruct((P, D), F32),
        compiler_params=_params("arbitrary", "arbitrary"),
        name="moe_experts",
    )(tile_expert, n_valid, row_tok, x, w1, w3, w2)


def _combine_kernel(dest_ref, x_ref, meta_ref, g_ref, b_ref, ys_hbm, o_ref, buf_ref, sem):
    bm = x_ref.shape[0]
    t0 = pl.program_id(0) * bm

    def copy(j, k, d):
        return pltpu.make_async_copy(ys_hbm.at[pl.ds(d, 1)], buf_ref.at[k, pl.ds(j, 1)], sem.at[k])

    def issue(j, carry):
        for k in range(2):
            copy(j, k, dest_ref[2 * (t0 + j) + k]).start()
        return carry

    lax.fori_loop(0, bm, issue, 0)

    def wait(j, carry):
        for k in range(2):
            copy(j, k, 0).wait()
        return carry

    lax.fori_loop(0, bm, wait, 0)
    meta = meta_ref[...]
    y = meta[:, 2:3] * buf_ref[0] + meta[:, 3:4] * buf_ref[1]
    o_ref[...] = _layer_norm_rows(ALPHA * x_ref[...] + y, g_ref[...], b_ref[...])


def _combine(x, meta, ys, dest, g, b, *, bm=256):
    M, D = x.shape
    bm = _tile(M, bm, 8)
    row = lambda n: pl.BlockSpec((bm, n), lambda i, d: (i, 0))
    vec = pl.BlockSpec((1, D), lambda i, d: (0, 0))
    return pl.pallas_call(
        _combine_kernel,
        grid_spec=pltpu.PrefetchScalarGridSpec(
            num_scalar_prefetch=1, grid=(M // bm,),
            in_specs=[row(D), row(LANE), vec, vec, pl.BlockSpec(memory_space=pl.ANY)],
            out_specs=row(D),
            scratch_shapes=[pltpu.VMEM((2, bm, D), F32), pltpu.SemaphoreType.DMA((2,))]),
        out_shape=jax.ShapeDtypeStruct((M, D), F32),
        compiler_params=_params("arbitrary"),
        name="moe_combine",
    )(dest, x, meta, g.reshape(1, D), b.reshape(1, D), ys)


def _even_layer(x, xb, w_in, ret_gn_g, w_out, ln1_g, ln1_b, w1, w3, w2, ln2_g, ln2_b, *, B, T):
    N, D = x.shape
    a_heads = D // 2 // A_HEAD_DIM
    r_heads = D // 2 // RET_VAL_DIM
    qa_w, kv_w = a_heads * A_HEAD_DIM, A_KV_HEADS * A_HEAD_DIM
    qi_w = IDX_HEADS * IDX_DIM
    rk_w, rv_w = r_heads * RET_KEY_DIM, r_heads * RET_VAL_DIM
    sizes = (qa_w, kv_w, kv_w, qi_w, IDX_DIM, IDX_HEADS, rk_w, rk_w, rv_w, rv_w)
    offs = [0]
    for s in sizes:
        offs.append(offs[-1] + s)
    col = lambda a, b_: w_in[:, offs[a]:offs[b_]]
    w_qa, w_ka, w_va = col(0, 1), col(1, 2), col(2, 3)
    w_qi, w_ki, w_wi = col(3, 4), col(4, 5), col(5, 6)
    w_qb, w_kb, w_vb, w_gb = col(6, 7), col(7, 8), col(8, 9), col(9, 10)

    cos_a, sin_a = _rope_cos_sin(T, A_HEAD_DIM // 4, ROPE_THETA)
    tab_q = _lane_tables(cos_a, sin_a, A_HEAD_DIM, A_HEAD_DIM ** -0.5)
    tab_k = _lane_tables(cos_a, sin_a, A_HEAD_DIM)
    cos_i, sin_i = _rope_cos_sin(T, IDX_DIM // 4, ROPE_THETA)
    tab_i = _lane_tables(cos_i, sin_i, IDX_DIM)
    pass_c = jnp.ones((T, LANE - IDX_DIM), F32)
    pass_s = jnp.zeros((T, LANE - IDX_DIM), F32)
    tab_idx = tuple(jnp.concatenate([t, t[:, :IDX_DIM], p], 1)
                    for t, p in zip(tab_i, (pass_c, pass_s, pass_s)))
    inv = 1.0 / (RET_THETA ** jnp.linspace(0.0, 1.0, RET_KEY_DIM // 2, dtype=F32))
    ang = jnp.arange(T, dtype=F32)[:, None] * inv[None, :]
    cos_r, sin_r = jnp.cos(ang), jnp.sin(ang)
    sin_pair = jnp.concatenate([-sin_r, sin_r], 1)
    tab_r = (jnp.concatenate([cos_r, cos_r], 1), sin_pair, sin_pair)

    idx_pad = LANE - IDX_DIM - IDX_HEADS
    w_idx = jnp.concatenate([w_qi, w_ki, w_wi * (IDX_DIM ** -0.5 * IDX_HEADS ** -0.5),
                             jnp.zeros((D, idx_pad), F32)], 1).astype(BF16)
    w_rqk = jnp.concatenate([w_qb, w_kb * RET_KEY_DIM ** -0.5], 1).astype(BF16)
    w_pv = jnp.concatenate([w_va, w_vb, w_gb], 1).astype(BF16)
    qa = _proj(xb, w_qa.astype(BF16), out_dtype=BF16, tabs=tab_q, half=A_HEAD_DIM // 8,
               seq_len=T, name="proj_qa", **_pat(qa_w, 1024, 0))
    ka = _proj(xb, w_ka.astype(BF16), out_dtype=BF16, tabs=tab_k, half=A_HEAD_DIM // 8,
               seq_len=T, name="proj_ka", **_pat(kv_w, 1024, 0))
    n_idx = w_idx.shape[1]
    idx = _proj(xb, w_idx, out_dtype=F32, tabs=tab_idx, half=IDX_DIM // 8, seq_len=T, name="proj_idx",
                bm=512, bn=n_idx, slab_pat=(0,) * (qi_w // LANE) + (1,))
    rqk = _proj(xb, w_rqk, out_dtype=BF16, tabs=tab_r, mode="pair", seq_len=T, name="proj_ret_qk",
                bn=1024, slab_pat=(0, 1) * (_tile(2 * rk_w, 1024) // (2 * LANE)))
    pv = _proj(xb, w_pv, out_dtype=BF16, bn=768, seq_len=T, name="proj_v")

    ya = _dsa(qa.reshape(B, T, qa_w), ka.reshape(B, T, kv_w), pv.reshape(B, T, -1), idx.reshape(B, T, n_idx),
              B=B, T=T)
    yb = _retention(rqk, pv, ret_gn_g, B=B, T=T, heads=r_heads, v_blk0=kv_w // RET_VAL_DIM)
    mix = jnp.concatenate([ya.reshape(N, qa_w), yb], axis=1)
    y = _proj(mix, w_out.astype(BF16), out_dtype=F32, name="proj_out0")
    x1, x1b = _add_ln(x, y, ln1_g, ln1_b)
    hid = _swiglu_up(x1b, w1.astype(BF16), w3.astype(BF16))
    y = _mm_ksplit(hid, w2.astype(BF16), name="ffn_down")
    return _add_ln(x1, y, ln2_g, ln2_b)


def _pat(width, bn, p):
    bn = _tile(width, bn)
    return dict(bn=bn, slab_pat=(p,) * (bn // LANE))


def _odd_layer(x, xb, w_dq_dkv, q_norm_g, w_uq, kv_norm_g, w_ukv, w_out, ln1_g, ln1_b,
               router, we1, we3, we2, ln2_g, ln2_b, *, B, T):
    N, D = x.shape
    heads = D // MLA_V
    q_rank, kv_rank = q_norm_g.shape[0], kv_norm_g.shape[0]
    scale = (MLA_NOPE + MLA_ROPE) ** -0.5
    cos_c, sin_c = _rope_cos_sin(T, MLA_ROPE, ROPE_THETA)
    tab_kr = _lane_tables(cos_c, sin_c, MLA_ROPE)
    tab_qr = _lane_tables(cos_c, sin_c, MLA_ROPE, scale)

    w_kr = w_dq_dkv[:, q_rank + kv_rank:]
    w_down = jnp.concatenate([w_dq_dkv[:, :q_rank + kv_rank], w_kr, w_kr], 1).astype(BF16)
    cq, ckv, kr = _mla_down(xb, w_down, q_norm_g, kv_norm_g, tab_kr, T=T, q_rank=q_rank, kv_rank=kv_rank)
    w_uq3 = w_uq.reshape(q_rank, heads, MLA_NOPE + MLA_ROPE)
    w_qn = w_uq3[:, :, :MLA_NOPE].reshape(q_rank, heads * MLA_NOPE).astype(BF16)
    w_qr = w_uq3[:, :, MLA_NOPE:].reshape(q_rank, heads * MLA_ROPE).astype(BF16)
    w_kv3 = w_ukv.reshape(kv_rank, heads, MLA_NOPE + MLA_V)
    w_kv = jnp.concatenate([w_kv3[:, :, :MLA_NOPE].reshape(kv_rank, heads * MLA_NOPE),
                            w_kv3[:, :, MLA_NOPE:].reshape(kv_rank, heads * MLA_V)], 1).astype(BF16)
    qn = _proj(cq, w_qn, out_dtype=BF16, scale=scale, seq_len=T, name="proj_q_nope")
    qr = _proj(cq, w_qr, out_dtype=BF16, tabs=tab_qr, half=MLA_ROPE // 2, seq_len=T, name="proj_q_rope",
               **_pat(heads * MLA_ROPE, 1024, 0))
    kv = _proj(ckv, w_kv, out_dtype=BF16, seq_len=T, name="proj_kv")
    att = _mla_attn(qn, qr, kv, kr, B=B, T=T, heads=heads)
    y = _proj(att, w_out.astype(BF16), out_dtype=F32, name="proj_out1")
    x1, _ = _add_ln(x, y, ln1_g, ln1_b)

    E = router.shape[1]
    F = we1.shape[2]
    tm = _tile(N, 512)
    meta, cnt = _router(x1, router)
    counts = cnt[0, :E].astype(I32)
    padded = (counts + tm - 1) // tm * tm
    ends = jnp.cumsum(padded)
    starts = ends - padded
    i1, i2 = meta[:, 0].astype(I32), meta[:, 1].astype(I32)
    dest = jnp.stack([starts[i1] + meta[:, 4].astype(I32), starts[i2] + meta[:, 5].astype(I32)], 1).reshape(-1)
    n_rows = 2 * N + E * tm
    n_tiles = n_rows // tm
    n_valid = (ends[-1] // tm).astype(I32).reshape(1)
    tile_start = jnp.arange(n_tiles, dtype=I32) * tm
    tile_expert = jnp.minimum(jnp.sum(tile_start[:, None] >= ends[None, :], axis=1), E - 1).astype(I32)
    tile_expert = jnp.where(jnp.arange(n_tiles) < n_valid[0], tile_expert, tile_expert[jnp.maximum(n_valid[0] - 1, 0)])
    row_tok = jnp.zeros((n_rows,), I32).at[dest].set(jnp.repeat(jnp.arange(N, dtype=I32), 2))
    ys = _experts(x1, we1.astype(BF16), we3.astype(BF16), we2.astype(BF16), tile_expert, n_valid, row_tok, tm=tm)
    return _combine(x1, meta, ys, dest, ln2_g, ln2_b)


def kernel(x, l0_w_in, l0_ret_gn_g, l0_w_out, l0_ln1_g, l0_ln1_b, l0_ffn_w1, l0_ffn_w3, l0_ffn_w2, l0_ln2_g, l0_ln2_b, l1_w_dq_dkv, l1_q_norm_g, l1_w_uq, l1_kv_norm_g, l1_w_ukv, l1_w_out, l1_ln1_g, l1_ln1_b, l1_router, l1_moe_w1, l1_moe_w3, l1_moe_w2, l1_ln2_g, l1_ln2_b):
    B, T, D = x.shape
    x2 = x.reshape(B * T, D)
    h, hb = _even_layer(x2, x2.astype(BF16), l0_w_in, l0_ret_gn_g, l0_w_out, l0_ln1_g, l0_ln1_b,
                        l0_ffn_w1, l0_ffn_w3, l0_ffn_w2, l0_ln2_g, l0_ln2_b, B=B, T=T)
    out = _odd_layer(h, hb, l1_w_dq_dkv, l1_q_norm_g, l1_w_uq, l1_kv_norm_g, l1_w_ukv, l1_w_out,
                     l1_ln1_g, l1_ln1_b, l1_router, l1_moe_w1, l1_moe_w3, l1_moe_w2, l1_ln2_g, l1_ln2_b, B=B, T=T)
    return out.reshape(B, T, D)
```

```python
import functools

import jax
import jax.numpy as jnp
from jax import lax
from jax.experimental import pallas as pl
from jax.experimental.pallas import tpu as pltpu

F32 = jnp.float32
BF16 = jnp.bfloat16
I32 = jnp.int32

A_HEAD_DIM = 128
A_KV_HEADS = 4
IDX_HEADS = 16
IDX_DIM = 64
DSA_TOPK_MAX = 256
RET_KEY_DIM = 256
RET_VAL_DIM = 256
RET_CHUNK = 128
RET_THETA = 10000.0
MLA_V = 128
MLA_NOPE = 128
MLA_ROPE = 64
ROPE_THETA = 500000.0
Q_BLOCK = 128
LN_EPS = 1e-5
RMS_EPS = 1e-6
DEPTH = 2
ALPHA = (2.0 * DEPTH) ** 0.25

LANE = 128
V7X_VMEM_BYTES = 64 * 1024 * 1024
VMEM_LIMIT = V7X_VMEM_BYTES - 8 * 1024 * 1024
MASKED = -1e30
INT_MIN = -(2 ** 31)

NT_DIMS = (((1,), (1,)), ((), ()))
TN_DIMS = (((0,), (0,)), ((), ()))


def _tile(n, pref, mult=LANE):
    if n <= pref:
        return n
    t = (pref // mult) * mult
    while t > mult and n % t:
        t -= mult
    assert n % t == 0, (n, pref, mult)
    return t


def _params(*sem):
    return pltpu.CompilerParams(dimension_semantics=sem, vmem_limit_bytes=VMEM_LIMIT)


def _rope_cos_sin(T, rot_dim, theta):
    inv = theta ** (-jnp.arange(0, rot_dim, 2, dtype=F32) / rot_dim)
    ang = jnp.arange(T, dtype=F32)[:, None] * inv[None, :]
    return jnp.cos(ang), jnp.sin(ang)


def _lane_tables(cos, sin, head_dim, scale=1.0):
    T, half = cos.shape
    rest = head_dim - 2 * half
    zh = jnp.zeros((T, half), F32)
    c = jnp.concatenate([cos, cos, jnp.ones((T, rest), F32)], 1)
    sa = jnp.concatenate([-sin, zh, jnp.zeros((T, rest), F32)], 1)
    sb = jnp.concatenate([zh, sin, jnp.zeros((T, rest), F32)], 1)
    reps = LANE // head_dim
    return tuple(jnp.tile(t * scale, (1, reps)) for t in (c, sa, sb))


def _proj_kernel(*refs, slab_pat, mode, half, scale, with_tab):
    if with_tab:
        x_ref, w_ref, c_ref, sa_ref, sb_ref, o_ref = refs
    else:
        x_ref, w_ref, o_ref = refs
    acc = jnp.dot(x_ref[...], w_ref[...], preferred_element_type=F32)
    for s, p in enumerate(slab_pat):
        a = acc[:, s * LANE:(s + 1) * LANE]
        if p < 0:
            out = a if scale == 1.0 else a * scale
        else:
            c = c_ref[:, p * LANE:(p + 1) * LANE]
            sa = sa_ref[:, p * LANE:(p + 1) * LANE]
            if mode == "lane":
                sb = sb_ref[:, p * LANE:(p + 1) * LANE]
                out = a * c + pltpu.roll(a, LANE - half, 1) * sa + pltpu.roll(a, half, 1) * sb
            else:
                q = s ^ 1
                out = a * c + acc[:, q * LANE:(q + 1) * LANE] * sa
        o_ref[:, s * LANE:(s + 1) * LANE] = out.astype(o_ref.dtype)


def _proj(x, w, *, out_dtype, bm=1024, bn=1024, tabs=None, slab_pat=None, mode="lane", half=0,
          scale=1.0, seq_len=None, name="proj"):
    M, K = x.shape
    N = w.shape[1]
    bm = _tile(M, bm) if seq_len is None else _tile(seq_len, bm)
    bn = _tile(N, bn)
    if slab_pat is None:
        slab_pat = (-1,) * (bn // LANE)
    assert len(slab_pat) == bn // LANE
    in_specs = [pl.BlockSpec((bm, K), lambda i, j: (i, 0)),
                pl.BlockSpec((K, bn), lambda i, j: (0, j))]
    args = [x, w]
    if tabs is not None:
        tb = seq_len // bm
        tw = tabs[0].shape[1]
        in_specs += [pl.BlockSpec((bm, tw), lambda i, j: (i % tb, 0))] * 3
        args += list(tabs)
    kern = functools.partial(_proj_kernel, slab_pat=tuple(slab_pat), mode=mode, half=half,
                             scale=scale, with_tab=tabs is not None)
    return pl.pallas_call(
        kern,
        grid=(M // bm, N // bn),
        in_specs=in_specs,
        out_specs=pl.BlockSpec((bm, bn), lambda i, j: (i, j)),
        out_shape=jax.ShapeDtypeStruct((M, N), out_dtype),
        compiler_params=_params("parallel", "parallel"),
        name=name,
    )(*args)


def _mm_ksplit_kernel(x_ref, w_ref, o_ref):
    part = jnp.dot(x_ref[...], w_ref[...], preferred_element_type=F32)

    @pl.when(pl.program_id(2) == 0)
    def _():
        o_ref[...] = part

    @pl.when(pl.program_id(2) > 0)
    def _():
        o_ref[...] += part


def _mm_ksplit(x, w, *, bm=1024, bn=1024, bk=2048, name="mm_ksplit"):
    M, K = x.shape
    N = w.shape[1]
    bm, bn, bk = _tile(M, bm), _tile(N, bn), _tile(K, bk)
    return pl.pallas_call(
        _mm_ksplit_kernel,
        grid=(M // bm, N // bn, K // bk),
        in_specs=[pl.BlockSpec((bm, bk), lambda i, j, k: (i, k)),
                  pl.BlockSpec((bk, bn), lambda i, j, k: (k, j))],
        out_specs=pl.BlockSpec((bm, bn), lambda i, j, k: (i, j)),
        out_shape=jax.ShapeDtypeStruct((M, N), F32),
        compiler_params=_params("parallel", "parallel", "arbitrary"),
        name=name,
    )(x, w)


def _swiglu_up_kernel(x_ref, w1_ref, w3_ref, o_ref):
    x = x_ref[...]
    a = jnp.dot(x, w1_ref[...], preferred_element_type=F32)
    b = jnp.dot(x, w3_ref[...], preferred_element_type=F32)
    o_ref[...] = (a * jax.nn.sigmoid(a) * b).astype(o_ref.dtype)


def _swiglu_up(x, w1, w3, *, bm=1024, bn=512):
    M, K = x.shape
    N = w1.shape[1]
    bm, bn = _tile(M, bm), _tile(N, bn)
    return pl.pallas_call(
        _swiglu_up_kernel,
        grid=(M // bm, N // bn),
        in_specs=[pl.BlockSpec((bm, K), lambda i, j: (i, 0)),
                  pl.BlockSpec((K, bn), lambda i, j: (0, j)),
                  pl.BlockSpec((K, bn), lambda i, j: (0, j))],
        out_specs=pl.BlockSpec((bm, bn), lambda i, j: (i, j)),
        out_shape=jax.ShapeDtypeStruct((M, N), BF16),
        compiler_params=_params("parallel", "parallel"),
        name="swiglu_up",
    )(x, w1, w3)


def _layer_norm_rows(z, g, b):
    mu = jnp.mean(z, axis=-1, keepdims=True)
    zc = z - mu
    var = jnp.mean(zc * zc, axis=-1, keepdims=True)
    return zc * lax.rsqrt(var + LN_EPS) * g + b


def _add_ln_kernel(x_ref, y_ref, g_ref, b_ref, of_ref, ob_ref):
    out = _layer_norm_rows(ALPHA * x_ref[...] + y_ref[...], g_ref[...], b_ref[...])
    of_ref[...] = out
    ob_ref[...] = out.astype(ob_ref.dtype)


def _add_ln(x, y, g, b, *, bm=256):
    M, D = x.shape
    bm = _tile(M, bm, 8)
    row = pl.BlockSpec((bm, D), lambda i: (i, 0))
    vec = pl.BlockSpec((1, D), lambda i: (0, 0))
    return pl.pallas_call(
        _add_ln_kernel,
        grid=(M // bm,),
        in_specs=[row, row, vec, vec],
        out_specs=[row, row],
        out_shape=[jax.ShapeDtypeStruct((M, D), F32), jax.ShapeDtypeStruct((M, D), BF16)],
        compiler_params=_params("parallel"),
        name="add_ln",
    )(x, y, g.reshape(1, D), b.reshape(1, D))


LOG2E = 1.4426950408889634
SOFTMAX_ROW_BLOCK = 512


def _online_softmax(s, m, l):
    rows = s.shape[0]
    rb = min(SOFTMAX_ROW_BLOCK, rows)
    ms, ls, scales, ps = [], [], [], []
    for r in range(rows // rb):
        sl = slice(r * rb, (r + 1) * rb)
        s_r = s[sl]
        m_r = jnp.maximum(m[sl], jnp.max(s_r, axis=1, keepdims=True))
        a_r = jnp.exp2(m[sl] - m_r)
        p_r = jnp.exp2(s_r - m_r)
        ms.append(m_r)
        ls.append(a_r * l[sl] + jnp.sum(p_r, axis=1, keepdims=True))
        scales.append(a_r)
        ps.append(p_r.astype(BF16))
    cat = lambda parts: jnp.concatenate(parts, axis=0)
    return cat(ms), cat(ls), cat(scales), cat(ps)


def _dsa_kernel(q_ref, k_ref, v_ref, iq_ref, ik_ref, o_ref, keys_ref, bias_ref, *, topk, ck, n_rep, idx_bits):
    blk = pl.program_id(1)
    n_chunks = (blk * Q_BLOCK + Q_BLOCK + ck - 1) // ck
    row = lax.broadcasted_iota(I32, (Q_BLOCK, ck), 0) + blk * Q_BLOCK
    lane = lax.broadcasted_iota(I32, (Q_BLOCK, ck), 1)
    w_off = IDX_HEADS * IDX_DIM + IDX_DIM
    iq = iq_ref[0]
    wi = iq[:, w_off:w_off + IDX_HEADS]
    q_idx = jnp.concatenate([iq[:, h * IDX_DIM:(h + 1) * IDX_DIM].astype(BF16) for h in range(IDX_HEADS)], axis=0)

    def score_body(c, carry):
        off = pl.multiple_of(c * ck, ck)
        kc = ik_ref[0, pl.ds(off, ck), :][:, :IDX_DIM].astype(BF16)
        lg = lax.dot_general(q_idx, kc, NT_DIMS, preferred_element_type=F32)
        s = jnp.zeros((Q_BLOCK, ck), F32)
        for h in range(IDX_HEADS):
            s = s + wi[:, h:h + 1] * jnp.maximum(lg[h * Q_BLOCK:(h + 1) * Q_BLOCK], 0.0)
        bits = pltpu.bitcast(s, I32)
        key = bits ^ ((bits >> 31) & 0x7FFFFFFF)
        keys_ref[c] = jnp.where(lane + off <= row, key, INT_MIN)
        return carry

    lax.fori_loop(0, n_chunks, score_body, 0)

    def count(indicator):
        def body(c, acc):
            part = indicator(keys_ref[c], lane + c * ck)
            for j in range(ck // LANE):
                acc = acc + part[:, j * LANE:(j + 1) * LANE]
            return acc
        acc = lax.fori_loop(0, n_chunks, body, jnp.zeros((Q_BLOCK, LANE), F32))
        return jnp.sum(acc, axis=1, keepdims=True)

    kf = float(topk)
    ok = count(lambda kc, idx: jnp.where(kc >= 0, 1.0, 0.0)) >= kf
    thr = jnp.where(ok, 0, INT_MIN).astype(I32)

    def bit_body(i, thr):
        cand = thr + lax.shift_left(jnp.int32(1), 30 - i)
        ok = count(lambda kc, idx: jnp.where(kc >= cand, 1.0, 0.0)) >= kf
        return jnp.where(ok, cand, thr)

    thr = lax.fori_loop(0, 31, bit_body, thr)

    n_gt = count(lambda kc, idx: jnp.where(kc > thr, 1.0, 0.0))
    n_ge = count(lambda kc, idx: jnp.where(kc >= thr, 1.0, 0.0))
    need = kf - n_gt
    has_thr = thr > INT_MIN
    surplus = jnp.where(has_thr, n_ge - n_gt - need, 0.0)

    def tie_search():
        def tie_body(i, last):
            cand = last + lax.shift_left(jnp.int32(1), idx_bits - 1 - i)
            ok = count(lambda kc, idx: jnp.where(kc == thr, jnp.where(idx < cand, 1.0, 0.0), 0.0)) < need
            return jnp.where(ok, cand, last)
        return lax.fori_loop(0, idx_bits, tie_body, jnp.zeros((Q_BLOCK, 1), I32))

    last = lax.cond(jnp.max(surplus) > 0.0, tie_search, lambda: jnp.full((Q_BLOCK, 1), 2 ** idx_bits, I32))
    last = jnp.where(has_thr, last, -1)

    def bias_body(c, carry):
        kc = keys_ref[c]
        tie_bias = jnp.where(lane + c * ck <= last, 0.0, MASKED)
        bias_ref[c] = jnp.where(kc == thr, tie_bias, jnp.where(kc > thr, 0.0, MASKED))
        return carry

    lax.fori_loop(0, n_chunks, bias_body, 0)

    q = q_ref[0]
    rows = n_rep * Q_BLOCK
    qgs = [jnp.concatenate([q[:, (g * n_rep + r) * A_HEAD_DIM:(g * n_rep + r + 1) * A_HEAD_DIM]
                            for r in range(n_rep)], axis=0) for g in range(A_KV_HEADS)]

    def att_body(c, carry):
        off = pl.multiple_of(c * ck, ck)
        b = bias_ref[c]
        bias = jnp.concatenate([b] * n_rep, axis=0)
        out = []
        for g in range(A_KV_HEADS):
            m, l, acc = carry[g]
            kc = k_ref[0, pl.ds(off, ck), g * A_HEAD_DIM:(g + 1) * A_HEAD_DIM]
            vc = v_ref[0, pl.ds(off, ck), g * A_HEAD_DIM:(g + 1) * A_HEAD_DIM]
            s = lax.dot_general(qgs[g], kc, NT_DIMS, preferred_element_type=F32) + bias
            m, l, a, p = _online_softmax(s, m, l)
            out.append((m, l, a * acc + jnp.dot(p, vc, preferred_element_type=F32)))
        return tuple(out)

    one = (jnp.full((rows, 1), MASKED, F32), jnp.zeros((rows, 1), F32), jnp.zeros((rows, A_HEAD_DIM), F32))
    carry = lax.fori_loop(0, n_chunks, att_body, (one,) * A_KV_HEADS)
    for g in range(A_KV_HEADS):
        _, l, acc = carry[g]
        o = acc / l
        for r in range(n_rep):
            col = (g * n_rep + r) * A_HEAD_DIM
            o_ref[0, :, col:col + A_HEAD_DIM] = o[r * Q_BLOCK:(r + 1) * Q_BLOCK].astype(o_ref.dtype)


def _dsa(q, k, v, idx, *, B, T):
    a_heads = q.shape[-1] // A_HEAD_DIM
    n_rep = a_heads // A_KV_HEADS
    topk = min(DSA_TOPK_MAX, T // 4)
    ck = _tile(T, 512)
    idx_w = idx.shape[-1]
    kv_w = k.shape[-1]
    kern = functools.partial(_dsa_kernel, topk=topk, ck=ck, n_rep=n_rep, idx_bits=max(1, (T - 1).bit_length()))
    return pl.pallas_call(
        kern,
        grid=(B, T // Q_BLOCK),
        in_specs=[pl.BlockSpec((1, Q_BLOCK, q.shape[-1]), lambda b, i: (b, i, 0)),
                  pl.BlockSpec((1, T, kv_w), lambda b, i: (b, 0, 0)),
                  pl.BlockSpec((1, T, kv_w), lambda b, i: (b, 0, 0)),
                  pl.BlockSpec((1, Q_BLOCK, idx_w), lambda b, i: (b, i, 0)),
                  pl.BlockSpec((1, T, LANE), lambda b, i: (b, 0, IDX_HEADS * IDX_DIM // LANE))],
        out_specs=pl.BlockSpec((1, Q_BLOCK, q.shape[-1]), lambda b, i: (b, i, 0)),
        out_shape=jax.ShapeDtypeStruct(q.shape, BF16),
        scratch_shapes=[pltpu.VMEM((T // ck, Q_BLOCK, ck), I32), pltpu.VMEM((T // ck, Q_BLOCK, ck), F32)],
        compiler_params=_params("parallel", "parallel"),
        name="dsa",
    )(q, k, v, idx, idx)


def _retention_kernel(q_ref, k_ref, v_ref, g_ref, gn_ref, din_ref, qd_ref, kd_ref, cd_ref, o_ref, state_ref, *, n_sub):
    @pl.when(pl.program_id(2) == 0)
    def _():
        state_ref[...] = jnp.zeros_like(state_ref)

    din = din_ref[0]
    qd = qd_ref[0]
    kd = kd_ref[0]
    cd = cd_ref[0]
    gn = gn_ref[...]
    for s in range(n_sub):
        sl = pl.ds(s * RET_CHUNK, RET_CHUNK)
        qc = q_ref[sl, :]
        kc = k_ref[sl, :]
        vc = v_ref[sl, :]
        st = state_ref[...]
        inner = lax.dot_general(qc, kc, NT_DIMS, preferred_element_type=F32) * din
        o = (jnp.dot(inner.astype(BF16), vc, preferred_element_type=F32)
             + jnp.dot(qc, st.astype(BF16), preferred_element_type=F32) * qd)
        vk = (vc.astype(F32) * kd).astype(BF16)
        state_ref[...] = st * cd + lax.dot_general(kc, vk, TN_DIMS, preferred_element_type=F32)
        mu = jnp.mean(o, axis=-1, keepdims=True)
        oc = o - mu
        var = jnp.mean(oc * oc, axis=-1, keepdims=True)
        gate = g_ref[sl, :].astype(F32)
        o_ref[sl, :] = (gate * jax.nn.sigmoid(gate) * (oc * lax.rsqrt(var + LN_EPS) * gn)).astype(o_ref.dtype)


def _retention(qk, pv, gn_g, *, B, T, heads, v_blk0):
    N = qk.shape[0]
    C = RET_CHUNK
    rb = _tile(T, 512)
    n_sub = rb // C
    nr = T // rb
    log_gamma = jnp.log(1.0 - 2.0 ** (-5.0 - jnp.arange(heads, dtype=F32)))
    pos = jnp.arange(C, dtype=F32)
    diff = pos[:, None] - pos[None, :]
    din = jnp.exp(jnp.where(diff[None] >= 0, log_gamma[:, None, None] * diff[None], -jnp.inf))
    qd = jnp.exp(log_gamma[:, None] * (pos[None] + 1.0))[:, :, None]
    kd = jnp.exp(log_gamma[:, None] * (C - 1.0 - pos[None]))[:, :, None]
    cd = jnp.exp(log_gamma * C)[:, None, None]
    W = RET_VAL_DIM
    blk = lambda off: pl.BlockSpec((rb, W), lambda b, h, r: (b * nr + r, off + h))
    per_head = lambda shape: pl.BlockSpec((1,) + shape, lambda b, h, r: (h, 0, 0))
    return pl.pallas_call(
        functools.partial(_retention_kernel, n_sub=n_sub),
        grid=(B, heads, nr),
        in_specs=[blk(0), blk(heads), blk(v_blk0), blk(v_blk0 + heads),
                  pl.BlockSpec((1, W), lambda b, h, r: (0, h)),
                  per_head((C, C)), per_head((C, 1)), per_head((C, 1)), per_head((1, 1))],
        out_specs=pl.BlockSpec((rb, W), lambda b, h, r: (b * nr + r, h)),
        out_shape=jax.ShapeDtypeStruct((N, heads * W), BF16),
        scratch_shapes=[pltpu.VMEM((RET_KEY_DIM, RET_VAL_DIM), F32)],
        compiler_params=_params("parallel", "parallel", "arbitrary"),
        name="retention",
    )(qk, qk, pv, pv, gn_g.reshape(1, heads * W), din, qd, kd, cd)


def _mla_down_kernel(x_ref, w_ref, qg_ref, kvg_ref, c_ref, sa_ref, sb_ref, cq_ref, ckv_ref, kr_ref, *, q_rank, kv_rank):
    acc = jnp.dot(x_ref[...], w_ref[...], preferred_element_type=F32)

    def rms(a, g):
        return a * lax.rsqrt(jnp.mean(a * a, axis=-1, keepdims=True) + RMS_EPS) * g

    cq_ref[...] = rms(acc[:, :q_rank], qg_ref[...]).astype(cq_ref.dtype)
    ckv_ref[...] = rms(acc[:, q_rank:q_rank + kv_rank], kvg_ref[...]).astype(ckv_ref.dtype)
    kr = acc[:, q_rank + kv_rank:]
    half = MLA_ROPE // 2
    kr = kr * c_ref[...] + pltpu.roll(kr, LANE - half, 1) * sa_ref[...] + pltpu.roll(kr, half, 1) * sb_ref[...]
    kr_ref[...] = kr.astype(kr_ref.dtype)


def _mla_down(x, w, q_g, kv_g, tabs, *, T, q_rank, kv_rank, bm=512):
    M, K = x.shape
    Nw = w.shape[1]
    bm = _tile(T, bm)
    tb = T // bm
    row = lambda n: pl.BlockSpec((bm, n), lambda i: (i, 0))
    tab = pl.BlockSpec((bm, LANE), lambda i: (i % tb, 0))
    return pl.pallas_call(
        functools.partial(_mla_down_kernel, q_rank=q_rank, kv_rank=kv_rank),
        grid=(M // bm,),
        in_specs=[row(K), pl.BlockSpec((K, Nw), lambda i: (0, 0)),
                  pl.BlockSpec((1, q_rank), lambda i: (0, 0)), pl.BlockSpec((1, kv_rank), lambda i: (0, 0)),
                  tab, tab, tab],
        out_specs=[row(q_rank), row(kv_rank), row(LANE)],
        out_shape=[jax.ShapeDtypeStruct((M, q_rank), BF16), jax.ShapeDtypeStruct((M, kv_rank), BF16),
                   jax.ShapeDtypeStruct((M, LANE), BF16)],
        compiler_params=_params("parallel"),
        name="mla_down",
    )(x, w, q_g.reshape(1, q_rank), kv_g.reshape(1, kv_rank), *tabs)


MLA_HEADS_PER_STEP = 2


def _mla_attn_kernel(qn_ref, qr_ref, kn_ref, kr_ref, v_ref, o_ref, *, tq):
    i = pl.program_id(2)
    lane = lax.broadcasted_iota(I32, (tq, LANE), 1)
    qr_all = qr_ref[...].astype(F32)
    qs = []
    for j in range(MLA_HEADS_PER_STEP):
        own = jnp.where((lane >= j * MLA_ROPE) & (lane < (j + 1) * MLA_ROPE), 1.0, 0.0)
        qr = (qr_all * own).astype(BF16)
        qs.append(jnp.concatenate([qn_ref[:, j * LANE:(j + 1) * LANE], qr], axis=1))

    def step(c, carry, masked):
        off = pl.multiple_of(c * tq, tq)
        kr = kr_ref[pl.ds(off, tq), :]
        out = []
        for j in range(MLA_HEADS_PER_STEP):
            m, l, acc = carry[j]
            k = jnp.concatenate([kn_ref[pl.ds(off, tq), j * LANE:(j + 1) * LANE], kr], axis=1)
            s = lax.dot_general(qs[j], k, NT_DIMS, preferred_element_type=F32)
            if masked:
                r_io = lax.broadcasted_iota(I32, (tq, tq), 0)
                c_io = lax.broadcasted_iota(I32, (tq, tq), 1)
                s = jnp.where(c_io <= r_io, s, MASKED)
            m, l, a, p = _online_softmax(s, m, l)
            v = v_ref[pl.ds(off, tq), j * MLA_V:(j + 1) * MLA_V]
            out.append((m, l, a * acc + jnp.dot(p, v, preferred_element_type=F32)))
        return tuple(out)

    one = (jnp.full((tq, 1), MASKED, F32), jnp.zeros((tq, 1), F32), jnp.zeros((tq, MLA_V), F32))
    carry = lax.fori_loop(0, i, functools.partial(step, masked=False), (one,) * MLA_HEADS_PER_STEP)
    carry = step(i, carry, True)
    for j in range(MLA_HEADS_PER_STEP):
        _, l, acc = carry[j]
        o_ref[:, j * MLA_V:(j + 1) * MLA_V] = (acc / l).astype(o_ref.dtype)


def _mla_attn(qn, qr, kv, kr, *, B, T, heads):
    N = qn.shape[0]
    tq = _tile(T, 512)
    nq = T // tq
    hp = MLA_HEADS_PER_STEP
    w = hp * LANE
    return pl.pallas_call(
        functools.partial(_mla_attn_kernel, tq=tq),
        grid=(B, heads // hp, nq),
        in_specs=[pl.BlockSpec((tq, w), lambda b, h, i: (b * nq + i, h)),
                  pl.BlockSpec((tq, LANE), lambda b, h, i: (b * nq + i, h)),
                  pl.BlockSpec((T, w), lambda b, h, i: (b, h)),
                  pl.BlockSpec((T, LANE), lambda b, h, i: (b, 0)),
                  pl.BlockSpec((T, w), lambda b, h, i: (b, heads // hp + h))],
        out_specs=pl.BlockSpec((tq, w), lambda b, h, i: (b * nq + i, h)),
        out_shape=jax.ShapeDtypeStruct((N, heads * MLA_V), BF16),
        compiler_params=_params("parallel", "parallel", "parallel"),
        name="mla_attn",
    )(qn, qr, kv, kr, kv)


def _router_kernel(x_ref, r_ref, meta_ref, cnt_ref, carry_ref, *, n_exp):
    @pl.when(pl.program_id(0) == 0)
    def _():
        carry_ref[...] = jnp.zeros_like(carry_ref)

    bm = x_ref.shape[0]
    logits = jnp.dot(x_ref[...], r_ref[...], preferred_element_type=F32, precision=lax.Precision.HIGHEST)
    lane = lax.broadcasted_iota(I32, (bm, LANE), 1).astype(F32)
    logits = jnp.where(lane < n_exp, logits, -jnp.inf)
    m1 = jnp.max(logits, axis=1, keepdims=True)
    i1 = jnp.min(jnp.where(logits == m1, lane, float(LANE)), axis=1, keepdims=True)
    rest = jnp.where(lane == i1, -jnp.inf, logits)
    m2 = jnp.max(rest, axis=1, keepdims=True)
    i2 = jnp.min(jnp.where(rest == m2, lane, float(LANE)), axis=1, keepdims=True)
    e = jnp.exp(m2 - m1)
    g1 = 1.0 / (1.0 + e)
    g2 = e / (1.0 + e)
    sel = jnp.where(lane == i1, 1.0, jnp.where(lane == i2, 1.0, 0.0))
    r_io = lax.broadcasted_iota(I32, (bm, bm), 0)
    c_io = lax.broadcasted_iota(I32, (bm, bm), 1)
    below = jnp.where(c_io < r_io, 1.0, 0.0).astype(BF16)
    carry = carry_ref[0:1, :]
    rank = jnp.dot(below, sel.astype(BF16), preferred_element_type=F32) + carry
    r1 = jnp.sum(jnp.where(lane == i1, rank, 0.0), axis=1, keepdims=True)
    r2 = jnp.sum(jnp.where(lane == i2, rank, 0.0), axis=1, keepdims=True)
    meta = jnp.where(lane == 0, i1, 0.0)
    meta = jnp.where(lane == 1, i2, meta)
    meta = jnp.where(lane == 2, g1, meta)
    meta = jnp.where(lane == 3, g2, meta)
    meta = jnp.where(lane == 4, r1, meta)
    meta = jnp.where(lane == 5, r2, meta)
    meta_ref[...] = meta
    total = carry + jnp.sum(sel, axis=0, keepdims=True)
    carry_ref[...] = jnp.broadcast_to(total, carry_ref.shape)
    cnt_ref[...] = jnp.broadcast_to(total, cnt_ref.shape)


def _router(x, router, *, bm=512):
    M, D = x.shape
    n_exp = router.shape[1]
    bm = _tile(M, bm)
    r_pad = jnp.zeros((D, LANE), F32).at[:, :n_exp].set(router)
    return pl.pallas_call(
        functools.partial(_router_kernel, n_exp=n_exp),
        grid=(M // bm,),
        in_specs=[pl.BlockSpec((bm, D), lambda i: (i, 0)), pl.BlockSpec((D, LANE), lambda i: (0, 0))],
        out_specs=[pl.BlockSpec((bm, LANE), lambda i: (i, 0)), pl.BlockSpec((8, LANE), lambda i: (0, 0))],
        out_shape=[jax.ShapeDtypeStruct((M, LANE), F32), jax.ShapeDtypeStruct((8, LANE), F32)],
        scratch_shapes=[pltpu.VMEM((8, LANE), F32)],
        compiler_params=_params("arbitrary"),
        name="router",
    )(x, r_pad)


def _experts_kernel(te_ref, nv_ref, tok_ref, x_hbm, w1_ref, w3_ref, w2_ref, o_ref, stage_ref, xb_ref, hid_ref, sem,
                    *, tm, tf, n_a):
    r = pl.program_id(0)
    s = pl.program_id(1)
    nv = nv_ref[0]

    def row_copy(j, tok):
        return pltpu.make_async_copy(x_hbm.at[pl.ds(tok, 1)], stage_ref.at[pl.ds(j, 1)], sem)

    def gather_start(tile):
        def body(j, carry):
            row_copy(j, tok_ref[tile * tm + j]).start()
            return carry
        lax.fori_loop(0, tm, body, 0)

    def gather_wait():
        def body(j, carry):
            row_copy(j, 0).wait()
            return carry
        lax.fori_loop(0, tm, body, 0)

    @pl.when(r < nv)
    def _():
        @pl.when(s == 0)
        def _():
            @pl.when(r == 0)
            def _():
                gather_start(0)

            gather_wait()
            xb_ref[...] = stage_ref[...].astype(BF16)

            @pl.when(r + 1 < nv)
            def _():
                gather_start(r + 1)

        @pl.when(s < n_a)
        def _():
            xb = xb_ref[...]
            a = jnp.dot(xb, w1_ref[0], preferred_element_type=F32)
            b = jnp.dot(xb, w3_ref[0], preferred_element_type=F32)
            hid = (a * jax.nn.sigmoid(a) * b).astype(BF16)
            for f in range(n_a):
                @pl.when(s == f)
                def _(f=f):
                    hid_ref[:, f * tf:(f + 1) * tf] = hid

        @pl.when(s >= n_a)
        def _():
            o_ref[...] = jnp.dot(hid_ref[...], w2_ref[0], preferred_element_type=F32)


def _experts(x, w1, w3, w2, tile_expert, n_valid, row_tok, *, tm, tf=512, tn=512):
    D = x.shape[1]
    P = row_tok.shape[0]
    E, _, F = w1.shape
    tf, tn = _tile(F, tf), _tile(D, tn)
    n_a, n_b = F // tf, D // tn
    n_tiles = P // tm

    def a_idx(r, s, nv):
        return jnp.where(r < nv[0], jnp.minimum(s, n_a - 1), n_a - 1)

    def b_idx(r, s, nv):
        return jnp.where(r < nv[0], jnp.maximum(s - n_a, 0), n_b - 1)

    return pl.pallas_call(
        functools.partial(_experts_kernel, tm=tm, tf=tf, n_a=n_a),
        grid_spec=pltpu.PrefetchScalarGridSpec(
            num_scalar_prefetch=3, grid=(n_tiles, n_a + n_b),
            in_specs=[pl.BlockSpec(memory_space=pl.ANY),
                      pl.BlockSpec((1, D, tf), lambda r, s, te, nv, tok: (te[r], 0, a_idx(r, s, nv))),
                      pl.BlockSpec((1, D, tf), lambda r, s, te, nv, tok: (te[r], 0, a_idx(r, s, nv))),
                      pl.BlockSpec((1, F, tn), lambda r, s, te, nv, tok: (te[r], 0, b_idx(r, s, nv)))],
            out_specs=pl.BlockSpec((tm, tn), lambda r, s, te, nv, tok: (jnp.minimum(r, nv[0] - 1), b_idx(r, s, nv))),
            scratch_shapes=[pltpu.VMEM((tm, D), F32), pltpu.VMEM((tm, D), BF16), pltpu.VMEM((tm, F), BF16),
                            pltpu.SemaphoreType.DMA(())]),
        out_shape=jax.ShapeDtypeStruct((P, D), F32),
        compiler_params=_params("arbitrary", "arbitrary"),
        name="moe_experts",
    )(tile_expert, n_valid, row_tok, x, w1, w3, w2)


def _combine_kernel(dest_ref, x_ref, meta_ref, g_ref, b_ref, ys_hbm, o_ref, buf_ref, sem):
    bm = x_ref.shape[0]
    t0 = pl.program_id(0) * bm

    def copy(j, k, d):
        return pltpu.make_async_copy(ys_hbm.at[pl.ds(d, 1)], buf_ref.at[k, pl.ds(j, 1)], sem.at[k])

    def issue(j, carry):
        for k in range(2):
            copy(j, k, dest_ref[2 * (t0 + j) + k]).start()
        return carry

    lax.fori_loop(0, bm, issue, 0)

    def wait(j, carry):
        for k in range(2):
            copy(j, k, 0).wait()
        return carry

    lax.fori_loop(0, bm, wait, 0)
    meta = meta_ref[...]
    y = meta[:, 2:3] * buf_ref[0] + meta[:, 3:4] * buf_ref[1]
    o_ref[...] = _layer_norm_rows(ALPHA * x_ref[...] + y, g_ref[...], b_ref[...])


def _combine(x, meta, ys, dest, g, b, *, bm=256):
    M, D = x.shape
    bm = _tile(M, bm, 8)
    row = lambda n: pl.BlockSpec((bm, n), lambda i, d: (i, 0))
    vec = pl.BlockSpec((1, D), lambda i, d: (0, 0))
    return pl.pallas_call(
        _combine_kernel,
        grid_spec=pltpu.PrefetchScalarGridSpec(
            num_scalar_prefetch=1, grid=(M // bm,),
            in_specs=[row(D), row(LANE), vec, vec, pl.BlockSpec(memory_space=pl.ANY)],
            out_specs=row(D),
            scratch_shapes=[pltpu.VMEM((2, bm, D), F32), pltpu.SemaphoreType.DMA((2,))]),
        out_shape=jax.ShapeDtypeStruct((M, D), F32),
        compiler_params=_params("arbitrary"),
        name="moe_combine",
    )(dest, x, meta, g.reshape(1, D), b.reshape(1, D), ys)


def _even_layer(x, xb, w_in, ret_gn_g, w_out, ln1_g, ln1_b, w1, w3, w2, ln2_g, ln2_b, *, B, T):
    N, D = x.shape
    a_heads = D // 2 // A_HEAD_DIM
    r_heads = D // 2 // RET_VAL_DIM
    qa_w, kv_w = a_heads * A_HEAD_DIM, A_KV_HEADS * A_HEAD_DIM
    qi_w = IDX_HEADS * IDX_DIM
    rk_w, rv_w = r_heads * RET_KEY_DIM, r_heads * RET_VAL_DIM
    sizes = (qa_w, kv_w, kv_w, qi_w, IDX_DIM, IDX_HEADS, rk_w, rk_w, rv_w, rv_w)
    offs = [0]
    for s in sizes:
        offs.append(offs[-1] + s)
    col = lambda a, b_: w_in[:, offs[a]:offs[b_]]
    w_qa, w_ka, w_va = col(0, 1), col(1, 2), col(2, 3)
    w_qi, w_ki, w_wi = col(3, 4), col(4, 5), col(5, 6)
    w_qb, w_kb, w_vb, w_gb = col(6, 7), col(7, 8), col(8, 9), col(9, 10)

    cos_a, sin_a = _rope_cos_sin(T, A_HEAD_DIM // 4, ROPE_THETA)
    tab_q = _lane_tables(cos_a, sin_a, A_HEAD_DIM, A_HEAD_DIM ** -0.5 * LOG2E)
    tab_k = _lane_tables(cos_a, sin_a, A_HEAD_DIM)
    cos_i, sin_i = _rope_cos_sin(T, IDX_DIM // 4, ROPE_THETA)
    tab_i = _lane_tables(cos_i, sin_i, IDX_DIM)
    pass_c = jnp.ones((T, LANE - IDX_DIM), F32)
    pass_s = jnp.zeros((T, LANE - IDX_DIM), F32)
    tab_idx = tuple(jnp.concatenate([t, t[:, :IDX_DIM], p], 1)
                    for t, p in zip(tab_i, (pass_c, pass_s, pass_s)))
    inv = 1.0 / (RET_THETA ** jnp.linspace(0.0, 1.0, RET_KEY_DIM // 2, dtype=F32))
    ang = jnp.arange(T, dtype=F32)[:, None] * inv[None, :]
    cos_r, sin_r = jnp.cos(ang), jnp.sin(ang)
    sin_pair = jnp.concatenate([-sin_r, sin_r], 1)
    tab_r = (jnp.concatenate([cos_r, cos_r], 1), sin_pair, sin_pair)

    idx_pad = LANE - IDX_DIM - IDX_HEADS
    w_idx = jnp.concatenate([w_qi, w_ki, w_wi * (IDX_DIM ** -0.5 * IDX_HEADS ** -0.5),
                             jnp.zeros((D, idx_pad), F32)], 1).astype(BF16)
    w_rqk = jnp.concatenate([w_qb, w_kb * RET_KEY_DIM ** -0.5], 1).astype(BF16)
    w_pv = jnp.concatenate([w_va, w_vb, w_gb], 1).astype(BF16)
    qa = _proj(xb, w_qa.astype(BF16), out_dtype=BF16, tabs=tab_q, half=A_HEAD_DIM // 8,
               seq_len=T, name="proj_qa", **_pat(qa_w, 1024, 0))
    ka = _proj(xb, w_ka.astype(BF16), out_dtype=BF16, tabs=tab_k, half=A_HEAD_DIM // 8,
               seq_len=T, name="proj_ka", **_pat(kv_w, 1024, 0))
    n_idx = w_idx.shape[1]
    idx = _proj(xb, w_idx, out_dtype=F32, tabs=tab_idx, half=IDX_DIM // 8, seq_len=T, name="proj_idx",
                bm=512, bn=n_idx, slab_pat=(0,) * (qi_w // LANE) + (1,))
    rqk = _proj(xb, w_rqk, out_dtype=BF16, tabs=tab_r, mode="pair", seq_len=T, name="proj_ret_qk",
                bn=1024, slab_pat=(0, 1) * (_tile(2 * rk_w, 1024) // (2 * LANE)))
    pv = _proj(xb, w_pv, out_dtype=BF16, bn=768, seq_len=T, name="proj_v")

    ya = _dsa(qa.reshape(B, T, qa_w), ka.reshape(B, T, kv_w), pv.reshape(B, T, -1), idx.reshape(B, T, n_idx),
              B=B, T=T)
    yb = _retention(rqk, pv, ret_gn_g, B=B, T=T, heads=r_heads, v_blk0=kv_w // RET_VAL_DIM)
    mix = jnp.concatenate([ya.reshape(N, qa_w), yb], axis=1)
    y = _proj(mix, w_out.astype(BF16), out_dtype=F32, name="proj_out0")
    x1, x1b = _add_ln(x, y, ln1_g, ln1_b)
    hid = _swiglu_up(x1b, w1.astype(BF16), w3.astype(BF16))
    y = _mm_ksplit(hid, w2.astype(BF16), name="ffn_down")
    return _add_ln(x1, y, ln2_g, ln2_b)


def _pat(width, bn, p):
    bn = _tile(width, bn)
    return dict(bn=bn, slab_pat=(p,) * (bn // LANE))


def _odd_layer(x, xb, w_dq_dkv, q_norm_g, w_uq, kv_norm_g, w_ukv, w_out, ln1_g, ln1_b,
               router, we1, we3, we2, ln2_g, ln2_b, *, B, T):
    N, D = x.shape
    heads = D // MLA_V
    q_rank, kv_rank = q_norm_g.shape[0], kv_norm_g.shape[0]
    scale = (MLA_NOPE + MLA_ROPE) ** -0.5 * LOG2E
    cos_c, sin_c = _rope_cos_sin(T, MLA_ROPE, ROPE_THETA)
    tab_kr = _lane_tables(cos_c, sin_c, MLA_ROPE)
    tab_qr = _lane_tables(cos_c, sin_c, MLA_ROPE, scale)

    w_kr = w_dq_dkv[:, q_rank + kv_rank:]
    w_down = jnp.concatenate([w_dq_dkv[:, :q_rank + kv_rank], w_kr, w_kr], 1).astype(BF16)
    cq, ckv, kr = _mla_down(xb, w_down, q_norm_g, kv_norm_g, tab_kr, T=T, q_rank=q_rank, kv_rank=kv_rank)
    w_uq3 = w_uq.reshape(q_rank, heads, MLA_NOPE + MLA_ROPE)
    w_qn = w_uq3[:, :, :MLA_NOPE].reshape(q_rank, heads * MLA_NOPE).astype(BF16)
    w_qr = w_uq3[:, :, MLA_NOPE:].reshape(q_rank, heads * MLA_ROPE).astype(BF16)
    w_kv3 = w_ukv.reshape(kv_rank, heads, MLA_NOPE + MLA_V)
    w_kv = jnp.concatenate([w_kv3[:, :, :MLA_NOPE].reshape(kv_rank, heads * MLA_NOPE),
                            w_kv3[:, :, MLA_NOPE:].reshape(kv_rank, heads * MLA_V)], 1).astype(BF16)
    qn = _proj(cq, w_qn, out_dtype=BF16, scale=scale, seq_len=T, name="proj_q_nope")
    qr = _proj(cq, w_qr, out_dtype=BF16, tabs=tab_qr, half=MLA_ROPE // 2, seq_len=T, name="proj_q_rope",
               **_pat(heads * MLA_ROPE, 1024, 0))
    kv = _proj(ckv, w_kv, out_dtype=BF16, seq_len=T, name="proj_kv")
    att = _mla_attn(qn, qr, kv, kr, B=B, T=T, heads=heads)
    y = _proj(att, w_out.astype(BF16), out_dtype=F32, name="proj_out1")
    x1, _ = _add_ln(x, y, ln1_g, ln1_b)

    E = router.shape[1]
    F = we1.shape[2]
    tm = _tile(N, 512)
    meta, cnt = _router(x1, router)
    counts = cnt[0, :E].astype(I32)
    padded = (counts + tm - 1) // tm * tm
    ends = jnp.cumsum(padded)
    starts = ends - padded
    i1, i2 = meta[:, 0].astype(I32), meta[:, 1].astype(I32)
    dest = jnp.stack([starts[i1] + meta[:, 4].astype(I32), starts[i2] + meta[:, 5].astype(I32)], 1).reshape(-1)
    n_rows = 2 * N + E * tm
    n_tiles = n_rows // tm
    n_valid = (ends[-1] // tm).astype(I32).reshape(1)
    tile_start = jnp.arange(n_tiles, dtype=I32) * tm
    tile_expert = jnp.minimum(jnp.sum(tile_start[:, None] >= ends[None, :], axis=1), E - 1).astype(I32)
    tile_expert = jnp.where(jnp.arange(n_tiles) < n_valid[0], tile_expert, tile_expert[jnp.maximum(n_valid[0] - 1, 0)])
    row_tok = jnp.zeros((n_rows,), I32).at[dest].set(jnp.repeat(jnp.arange(N, dtype=I32), 2))
    ys = _experts(x1, we1.astype(BF16), we3.astype(BF16), we2.astype(BF16), tile_expert, n_valid, row_tok, tm=tm)
    return _combine(x1, meta, ys, dest, ln2_g, ln2_b)


def kernel(x, l0_w_in, l0_ret_gn_g, l0_w_out, l0_ln1_g, l0_ln1_b, l0_ffn_w1, l0_ffn_w3, l0_ffn_w2, l0_ln2_g, l0_ln2_b, l1_w_dq_dkv, l1_q_norm_g, l1_w_uq, l1_kv_norm_g, l1_w_ukv, l1_w_out, l1_ln1_g, l1_ln1_b, l1_router, l1_moe_w1, l1_moe_w3, l1_moe_w2, l1_ln2_g, l1_ln2_b):
    B, T, D = x.shape
    x2 = x.reshape(B * T, D)
    h, hb = _even_layer(x2, x2.astype(BF16), l0_w_in, l0_ret_gn_g, l0_w_out, l0_ln1_g, l0_ln1_b,
                        l0_ffn_w1, l0_ffn_w3, l0_ffn_w2, l0_ln2_g, l0_ln2_b, B=B, T=T)
    out = _odd_layer(h, hb, l1_w_dq_dkv, l1_q_norm_g, l1_w_uq, l1_kv_norm_g, l1_w_ukv, l1_w_out,
                     l1_ln1_g, l1_ln1_b, l1_router, l1_moe_w1, l1_moe_w3, l1_moe_w2, l1_ln2_g, l1_ln2_b, B=B, T=T)
    return out.reshape(B, T, D)
```

```python
import functools

import jax
import jax.numpy as jnp
from jax import lax
from jax.experimental import pallas as pl
from jax.experimental.pallas import tpu as pltpu

F32 = jnp.float32
BF16 = jnp.bfloat16
I32 = jnp.int32

A_HEAD_DIM = 128
A_KV_HEADS = 4
IDX_HEADS = 16
IDX_DIM = 64
DSA_TOPK_MAX = 256
RET_KEY_DIM = 256
RET_VAL_DIM = 256
RET_CHUNK = 128
RET_THETA = 10000.0
MLA_V = 128
MLA_NOPE = 128
MLA_ROPE = 64
ROPE_THETA = 500000.0
Q_BLOCK = 128
LN_EPS = 1e-5
RMS_EPS = 1e-6
DEPTH = 2
ALPHA = (2.0 * DEPTH) ** 0.25

LANE = 128
V7X_VMEM_BYTES = 64 * 1024 * 1024
VMEM_LIMIT = V7X_VMEM_BYTES - 8 * 1024 * 1024
MASKED = -1e30
INT_MIN = -(2 ** 31)

NT_DIMS = (((1,), (1,)), ((), ()))
TN_DIMS = (((0,), (0,)), ((), ()))


def _tile(n, pref, mult=LANE):
    if n <= pref:
        return n
    t = (pref // mult) * mult
    while t > mult and n % t:
        t -= mult
    assert n % t == 0, (n, pref, mult)
    return t


def _params(*sem):
    return pltpu.CompilerParams(dimension_semantics=sem, vmem_limit_bytes=VMEM_LIMIT)


def _rope_cos_sin(T, rot_dim, theta):
    inv = theta ** (-jnp.arange(0, rot_dim, 2, dtype=F32) / rot_dim)
    ang = jnp.arange(T, dtype=F32)[:, None] * inv[None, :]
    return jnp.cos(ang), jnp.sin(ang)


def _lane_tables(cos, sin, head_dim, scale=1.0):
    T, half = cos.shape
    rest = head_dim - 2 * half
    zh = jnp.zeros((T, half), F32)
    c = jnp.concatenate([cos, cos, jnp.ones((T, rest), F32)], 1)
    sa = jnp.concatenate([-sin, zh, jnp.zeros((T, rest), F32)], 1)
    sb = jnp.concatenate([zh, sin, jnp.zeros((T, rest), F32)], 1)
    reps = LANE // head_dim
    return tuple(jnp.tile(t * scale, (1, reps)) for t in (c, sa, sb))


def _proj_kernel(*refs, slab_pat, mode, half, scale, with_tab, two_inputs):
    x_ref, w_ref = refs[:2]
    acc = jnp.dot(x_ref[...], w_ref[...], preferred_element_type=F32)
    refs = refs[2:]
    if two_inputs:
        acc = acc + jnp.dot(refs[0][...], refs[1][...], preferred_element_type=F32)
        refs = refs[2:]
    if with_tab:
        c_ref, sa_ref, sb_ref, o_ref = refs
    else:
        (o_ref,) = refs
    for s, p in enumerate(slab_pat):
        a = acc[:, s * LANE:(s + 1) * LANE]
        if p < 0:
            out = a if scale == 1.0 else a * scale
        else:
            c = c_ref[:, p * LANE:(p + 1) * LANE]
            sa = sa_ref[:, p * LANE:(p + 1) * LANE]
            if mode == "lane":
                sb = sb_ref[:, p * LANE:(p + 1) * LANE]
                out = a * c + pltpu.roll(a, LANE - half, 1) * sa + pltpu.roll(a, half, 1) * sb
            else:
                q = s ^ 1
                out = a * c + acc[:, q * LANE:(q + 1) * LANE] * sa
        o_ref[:, s * LANE:(s + 1) * LANE] = out.astype(o_ref.dtype)


def _proj(x, w, *, out_dtype, bm=1024, bn=1024, tabs=None, slab_pat=None, mode="lane", half=0,
          scale=1.0, seq_len=None, second=None, name="proj"):
    M, K = x.shape
    N = w.shape[1]
    bm = _tile(M, bm) if seq_len is None else _tile(seq_len, bm)
    bn = _tile(N, bn)
    if slab_pat is None:
        slab_pat = (-1,) * (bn // LANE)
    assert len(slab_pat) == bn // LANE
    in_specs = [pl.BlockSpec((bm, K), lambda i, j: (i, 0)),
                pl.BlockSpec((K, bn), lambda i, j: (0, j))]
    args = [x, w]
    if second is not None:
        x2, w2 = second
        in_specs += [pl.BlockSpec((bm, x2.shape[1]), lambda i, j: (i, 0)),
                     pl.BlockSpec((x2.shape[1], bn), lambda i, j: (0, j))]
        args += [x2, w2]
    if tabs is not None:
        tb = seq_len // bm
        tw = tabs[0].shape[1]
        in_specs += [pl.BlockSpec((bm, tw), lambda i, j: (i % tb, 0))] * 3
        args += list(tabs)
    kern = functools.partial(_proj_kernel, slab_pat=tuple(slab_pat), mode=mode, half=half,
                             scale=scale, with_tab=tabs is not None, two_inputs=second is not None)
    return pl.pallas_call(
        kern,
        grid=(M // bm, N // bn),
        in_specs=in_specs,
        out_specs=pl.BlockSpec((bm, bn), lambda i, j: (i, j)),
        out_shape=jax.ShapeDtypeStruct((M, N), out_dtype),
        compiler_params=_params("parallel", "parallel"),
        name=name,
    )(*args)


def _mm_ksplit_kernel(x_ref, w_ref, o_ref):
    part = jnp.dot(x_ref[...], w_ref[...], preferred_element_type=F32)

    @pl.when(pl.program_id(2) == 0)
    def _():
        o_ref[...] = part

    @pl.when(pl.program_id(2) > 0)
    def _():
        o_ref[...] += part


def _mm_ksplit(x, w, *, bm=1024, bn=1024, bk=2048, name="mm_ksplit"):
    M, K = x.shape
    N = w.shape[1]
    bm, bn, bk = _tile(M, bm), _tile(N, bn), _tile(K, bk)
    return pl.pallas_call(
        _mm_ksplit_kernel,
        grid=(M // bm, N // bn, K // bk),
        in_specs=[pl.BlockSpec((bm, bk), lambda i, j, k: (i, k)),
                  pl.BlockSpec((bk, bn), lambda i, j, k: (k, j))],
        out_specs=pl.BlockSpec((bm, bn), lambda i, j, k: (i, j)),
        out_shape=jax.ShapeDtypeStruct((M, N), F32),
        compiler_params=_params("parallel", "parallel", "arbitrary"),
        name=name,
    )(x, w)


def _swiglu_up_kernel(x_ref, w1_ref, w3_ref, o_ref):
    x = x_ref[...]
    a = jnp.dot(x, w1_ref[...], preferred_element_type=F32)
    b = jnp.dot(x, w3_ref[...], preferred_element_type=F32)
    o_ref[...] = (a * jax.nn.sigmoid(a) * b).astype(o_ref.dtype)


def _swiglu_up(x, w1, w3, *, bm=1024, bn=512):
    M, K = x.shape
    N = w1.shape[1]
    bm, bn = _tile(M, bm), _tile(N, bn)
    return pl.pallas_call(
        _swiglu_up_kernel,
        grid=(M // bm, N // bn),
        in_specs=[pl.BlockSpec((bm, K), lambda i, j: (i, 0)),
                  pl.BlockSpec((K, bn), lambda i, j: (0, j)),
                  pl.BlockSpec((K, bn), lambda i, j: (0, j))],
        out_specs=pl.BlockSpec((bm, bn), lambda i, j: (i, j)),
        out_shape=jax.ShapeDtypeStruct((M, N), BF16),
        compiler_params=_params("parallel", "parallel"),
        name="swiglu_up",
    )(x, w1, w3)


def _layer_norm_rows(z, g, b):
    mu = jnp.mean(z, axis=-1, keepdims=True)
    zc = z - mu
    var = jnp.mean(zc * zc, axis=-1, keepdims=True)
    return zc * lax.rsqrt(var + LN_EPS) * g + b


def _add_ln_kernel(x_ref, y_ref, g_ref, b_ref, of_ref, *maybe_ob_ref):
    out = _layer_norm_rows(ALPHA * x_ref[...] + y_ref[...], g_ref[...], b_ref[...])
    of_ref[...] = out
    for ob_ref in maybe_ob_ref:
        ob_ref[...] = out.astype(ob_ref.dtype)


def _add_ln(x, y, g, b, *, bf16_copy=True, bm=256):
    M, D = x.shape
    bm = _tile(M, bm, 8)
    row = pl.BlockSpec((bm, D), lambda i: (i, 0))
    vec = pl.BlockSpec((1, D), lambda i: (0, 0))
    out_dtypes = (F32, BF16) if bf16_copy else (F32,)
    return pl.pallas_call(
        _add_ln_kernel,
        grid=(M // bm,),
        in_specs=[row, row, vec, vec],
        out_specs=[row] * len(out_dtypes),
        out_shape=[jax.ShapeDtypeStruct((M, D), dt) for dt in out_dtypes],
        compiler_params=_params("parallel"),
        name="add_ln",
    )(x, y, g.reshape(1, D), b.reshape(1, D))


LOG2E = 1.4426950408889634
SOFTMAX_ROW_BLOCK = 512


def _online_softmax(s, m, l):
    rows = s.shape[0]
    rb = min(SOFTMAX_ROW_BLOCK, rows)
    ms, ls, scales, ps = [], [], [], []
    for r in range(rows // rb):
        sl = slice(r * rb, (r + 1) * rb)
        s_r = s[sl]
        m_r = jnp.maximum(m[sl], jnp.max(s_r, axis=1, keepdims=True))
        a_r = jnp.exp2(m[sl] - m_r)
        p_r = jnp.exp2((s_r - m_r).astype(BF16))
        ms.append(m_r)
        ls.append(a_r * l[sl] + jnp.sum(p_r.astype(F32), axis=1, keepdims=True))
        scales.append(a_r)
        ps.append(p_r)
    cat = lambda parts: jnp.concatenate(parts, axis=0)
    return cat(ms), cat(ls), cat(scales), cat(ps)


def _dsa_kernel(q_ref, k_ref, v_ref, iq_ref, ik_ref, o_ref, keys_ref, bias_ref, *, topk, ck, n_rep, idx_bits):
    blk = pl.program_id(1)
    n_chunks = (blk * Q_BLOCK + Q_BLOCK + ck - 1) // ck
    row = lax.broadcasted_iota(I32, (Q_BLOCK, ck), 0) + blk * Q_BLOCK
    lane = lax.broadcasted_iota(I32, (Q_BLOCK, ck), 1)
    w_off = IDX_HEADS * IDX_DIM + IDX_DIM
    iq = iq_ref[0]
    wi = iq[:, w_off:w_off + IDX_HEADS]
    q_idx = jnp.concatenate([iq[:, h * IDX_DIM:(h + 1) * IDX_DIM].astype(BF16) for h in range(IDX_HEADS)], axis=0)

    def score_body(c, carry):
        off = pl.multiple_of(c * ck, ck)
        kc = ik_ref[0, pl.ds(off, ck), :][:, :IDX_DIM].astype(BF16)
        lg = lax.dot_general(q_idx, kc, NT_DIMS, preferred_element_type=F32)
        s = jnp.zeros((Q_BLOCK, ck), F32)
        for h in range(IDX_HEADS):
            s = s + wi[:, h:h + 1] * jnp.maximum(lg[h * Q_BLOCK:(h + 1) * Q_BLOCK], 0.0)
        bits = pltpu.bitcast(s, I32)
        key = bits ^ ((bits >> 31) & 0x7FFFFFFF)
        keys_ref[c] = jnp.where(lane + off <= row, key, INT_MIN)
        return carry

    lax.fori_loop(0, n_chunks, score_body, 0)

    def count(indicator):
        def body(c, acc):
            part = indicator(keys_ref[c], lane + c * ck)
            for j in range(ck // LANE):
                acc = acc + part[:, j * LANE:(j + 1) * LANE]
            return acc
        acc = lax.fori_loop(0, n_chunks, body, jnp.zeros((Q_BLOCK, LANE), F32))
        return jnp.sum(acc, axis=1, keepdims=True)

    kf = float(topk)
    ok = count(lambda kc, idx: jnp.where(kc >= 0, 1.0, 0.0)) >= kf
    thr = jnp.where(ok, 0, INT_MIN).astype(I32)

    def bit_body(i, thr):
        cand = thr + lax.shift_left(jnp.int32(1), 30 - i)
        ok = count(lambda kc, idx: jnp.where(kc >= cand, 1.0, 0.0)) >= kf
        return jnp.where(ok, cand, thr)

    thr = lax.fori_loop(0, 31, bit_body, thr)

    n_gt = count(lambda kc, idx: jnp.where(kc > thr, 1.0, 0.0))
    n_ge = count(lambda kc, idx: jnp.where(kc >= thr, 1.0, 0.0))
    need = kf - n_gt
    has_thr = thr > INT_MIN
    surplus = jnp.where(has_thr, n_ge - n_gt - need, 0.0)

    def tie_search():
        def tie_body(i, last):
            cand = last + lax.shift_left(jnp.int32(1), idx_bits - 1 - i)
            ok = count(lambda kc, idx: jnp.where(kc == thr, jnp.where(idx < cand, 1.0, 0.0), 0.0)) < need
            return jnp.where(ok, cand, last)
        return lax.fori_loop(0, idx_bits, tie_body, jnp.zeros((Q_BLOCK, 1), I32))

    last = lax.cond(jnp.max(surplus) > 0.0, tie_search, lambda: jnp.full((Q_BLOCK, 1), 2 ** idx_bits, I32))
    last = jnp.where(has_thr, last, -1)

    def bias_body(c, carry):
        kc = keys_ref[c]
        tie_bias = jnp.where(lane + c * ck <= last, 0.0, MASKED)
        bias_ref[c] = jnp.where(kc == thr, tie_bias, jnp.where(kc > thr, 0.0, MASKED))
        return carry

    lax.fori_loop(0, n_chunks, bias_body, 0)

    q = q_ref[0]
    rows = n_rep * Q_BLOCK
    qgs = [jnp.concatenate([q[:, (g * n_rep + r) * A_HEAD_DIM:(g * n_rep + r + 1) * A_HEAD_DIM]
                            for r in range(n_rep)], axis=0) for g in range(A_KV_HEADS)]

    def att_body(c, carry):
        off = pl.multiple_of(c * ck, ck)
        b = bias_ref[c]
        bias = jnp.concatenate([b] * n_rep, axis=0)
        out = []
        for g in range(A_KV_HEADS):
            m, l, acc = carry[g]
            kc = k_ref[0, pl.ds(off, ck), g * A_HEAD_DIM:(g + 1) * A_HEAD_DIM]
            vc = v_ref[0, pl.ds(off, ck), g * A_HEAD_DIM:(g + 1) * A_HEAD_DIM]
            s = lax.dot_general(qgs[g], kc, NT_DIMS, preferred_element_type=F32) + bias
            m, l, a, p = _online_softmax(s, m, l)
            out.append((m, l, a * acc + jnp.dot(p, vc, preferred_element_type=F32)))
        return tuple(out)

    one = (jnp.full((rows, 1), MASKED, F32), jnp.zeros((rows, 1), F32), jnp.zeros((rows, A_HEAD_DIM), F32))
    carry = lax.fori_loop(0, n_chunks, att_body, (one,) * A_KV_HEADS)
    for g in range(A_KV_HEADS):
        _, l, acc = carry[g]
        o = acc / l
        for r in range(n_rep):
            col = (g * n_rep + r) * A_HEAD_DIM
            o_ref[0, :, col:col + A_HEAD_DIM] = o[r * Q_BLOCK:(r + 1) * Q_BLOCK].astype(o_ref.dtype)


def _dsa(q, k, v, idx, *, B, T):
    a_heads = q.shape[-1] // A_HEAD_DIM
    n_rep = a_heads // A_KV_HEADS
    topk = min(DSA_TOPK_MAX, T // 4)
    ck = _tile(T, 512)
    idx_w = idx.shape[-1]
    kv_w = k.shape[-1]
    kern = functools.partial(_dsa_kernel, topk=topk, ck=ck, n_rep=n_rep, idx_bits=max(1, (T - 1).bit_length()))
    return pl.pallas_call(
        kern,
        grid=(B, T // Q_BLOCK),
        in_specs=[pl.BlockSpec((1, Q_BLOCK, q.shape[-1]), lambda b, i: (b, i, 0)),
                  pl.BlockSpec((1, T, kv_w), lambda b, i: (b, 0, 0)),
                  pl.BlockSpec((1, T, kv_w), lambda b, i: (b, 0, 0)),
                  pl.BlockSpec((1, Q_BLOCK, idx_w), lambda b, i: (b, i, 0)),
                  pl.BlockSpec((1, T, LANE), lambda b, i: (b, 0, IDX_HEADS * IDX_DIM // LANE))],
        out_specs=pl.BlockSpec((1, Q_BLOCK, q.shape[-1]), lambda b, i: (b, i, 0)),
        out_shape=jax.ShapeDtypeStruct(q.shape, BF16),
        scratch_shapes=[pltpu.VMEM((T // ck, Q_BLOCK, ck), I32), pltpu.VMEM((T // ck, Q_BLOCK, ck), F32)],
        compiler_params=_params("parallel", "parallel"),
        name="dsa",
    )(q, k, v, idx, idx)


def _retention_kernel(q_ref, k_ref, v_ref, g_ref, gn_ref, din_ref, qd_ref, kd_ref, cd_ref, o_ref, state_ref, *, n_sub):
    @pl.when(pl.program_id(2) == 0)
    def _():
        state_ref[...] = jnp.zeros_like(state_ref)

    din = din_ref[0]
    qd = qd_ref[0]
    kd = kd_ref[0]
    cd = cd_ref[0]
    gn = gn_ref[...]
    for s in range(n_sub):
        sl = pl.ds(s * RET_CHUNK, RET_CHUNK)
        qc = q_ref[sl, :]
        kc = k_ref[sl, :]
        vc = v_ref[sl, :]
        st = state_ref[...]
        inner = lax.dot_general(qc, kc, NT_DIMS, preferred_element_type=F32) * din
        o = (jnp.dot(inner.astype(BF16), vc, preferred_element_type=F32)
             + jnp.dot(qc, st.astype(BF16), preferred_element_type=F32) * qd)
        vk = (vc.astype(F32) * kd).astype(BF16)
        state_ref[...] = st * cd + lax.dot_general(kc, vk, TN_DIMS, preferred_element_type=F32)
        mu = jnp.mean(o, axis=-1, keepdims=True)
        oc = o - mu
        var = jnp.mean(oc * oc, axis=-1, keepdims=True)
        gate = g_ref[sl, :].astype(F32)
        o_ref[sl, :] = (gate * jax.nn.sigmoid(gate) * (oc * lax.rsqrt(var + LN_EPS) * gn)).astype(o_ref.dtype)


def _retention(qk, pv, gn_g, *, B, T, heads, v_blk0):
    N = qk.shape[0]
    C = RET_CHUNK
    rb = _tile(T, 512)
    n_sub = rb // C
    nr = T // rb
    log_gamma = jnp.log(1.0 - 2.0 ** (-5.0 - jnp.arange(heads, dtype=F32)))
    pos = jnp.arange(C, dtype=F32)
    diff = pos[:, None] - pos[None, :]
    din = jnp.exp(jnp.where(diff[None] >= 0, log_gamma[:, None, None] * diff[None], -jnp.inf))
    qd = jnp.exp(log_gamma[:, None] * (pos[None] + 1.0))[:, :, None]
    kd = jnp.exp(log_gamma[:, None] * (C - 1.0 - pos[None]))[:, :, None]
    cd = jnp.exp(log_gamma * C)[:, None, None]
    W = RET_VAL_DIM
    blk = lambda off: pl.BlockSpec((rb, W), lambda b, h, r: (b * nr + r, off + h))
    per_head = lambda shape: pl.BlockSpec((1,) + shape, lambda b, h, r: (h, 0, 0))
    return pl.pallas_call(
        functools.partial(_retention_kernel, n_sub=n_sub),
        grid=(B, heads, nr),
        in_specs=[blk(0), blk(heads), blk(v_blk0), blk(v_blk0 + heads),
                  pl.BlockSpec((1, W), lambda b, h, r: (0, h)),
                  per_head((C, C)), per_head((C, 1)), per_head((C, 1)), per_head((1, 1))],
        out_specs=pl.BlockSpec((rb, W), lambda b, h, r: (b * nr + r, h)),
        out_shape=jax.ShapeDtypeStruct((N, heads * W), BF16),
        scratch_shapes=[pltpu.VMEM((RET_KEY_DIM, RET_VAL_DIM), F32)],
        compiler_params=_params("parallel", "parallel", "arbitrary"),
        name="retention",
    )(qk, qk, pv, pv, gn_g.reshape(1, heads * W), din, qd, kd, cd)


def _mla_down_kernel(x_ref, w_ref, qg_ref, kvg_ref, c_ref, sa_ref, sb_ref, cq_ref, ckv_ref, kr_ref, *, q_rank, kv_rank):
    acc = jnp.dot(x_ref[...], w_ref[...], preferred_element_type=F32)

    def rms(a, g):
        return a * lax.rsqrt(jnp.mean(a * a, axis=-1, keepdims=True) + RMS_EPS) * g

    cq_ref[...] = rms(acc[:, :q_rank], qg_ref[...]).astype(cq_ref.dtype)
    ckv_ref[...] = rms(acc[:, q_rank:q_rank + kv_rank], kvg_ref[...]).astype(ckv_ref.dtype)
    kr = acc[:, q_rank + kv_rank:]
    half = MLA_ROPE // 2
    kr = kr * c_ref[...] + pltpu.roll(kr, LANE - half, 1) * sa_ref[...] + pltpu.roll(kr, half, 1) * sb_ref[...]
    kr_ref[...] = kr.astype(kr_ref.dtype)


def _mla_down(x, w, q_g, kv_g, tabs, *, T, q_rank, kv_rank, bm=512):
    M, K = x.shape
    Nw = w.shape[1]
    bm = _tile(T, bm)
    tb = T // bm
    row = lambda n: pl.BlockSpec((bm, n), lambda i: (i, 0))
    tab = pl.BlockSpec((bm, LANE), lambda i: (i % tb, 0))
    return pl.pallas_call(
        functools.partial(_mla_down_kernel, q_rank=q_rank, kv_rank=kv_rank),
        grid=(M // bm,),
        in_specs=[row(K), pl.BlockSpec((K, Nw), lambda i: (0, 0)),
                  pl.BlockSpec((1, q_rank), lambda i: (0, 0)), pl.BlockSpec((1, kv_rank), lambda i: (0, 0)),
                  tab, tab, tab],
        out_specs=[row(q_rank), row(kv_rank), row(LANE)],
        out_shape=[jax.ShapeDtypeStruct((M, q_rank), BF16), jax.ShapeDtypeStruct((M, kv_rank), BF16),
                   jax.ShapeDtypeStruct((M, LANE), BF16)],
        compiler_params=_params("parallel"),
        name="mla_down",
    )(x, w, q_g.reshape(1, q_rank), kv_g.reshape(1, kv_rank), *tabs)


MLA_HEADS_PER_STEP = 4


def _mla_attn_kernel(qn_ref, qr_ref, kn_ref, kr_ref, v_ref, o_ref, *, tq):
    i = pl.program_id(2)
    lane = lax.broadcasted_iota(I32, (tq, LANE), 1)
    qs = []
    for j in range(MLA_HEADS_PER_STEP):
        lo = (j % 2) * MLA_ROPE
        own = jnp.where((lane >= lo) & (lane < lo + MLA_ROPE), 1.0, 0.0)
        pair = qr_ref[:, (j // 2) * LANE:(j // 2 + 1) * LANE].astype(F32)
        qr = (pair * own).astype(BF16)
        qs.append(jnp.concatenate([qn_ref[:, j * LANE:(j + 1) * LANE], qr], axis=1))

    def step(c, carry, masked):
        off = pl.multiple_of(c * tq, tq)
        kr = kr_ref[pl.ds(off, tq), :]
        out = []
        for j in range(MLA_HEADS_PER_STEP):
            m, l, acc = carry[j]
            k = jnp.concatenate([kn_ref[pl.ds(off, tq), j * LANE:(j + 1) * LANE], kr], axis=1)
            s = lax.dot_general(qs[j], k, NT_DIMS, preferred_element_type=F32)
            if masked:
                r_io = lax.broadcasted_iota(I32, (tq, tq), 0)
                c_io = lax.broadcasted_iota(I32, (tq, tq), 1)
                s = jnp.where(c_io <= r_io, s, MASKED)
            m, l, a, p = _online_softmax(s, m, l)
            v = v_ref[pl.ds(off, tq), j * MLA_V:(j + 1) * MLA_V]
            out.append((m, l, a * acc + jnp.dot(p, v, preferred_element_type=F32)))
        return tuple(out)

    one = (jnp.full((tq, 1), MASKED, F32), jnp.zeros((tq, 1), F32), jnp.zeros((tq, MLA_V), F32))
    carry = lax.fori_loop(0, i, functools.partial(step, masked=False), (one,) * MLA_HEADS_PER_STEP)
    carry = step(i, carry, True)
    for j in range(MLA_HEADS_PER_STEP):
        _, l, acc = carry[j]
        o_ref[:, j * MLA_V:(j + 1) * MLA_V] = (acc / l).astype(o_ref.dtype)


def _mla_attn(qn, qr, kv, kr, *, B, T, heads):
    N = qn.shape[0]
    tq = _tile(T, 512)
    nq = T // tq
    hp = MLA_HEADS_PER_STEP
    w = hp * LANE
    return pl.pallas_call(
        functools.partial(_mla_attn_kernel, tq=tq),
        grid=(B, heads // hp, nq),
        in_specs=[pl.BlockSpec((tq, w), lambda b, h, i: (b * nq + i, h)),
                  pl.BlockSpec((tq, hp * MLA_ROPE), lambda b, h, i: (b * nq + i, h)),
                  pl.BlockSpec((T, w), lambda b, h, i: (b, h)),
                  pl.BlockSpec((T, LANE), lambda b, h, i: (b, 0)),
                  pl.BlockSpec((T, w), lambda b, h, i: (b, heads // hp + h))],
        out_specs=pl.BlockSpec((tq, w), lambda b, h, i: (b * nq + i, h)),
        out_shape=jax.ShapeDtypeStruct((N, heads * MLA_V), BF16),
        compiler_params=_params("parallel", "parallel", "parallel"),
        name="mla_attn",
    )(qn, qr, kv, kr, kv)


def _router_kernel(x_ref, r_ref, meta_ref, cnt_ref, carry_ref, *, n_exp):
    @pl.when(pl.program_id(0) == 0)
    def _():
        carry_ref[...] = jnp.zeros_like(carry_ref)

    bm = x_ref.shape[0]
    logits = jnp.dot(x_ref[...], r_ref[...], preferred_element_type=F32, precision=lax.Precision.HIGHEST)
    lane = lax.broadcasted_iota(I32, (bm, LANE), 1).astype(F32)
    logits = jnp.where(lane < n_exp, logits, -jnp.inf)
    m1 = jnp.max(logits, axis=1, keepdims=True)
    i1 = jnp.min(jnp.where(logits == m1, lane, float(LANE)), axis=1, keepdims=True)
    rest = jnp.where(lane == i1, -jnp.inf, logits)
    m2 = jnp.max(rest, axis=1, keepdims=True)
    i2 = jnp.min(jnp.where(rest == m2, lane, float(LANE)), axis=1, keepdims=True)
    e = jnp.exp(m2 - m1)
    g1 = 1.0 / (1.0 + e)
    g2 = e / (1.0 + e)
    sel = jnp.where(lane == i1, 1.0, jnp.where(lane == i2, 1.0, 0.0))
    r_io = lax.broadcasted_iota(I32, (bm, bm), 0)
    c_io = lax.broadcasted_iota(I32, (bm, bm), 1)
    below = jnp.where(c_io < r_io, 1.0, 0.0).astype(BF16)
    carry = carry_ref[0:1, :]
    rank = jnp.dot(below, sel.astype(BF16), preferred_element_type=F32) + carry
    r1 = jnp.sum(jnp.where(lane == i1, rank, 0.0), axis=1, keepdims=True)
    r2 = jnp.sum(jnp.where(lane == i2, rank, 0.0), axis=1, keepdims=True)
    meta = jnp.where(lane == 0, i1, 0.0)
    meta = jnp.where(lane == 1, i2, meta)
    meta = jnp.where(lane == 2, g1, meta)
    meta = jnp.where(lane == 3, g2, meta)
    meta = jnp.where(lane == 4, r1, meta)
    meta = jnp.where(lane == 5, r2, meta)
    meta_ref[...] = meta
    total = carry + jnp.sum(sel, axis=0, keepdims=True)
    carry_ref[...] = jnp.broadcast_to(total, carry_ref.shape)
    cnt_ref[...] = jnp.broadcast_to(total, cnt_ref.shape)


def _router(x, router, *, bm=512):
    M, D = x.shape
    n_exp = router.shape[1]
    bm = _tile(M, bm)
    r_pad = jnp.zeros((D, LANE), F32).at[:, :n_exp].set(router)
    return pl.pallas_call(
        functools.partial(_router_kernel, n_exp=n_exp),
        grid=(M // bm,),
        in_specs=[pl.BlockSpec((bm, D), lambda i: (i, 0)), pl.BlockSpec((D, LANE), lambda i: (0, 0))],
        out_specs=[pl.BlockSpec((bm, LANE), lambda i: (i, 0)), pl.BlockSpec((8, LANE), lambda i: (0, 0))],
        out_shape=[jax.ShapeDtypeStruct((M, LANE), F32), jax.ShapeDtypeStruct((8, LANE), F32)],
        scratch_shapes=[pltpu.VMEM((8, LANE), F32)],
        compiler_params=_params("arbitrary"),
        name="router",
    )(x, r_pad)


def _experts_kernel(te_ref, nv_ref, tok_ref, x_hbm, w1_ref, w3_ref, w2_ref, o_ref, stage_ref, xb_ref, hid_ref, sem,
                    *, tm, tf, n_a):
    r = pl.program_id(0)
    s = pl.program_id(1)
    nv = nv_ref[0]

    def row_copy(j, tok):
        return pltpu.make_async_copy(x_hbm.at[pl.ds(tok, 1)], stage_ref.at[pl.ds(j, 1)], sem)

    def gather_start(tile):
        def body(j, carry):
            row_copy(j, tok_ref[tile * tm + j]).start()
            return carry
        lax.fori_loop(0, tm, body, 0)

    def gather_wait():
        def body(j, carry):
            row_copy(j, 0).wait()
            return carry
        lax.fori_loop(0, tm, body, 0)

    @pl.when(r < nv)
    def _():
        @pl.when(s == 0)
        def _():
            @pl.when(r == 0)
            def _():
                gather_start(0)

            gather_wait()
            xb_ref[...] = stage_ref[...].astype(BF16)

            @pl.when(r + 1 < nv)
            def _():
                gather_start(r + 1)

        @pl.when(s < n_a)
        def _():
            xb = xb_ref[...]
            a = jnp.dot(xb, w1_ref[0], preferred_element_type=F32)
            b = jnp.dot(xb, w3_ref[0], preferred_element_type=F32)
            hid = (a * jax.nn.sigmoid(a) * b).astype(BF16)
            for f in range(n_a):
                @pl.when(s == f)
                def _(f=f):
                    hid_ref[:, f * tf:(f + 1) * tf] = hid

        @pl.when(s >= n_a)
        def _():
            o_ref[...] = jnp.dot(hid_ref[...], w2_ref[0], preferred_element_type=F32)


def _experts(x, w1, w3, w2, tile_expert, n_valid, row_tok, *, tm, tf=512, tn=512):
    D = x.shape[1]
    P = row_tok.shape[0]
    E, _, F = w1.shape
    tf, tn = _tile(F, tf), _tile(D, tn)
    n_a, n_b = F // tf, D // tn
    n_tiles = P // tm

    def a_idx(r, s, nv):
        return jnp.where(r < nv[0], jnp.minimum(s, n_a - 1), n_a - 1)

    def b_idx(r, s, nv):
        return jnp.where(r < nv[0], jnp.maximum(s - n_a, 0), n_b - 1)

    return pl.pallas_call(
        functools.partial(_experts_kernel, tm=tm, tf=tf, n_a=n_a),
        grid_spec=pltpu.PrefetchScalarGridSpec(
            num_scalar_prefetch=3, grid=(n_tiles, n_a + n_b),
            in_specs=[pl.BlockSpec(memory_space=pl.ANY),
                      pl.BlockSpec((1, D, tf), lambda r, s, te, nv, tok: (te[r], 0, a_idx(r, s, nv))),
                      pl.BlockSpec((1, D, tf), lambda r, s, te, nv, tok: (te[r], 0, a_idx(r, s, nv))),
                      pl.BlockSpec((1, F, tn), lambda r, s, te, nv, tok: (te[r], 0, b_idx(r, s, nv)))],
            out_specs=pl.BlockSpec((tm, tn), lambda r, s, te, nv, tok: (jnp.minimum(r, nv[0] - 1), b_idx(r, s, nv))),
            scratch_shapes=[pltpu.VMEM((tm, D), F32), pltpu.VMEM((tm, D), BF16), pltpu.VMEM((tm, F), BF16),
                            pltpu.SemaphoreType.DMA(())]),
        out_shape=jax.ShapeDtypeStruct((P, D), F32),
        compiler_params=_params("arbitrary", "arbitrary"),
        name="moe_experts",
    )(tile_expert, n_valid, row_tok, x, w1, w3, w2)


def _combine_kernel(dest_ref, x_ref, meta_ref, g_ref, b_ref, ys_hbm, o_ref, buf_ref, sem):
    bm = x_ref.shape[0]
    i = pl.program_id(0)
    n = pl.num_programs(0)

    def copy(slot, j, k, d):
        return pltpu.make_async_copy(ys_hbm.at[pl.ds(d, 1)], buf_ref.at[slot, k, pl.ds(j, 1)], sem.at[slot, k])

    def gather_start(blk):
        slot = blk % 2

        def body(j, carry):
            for k in range(2):
                copy(slot, j, k, dest_ref[2 * (blk * bm + j) + k]).start()
            return carry
        lax.fori_loop(0, bm, body, 0)

    @pl.when(i == 0)
    def _():
        gather_start(0)

    @pl.when(i + 1 < n)
    def _():
        gather_start(i + 1)

    slot = i % 2

    def wait(j, carry):
        for k in range(2):
            copy(slot, j, k, 0).wait()
        return carry

    lax.fori_loop(0, bm, wait, 0)
    meta = meta_ref[...]
    y = meta[:, 2:3] * buf_ref[slot, 0] + meta[:, 3:4] * buf_ref[slot, 1]
    o_ref[...] = _layer_norm_rows(ALPHA * x_ref[...] + y, g_ref[...], b_ref[...])


def _combine(x, meta, ys, dest, g, b, *, bm=256):
    M, D = x.shape
    bm = _tile(M, bm, 8)
    row = lambda n: pl.BlockSpec((bm, n), lambda i, d: (i, 0))
    vec = pl.BlockSpec((1, D), lambda i, d: (0, 0))
    return pl.pallas_call(
        _combine_kernel,
        grid_spec=pltpu.PrefetchScalarGridSpec(
            num_scalar_prefetch=1, grid=(M // bm,),
            in_specs=[row(D), row(LANE), vec, vec, pl.BlockSpec(memory_space=pl.ANY)],
            out_specs=row(D),
            scratch_shapes=[pltpu.VMEM((2, 2, bm, D), F32), pltpu.SemaphoreType.DMA((2, 2))]),
        out_shape=jax.ShapeDtypeStruct((M, D), F32),
        compiler_params=_params("arbitrary"),
        name="moe_combine",
    )(dest, x, meta, g.reshape(1, D), b.reshape(1, D), ys)


def _even_layer(x, xb, w_in, ret_gn_g, w_out, ln1_g, ln1_b, w1, w3, w2, ln2_g, ln2_b, *, B, T):
    N, D = x.shape
    a_heads = D // 2 // A_HEAD_DIM
    r_heads = D // 2 // RET_VAL_DIM
    qa_w, kv_w = a_heads * A_HEAD_DIM, A_KV_HEADS * A_HEAD_DIM
    qi_w = IDX_HEADS * IDX_DIM
    rk_w, rv_w = r_heads * RET_KEY_DIM, r_heads * RET_VAL_DIM
    sizes = (qa_w, kv_w, kv_w, qi_w, IDX_DIM, IDX_HEADS, rk_w, rk_w, rv_w, rv_w)
    offs = [0]
    for s in sizes:
        offs.append(offs[-1] + s)
    col = lambda a, b_: w_in[:, offs[a]:offs[b_]]
    w_qa, w_ka, w_va = col(0, 1), col(1, 2), col(2, 3)
    w_qi, w_ki, w_wi = col(3, 4), col(4, 5), col(5, 6)
    w_qb, w_kb, w_vb, w_gb = col(6, 7), col(7, 8), col(8, 9), col(9, 10)

    cos_a, sin_a = _rope_cos_sin(T, A_HEAD_DIM // 4, ROPE_THETA)
    tab_q = _lane_tables(cos_a, sin_a, A_HEAD_DIM, A_HEAD_DIM ** -0.5 * LOG2E)
    tab_k = _lane_tables(cos_a, sin_a, A_HEAD_DIM)
    cos_i, sin_i = _rope_cos_sin(T, IDX_DIM // 4, ROPE_THETA)
    tab_i = _lane_tables(cos_i, sin_i, IDX_DIM)
    pass_c = jnp.ones((T, LANE - IDX_DIM), F32)
    pass_s = jnp.zeros((T, LANE - IDX_DIM), F32)
    tab_idx = tuple(jnp.concatenate([t, t[:, :IDX_DIM], p], 1)
                    for t, p in zip(tab_i, (pass_c, pass_s, pass_s)))
    inv = 1.0 / (RET_THETA ** jnp.linspace(0.0, 1.0, RET_KEY_DIM // 2, dtype=F32))
    ang = jnp.arange(T, dtype=F32)[:, None] * inv[None, :]
    cos_r, sin_r = jnp.cos(ang), jnp.sin(ang)
    sin_pair = jnp.concatenate([-sin_r, sin_r], 1)
    tab_r = (jnp.concatenate([cos_r, cos_r], 1), sin_pair, sin_pair)

    idx_pad = LANE - IDX_DIM - IDX_HEADS
    w_idx = jnp.concatenate([w_qi, w_ki, w_wi * (IDX_DIM ** -0.5 * IDX_HEADS ** -0.5),
                             jnp.zeros((D, idx_pad), F32)], 1).astype(BF16)
    w_rqk = jnp.concatenate([w_qb, w_kb * RET_KEY_DIM ** -0.5], 1).astype(BF16)
    w_pv = jnp.concatenate([w_va, w_vb, w_gb], 1).astype(BF16)
    qa = _proj(xb, w_qa.astype(BF16), out_dtype=BF16, tabs=tab_q, half=A_HEAD_DIM // 8,
               seq_len=T, name="proj_qa", **_pat(qa_w, 1024, 0))
    ka = _proj(xb, w_ka.astype(BF16), out_dtype=BF16, tabs=tab_k, half=A_HEAD_DIM // 8,
               seq_len=T, name="proj_ka", **_pat(kv_w, 1024, 0))
    n_idx = w_idx.shape[1]
    idx = _proj(xb, w_idx, out_dtype=F32, tabs=tab_idx, half=IDX_DIM // 8, seq_len=T, name="proj_idx",
                bm=512, bn=n_idx, slab_pat=(0,) * (qi_w // LANE) + (1,))
    rqk = _proj(xb, w_rqk, out_dtype=BF16, tabs=tab_r, mode="pair", seq_len=T, name="proj_ret_qk",
                bn=1024, slab_pat=(0, 1) * (_tile(2 * rk_w, 1024) // (2 * LANE)))
    pv = _proj(xb, w_pv, out_dtype=BF16, bn=768, seq_len=T, name="proj_v")

    ya = _dsa(qa.reshape(B, T, qa_w), ka.reshape(B, T, kv_w), pv.reshape(B, T, -1), idx.reshape(B, T, n_idx),
              B=B, T=T)
    yb = _retention(rqk, pv, ret_gn_g, B=B, T=T, heads=r_heads, v_blk0=kv_w // RET_VAL_DIM)
    w_out_b = w_out.astype(BF16)
    y = _proj(ya.reshape(N, qa_w), w_out_b[:qa_w], second=(yb, w_out_b[qa_w:]), out_dtype=F32, name="proj_out0")
    x1, x1b = _add_ln(x, y, ln1_g, ln1_b)
    hid = _swiglu_up(x1b, w1.astype(BF16), w3.astype(BF16))
    y = _mm_ksplit(hid, w2.astype(BF16), name="ffn_down")
    return _add_ln(x1, y, ln2_g, ln2_b)


def _pat(width, bn, p):
    bn = _tile(width, bn)
    return dict(bn=bn, slab_pat=(p,) * (bn // LANE))


def _odd_layer(x, xb, w_dq_dkv, q_norm_g, w_uq, kv_norm_g, w_ukv, w_out, ln1_g, ln1_b,
               router, we1, we3, we2, ln2_g, ln2_b, *, B, T):
    N, D = x.shape
    heads = D // MLA_V
    q_rank, kv_rank = q_norm_g.shape[0], kv_norm_g.shape[0]
    scale = (MLA_NOPE + MLA_ROPE) ** -0.5 * LOG2E
    cos_c, sin_c = _rope_cos_sin(T, MLA_ROPE, ROPE_THETA)
    tab_kr = _lane_tables(cos_c, sin_c, MLA_ROPE)
    tab_qr = _lane_tables(cos_c, sin_c, MLA_ROPE, scale)

    w_kr = w_dq_dkv[:, q_rank + kv_rank:]
    w_down = jnp.concatenate([w_dq_dkv[:, :q_rank + kv_rank], w_kr, w_kr], 1).astype(BF16)
    cq, ckv, kr = _mla_down(xb, w_down, q_norm_g, kv_norm_g, tab_kr, T=T, q_rank=q_rank, kv_rank=kv_rank)
    w_uq3 = w_uq.reshape(q_rank, heads, MLA_NOPE + MLA_ROPE)
    w_qn = w_uq3[:, :, :MLA_NOPE].reshape(q_rank, heads * MLA_NOPE).astype(BF16)
    w_qr = w_uq3[:, :, MLA_NOPE:].reshape(q_rank, heads * MLA_ROPE).astype(BF16)
    w_kv3 = w_ukv.reshape(kv_rank, heads, MLA_NOPE + MLA_V)
    w_kv = jnp.concatenate([w_kv3[:, :, :MLA_NOPE].reshape(kv_rank, heads * MLA_NOPE),
                            w_kv3[:, :, MLA_NOPE:].reshape(kv_rank, heads * MLA_V)], 1).astype(BF16)
    qn = _proj(cq, w_qn, out_dtype=BF16, scale=scale, seq_len=T, name="proj_q_nope")
    qr = _proj(cq, w_qr, out_dtype=BF16, tabs=tab_qr, half=MLA_ROPE // 2, seq_len=T, name="proj_q_rope",
               **_pat(heads * MLA_ROPE, 1024, 0))
    kv = _proj(ckv, w_kv, out_dtype=BF16, seq_len=T, name="proj_kv")
    att = _mla_attn(qn, qr, kv, kr, B=B, T=T, heads=heads)
    y = _proj(att, w_out.astype(BF16), out_dtype=F32, name="proj_out1")
    (x1,) = _add_ln(x, y, ln1_g, ln1_b, bf16_copy=False)

    E = router.shape[1]
    F = we1.shape[2]
    tm = _tile(N, 512)
    meta, cnt = _router(x1, router)
    counts = cnt[0, :E].astype(I32)
    padded = (counts + tm - 1) // tm * tm
    ends = jnp.cumsum(padded)
    starts = ends - padded
    i1, i2 = meta[:, 0].astype(I32), meta[:, 1].astype(I32)
    dest = jnp.stack([starts[i1] + meta[:, 4].astype(I32), starts[i2] + meta[:, 5].astype(I32)], 1).reshape(-1)
    n_rows = 2 * N + E * tm
    n_tiles = n_rows // tm
    n_valid = (ends[-1] // tm).astype(I32).reshape(1)
    tile_start = jnp.arange(n_tiles, dtype=I32) * tm
    tile_expert = jnp.minimum(jnp.sum(tile_start[:, None] >= ends[None, :], axis=1), E - 1).astype(I32)
    tile_expert = jnp.where(jnp.arange(n_tiles) < n_valid[0], tile_expert, tile_expert[jnp.maximum(n_valid[0] - 1, 0)])
    row_tok = jnp.zeros((n_rows,), I32).at[dest].set(jnp.repeat(jnp.arange(N, dtype=I32), 2))
    ys = _experts(x1, we1.astype(BF16), we3.astype(BF16), we2.astype(BF16), tile_expert, n_valid, row_tok, tm=tm)
    return _combine(x1, meta, ys, dest, ln2_g, ln2_b)


def kernel(x, l0_w_in, l0_ret_gn_g, l0_w_out, l0_ln1_g, l0_ln1_b, l0_ffn_w1, l0_ffn_w3, l0_ffn_w2, l0_ln2_g, l0_ln2_b, l1_w_dq_dkv, l1_q_norm_g, l1_w_uq, l1_kv_norm_g, l1_w_ukv, l1_w_out, l1_ln1_g, l1_ln1_b, l1_router, l1_moe_w1, l1_moe_w3, l1_moe_w2, l1_ln2_g, l1_ln2_b):
    B, T, D = x.shape
    x2 = x.reshape(B * T, D)
    h, hb = _even_layer(x2, x2.astype(BF16), l0_w_in, l0_ret_gn_g, l0_w_out, l0_ln1_g, l0_ln1_b,
                        l0_ffn_w1, l0_ffn_w3, l0_ffn_w2, l0_ln2_g, l0_ln2_b, B=B, T=T)
    out = _odd_layer(h, hb, l1_w_dq_dkv, l1_q_norm_g, l1_w_uq, l1_kv_norm_g, l1_w_ukv, l1_w_out,
                     l1_ln1_g, l1_ln1_b, l1_router, l1_moe_w1, l1_moe_w3, l1_moe_w2, l1_ln2_g, l1_ln2_b, B=B, T=T)
    return out.reshape(B, T, D)
```

```python
import functools

import jax
import jax.numpy as jnp
from jax import lax
from jax.experimental import pallas as pl
from jax.experimental.pallas import tpu as pltpu

F32 = jnp.float32
BF16 = jnp.bfloat16
I32 = jnp.int32

A_HEAD_DIM = 128
A_KV_HEADS = 4
IDX_HEADS = 16
IDX_DIM = 64
DSA_TOPK_MAX = 256
RET_KEY_DIM = 256
RET_VAL_DIM = 256
RET_CHUNK = 128
RET_THETA = 10000.0
MLA_V = 128
MLA_NOPE = 128
MLA_ROPE = 64
ROPE_THETA = 500000.0
Q_BLOCK = 128
LN_EPS = 1e-5
RMS_EPS = 1e-6
DEPTH = 2
ALPHA = (2.0 * DEPTH) ** 0.25

LANE = 128
V7X_VMEM_BYTES = 64 * 1024 * 1024
VMEM_LIMIT = V7X_VMEM_BYTES - 8 * 1024 * 1024
MASKED = -1e30
INT_MIN = -(2 ** 31)

NT_DIMS = (((1,), (1,)), ((), ()))
TN_DIMS = (((0,), (0,)), ((), ()))


def _tile(n, pref, mult=LANE):
    if n <= pref:
        return n
    t = (pref // mult) * mult
    while t > mult and n % t:
        t -= mult
    assert n % t == 0, (n, pref, mult)
    return t


def _params(*sem):
    return pltpu.CompilerParams(dimension_semantics=sem, vmem_limit_bytes=VMEM_LIMIT)


def _rope_cos_sin(T, rot_dim, theta):
    inv = theta ** (-jnp.arange(0, rot_dim, 2, dtype=F32) / rot_dim)
    ang = jnp.arange(T, dtype=F32)[:, None] * inv[None, :]
    return jnp.cos(ang), jnp.sin(ang)


def _lane_tables(cos, sin, head_dim, scale=1.0):
    T, half = cos.shape
    rest = head_dim - 2 * half
    zh = jnp.zeros((T, half), F32)
    c = jnp.concatenate([cos, cos, jnp.ones((T, rest), F32)], 1)
    sa = jnp.concatenate([-sin, zh, jnp.zeros((T, rest), F32)], 1)
    sb = jnp.concatenate([zh, sin, jnp.zeros((T, rest), F32)], 1)
    reps = LANE // head_dim
    return tuple(jnp.tile(t * scale, (1, reps)) for t in (c, sa, sb))


def _proj_kernel(*refs, slab_pat, mode, half, scale, with_tab, two_inputs):
    x_ref, w_ref = refs[:2]
    acc = jnp.dot(x_ref[...], w_ref[...], preferred_element_type=F32)
    refs = refs[2:]
    if two_inputs:
        acc = acc + jnp.dot(refs[0][...], refs[1][...], preferred_element_type=F32)
        refs = refs[2:]
    if with_tab:
        c_ref, sa_ref, sb_ref, o_ref = refs
    else:
        (o_ref,) = refs
    for s, p in enumerate(slab_pat):
        a = acc[:, s * LANE:(s + 1) * LANE]
        if p < 0:
            out = a if scale == 1.0 else a * scale
        else:
            c = c_ref[:, p * LANE:(p + 1) * LANE]
            sa = sa_ref[:, p * LANE:(p + 1) * LANE]
            if mode == "lane":
                sb = sb_ref[:, p * LANE:(p + 1) * LANE]
                out = a * c + pltpu.roll(a, LANE - half, 1) * sa + pltpu.roll(a, half, 1) * sb
            else:
                q = s ^ 1
                out = a * c + acc[:, q * LANE:(q + 1) * LANE] * sa
        o_ref[:, s * LANE:(s + 1) * LANE] = out.astype(o_ref.dtype)


def _proj(x, w, *, out_dtype, bm=1024, bn=1024, tabs=None, slab_pat=None, mode="lane", half=0,
          scale=1.0, seq_len=None, second=None, name="proj"):
    M, K = x.shape
    N = w.shape[1]
    bm = _tile(M, bm) if seq_len is None else _tile(seq_len, bm)
    bn = _tile(N, bn)
    if slab_pat is None:
        slab_pat = (-1,) * (bn // LANE)
    assert len(slab_pat) == bn // LANE
    in_specs = [pl.BlockSpec((bm, K), lambda i, j: (i, 0)),
                pl.BlockSpec((K, bn), lambda i, j: (0, j))]
    args = [x, w]
    if second is not None:
        x2, w2 = second
        in_specs += [pl.BlockSpec((bm, x2.shape[1]), lambda i, j: (i, 0)),
                     pl.BlockSpec((x2.shape[1], bn), lambda i, j: (0, j))]
        args += [x2, w2]
    if tabs is not None:
        tb = seq_len // bm
        tw = tabs[0].shape[1]
        in_specs += [pl.BlockSpec((bm, tw), lambda i, j: (i % tb, 0))] * 3
        args += list(tabs)
    kern = functools.partial(_proj_kernel, slab_pat=tuple(slab_pat), mode=mode, half=half,
                             scale=scale, with_tab=tabs is not None, two_inputs=second is not None)
    return pl.pallas_call(
        kern,
        grid=(M // bm, N // bn),
        in_specs=in_specs,
        out_specs=pl.BlockSpec((bm, bn), lambda i, j: (i, j)),
        out_shape=jax.ShapeDtypeStruct((M, N), out_dtype),
        compiler_params=_params("parallel", "parallel"),
        name=name,
    )(*args)


def _mm_ksplit_kernel(x_ref, w_ref, o_ref):
    part = jnp.dot(x_ref[...], w_ref[...], preferred_element_type=F32)

    @pl.when(pl.program_id(2) == 0)
    def _():
        o_ref[...] = part

    @pl.when(pl.program_id(2) > 0)
    def _():
        o_ref[...] += part


def _mm_ksplit(x, w, *, bm=1024, bn=1024, bk=2048, name="mm_ksplit"):
    M, K = x.shape
    N = w.shape[1]
    bm, bn, bk = _tile(M, bm), _tile(N, bn), _tile(K, bk)
    return pl.pallas_call(
        _mm_ksplit_kernel,
        grid=(M // bm, N // bn, K // bk),
        in_specs=[pl.BlockSpec((bm, bk), lambda i, j, k: (i, k)),
                  pl.BlockSpec((bk, bn), lambda i, j, k: (k, j))],
        out_specs=pl.BlockSpec((bm, bn), lambda i, j, k: (i, j)),
        out_shape=jax.ShapeDtypeStruct((M, N), F32),
        compiler_params=_params("parallel", "parallel", "arbitrary"),
        name=name,
    )(x, w)


def _swiglu_up_kernel(x_ref, w1_ref, w3_ref, o_ref):
    x = x_ref[...]
    a = jnp.dot(x, w1_ref[...], preferred_element_type=F32)
    b = jnp.dot(x, w3_ref[...], preferred_element_type=F32)
    o_ref[...] = (a * jax.nn.sigmoid(a) * b).astype(o_ref.dtype)


def _swiglu_up(x, w1, w3, *, bm=1024, bn=512):
    M, K = x.shape
    N = w1.shape[1]
    bm, bn = _tile(M, bm), _tile(N, bn)
    return pl.pallas_call(
        _swiglu_up_kernel,
        grid=(M // bm, N // bn),
        in_specs=[pl.BlockSpec((bm, K), lambda i, j: (i, 0)),
                  pl.BlockSpec((K, bn), lambda i, j: (0, j)),
                  pl.BlockSpec((K, bn), lambda i, j: (0, j))],
        out_specs=pl.BlockSpec((bm, bn), lambda i, j: (i, j)),
        out_shape=jax.ShapeDtypeStruct((M, N), BF16),
        compiler_params=_params("parallel", "parallel"),
        name="swiglu_up",
    )(x, w1, w3)


def _layer_norm_rows(z, g, b):
    mu = jnp.mean(z, axis=-1, keepdims=True)
    zc = z - mu
    var = jnp.mean(zc * zc, axis=-1, keepdims=True)
    return zc * lax.rsqrt(var + LN_EPS) * g + b


def _add_ln_kernel(x_ref, y_ref, g_ref, b_ref, of_ref, *maybe_ob_ref):
    out = _layer_norm_rows(ALPHA * x_ref[...] + y_ref[...], g_ref[...], b_ref[...])
    of_ref[...] = out
    for ob_ref in maybe_ob_ref:
        ob_ref[...] = out.astype(ob_ref.dtype)


def _add_ln(x, y, g, b, *, bf16_copy=True, bm=256):
    M, D = x.shape
    bm = _tile(M, bm, 8)
    row = pl.BlockSpec((bm, D), lambda i: (i, 0))
    vec = pl.BlockSpec((1, D), lambda i: (0, 0))
    out_dtypes = (F32, BF16) if bf16_copy else (F32,)
    return pl.pallas_call(
        _add_ln_kernel,
        grid=(M // bm,),
        in_specs=[row, row, vec, vec],
        out_specs=[row] * len(out_dtypes),
        out_shape=[jax.ShapeDtypeStruct((M, D), dt) for dt in out_dtypes],
        compiler_params=_params("parallel"),
        name="add_ln",
    )(x, y, g.reshape(1, D), b.reshape(1, D))


LOG2E = 1.4426950408889634
SOFTMAX_ROW_BLOCK = 512


def _online_softmax(s, m, l):
    rows = s.shape[0]
    rb = min(SOFTMAX_ROW_BLOCK, rows)
    ms, ls, scales, ps = [], [], [], []
    for r in range(rows // rb):
        sl = slice(r * rb, (r + 1) * rb)
        s_r = s[sl]
        m_r = jnp.maximum(m[sl], jnp.max(s_r, axis=1, keepdims=True))
        a_r = jnp.exp2(m[sl] - m_r)
        p_r = jnp.exp2((s_r - m_r).astype(BF16))
        ms.append(m_r)
        ls.append(a_r * l[sl] + jnp.sum(p_r.astype(F32), axis=1, keepdims=True))
        scales.append(a_r)
        ps.append(p_r)
    cat = lambda parts: jnp.concatenate(parts, axis=0)
    return cat(ms), cat(ls), cat(scales), cat(ps)


def _dsa_kernel(q_ref, k_ref, v_ref, iq_ref, ik_ref, o_ref, keys_ref, bias_ref, *, topk, ck, n_rep, idx_bits):
    blk = pl.program_id(1)
    n_chunks = (blk * Q_BLOCK + Q_BLOCK + ck - 1) // ck
    row = lax.broadcasted_iota(I32, (Q_BLOCK, ck), 0) + blk * Q_BLOCK
    lane = lax.broadcasted_iota(I32, (Q_BLOCK, ck), 1)
    w_off = IDX_HEADS * IDX_DIM + IDX_DIM
    iq = iq_ref[0]
    wi = iq[:, w_off:w_off + IDX_HEADS]
    q_idx = jnp.concatenate([iq[:, h * IDX_DIM:(h + 1) * IDX_DIM].astype(BF16) for h in range(IDX_HEADS)], axis=0)

    def score_body(c, carry):
        off = pl.multiple_of(c * ck, ck)
        kc = ik_ref[0, pl.ds(off, ck), :][:, :IDX_DIM].astype(BF16)
        lg = lax.dot_general(q_idx, kc, NT_DIMS, preferred_element_type=F32)
        s = jnp.zeros((Q_BLOCK, ck), F32)
        for h in range(IDX_HEADS):
            s = s + wi[:, h:h + 1] * jnp.maximum(lg[h * Q_BLOCK:(h + 1) * Q_BLOCK], 0.0)
        bits = pltpu.bitcast(s, I32)
        key = bits ^ ((bits >> 31) & 0x7FFFFFFF)
        keys_ref[c] = jnp.where(lane + off <= row, key, INT_MIN)
        return carry

    lax.fori_loop(0, n_chunks, score_body, 0)

    def count(indicator):
        def body(c, acc):
            part = indicator(keys_ref[c], lane + c * ck)
            for j in range(ck // LANE):
                acc = acc + part[:, j * LANE:(j + 1) * LANE]
            return acc
        acc = lax.fori_loop(0, n_chunks, body, jnp.zeros((Q_BLOCK, LANE), F32))
        return jnp.sum(acc, axis=1, keepdims=True)

    kf = float(topk)
    ok = count(lambda kc, idx: jnp.where(kc >= 0, 1.0, 0.0)) >= kf
    thr = jnp.where(ok, 0, INT_MIN).astype(I32)

    def bit_body(i, thr):
        cand = thr + lax.shift_left(jnp.int32(1), 30 - i)
        ok = count(lambda kc, idx: jnp.where(kc >= cand, 1.0, 0.0)) >= kf
        return jnp.where(ok, cand, thr)

    thr = lax.fori_loop(0, 31, bit_body, thr)

    n_gt = count(lambda kc, idx: jnp.where(kc > thr, 1.0, 0.0))
    n_ge = count(lambda kc, idx: jnp.where(kc >= thr, 1.0, 0.0))
    need = kf - n_gt
    has_thr = thr > INT_MIN
    surplus = jnp.where(has_thr, n_ge - n_gt - need, 0.0)

    def tie_search():
        def tie_body(i, last):
            cand = last + lax.shift_left(jnp.int32(1), idx_bits - 1 - i)
            ok = count(lambda kc, idx: jnp.where(kc == thr, jnp.where(idx < cand, 1.0, 0.0), 0.0)) < need
            return jnp.where(ok, cand, last)
        return lax.fori_loop(0, idx_bits, tie_body, jnp.zeros((Q_BLOCK, 1), I32))

    last = lax.cond(jnp.max(surplus) > 0.0, tie_search, lambda: jnp.full((Q_BLOCK, 1), 2 ** idx_bits, I32))
    last = jnp.where(has_thr, last, -1)

    def bias_body(c, carry):
        kc = keys_ref[c]
        tie_bias = jnp.where(lane + c * ck <= last, 0.0, MASKED)
        bias_ref[c] = jnp.where(kc == thr, tie_bias, jnp.where(kc > thr, 0.0, MASKED))
        return carry

    lax.fori_loop(0, n_chunks, bias_body, 0)

    q = q_ref[0]
    rows = n_rep * Q_BLOCK
    qgs = [jnp.concatenate([q[:, (g * n_rep + r) * A_HEAD_DIM:(g * n_rep + r + 1) * A_HEAD_DIM]
                            for r in range(n_rep)], axis=0) for g in range(A_KV_HEADS)]

    def att_body(c, carry):
        off = pl.multiple_of(c * ck, ck)
        b = bias_ref[c]
        bias = jnp.concatenate([b] * n_rep, axis=0)
        out = []
        for g in range(A_KV_HEADS):
            m, l, acc = carry[g]
            kc = k_ref[0, pl.ds(off, ck), g * A_HEAD_DIM:(g + 1) * A_HEAD_DIM]
            vc = v_ref[0, pl.ds(off, ck), g * A_HEAD_DIM:(g + 1) * A_HEAD_DIM]
            s = lax.dot_general(qgs[g], kc, NT_DIMS, preferred_element_type=F32) + bias
            m, l, a, p = _online_softmax(s, m, l)
            out.append((m, l, a * acc + jnp.dot(p, vc, preferred_element_type=F32)))
        return tuple(out)

    one = (jnp.full((rows, 1), MASKED, F32), jnp.zeros((rows, 1), F32), jnp.zeros((rows, A_HEAD_DIM), F32))
    carry = lax.fori_loop(0, n_chunks, att_body, (one,) * A_KV_HEADS)
    for g in range(A_KV_HEADS):
        _, l, acc = carry[g]
        o = acc / l
        for r in range(n_rep):
            col = (g * n_rep + r) * A_HEAD_DIM
            o_ref[0, :, col:col + A_HEAD_DIM] = o[r * Q_BLOCK:(r + 1) * Q_BLOCK].astype(o_ref.dtype)


def _dsa(q, k, v, idx, *, B, T):
    a_heads = q.shape[-1] // A_HEAD_DIM
    n_rep = a_heads // A_KV_HEADS
    topk = min(DSA_TOPK_MAX, T // 4)
    ck = _tile(T, 512)
    idx_w = idx.shape[-1]
    kv_w = k.shape[-1]
    kern = functools.partial(_dsa_kernel, topk=topk, ck=ck, n_rep=n_rep, idx_bits=max(1, (T - 1).bit_length()))
    return pl.pallas_call(
        kern,
        grid=(B, T // Q_BLOCK),
        in_specs=[pl.BlockSpec((1, Q_BLOCK, q.shape[-1]), lambda b, i: (b, i, 0)),
                  pl.BlockSpec((1, T, kv_w), lambda b, i: (b, 0, 0)),
                  pl.BlockSpec((1, T, kv_w), lambda b, i: (b, 0, 0)),
                  pl.BlockSpec((1, Q_BLOCK, idx_w), lambda b, i: (b, i, 0)),
                  pl.BlockSpec((1, T, LANE), lambda b, i: (b, 0, IDX_HEADS * IDX_DIM // LANE))],
        out_specs=pl.BlockSpec((1, Q_BLOCK, q.shape[-1]), lambda b, i: (b, i, 0)),
        out_shape=jax.ShapeDtypeStruct(q.shape, BF16),
        scratch_shapes=[pltpu.VMEM((T // ck, Q_BLOCK, ck), I32), pltpu.VMEM((T // ck, Q_BLOCK, ck), F32)],
        compiler_params=_params("parallel", "parallel"),
        name="dsa",
    )(q, k, v, idx, idx)


def _retention_kernel(q_ref, k_ref, v_ref, g_ref, gn_ref, din_ref, qd_ref, kd_ref, cd_ref, o_ref, state_ref, *, n_sub):
    @pl.when(pl.program_id(2) == 0)
    def _():
        state_ref[...] = jnp.zeros_like(state_ref)

    din = din_ref[0]
    qd = qd_ref[0]
    kd = kd_ref[0]
    cd = cd_ref[0]
    gn = gn_ref[...]
    for s in range(n_sub):
        sl = pl.ds(s * RET_CHUNK, RET_CHUNK)
        qc = q_ref[sl, :]
        kc = k_ref[sl, :]
        vc = v_ref[sl, :]
        st = state_ref[...]
        inner = lax.dot_general(qc, kc, NT_DIMS, preferred_element_type=F32) * din
        o = (jnp.dot(inner.astype(BF16), vc, preferred_element_type=F32)
             + jnp.dot(qc, st.astype(BF16), preferred_element_type=F32) * qd)
        vk = (vc.astype(F32) * kd).astype(BF16)
        state_ref[...] = st * cd + lax.dot_general(kc, vk, TN_DIMS, preferred_element_type=F32)
        mu = jnp.mean(o, axis=-1, keepdims=True)
        oc = o - mu
        var = jnp.mean(oc * oc, axis=-1, keepdims=True)
        gate = g_ref[sl, :].astype(F32)
        o_ref[sl, :] = (gate * jax.nn.sigmoid(gate) * (oc * lax.rsqrt(var + LN_EPS) * gn)).astype(o_ref.dtype)


def _retention(qk, pv, gn_g, *, B, T, heads, v_blk0):
    N = qk.shape[0]
    C = RET_CHUNK
    rb = _tile(T, 512)
    n_sub = rb // C
    nr = T // rb
    log_gamma = jnp.log(1.0 - 2.0 ** (-5.0 - jnp.arange(heads, dtype=F32)))
    pos = jnp.arange(C, dtype=F32)
    diff = pos[:, None] - pos[None, :]
    din = jnp.exp(jnp.where(diff[None] >= 0, log_gamma[:, None, None] * diff[None], -jnp.inf))
    qd = jnp.exp(log_gamma[:, None] * (pos[None] + 1.0))[:, :, None]
    kd = jnp.exp(log_gamma[:, None] * (C - 1.0 - pos[None]))[:, :, None]
    cd = jnp.exp(log_gamma * C)[:, None, None]
    W = RET_VAL_DIM
    blk = lambda off: pl.BlockSpec((rb, W), lambda b, h, r: (b * nr + r, off + h))
    per_head = lambda shape: pl.BlockSpec((1,) + shape, lambda b, h, r: (h, 0, 0))
    return pl.pallas_call(
        functools.partial(_retention_kernel, n_sub=n_sub),
        grid=(B, heads, nr),
        in_specs=[blk(0), blk(heads), blk(v_blk0), blk(v_blk0 + heads),
                  pl.BlockSpec((1, W), lambda b, h, r: (0, h)),
                  per_head((C, C)), per_head((C, 1)), per_head((C, 1)), per_head((1, 1))],
        out_specs=pl.BlockSpec((rb, W), lambda b, h, r: (b * nr + r, h)),
        out_shape=jax.ShapeDtypeStruct((N, heads * W), BF16),
        scratch_shapes=[pltpu.VMEM((RET_KEY_DIM, RET_VAL_DIM), F32)],
        compiler_params=_params("parallel", "parallel", "arbitrary"),
        name="retention",
    )(qk, qk, pv, pv, gn_g.reshape(1, heads * W), din, qd, kd, cd)


def _mla_down_kernel(x_ref, w_ref, qg_ref, kvg_ref, c_ref, sa_ref, sb_ref, cq_ref, ckv_ref, kr_ref, *, q_rank, kv_rank):
    acc = jnp.dot(x_ref[...], w_ref[...], preferred_element_type=F32)

    def rms(a, g):
        return a * lax.rsqrt(jnp.mean(a * a, axis=-1, keepdims=True) + RMS_EPS) * g

    cq_ref[...] = rms(acc[:, :q_rank], qg_ref[...]).astype(cq_ref.dtype)
    ckv_ref[...] = rms(acc[:, q_rank:q_rank + kv_rank], kvg_ref[...]).astype(ckv_ref.dtype)
    kr = acc[:, q_rank + kv_rank:]
    half = MLA_ROPE // 2
    kr = kr * c_ref[...] + pltpu.roll(kr, LANE - half, 1) * sa_ref[...] + pltpu.roll(kr, half, 1) * sb_ref[...]
    kr_ref[...] = kr.astype(kr_ref.dtype)


def _mla_down(x, w, q_g, kv_g, tabs, *, T, q_rank, kv_rank, bm=512):
    M, K = x.shape
    Nw = w.shape[1]
    bm = _tile(T, bm)
    tb = T // bm
    row = lambda n: pl.BlockSpec((bm, n), lambda i: (i, 0))
    tab = pl.BlockSpec((bm, LANE), lambda i: (i % tb, 0))
    return pl.pallas_call(
        functools.partial(_mla_down_kernel, q_rank=q_rank, kv_rank=kv_rank),
        grid=(M // bm,),
        in_specs=[row(K), pl.BlockSpec((K, Nw), lambda i: (0, 0)),
                  pl.BlockSpec((1, q_rank), lambda i: (0, 0)), pl.BlockSpec((1, kv_rank), lambda i: (0, 0)),
                  tab, tab, tab],
        out_specs=[row(q_rank), row(kv_rank), row(LANE)],
        out_shape=[jax.ShapeDtypeStruct((M, q_rank), BF16), jax.ShapeDtypeStruct((M, kv_rank), BF16),
                   jax.ShapeDtypeStruct((M, LANE), BF16)],
        compiler_params=_params("parallel"),
        name="mla_down",
    )(x, w, q_g.reshape(1, q_rank), kv_g.reshape(1, kv_rank), *tabs)


MLA_HEADS_PER_STEP = 4


def _mla_attn_kernel(qn_ref, qr_ref, kn_ref, kr_ref, v_ref, o_ref, *, tq):
    i = pl.program_id(2)
    lane = lax.broadcasted_iota(I32, (tq, LANE), 1)
    qs = []
    for j in range(MLA_HEADS_PER_STEP):
        lo = (j % 2) * MLA_ROPE
        own = jnp.where((lane >= lo) & (lane < lo + MLA_ROPE), 1.0, 0.0)
        pair = qr_ref[:, (j // 2) * LANE:(j // 2 + 1) * LANE].astype(F32)
        qr = (pair * own).astype(BF16)
        qs.append(jnp.concatenate([qn_ref[:, j * LANE:(j + 1) * LANE], qr], axis=1))

    def step(c, carry, masked):
        off = pl.multiple_of(c * tq, tq)
        kr = kr_ref[pl.ds(off, tq), :]
        out = []
        for j in range(MLA_HEADS_PER_STEP):
            m, l, acc = carry[j]
            k = jnp.concatenate([kn_ref[pl.ds(off, tq), j * LANE:(j + 1) * LANE], kr], axis=1)
            s = lax.dot_general(qs[j], k, NT_DIMS, preferred_element_type=F32)
            if masked:
                r_io = lax.broadcasted_iota(I32, (tq, tq), 0)
                c_io = lax.broadcasted_iota(I32, (tq, tq), 1)
                s = jnp.where(c_io <= r_io, s, MASKED)
            m, l, a, p = _online_softmax(s, m, l)
            v = v_ref[pl.ds(off, tq), j * MLA_V:(j + 1) * MLA_V]
            out.append((m, l, a * acc + jnp.dot(p, v, preferred_element_type=F32)))
        return tuple(out)

    one = (jnp.full((tq, 1), MASKED, F32), jnp.zeros((tq, 1), F32), jnp.zeros((tq, MLA_V), F32))
    carry = lax.fori_loop(0, i, functools.partial(step, masked=False), (one,) * MLA_HEADS_PER_STEP)
    carry = step(i, carry, True)
    for j in range(MLA_HEADS_PER_STEP):
        _, l, acc = carry[j]
        o_ref[:, j * MLA_V:(j + 1) * MLA_V] = (acc / l).astype(o_ref.dtype)


def _mla_attn(qn, qr, kv, kr, *, B, T, heads):
    N = qn.shape[0]
    tq = _tile(T, 512)
    nq = T // tq
    hp = MLA_HEADS_PER_STEP
    w = hp * LANE
    return pl.pallas_call(
        functools.partial(_mla_attn_kernel, tq=tq),
        grid=(B, heads // hp, nq),
        in_specs=[pl.BlockSpec((tq, w), lambda b, h, i: (b * nq + i, h)),
                  pl.BlockSpec((tq, hp * MLA_ROPE), lambda b, h, i: (b * nq + i, h)),
                  pl.BlockSpec((T, w), lambda b, h, i: (b, h)),
                  pl.BlockSpec((T, LANE), lambda b, h, i: (b, 0)),
                  pl.BlockSpec((T, w), lambda b, h, i: (b, heads // hp + h))],
        out_specs=pl.BlockSpec((tq, w), lambda b, h, i: (b * nq + i, h)),
        out_shape=jax.ShapeDtypeStruct((N, heads * MLA_V), BF16),
        compiler_params=_params("parallel", "parallel", "parallel"),
        name="mla_attn",
    )(qn, qr, kv, kr, kv)


def _router_kernel(x_ref, r_ref, meta_ref, cnt_ref, carry_ref, *, n_exp):
    @pl.when(pl.program_id(0) == 0)
    def _():
        carry_ref[...] = jnp.zeros_like(carry_ref)

    bm = x_ref.shape[0]
    logits = jnp.dot(x_ref[...], r_ref[...], preferred_element_type=F32, precision=lax.Precision.HIGHEST)
    lane = lax.broadcasted_iota(I32, (bm, LANE), 1).astype(F32)
    logits = jnp.where(lane < n_exp, logits, -jnp.inf)
    m1 = jnp.max(logits, axis=1, keepdims=True)
    i1 = jnp.min(jnp.where(logits == m1, lane, float(LANE)), axis=1, keepdims=True)
    rest = jnp.where(lane == i1, -jnp.inf, logits)
    m2 = jnp.max(rest, axis=1, keepdims=True)
    i2 = jnp.min(jnp.where(rest == m2, lane, float(LANE)), axis=1, keepdims=True)
    e = jnp.exp(m2 - m1)
    g1 = 1.0 / (1.0 + e)
    g2 = e / (1.0 + e)
    sel = jnp.where(lane == i1, 1.0, jnp.where(lane == i2, 1.0, 0.0))
    r_io = lax.broadcasted_iota(I32, (bm, bm), 0)
    c_io = lax.broadcasted_iota(I32, (bm, bm), 1)
    below = jnp.where(c_io < r_io, 1.0, 0.0).astype(BF16)
    carry = carry_ref[0:1, :]
    rank = jnp.dot(below, sel.astype(BF16), preferred_element_type=F32) + carry
    r1 = jnp.sum(jnp.where(lane == i1, rank, 0.0), axis=1, keepdims=True)
    r2 = jnp.sum(jnp.where(lane == i2, rank, 0.0), axis=1, keepdims=True)
    meta = jnp.where(lane == 0, i1, 0.0)
    meta = jnp.where(lane == 1, i2, meta)
    meta = jnp.where(lane == 2, g1, meta)
    meta = jnp.where(lane == 3, g2, meta)
    meta = jnp.where(lane == 4, r1, meta)
    meta = jnp.where(lane == 5, r2, meta)
    meta_ref[...] = meta
    total = carry + jnp.sum(sel, axis=0, keepdims=True)
    carry_ref[...] = jnp.broadcast_to(total, carry_ref.shape)
    cnt_ref[...] = jnp.broadcast_to(total, cnt_ref.shape)


def _router(x, router, *, bm=512):
    M, D = x.shape
    n_exp = router.shape[1]
    bm = _tile(M, bm)
    r_pad = jnp.zeros((D, LANE), F32).at[:, :n_exp].set(router)
    return pl.pallas_call(
        functools.partial(_router_kernel, n_exp=n_exp),
        grid=(M // bm,),
        in_specs=[pl.BlockSpec((bm, D), lambda i: (i, 0)), pl.BlockSpec((D, LANE), lambda i: (0, 0))],
        out_specs=[pl.BlockSpec((bm, LANE), lambda i: (i, 0)), pl.BlockSpec((8, LANE), lambda i: (0, 0))],
        out_shape=[jax.ShapeDtypeStruct((M, LANE), F32), jax.ShapeDtypeStruct((8, LANE), F32)],
        scratch_shapes=[pltpu.VMEM((8, LANE), F32)],
        compiler_params=_params("arbitrary"),
        name="router",
    )(x, r_pad)


def _moe_gather_kernel(tok_ref, nv_ref, x_hbm, o_ref, stage_ref, sem, *, tm):
    r = pl.program_id(0)
    nv = nv_ref[0]

    def row_copy(j, tok):
        return pltpu.make_async_copy(x_hbm.at[pl.ds(tok, 1)], stage_ref.at[pl.ds(j, 1)], sem)

    def gather_start(tile):
        def body(j, carry):
            row_copy(j, tok_ref[tile * tm + j]).start()
            return carry
        lax.fori_loop(0, tm, body, 0)

    def gather_wait():
        def body(j, carry):
            row_copy(j, 0).wait()
            return carry
        lax.fori_loop(0, tm, body, 0)

    @pl.when(r < nv)
    def _():
        @pl.when(r == 0)
        def _():
            gather_start(0)

        gather_wait()
        o_ref[...] = stage_ref[...].astype(BF16)

        @pl.when(r + 1 < nv)
        def _():
            gather_start(r + 1)


def _moe_up_kernel(te_ref, nv_ref, xs_ref, w1_ref, w3_ref, o_ref, wb1_ref, wb3_ref):
    r = pl.program_id(1)

    @pl.when(r < nv_ref[0])
    def _():
        @pl.when((r == 0) | (te_ref[r] != te_ref[jnp.maximum(r - 1, 0)]))
        def _():
            wb1_ref[...] = w1_ref[0].astype(BF16)
            wb3_ref[...] = w3_ref[0].astype(BF16)

        xb = xs_ref[...]
        a = jnp.dot(xb, wb1_ref[...], preferred_element_type=F32)
        b = jnp.dot(xb, wb3_ref[...], preferred_element_type=F32)
        o_ref[...] = (a * jax.nn.sigmoid(a) * b).astype(o_ref.dtype)


def _moe_down_kernel(te_ref, nv_ref, h_ref, w2_ref, o_ref, wb_ref):
    r = pl.program_id(1)

    @pl.when(r < nv_ref[0])
    def _():
        @pl.when((r == 0) | (te_ref[r] != te_ref[jnp.maximum(r - 1, 0)]))
        def _():
            wb_ref[...] = w2_ref[0].astype(BF16)

        o_ref[...] = jnp.dot(h_ref[...], wb_ref[...], preferred_element_type=F32)


def _experts(x, w1, w3, w2, tile_expert, n_valid, row_tok, *, tm, tf=512, tn=512):
    D = x.shape[1]
    P = row_tok.shape[0]
    E, _, F = w1.shape
    tf, tn = _tile(F, tf), _tile(D, tn)
    n_tiles = P // tm
    row = lambda r, nv: jnp.minimum(r, nv[0] - 1)

    xs = pl.pallas_call(
        functools.partial(_moe_gather_kernel, tm=tm),
        grid_spec=pltpu.PrefetchScalarGridSpec(
            num_scalar_prefetch=2, grid=(n_tiles,),
            in_specs=[pl.BlockSpec(memory_space=pl.ANY)],
            out_specs=pl.BlockSpec((tm, D), lambda r, tok, nv: (row(r, nv), 0)),
            scratch_shapes=[pltpu.VMEM((tm, D), F32), pltpu.SemaphoreType.DMA(())]),
        out_shape=jax.ShapeDtypeStruct((P, D), BF16),
        compiler_params=_params("arbitrary"),
        name="moe_gather",
    )(row_tok, n_valid, x)

    hid = pl.pallas_call(
        _moe_up_kernel,
        grid_spec=pltpu.PrefetchScalarGridSpec(
            num_scalar_prefetch=2, grid=(F // tf, n_tiles),
            in_specs=[pl.BlockSpec((tm, D), lambda f, r, te, nv: (row(r, nv), 0)),
                      pl.BlockSpec((1, D, tf), lambda f, r, te, nv: (te[r], 0, f)),
                      pl.BlockSpec((1, D, tf), lambda f, r, te, nv: (te[r], 0, f))],
            out_specs=pl.BlockSpec((tm, tf), lambda f, r, te, nv: (row(r, nv), f)),
            scratch_shapes=[pltpu.VMEM((D, tf), BF16), pltpu.VMEM((D, tf), BF16)]),
        out_shape=jax.ShapeDtypeStruct((P, F), BF16),
        compiler_params=_params("arbitrary", "arbitrary"),
        name="moe_up",
    )(tile_expert, n_valid, xs, w1, w3)

    return pl.pallas_call(
        _moe_down_kernel,
        grid_spec=pltpu.PrefetchScalarGridSpec(
            num_scalar_prefetch=2, grid=(D // tn, n_tiles),
            in_specs=[pl.BlockSpec((tm, F), lambda n, r, te, nv: (row(r, nv), 0)),
                      pl.BlockSpec((1, F, tn), lambda n, r, te, nv: (te[r], 0, n))],
            out_specs=pl.BlockSpec((tm, tn), lambda n, r, te, nv: (row(r, nv), n)),
            scratch_shapes=[pltpu.VMEM((F, tn), BF16)]),
        out_shape=jax.ShapeDtypeStruct((P, D), F32),
        compiler_params=_params("arbitrary", "arbitrary"),
        name="moe_down",
    )(tile_expert, n_valid, hid, w2)


def _combine_kernel(dest_ref, x_ref, meta_ref, g_ref, b_ref, ys_hbm, o_ref, buf_ref, sem):
    bm = x_ref.shape[0]
    i = pl.program_id(0)
    n = pl.num_programs(0)

    def copy(slot, j, k, d):
        return pltpu.make_async_copy(ys_hbm.at[pl.ds(d, 1)], buf_ref.at[slot, k, pl.ds(j, 1)], sem.at[slot, k])

    def gather_start(blk):
        slot = blk % 2

        def body(j, carry):
            for k in range(2):
                copy(slot, j, k, dest_ref[2 * (blk * bm + j) + k]).start()
            return carry
        lax.fori_loop(0, bm, body, 0)

    @pl.when(i == 0)
    def _():
        gather_start(0)

    @pl.when(i + 1 < n)
    def _():
        gather_start(i + 1)

    slot = i % 2

    def wait(j, carry):
        for k in range(2):
            copy(slot, j, k, 0).wait()
        return carry

    lax.fori_loop(0, bm, wait, 0)
    meta = meta_ref[...]
    y = meta[:, 2:3] * buf_ref[slot, 0] + meta[:, 3:4] * buf_ref[slot, 1]
    o_ref[...] = _layer_norm_rows(ALPHA * x_ref[...] + y, g_ref[...], b_ref[...])


def _combine(x, meta, ys, dest, g, b, *, bm=256):
    M, D = x.shape
    bm = _tile(M, bm, 8)
    row = lambda n: pl.BlockSpec((bm, n), lambda i, d: (i, 0))
    vec = pl.BlockSpec((1, D), lambda i, d: (0, 0))
    return pl.pallas_call(
        _combine_kernel,
        grid_spec=pltpu.PrefetchScalarGridSpec(
            num_scalar_prefetch=1, grid=(M // bm,),
            in_specs=[row(D), row(LANE), vec, vec, pl.BlockSpec(memory_space=pl.ANY)],
            out_specs=row(D),
            scratch_shapes=[pltpu.VMEM((2, 2, bm, D), F32), pltpu.SemaphoreType.DMA((2, 2))]),
        out_shape=jax.ShapeDtypeStruct((M, D), F32),
        compiler_params=_params("arbitrary"),
        name="moe_combine",
    )(dest, x, meta, g.reshape(1, D), b.reshape(1, D), ys)


def _even_layer(x, xb, w_in, ret_gn_g, w_out, ln1_g, ln1_b, w1, w3, w2, ln2_g, ln2_b, *, B, T):
    N, D = x.shape
    a_heads = D // 2 // A_HEAD_DIM
    r_heads = D // 2 // RET_VAL_DIM
    qa_w, kv_w = a_heads * A_HEAD_DIM, A_KV_HEADS * A_HEAD_DIM
    qi_w = IDX_HEADS * IDX_DIM
    rk_w, rv_w = r_heads * RET_KEY_DIM, r_heads * RET_VAL_DIM
    sizes = (qa_w, kv_w, kv_w, qi_w, IDX_DIM, IDX_HEADS, rk_w, rk_w, rv_w, rv_w)
    offs = [0]
    for s in sizes:
        offs.append(offs[-1] + s)
    col = lambda a, b_: w_in[:, offs[a]:offs[b_]]
    w_qa, w_ka, w_va = col(0, 1), col(1, 2), col(2, 3)
    w_qi, w_ki, w_wi = col(3, 4), col(4, 5), col(5, 6)
    w_qb, w_kb, w_vb, w_gb = col(6, 7), col(7, 8), col(8, 9), col(9, 10)

    cos_a, sin_a = _rope_cos_sin(T, A_HEAD_DIM // 4, ROPE_THETA)
    tab_q = _lane_tables(cos_a, sin_a, A_HEAD_DIM, A_HEAD_DIM ** -0.5 * LOG2E)
    tab_k = _lane_tables(cos_a, sin_a, A_HEAD_DIM)
    cos_i, sin_i = _rope_cos_sin(T, IDX_DIM // 4, ROPE_THETA)
    tab_i = _lane_tables(cos_i, sin_i, IDX_DIM)
    pass_c = jnp.ones((T, LANE - IDX_DIM), F32)
    pass_s = jnp.zeros((T, LANE - IDX_DIM), F32)
    tab_idx = tuple(jnp.concatenate([t, t[:, :IDX_DIM], p], 1)
                    for t, p in zip(tab_i, (pass_c, pass_s, pass_s)))
    inv = 1.0 / (RET_THETA ** jnp.linspace(0.0, 1.0, RET_KEY_DIM // 2, dtype=F32))
    ang = jnp.arange(T, dtype=F32)[:, None] * inv[None, :]
    cos_r, sin_r = jnp.cos(ang), jnp.sin(ang)
    sin_pair = jnp.concatenate([-sin_r, sin_r], 1)
    tab_r = (jnp.concatenate([cos_r, cos_r], 1), sin_pair, sin_pair)

    idx_pad = LANE - IDX_DIM - IDX_HEADS
    w_idx = jnp.concatenate([w_qi, w_ki, w_wi * (IDX_DIM ** -0.5 * IDX_HEADS ** -0.5),
                             jnp.zeros((D, idx_pad), F32)], 1).astype(BF16)
    w_rqk = jnp.concatenate([w_qb, w_kb * RET_KEY_DIM ** -0.5], 1).astype(BF16)
    w_pv = jnp.concatenate([w_va, w_vb, w_gb], 1).astype(BF16)
    qa = _proj(xb, w_qa.astype(BF16), out_dtype=BF16, tabs=tab_q, half=A_HEAD_DIM // 8,
               seq_len=T, name="proj_qa", **_pat(qa_w, 1024, 0))
    ka = _proj(xb, w_ka.astype(BF16), out_dtype=BF16, tabs=tab_k, half=A_HEAD_DIM // 8,
               seq_len=T, name="proj_ka", **_pat(kv_w, 1024, 0))
    n_idx = w_idx.shape[1]
    idx = _proj(xb, w_idx, out_dtype=F32, tabs=tab_idx, half=IDX_DIM // 8, seq_len=T, name="proj_idx",
                bm=512, bn=n_idx, slab_pat=(0,) * (qi_w // LANE) + (1,))
    rqk = _proj(xb, w_rqk, out_dtype=BF16, tabs=tab_r, mode="pair", seq_len=T, name="proj_ret_qk",
                bn=1024, slab_pat=(0, 1) * (_tile(2 * rk_w, 1024) // (2 * LANE)))
    pv = _proj(xb, w_pv, out_dtype=BF16, bn=768, seq_len=T, name="proj_v")

    ya = _dsa(qa.reshape(B, T, qa_w), ka.reshape(B, T, kv_w), pv.reshape(B, T, -1), idx.reshape(B, T, n_idx),
              B=B, T=T)
    yb = _retention(rqk, pv, ret_gn_g, B=B, T=T, heads=r_heads, v_blk0=kv_w // RET_VAL_DIM)
    w_out_b = w_out.astype(BF16)
    y = _proj(ya.reshape(N, qa_w), w_out_b[:qa_w], second=(yb, w_out_b[qa_w:]), out_dtype=F32, name="proj_out0")
    x1, x1b = _add_ln(x, y, ln1_g, ln1_b)
    hid = _swiglu_up(x1b, w1.astype(BF16), w3.astype(BF16))
    y = _mm_ksplit(hid, w2.astype(BF16), name="ffn_down")
    return _add_ln(x1, y, ln2_g, ln2_b)


def _pat(width, bn, p):
    bn = _tile(width, bn)
    return dict(bn=bn, slab_pat=(p,) * (bn // LANE))


def _odd_layer(x, xb, w_dq_dkv, q_norm_g, w_uq, kv_norm_g, w_ukv, w_out, ln1_g, ln1_b,
               router, we1, we3, we2, ln2_g, ln2_b, *, B, T):
    N, D = x.shape
    heads = D // MLA_V
    q_rank, kv_rank = q_norm_g.shape[0], kv_norm_g.shape[0]
    scale = (MLA_NOPE + MLA_ROPE) ** -0.5 * LOG2E
    cos_c, sin_c = _rope_cos_sin(T, MLA_ROPE, ROPE_THETA)
    tab_kr = _lane_tables(cos_c, sin_c, MLA_ROPE)
    tab_qr = _lane_tables(cos_c, sin_c, MLA_ROPE, scale)

    w_kr = w_dq_dkv[:, q_rank + kv_rank:]
    w_down = jnp.concatenate([w_dq_dkv[:, :q_rank + kv_rank], w_kr, w_kr], 1).astype(BF16)
    cq, ckv, kr = _mla_down(xb, w_down, q_norm_g, kv_norm_g, tab_kr, T=T, q_rank=q_rank, kv_rank=kv_rank)
    w_uq3 = w_uq.reshape(q_rank, heads, MLA_NOPE + MLA_ROPE)
    w_qn = w_uq3[:, :, :MLA_NOPE].reshape(q_rank, heads * MLA_NOPE).astype(BF16)
    w_qr = w_uq3[:, :, MLA_NOPE:].reshape(q_rank, heads * MLA_ROPE).astype(BF16)
    w_kv3 = w_ukv.reshape(kv_rank, heads, MLA_NOPE + MLA_V)
    w_kv = jnp.concatenate([w_kv3[:, :, :MLA_NOPE].reshape(kv_rank, heads * MLA_NOPE),
                            w_kv3[:, :, MLA_NOPE:].reshape(kv_rank, heads * MLA_V)], 1).astype(BF16)
    qn = _proj(cq, w_qn, out_dtype=BF16, scale=scale, seq_len=T, name="proj_q_nope")
    qr = _proj(cq, w_qr, out_dtype=BF16, tabs=tab_qr, half=MLA_ROPE // 2, seq_len=T, name="proj_q_rope",
               **_pat(heads * MLA_ROPE, 1024, 0))
    kv = _proj(ckv, w_kv, out_dtype=BF16, seq_len=T, name="proj_kv")
    att = _mla_attn(qn, qr, kv, kr, B=B, T=T, heads=heads)
    y = _proj(att, w_out.astype(BF16), out_dtype=F32, name="proj_out1")
    (x1,) = _add_ln(x, y, ln1_g, ln1_b, bf16_copy=False)

    E = router.shape[1]
    F = we1.shape[2]
    tm = _tile(N, 512)
    meta, cnt = _router(x1, router)
    counts = cnt[0, :E].astype(I32)
    padded = (counts + tm - 1) // tm * tm
    ends = jnp.cumsum(padded)
    starts = ends - padded
    i1, i2 = meta[:, 0].astype(I32), meta[:, 1].astype(I32)
    dest = jnp.stack([starts[i1] + meta[:, 4].astype(I32), starts[i2] + meta[:, 5].astype(I32)], 1).reshape(-1)
    n_rows = 2 * N + E * tm
    n_tiles = n_rows // tm
    n_valid = (ends[-1] // tm).astype(I32).reshape(1)
    tile_start = jnp.arange(n_tiles, dtype=I32) * tm
    tile_expert = jnp.minimum(jnp.sum(tile_start[:, None] >= ends[None, :], axis=1), E - 1).astype(I32)
    tile_expert = jnp.where(jnp.arange(n_tiles) < n_valid[0], tile_expert, tile_expert[jnp.maximum(n_valid[0] - 1, 0)])
    row_tok = jnp.zeros((n_rows,), I32).at[dest].set(jnp.repeat(jnp.arange(N, dtype=I32), 2))
    ys = _experts(x1, we1, we3, we2, tile_expert, n_valid, row_tok, tm=tm)
    return _combine(x1, meta, ys, dest, ln2_g, ln2_b)


def kernel(x, l0_w_in, l0_ret_gn_g, l0_w_out, l0_ln1_g, l0_ln1_b, l0_ffn_w1, l0_ffn_w3, l0_ffn_w2, l0_ln2_g, l0_ln2_b, l1_w_dq_dkv, l1_q_norm_g, l1_w_uq, l1_kv_norm_g, l1_w_ukv, l1_w_out, l1_ln1_g, l1_ln1_b, l1_router, l1_moe_w1, l1_moe_w3, l1_moe_w2, l1_ln2_g, l1_ln2_b):
    B, T, D = x.shape
    x2 = x.reshape(B * T, D)
    h, hb = _even_layer(x2, x2.astype(BF16), l0_w_in, l0_ret_gn_g, l0_w_out, l0_ln1_g, l0_ln1_b,
                        l0_ffn_w1, l0_ffn_w3, l0_ffn_w2, l0_ln2_g, l0_ln2_b, B=B, T=T)
    out = _odd_layer(h, hb, l1_w_dq_dkv, l1_q_norm_g, l1_w_uq, l1_kv_norm_g, l1_w_ukv, l1_w_out,
                     l1_ln1_g, l1_ln1_b, l1_router, l1_moe_w1, l1_moe_w3, l1_moe_w2, l1_ln2_g, l1_ln2_b, B=B, T=T)
    return out.reshape(B, T, D)
```

```python
import functools

import jax
import jax.numpy as jnp
from jax import lax
from jax.experimental import pallas as pl
from jax.experimental.pallas import tpu as pltpu

F32 = jnp.float32
BF16 = jnp.bfloat16
I32 = jnp.int32

A_HEAD_DIM = 128
A_KV_HEADS = 4
IDX_HEADS = 16
IDX_DIM = 64
DSA_TOPK_MAX = 256
RET_KEY_DIM = 256
RET_VAL_DIM = 256
RET_CHUNK = 128
RET_THETA = 10000.0
MLA_V = 128
MLA_NOPE = 128
MLA_ROPE = 64
ROPE_THETA = 500000.0
Q_BLOCK = 128
LN_EPS = 1e-5
RMS_EPS = 1e-6
DEPTH = 2
ALPHA = (2.0 * DEPTH) ** 0.25

LANE = 128
V7X_VMEM_BYTES = 64 * 1024 * 1024
VMEM_LIMIT = V7X_VMEM_BYTES - 8 * 1024 * 1024
MASKED = -1e30
INT_MIN = -(2 ** 31)

NT_DIMS = (((1,), (1,)), ((), ()))
TN_DIMS = (((0,), (0,)), ((), ()))


def _tile(n, pref, mult=LANE):
    if n <= pref:
        return n
    t = (pref // mult) * mult
    while t > mult and n % t:
        t -= mult
    assert n % t == 0, (n, pref, mult)
    return t


def _params(*sem):
    return pltpu.CompilerParams(dimension_semantics=sem, vmem_limit_bytes=VMEM_LIMIT)


def _rope_cos_sin(T, rot_dim, theta):
    inv = theta ** (-jnp.arange(0, rot_dim, 2, dtype=F32) / rot_dim)
    ang = jnp.arange(T, dtype=F32)[:, None] * inv[None, :]
    return jnp.cos(ang), jnp.sin(ang)


def _lane_tables(cos, sin, head_dim, scale=1.0):
    T, half = cos.shape
    rest = head_dim - 2 * half
    zh = jnp.zeros((T, half), F32)
    c = jnp.concatenate([cos, cos, jnp.ones((T, rest), F32)], 1)
    sa = jnp.concatenate([-sin, zh, jnp.zeros((T, rest), F32)], 1)
    sb = jnp.concatenate([zh, sin, jnp.zeros((T, rest), F32)], 1)
    reps = LANE // head_dim
    return tuple(jnp.tile(t * scale, (1, reps)) for t in (c, sa, sb))


def _proj_kernel(*refs, slab_pat, mode, half, scale, with_tab, two_inputs):
    x_ref, w_ref = refs[:2]
    acc = jnp.dot(x_ref[...], w_ref[...], preferred_element_type=F32)
    refs = refs[2:]
    if two_inputs:
        acc = acc + jnp.dot(refs[0][...], refs[1][...], preferred_element_type=F32)
        refs = refs[2:]
    if with_tab:
        c_ref, sa_ref, sb_ref, o_ref = refs
    else:
        (o_ref,) = refs
    for s, p in enumerate(slab_pat):
        a = acc[:, s * LANE:(s + 1) * LANE]
        if p < 0:
            out = a if scale == 1.0 else a * scale
        else:
            c = c_ref[:, p * LANE:(p + 1) * LANE]
            sa = sa_ref[:, p * LANE:(p + 1) * LANE]
            if mode == "lane":
                sb = sb_ref[:, p * LANE:(p + 1) * LANE]
                out = a * c + pltpu.roll(a, LANE - half, 1) * sa + pltpu.roll(a, half, 1) * sb
            else:
                q = s ^ 1
                out = a * c + acc[:, q * LANE:(q + 1) * LANE] * sa
        o_ref[:, s * LANE:(s + 1) * LANE] = out.astype(o_ref.dtype)


def _proj(x, w, *, out_dtype, bm=1024, bn=1024, tabs=None, slab_pat=None, mode="lane", half=0,
          scale=1.0, seq_len=None, second=None, name="proj"):
    M, K = x.shape
    N = w.shape[1]
    bm = _tile(M, bm) if seq_len is None else _tile(seq_len, bm)
    bn = _tile(N, bn)
    if slab_pat is None:
        slab_pat = (-1,) * (bn // LANE)
    assert len(slab_pat) == bn // LANE
    in_specs = [pl.BlockSpec((bm, K), lambda i, j: (i, 0)),
                pl.BlockSpec((K, bn), lambda i, j: (0, j))]
    args = [x, w]
    if second is not None:
        x2, w2 = second
        in_specs += [pl.BlockSpec((bm, x2.shape[1]), lambda i, j: (i, 0)),
                     pl.BlockSpec((x2.shape[1], bn), lambda i, j: (0, j))]
        args += [x2, w2]
    if tabs is not None:
        tb = seq_len // bm
        tw = tabs[0].shape[1]
        in_specs += [pl.BlockSpec((bm, tw), lambda i, j: (i % tb, 0))] * 3
        args += list(tabs)
    kern = functools.partial(_proj_kernel, slab_pat=tuple(slab_pat), mode=mode, half=half,
                             scale=scale, with_tab=tabs is not None, two_inputs=second is not None)
    return pl.pallas_call(
        kern,
        grid=(M // bm, N // bn),
        in_specs=in_specs,
        out_specs=pl.BlockSpec((bm, bn), lambda i, j: (i, j)),
        out_shape=jax.ShapeDtypeStruct((M, N), out_dtype),
        compiler_params=_params("parallel", "parallel"),
        name=name,
    )(*args)


def _mm_ksplit_kernel(x_ref, w_ref, o_ref):
    part = jnp.dot(x_ref[...], w_ref[...], preferred_element_type=F32)

    @pl.when(pl.program_id(2) == 0)
    def _():
        o_ref[...] = part

    @pl.when(pl.program_id(2) > 0)
    def _():
        o_ref[...] += part


def _mm_ksplit(x, w, *, bm=1024, bn=1024, bk=2048, name="mm_ksplit"):
    M, K = x.shape
    N = w.shape[1]
    bm, bn, bk = _tile(M, bm), _tile(N, bn), _tile(K, bk)
    return pl.pallas_call(
        _mm_ksplit_kernel,
        grid=(M // bm, N // bn, K // bk),
        in_specs=[pl.BlockSpec((bm, bk), lambda i, j, k: (i, k)),
                  pl.BlockSpec((bk, bn), lambda i, j, k: (k, j))],
        out_specs=pl.BlockSpec((bm, bn), lambda i, j, k: (i, j)),
        out_shape=jax.ShapeDtypeStruct((M, N), F32),
        compiler_params=_params("parallel", "parallel", "arbitrary"),
        name=name,
    )(x, w)


def _swiglu_up_kernel(x_ref, w1_ref, w3_ref, o_ref):
    x = x_ref[...]
    a = jnp.dot(x, w1_ref[...], preferred_element_type=F32)
    b = jnp.dot(x, w3_ref[...], preferred_element_type=F32)
    o_ref[...] = (a * jax.nn.sigmoid(a) * b).astype(o_ref.dtype)


def _swiglu_up(x, w1, w3, *, bm=1024, bn=512):
    M, K = x.shape
    N = w1.shape[1]
    bm, bn = _tile(M, bm), _tile(N, bn)
    return pl.pallas_call(
        _swiglu_up_kernel,
        grid=(M // bm, N // bn),
        in_specs=[pl.BlockSpec((bm, K), lambda i, j: (i, 0)),
                  pl.BlockSpec((K, bn), lambda i, j: (0, j)),
                  pl.BlockSpec((K, bn), lambda i, j: (0, j))],
        out_specs=pl.BlockSpec((bm, bn), lambda i, j: (i, j)),
        out_shape=jax.ShapeDtypeStruct((M, N), BF16),
        compiler_params=_params("parallel", "parallel"),
        name="swiglu_up",
    )(x, w1, w3)


def _layer_norm_rows(z, g, b):
    mu = jnp.mean(z, axis=-1, keepdims=True)
    zc = z - mu
    var = jnp.mean(zc * zc, axis=-1, keepdims=True)
    return zc * lax.rsqrt(var + LN_EPS) * g + b


def _add_ln_kernel(x_ref, y_ref, g_ref, b_ref, of_ref, *maybe_ob_ref):
    out = _layer_norm_rows(ALPHA * x_ref[...] + y_ref[...], g_ref[...], b_ref[...])
    of_ref[...] = out
    for ob_ref in maybe_ob_ref:
        ob_ref[...] = out.astype(ob_ref.dtype)


def _add_ln(x, y, g, b, *, bf16_copy=True, bm=256):
    M, D = x.shape
    bm = _tile(M, bm, 8)
    row = pl.BlockSpec((bm, D), lambda i: (i, 0))
    vec = pl.BlockSpec((1, D), lambda i: (0, 0))
    out_dtypes = (F32, BF16) if bf16_copy else (F32,)
    return pl.pallas_call(
        _add_ln_kernel,
        grid=(M // bm,),
        in_specs=[row, row, vec, vec],
        out_specs=[row] * len(out_dtypes),
        out_shape=[jax.ShapeDtypeStruct((M, D), dt) for dt in out_dtypes],
        compiler_params=_params("parallel"),
        name="add_ln",
    )(x, y, g.reshape(1, D), b.reshape(1, D))


LOG2E = 1.4426950408889634
SOFTMAX_ROW_BLOCK = 512


def _online_softmax(s, m, l):
    rows = s.shape[0]
    rb = min(SOFTMAX_ROW_BLOCK, rows)
    ms, ls, scales, ps = [], [], [], []
    for r in range(rows // rb):
        sl = slice(r * rb, (r + 1) * rb)
        s_r = s[sl]
        m_r = jnp.maximum(m[sl], jnp.max(s_r, axis=1, keepdims=True))
        a_r = jnp.exp2(m[sl] - m_r)
        p_r = jnp.exp2((s_r - m_r).astype(BF16))
        ms.append(m_r)
        ls.append(a_r * l[sl] + jnp.sum(p_r.astype(F32), axis=1, keepdims=True))
        scales.append(a_r)
        ps.append(p_r)
    cat = lambda parts: jnp.concatenate(parts, axis=0)
    return cat(ms), cat(ls), cat(scales), cat(ps)


def _dsa_kernel(q_ref, k_ref, v_ref, iq_ref, ik_ref, o_ref, keys_ref, bias_ref, *, topk, ck, n_rep, idx_bits):
    blk = pl.program_id(1)
    n_chunks = (blk * Q_BLOCK + Q_BLOCK + ck - 1) // ck
    row = lax.broadcasted_iota(I32, (Q_BLOCK, ck), 0) + blk * Q_BLOCK
    lane = lax.broadcasted_iota(I32, (Q_BLOCK, ck), 1)
    w_off = IDX_HEADS * IDX_DIM + IDX_DIM
    iq = iq_ref[0]
    wi = iq[:, w_off:w_off + IDX_HEADS]
    q_idx = jnp.concatenate([iq[:, h * IDX_DIM:(h + 1) * IDX_DIM].astype(BF16) for h in range(IDX_HEADS)], axis=0)

    def score_body(c, carry):
        off = pl.multiple_of(c * ck, ck)
        kc = ik_ref[0, pl.ds(off, ck), :][:, :IDX_DIM].astype(BF16)
        lg = lax.dot_general(q_idx, kc, NT_DIMS, preferred_element_type=F32)
        s = jnp.zeros((Q_BLOCK, ck), F32)
        for h in range(IDX_HEADS):
            s = s + wi[:, h:h + 1] * jnp.maximum(lg[h * Q_BLOCK:(h + 1) * Q_BLOCK], 0.0)
        bits = pltpu.bitcast(s, I32)
        key = bits ^ ((bits >> 31) & 0x7FFFFFFF)
        keys_ref[c] = jnp.where(lane + off <= row, key, INT_MIN)
        return carry

    lax.fori_loop(0, n_chunks, score_body, 0)

    def count(indicator):
        def body(c, acc):
            part = indicator(keys_ref[c], lane + c * ck)
            for j in range(ck // LANE):
                acc = acc + part[:, j * LANE:(j + 1) * LANE]
            return acc
        acc = lax.fori_loop(0, n_chunks, body, jnp.zeros((Q_BLOCK, LANE), F32))
        return jnp.sum(acc, axis=1, keepdims=True)

    kf = float(topk)
    ok = count(lambda kc, idx: jnp.where(kc >= 0, 1.0, 0.0)) >= kf
    thr = jnp.where(ok, 0, INT_MIN).astype(I32)

    def bit_body(i, thr):
        cand = thr + lax.shift_left(jnp.int32(1), 30 - i)
        ok = count(lambda kc, idx: jnp.where(kc >= cand, 1.0, 0.0)) >= kf
        return jnp.where(ok, cand, thr)

    thr = lax.fori_loop(0, 31, bit_body, thr)

    n_gt = count(lambda kc, idx: jnp.where(kc > thr, 1.0, 0.0))
    n_ge = count(lambda kc, idx: jnp.where(kc >= thr, 1.0, 0.0))
    need = kf - n_gt
    has_thr = thr > INT_MIN
    surplus = jnp.where(has_thr, n_ge - n_gt - need, 0.0)

    def tie_search():
        def tie_body(i, last):
            cand = last + lax.shift_left(jnp.int32(1), idx_bits - 1 - i)
            ok = count(lambda kc, idx: jnp.where(kc == thr, jnp.where(idx < cand, 1.0, 0.0), 0.0)) < need
            return jnp.where(ok, cand, last)
        return lax.fori_loop(0, idx_bits, tie_body, jnp.zeros((Q_BLOCK, 1), I32))

    last = lax.cond(jnp.max(surplus) > 0.0, tie_search, lambda: jnp.full((Q_BLOCK, 1), 2 ** idx_bits, I32))
    last = jnp.where(has_thr, last, -1)

    def bias_body(c, carry):
        kc = keys_ref[c]
        tie_bias = jnp.where(lane + c * ck <= last, 0.0, MASKED)
        bias_ref[c] = jnp.where(kc == thr, tie_bias, jnp.where(kc > thr, 0.0, MASKED))
        return carry

    lax.fori_loop(0, n_chunks, bias_body, 0)

    q = q_ref[0]
    rows = n_rep * Q_BLOCK
    qgs = [jnp.concatenate([q[:, (g * n_rep + r) * A_HEAD_DIM:(g * n_rep + r + 1) * A_HEAD_DIM]
                            for r in range(n_rep)], axis=0) for g in range(A_KV_HEADS)]

    def att_body(c, carry, n_sub):
        width = n_sub * ck
        off = pl.multiple_of(c * width, width)
        b = jnp.concatenate([bias_ref[c * n_sub + t] for t in range(n_sub)], axis=1)
        bias = jnp.concatenate([b] * n_rep, axis=0)
        out = []
        for g in range(A_KV_HEADS):
            m, l, acc = carry[g]
            kc = k_ref[0, pl.ds(off, width), g * A_HEAD_DIM:(g + 1) * A_HEAD_DIM]
            vc = v_ref[0, pl.ds(off, width), g * A_HEAD_DIM:(g + 1) * A_HEAD_DIM]
            s = lax.dot_general(qgs[g], kc, NT_DIMS, preferred_element_type=F32) + bias
            m, l, a, p = _online_softmax(s, m, l)
            out.append((m, l, a * acc + jnp.dot(p, vc, preferred_element_type=F32)))
        return tuple(out)

    one = (jnp.full((rows, 1), MASKED, F32), jnp.zeros((rows, 1), F32), jnp.zeros((rows, A_HEAD_DIM), F32))
    carry = (one,) * A_KV_HEADS
    if k_ref.shape[1] >= 2 * ck:
        carry = lax.fori_loop(0, n_chunks // 2, functools.partial(att_body, n_sub=2), carry)
        carry = lax.cond(n_chunks % 2 == 1, lambda c: att_body(n_chunks - 1, c, 1), lambda c: c, carry)
    else:
        carry = att_body(0, carry, 1)
    for g in range(A_KV_HEADS):
        _, l, acc = carry[g]
        o = acc / l
        for r in range(n_rep):
            col = (g * n_rep + r) * A_HEAD_DIM
            o_ref[0, :, col:col + A_HEAD_DIM] = o[r * Q_BLOCK:(r + 1) * Q_BLOCK].astype(o_ref.dtype)


def _dsa(q, k, v, idx, *, B, T):
    a_heads = q.shape[-1] // A_HEAD_DIM
    n_rep = a_heads // A_KV_HEADS
    topk = min(DSA_TOPK_MAX, T // 4)
    ck = _tile(T, 512)
    idx_w = idx.shape[-1]
    kv_w = k.shape[-1]
    kern = functools.partial(_dsa_kernel, topk=topk, ck=ck, n_rep=n_rep, idx_bits=max(1, (T - 1).bit_length()))
    return pl.pallas_call(
        kern,
        grid=(B, T // Q_BLOCK),
        in_specs=[pl.BlockSpec((1, Q_BLOCK, q.shape[-1]), lambda b, i: (b, i, 0)),
                  pl.BlockSpec((1, T, kv_w), lambda b, i: (b, 0, 0)),
                  pl.BlockSpec((1, T, kv_w), lambda b, i: (b, 0, 0)),
                  pl.BlockSpec((1, Q_BLOCK, idx_w), lambda b, i: (b, i, 0)),
                  pl.BlockSpec((1, T, LANE), lambda b, i: (b, 0, IDX_HEADS * IDX_DIM // LANE))],
        out_specs=pl.BlockSpec((1, Q_BLOCK, q.shape[-1]), lambda b, i: (b, i, 0)),
        out_shape=jax.ShapeDtypeStruct(q.shape, BF16),
        scratch_shapes=[pltpu.VMEM((T // ck, Q_BLOCK, ck), I32), pltpu.VMEM((T // ck, Q_BLOCK, ck), F32)],
        compiler_params=_params("parallel", "parallel"),
        name="dsa",
    )(q, k, v, idx, idx)


def _retention_kernel(q_ref, k_ref, v_ref, g_ref, gn_ref, din_ref, qd_ref, kd_ref, cd_ref, o_ref, state_ref, *, n_sub):
    @pl.when(pl.program_id(2) == 0)
    def _():
        state_ref[...] = jnp.zeros_like(state_ref)

    din = din_ref[0]
    qd = qd_ref[0]
    kd = kd_ref[0]
    cd = cd_ref[0]
    gn = gn_ref[...]
    for s in range(n_sub):
        sl = pl.ds(s * RET_CHUNK, RET_CHUNK)
        qc = q_ref[sl, :]
        kc = k_ref[sl, :]
        vc = v_ref[sl, :]
        st = state_ref[...]
        inner = lax.dot_general(qc, kc, NT_DIMS, preferred_element_type=F32) * din
        o = (jnp.dot(inner.astype(BF16), vc, preferred_element_type=F32)
             + jnp.dot(qc, st.astype(BF16), preferred_element_type=F32) * qd)
        vk = (vc.astype(F32) * kd).astype(BF16)
        state_ref[...] = st * cd + lax.dot_general(kc, vk, TN_DIMS, preferred_element_type=F32)
        mu = jnp.mean(o, axis=-1, keepdims=True)
        oc = o - mu
        var = jnp.mean(oc * oc, axis=-1, keepdims=True)
        gate = g_ref[sl, :].astype(F32)
        o_ref[sl, :] = (gate * jax.nn.sigmoid(gate) * (oc * lax.rsqrt(var + LN_EPS) * gn)).astype(o_ref.dtype)


def _retention(qk, pv, gn_g, *, B, T, heads, v_blk0):
    N = qk.shape[0]
    C = RET_CHUNK
    rb = _tile(T, 512)
    n_sub = rb // C
    nr = T // rb
    log_gamma = jnp.log(1.0 - 2.0 ** (-5.0 - jnp.arange(heads, dtype=F32)))
    pos = jnp.arange(C, dtype=F32)
    diff = pos[:, None] - pos[None, :]
    din = jnp.exp(jnp.where(diff[None] >= 0, log_gamma[:, None, None] * diff[None], -jnp.inf))
    qd = jnp.exp(log_gamma[:, None] * (pos[None] + 1.0))[:, :, None]
    kd = jnp.exp(log_gamma[:, None] * (C - 1.0 - pos[None]))[:, :, None]
    cd = jnp.exp(log_gamma * C)[:, None, None]
    W = RET_VAL_DIM
    blk = lambda off: pl.BlockSpec((rb, W), lambda b, h, r: (b * nr + r, off + h))
    per_head = lambda shape: pl.BlockSpec((1,) + shape, lambda b, h, r: (h, 0, 0))
    return pl.pallas_call(
        functools.partial(_retention_kernel, n_sub=n_sub),
        grid=(B, heads, nr),
        in_specs=[blk(0), blk(heads), blk(v_blk0), blk(v_blk0 + heads),
                  pl.BlockSpec((1, W), lambda b, h, r: (0, h)),
                  per_head((C, C)), per_head((C, 1)), per_head((C, 1)), per_head((1, 1))],
        out_specs=pl.BlockSpec((rb, W), lambda b, h, r: (b * nr + r, h)),
        out_shape=jax.ShapeDtypeStruct((N, heads * W), BF16),
        scratch_shapes=[pltpu.VMEM((RET_KEY_DIM, RET_VAL_DIM), F32)],
        compiler_params=_params("parallel", "parallel", "arbitrary"),
        name="retention",
    )(qk, qk, pv, pv, gn_g.reshape(1, heads * W), din, qd, kd, cd)


def _mla_down_kernel(x_ref, w_ref, qg_ref, kvg_ref, c_ref, sa_ref, sb_ref, cq_ref, ckv_ref, kr_ref, *, q_rank, kv_rank):
    acc = jnp.dot(x_ref[...], w_ref[...], preferred_element_type=F32)

    def rms(a, g):
        return a * lax.rsqrt(jnp.mean(a * a, axis=-1, keepdims=True) + RMS_EPS) * g

    cq_ref[...] = rms(acc[:, :q_rank], qg_ref[...]).astype(cq_ref.dtype)
    ckv_ref[...] = rms(acc[:, q_rank:q_rank + kv_rank], kvg_ref[...]).astype(ckv_ref.dtype)
    kr = acc[:, q_rank + kv_rank:]
    half = MLA_ROPE // 2
    kr = kr * c_ref[...] + pltpu.roll(kr, LANE - half, 1) * sa_ref[...] + pltpu.roll(kr, half, 1) * sb_ref[...]
    kr_ref[...] = kr.astype(kr_ref.dtype)


def _mla_down(x, w, q_g, kv_g, tabs, *, T, q_rank, kv_rank, bm=512):
    M, K = x.shape
    Nw = w.shape[1]
    bm = _tile(T, bm)
    tb = T // bm
    row = lambda n: pl.BlockSpec((bm, n), lambda i: (i, 0))
    tab = pl.BlockSpec((bm, LANE), lambda i: (i % tb, 0))
    return pl.pallas_call(
        functools.partial(_mla_down_kernel, q_rank=q_rank, kv_rank=kv_rank),
        grid=(M // bm,),
        in_specs=[row(K), pl.BlockSpec((K, Nw), lambda i: (0, 0)),
                  pl.BlockSpec((1, q_rank), lambda i: (0, 0)), pl.BlockSpec((1, kv_rank), lambda i: (0, 0)),
                  tab, tab, tab],
        out_specs=[row(q_rank), row(kv_rank), row(LANE)],
        out_shape=[jax.ShapeDtypeStruct((M, q_rank), BF16), jax.ShapeDtypeStruct((M, kv_rank), BF16),
                   jax.ShapeDtypeStruct((M, LANE), BF16)],
        compiler_params=_params("parallel"),
        name="mla_down",
    )(x, w, q_g.reshape(1, q_rank), kv_g.reshape(1, kv_rank), *tabs)


MLA_HEADS_PER_STEP = 4


def _mla_attn_kernel(qn_ref, qr_ref, kn_ref, kr_ref, v_ref, o_ref, *, tq):
    i = pl.program_id(2)
    lane = lax.broadcasted_iota(I32, (tq, LANE), 1)
    qs = []
    for j in range(MLA_HEADS_PER_STEP):
        lo = (j % 2) * MLA_ROPE
        own = jnp.where((lane >= lo) & (lane < lo + MLA_ROPE), 1.0, 0.0)
        pair = qr_ref[:, (j // 2) * LANE:(j // 2 + 1) * LANE].astype(F32)
        qr = (pair * own).astype(BF16)
        qs.append(jnp.concatenate([qn_ref[:, j * LANE:(j + 1) * LANE], qr], axis=1))

    def step(c, carry, masked, width):
        off = pl.multiple_of(c * width, width)
        kr = kr_ref[pl.ds(off, width), :]
        out = []
        for j in range(MLA_HEADS_PER_STEP):
            m, l, acc = carry[j]
            k = jnp.concatenate([kn_ref[pl.ds(off, width), j * LANE:(j + 1) * LANE], kr], axis=1)
            s = lax.dot_general(qs[j], k, NT_DIMS, preferred_element_type=F32)
            if masked:
                r_io = lax.broadcasted_iota(I32, (tq, width), 0)
                c_io = lax.broadcasted_iota(I32, (tq, width), 1)
                s = jnp.where(c_io <= r_io, s, MASKED)
            m, l, a, p = _online_softmax(s, m, l)
            v = v_ref[pl.ds(off, width), j * MLA_V:(j + 1) * MLA_V]
            out.append((m, l, a * acc + jnp.dot(p, v, preferred_element_type=F32)))
        return tuple(out)

    one = (jnp.full((tq, 1), MASKED, F32), jnp.zeros((tq, 1), F32), jnp.zeros((tq, MLA_V), F32))
    carry = (one,) * MLA_HEADS_PER_STEP
    if kn_ref.shape[0] >= 2 * tq:
        carry = lax.fori_loop(0, i // 2, functools.partial(step, masked=False, width=2 * tq), carry)
        carry = lax.cond(i % 2 == 1, lambda c: step(i - 1, c, False, tq), lambda c: c, carry)
    carry = step(i, carry, True, tq)
    for j in range(MLA_HEADS_PER_STEP):
        _, l, acc = carry[j]
        o_ref[:, j * MLA_V:(j + 1) * MLA_V] = (acc / l).astype(o_ref.dtype)


def _mla_attn(qn, qr, kv, kr, *, B, T, heads):
    N = qn.shape[0]
    tq = _tile(T, 512)
    nq = T // tq
    hp = MLA_HEADS_PER_STEP
    w = hp * LANE
    return pl.pallas_call(
        functools.partial(_mla_attn_kernel, tq=tq),
        grid=(B, heads // hp, nq),
        in_specs=[pl.BlockSpec((tq, w), lambda b, h, i: (b * nq + i, h)),
                  pl.BlockSpec((tq, hp * MLA_ROPE), lambda b, h, i: (b * nq + i, h)),
                  pl.BlockSpec((T, w), lambda b, h, i: (b, h)),
                  pl.BlockSpec((T, LANE), lambda b, h, i: (b, 0)),
                  pl.BlockSpec((T, w), lambda b, h, i: (b, heads // hp + h))],
        out_specs=pl.BlockSpec((tq, w), lambda b, h, i: (b * nq + i, h)),
        out_shape=jax.ShapeDtypeStruct((N, heads * MLA_V), BF16),
        compiler_params=_params("parallel", "parallel", "parallel"),
        name="mla_attn",
    )(qn, qr, kv, kr, kv)


def _router_kernel(x_ref, r_ref, meta_ref, cnt_ref, carry_ref, *, n_exp):
    @pl.when(pl.program_id(0) == 0)
    def _():
        carry_ref[...] = jnp.zeros_like(carry_ref)

    bm = x_ref.shape[0]
    logits = jnp.dot(x_ref[...], r_ref[...], preferred_element_type=F32, precision=lax.Precision.HIGHEST)
    lane = lax.broadcasted_iota(I32, (bm, LANE), 1).astype(F32)
    logits = jnp.where(lane < n_exp, logits, -jnp.inf)
    m1 = jnp.max(logits, axis=1, keepdims=True)
    i1 = jnp.min(jnp.where(logits == m1, lane, float(LANE)), axis=1, keepdims=True)
    rest = jnp.where(lane == i1, -jnp.inf, logits)
    m2 = jnp.max(rest, axis=1, keepdims=True)
    i2 = jnp.min(jnp.where(rest == m2, lane, float(LANE)), axis=1, keepdims=True)
    e = jnp.exp(m2 - m1)
    g1 = 1.0 / (1.0 + e)
    g2 = e / (1.0 + e)
    sel = jnp.where(lane == i1, 1.0, jnp.where(lane == i2, 1.0, 0.0))
    r_io = lax.broadcasted_iota(I32, (bm, bm), 0)
    c_io = lax.broadcasted_iota(I32, (bm, bm), 1)
    below = jnp.where(c_io < r_io, 1.0, 0.0).astype(BF16)
    carry = carry_ref[0:1, :]
    rank = jnp.dot(below, sel.astype(BF16), preferred_element_type=F32) + carry
    r1 = jnp.sum(jnp.where(lane == i1, rank, 0.0), axis=1, keepdims=True)
    r2 = jnp.sum(jnp.where(lane == i2, rank, 0.0), axis=1, keepdims=True)
    meta = jnp.where(lane == 0, i1, 0.0)
    meta = jnp.where(lane == 1, i2, meta)
    meta = jnp.where(lane == 2, g1, meta)
    meta = jnp.where(lane == 3, g2, meta)
    meta = jnp.where(lane == 4, r1, meta)
    meta = jnp.where(lane == 5, r2, meta)
    meta_ref[...] = meta
    total = carry + jnp.sum(sel, axis=0, keepdims=True)
    carry_ref[...] = jnp.broadcast_to(total, carry_ref.shape)
    cnt_ref[...] = jnp.broadcast_to(total, cnt_ref.shape)


def _router(x, router, *, bm=512):
    M, D = x.shape
    n_exp = router.shape[1]
    bm = _tile(M, bm)
    r_pad = jnp.zeros((D, LANE), F32).at[:, :n_exp].set(router)
    return pl.pallas_call(
        functools.partial(_router_kernel, n_exp=n_exp),
        grid=(M // bm,),
        in_specs=[pl.BlockSpec((bm, D), lambda i: (i, 0)), pl.BlockSpec((D, LANE), lambda i: (0, 0))],
        out_specs=[pl.BlockSpec((bm, LANE), lambda i: (i, 0)), pl.BlockSpec((8, LANE), lambda i: (0, 0))],
        out_shape=[jax.ShapeDtypeStruct((M, LANE), F32), jax.ShapeDtypeStruct((8, LANE), F32)],
        scratch_shapes=[pltpu.VMEM((8, LANE), F32)],
        compiler_params=_params("arbitrary"),
        name="router",
    )(x, r_pad)


def _moe_gather_kernel(tok_ref, nv_ref, x_hbm, o_ref, stage_ref, sem, *, tm):
    r = pl.program_id(0)
    nv = nv_ref[0]

    def row_copy(slot, j, tok):
        return pltpu.make_async_copy(x_hbm.at[pl.ds(tok, 1)], stage_ref.at[slot, pl.ds(j, 1)], sem.at[slot])

    def gather_start(tile):
        def body(j, carry):
            row_copy(tile % 2, j, tok_ref[tile * tm + j]).start()
            return carry
        lax.fori_loop(0, tm, body, 0)

    @pl.when(r < nv)
    def _():
        @pl.when(r == 0)
        def _():
            gather_start(0)

        @pl.when(r + 1 < nv)
        def _():
            gather_start(r + 1)

        def wait(j, carry):
            row_copy(r % 2, j, 0).wait()
            return carry

        lax.fori_loop(0, tm, wait, 0)
        o_ref[...] = stage_ref[r % 2].astype(BF16)


def _moe_up_kernel(te_ref, nv_ref, xs_ref, w1_ref, w3_ref, o_ref, wb1_ref, wb3_ref):
    r = pl.program_id(1)

    @pl.when(r < nv_ref[0])
    def _():
        @pl.when((r == 0) | (te_ref[r] != te_ref[jnp.maximum(r - 1, 0)]))
        def _():
            wb1_ref[...] = w1_ref[0].astype(BF16)
            wb3_ref[...] = w3_ref[0].astype(BF16)

        xb = xs_ref[...]
        a = jnp.dot(xb, wb1_ref[...], preferred_element_type=F32)
        b = jnp.dot(xb, wb3_ref[...], preferred_element_type=F32)
        o_ref[...] = (a * jax.nn.sigmoid(a) * b).astype(o_ref.dtype)


def _moe_down_kernel(te_ref, nv_ref, h_ref, w2_ref, o_ref, wb_ref):
    r = pl.program_id(1)

    @pl.when(r < nv_ref[0])
    def _():
        @pl.when((r == 0) | (te_ref[r] != te_ref[jnp.maximum(r - 1, 0)]))
        def _():
            wb_ref[...] = w2_ref[0].astype(BF16)

        o_ref[...] = jnp.dot(h_ref[...], wb_ref[...], preferred_element_type=F32)


def _experts(x, w1, w3, w2, tile_expert, n_valid, row_tok, *, tm, tf=512, tn=512):
    D = x.shape[1]
    P = row_tok.shape[0]
    E, _, F = w1.shape
    tf, tn = _tile(F, tf), _tile(D, tn)
    n_tiles = P // tm
    row = lambda r, nv: jnp.minimum(r, nv[0] - 1)

    xs = pl.pallas_call(
        functools.partial(_moe_gather_kernel, tm=tm),
        grid_spec=pltpu.PrefetchScalarGridSpec(
            num_scalar_prefetch=2, grid=(n_tiles,),
            in_specs=[pl.BlockSpec(memory_space=pl.ANY)],
            out_specs=pl.BlockSpec((tm, D), lambda r, tok, nv: (row(r, nv), 0)),
            scratch_shapes=[pltpu.VMEM((2, tm, D), F32), pltpu.SemaphoreType.DMA((2,))]),
        out_shape=jax.ShapeDtypeStruct((P, D), BF16),
        compiler_params=_params("arbitrary"),
        name="moe_gather",
    )(row_tok, n_valid, x)

    hid = pl.pallas_call(
        _moe_up_kernel,
        grid_spec=pltpu.PrefetchScalarGridSpec(
            num_scalar_prefetch=2, grid=(F // tf, n_tiles),
            in_specs=[pl.BlockSpec((tm, D), lambda f, r, te, nv: (row(r, nv), 0)),
                      pl.BlockSpec((1, D, tf), lambda f, r, te, nv: (te[r], 0, f)),
                      pl.BlockSpec((1, D, tf), lambda f, r, te, nv: (te[r], 0, f))],
            out_specs=pl.BlockSpec((tm, tf), lambda f, r, te, nv: (row(r, nv), f)),
            scratch_shapes=[pltpu.VMEM((D, tf), BF16), pltpu.VMEM((D, tf), BF16)]),
        out_shape=jax.ShapeDtypeStruct((P, F), BF16),
        compiler_params=_params("arbitrary", "arbitrary"),
        name="moe_up",
    )(tile_expert, n_valid, xs, w1, w3)

    return pl.pallas_call(
        _moe_down_kernel,
        grid_spec=pltpu.PrefetchScalarGridSpec(
            num_scalar_prefetch=2, grid=(D // tn, n_tiles),
            in_specs=[pl.BlockSpec((tm, F), lambda n, r, te, nv: (row(r, nv), 0)),
                      pl.BlockSpec((1, F, tn), lambda n, r, te, nv: (te[r], 0, n))],
            out_specs=pl.BlockSpec((tm, tn), lambda n, r, te, nv: (row(r, nv), n)),
            scratch_shapes=[pltpu.VMEM((F, tn), BF16)]),
        out_shape=jax.ShapeDtypeStruct((P, D), F32),
        compiler_params=_params("arbitrary", "arbitrary"),
        name="moe_down",
    )(tile_expert, n_valid, hid, w2)


def _combine_kernel(dest_ref, x_ref, meta_ref, g_ref, b_ref, ys_hbm, o_ref, buf_ref, sem):
    bm = x_ref.shape[0]
    i = pl.program_id(0)
    n = pl.num_programs(0)

    def copy(slot, j, k, d):
        return pltpu.make_async_copy(ys_hbm.at[pl.ds(d, 1)], buf_ref.at[slot, k, pl.ds(j, 1)], sem.at[slot, k])

    def gather_start(blk):
        slot = blk % 2

        def body(j, carry):
            for k in range(2):
                copy(slot, j, k, dest_ref[2 * (blk * bm + j) + k]).start()
            return carry
        lax.fori_loop(0, bm, body, 0)

    @pl.when(i == 0)
    def _():
        gather_start(0)

    @pl.when(i + 1 < n)
    def _():
        gather_start(i + 1)

    slot = i % 2

    def wait(j, carry):
        for k in range(2):
            copy(slot, j, k, 0).wait()
        return carry

    lax.fori_loop(0, bm, wait, 0)
    meta = meta_ref[...]
    y = meta[:, 2:3] * buf_ref[slot, 0] + meta[:, 3:4] * buf_ref[slot, 1]
    o_ref[...] = _layer_norm_rows(ALPHA * x_ref[...] + y, g_ref[...], b_ref[...])


def _combine(x, meta, ys, dest, g, b, *, bm=256):
    M, D = x.shape
    bm = _tile(M, bm, 8)
    row = lambda n: pl.BlockSpec((bm, n), lambda i, d: (i, 0))
    vec = pl.BlockSpec((1, D), lambda i, d: (0, 0))
    return pl.pallas_call(
        _combine_kernel,
        grid_spec=pltpu.PrefetchScalarGridSpec(
            num_scalar_prefetch=1, grid=(M // bm,),
            in_specs=[row(D), row(LANE), vec, vec, pl.BlockSpec(memory_space=pl.ANY)],
            out_specs=row(D),
            scratch_shapes=[pltpu.VMEM((2, 2, bm, D), F32), pltpu.SemaphoreType.DMA((2, 2))]),
        out_shape=jax.ShapeDtypeStruct((M, D), F32),
        compiler_params=_params("arbitrary"),
        name="moe_combine",
    )(dest, x, meta, g.reshape(1, D), b.reshape(1, D), ys)


def _even_layer(x, xb, w_in, ret_gn_g, w_out, ln1_g, ln1_b, w1, w3, w2, ln2_g, ln2_b, *, B, T):
    N, D = x.shape
    a_heads = D // 2 // A_HEAD_DIM
    r_heads = D // 2 // RET_VAL_DIM
    qa_w, kv_w = a_heads * A_HEAD_DIM, A_KV_HEADS * A_HEAD_DIM
    qi_w = IDX_HEADS * IDX_DIM
    rk_w, rv_w = r_heads * RET_KEY_DIM, r_heads * RET_VAL_DIM
    sizes = (qa_w, kv_w, kv_w, qi_w, IDX_DIM, IDX_HEADS, rk_w, rk_w, rv_w, rv_w)
    offs = [0]
    for s in sizes:
        offs.append(offs[-1] + s)
    col = lambda a, b_: w_in[:, offs[a]:offs[b_]]
    w_qa, w_ka, w_va = col(0, 1), col(1, 2), col(2, 3)
    w_qi, w_ki, w_wi = col(3, 4), col(4, 5), col(5, 6)
    w_qb, w_kb, w_vb, w_gb = col(6, 7), col(7, 8), col(8, 9), col(9, 10)

    cos_a, sin_a = _rope_cos_sin(T, A_HEAD_DIM // 4, ROPE_THETA)
    tab_q = _lane_tables(cos_a, sin_a, A_HEAD_DIM, A_HEAD_DIM ** -0.5 * LOG2E)
    tab_k = _lane_tables(cos_a, sin_a, A_HEAD_DIM)
    cos_i, sin_i = _rope_cos_sin(T, IDX_DIM // 4, ROPE_THETA)
    tab_i = _lane_tables(cos_i, sin_i, IDX_DIM)
    pass_c = jnp.ones((T, LANE - IDX_DIM), F32)
    pass_s = jnp.zeros((T, LANE - IDX_DIM), F32)
    tab_idx = tuple(jnp.concatenate([t, t[:, :IDX_DIM], p], 1)
                    for t, p in zip(tab_i, (pass_c, pass_s, pass_s)))
    inv = 1.0 / (RET_THETA ** jnp.linspace(0.0, 1.0, RET_KEY_DIM // 2, dtype=F32))
    ang = jnp.arange(T, dtype=F32)[:, None] * inv[None, :]
    cos_r, sin_r = jnp.cos(ang), jnp.sin(ang)
    sin_pair = jnp.concatenate([-sin_r, sin_r], 1)
    tab_r = (jnp.concatenate([cos_r, cos_r], 1), sin_pair, sin_pair)

    idx_pad = LANE - IDX_DIM - IDX_HEADS
    w_idx = jnp.concatenate([w_qi, w_ki, w_wi * (IDX_DIM ** -0.5 * IDX_HEADS ** -0.5),
                             jnp.zeros((D, idx_pad), F32)], 1).astype(BF16)
    w_rqk = jnp.concatenate([w_qb, w_kb * RET_KEY_DIM ** -0.5], 1).astype(BF16)
    w_pv = jnp.concatenate([w_va, w_vb, w_gb], 1).astype(BF16)
    qa = _proj(xb, w_qa.astype(BF16), out_dtype=BF16, tabs=tab_q, half=A_HEAD_DIM // 8,
               seq_len=T, name="proj_qa", **_pat(qa_w, 1024, 0))
    ka = _proj(xb, w_ka.astype(BF16), out_dtype=BF16, tabs=tab_k, half=A_HEAD_DIM // 8,
               seq_len=T, name="proj_ka", **_pat(kv_w, 1024, 0))
    n_idx = w_idx.shape[1]
    idx = _proj(xb, w_idx, out_dtype=F32, tabs=tab_idx, half=IDX_DIM // 8, seq_len=T, name="proj_idx",
                bm=512, bn=n_idx, slab_pat=(0,) * (qi_w // LANE) + (1,))
    rqk = _proj(xb, w_rqk, out_dtype=BF16, tabs=tab_r, mode="pair", seq_len=T, name="proj_ret_qk",
                bn=1024, slab_pat=(0, 1) * (_tile(2 * rk_w, 1024) // (2 * LANE)))
    pv = _proj(xb, w_pv, out_dtype=BF16, bn=768, seq_len=T, name="proj_v")

    ya = _dsa(qa.reshape(B, T, qa_w), ka.reshape(B, T, kv_w), pv.reshape(B, T, -1), idx.reshape(B, T, n_idx),
              B=B, T=T)
    yb = _retention(rqk, pv, ret_gn_g, B=B, T=T, heads=r_heads, v_blk0=kv_w // RET_VAL_DIM)
    w_out_b = w_out.astype(BF16)
    y = _proj(ya.reshape(N, qa_w), w_out_b[:qa_w], second=(yb, w_out_b[qa_w:]), out_dtype=F32, name="proj_out0")
    x1, x1b = _add_ln(x, y, ln1_g, ln1_b)
    hid = _swiglu_up(x1b, w1.astype(BF16), w3.astype(BF16))
    y = _mm_ksplit(hid, w2.astype(BF16), bk=3584, name="ffn_down")
    return _add_ln(x1, y, ln2_g, ln2_b)


def _pat(width, bn, p):
    bn = _tile(width, bn)
    return dict(bn=bn, slab_pat=(p,) * (bn // LANE))


def _odd_layer(x, xb, w_dq_dkv, q_norm_g, w_uq, kv_norm_g, w_ukv, w_out, ln1_g, ln1_b,
               router, we1, we3, we2, ln2_g, ln2_b, *, B, T):
    N, D = x.shape
    heads = D // MLA_V
    q_rank, kv_rank = q_norm_g.shape[0], kv_norm_g.shape[0]
    scale = (MLA_NOPE + MLA_ROPE) ** -0.5 * LOG2E
    cos_c, sin_c = _rope_cos_sin(T, MLA_ROPE, ROPE_THETA)
    tab_kr = _lane_tables(cos_c, sin_c, MLA_ROPE)
    tab_qr = _lane_tables(cos_c, sin_c, MLA_ROPE, scale)

    w_kr = w_dq_dkv[:, q_rank + kv_rank:]
    w_down = jnp.concatenate([w_dq_dkv[:, :q_rank + kv_rank], w_kr, w_kr], 1).astype(BF16)
    cq, ckv, kr = _mla_down(xb, w_down, q_norm_g, kv_norm_g, tab_kr, T=T, q_rank=q_rank, kv_rank=kv_rank)
    w_uq3 = w_uq.reshape(q_rank, heads, MLA_NOPE + MLA_ROPE)
    w_qn = w_uq3[:, :, :MLA_NOPE].reshape(q_rank, heads * MLA_NOPE).astype(BF16)
    w_qr = w_uq3[:, :, MLA_NOPE:].reshape(q_rank, heads * MLA_ROPE).astype(BF16)
    w_kv3 = w_ukv.reshape(kv_rank, heads, MLA_NOPE + MLA_V)
    w_kv = jnp.concatenate([w_kv3[:, :, :MLA_NOPE].reshape(kv_rank, heads * MLA_NOPE),
                            w_kv3[:, :, MLA_NOPE:].reshape(kv_rank, heads * MLA_V)], 1).astype(BF16)
    qn = _proj(cq, w_qn, out_dtype=BF16, scale=scale, seq_len=T, name="proj_q_nope")
    qr = _proj(cq, w_qr, out_dtype=BF16, tabs=tab_qr, half=MLA_ROPE // 2, seq_len=T, name="proj_q_rope",
               **_pat(heads * MLA_ROPE, 1024, 0))
    kv = _proj(ckv, w_kv, out_dtype=BF16, seq_len=T, name="proj_kv")
    att = _mla_attn(qn, qr, kv, kr, B=B, T=T, heads=heads)
    y = _proj(att, w_out.astype(BF16), out_dtype=F32, name="proj_out1")
    (x1,) = _add_ln(x, y, ln1_g, ln1_b, bf16_copy=False)

    E = router.shape[1]
    F = we1.shape[2]
    tm = _tile(N, 512)
    meta, cnt = _router(x1, router)
    counts = cnt[0, :E].astype(I32)
    padded = (counts + tm - 1) // tm * tm
    ends = jnp.cumsum(padded)
    starts = ends - padded
    i1, i2 = meta[:, 0].astype(I32), meta[:, 1].astype(I32)
    dest = jnp.stack([starts[i1] + meta[:, 4].astype(I32), starts[i2] + meta[:, 5].astype(I32)], 1).reshape(-1)
    n_rows = 2 * N + E * tm
    n_tiles = n_rows // tm
    n_valid = (ends[-1] // tm).astype(I32).reshape(1)
    tile_start = jnp.arange(n_tiles, dtype=I32) * tm
    tile_expert = jnp.minimum(jnp.sum(tile_start[:, None] >= ends[None, :], axis=1), E - 1).astype(I32)
    tile_expert = jnp.where(jnp.arange(n_tiles) < n_valid[0], tile_expert, tile_expert[jnp.maximum(n_valid[0] - 1, 0)])
    row_tok = jnp.zeros((n_rows,), I32).at[dest].set(jnp.repeat(jnp.arange(N, dtype=I32), 2))
    ys = _experts(x1, we1, we3, we2, tile_expert, n_valid, row_tok, tm=tm)
    return _combine(x1, meta, ys, dest, ln2_g, ln2_b)


def kernel(x, l0_w_in, l0_ret_gn_g, l0_w_out, l0_ln1_g, l0_ln1_b, l0_ffn_w1, l0_ffn_w3, l0_ffn_w2, l0_ln2_g, l0_ln2_b, l1_w_dq_dkv, l1_q_norm_g, l1_w_uq, l1_kv_norm_g, l1_w_ukv, l1_w_out, l1_ln1_g, l1_ln1_b, l1_router, l1_moe_w1, l1_moe_w3, l1_moe_w2, l1_ln2_g, l1_ln2_b):
    B, T, D = x.shape
    x2 = x.reshape(B * T, D)
    h, hb = _even_layer(x2, x2.astype(BF16), l0_w_in, l0_ret_gn_g, l0_w_out, l0_ln1_g, l0_ln1_b,
                        l0_ffn_w1, l0_ffn_w3, l0_ffn_w2, l0_ln2_g, l0_ln2_b, B=B, T=T)
    out = _odd_layer(h, hb, l1_w_dq_dkv, l1_q_norm_g, l1_w_uq, l1_kv_norm_g, l1_w_ukv, l1_w_out,
                     l1_ln1_g, l1_ln1_b, l1_router, l1_moe_w1, l1_moe_w3, l1_moe_w2, l1_ln2_g, l1_ln2_b, B=B, T=T)
    return out.reshape(B, T, D)
```

```python
import functools

import jax
import jax.numpy as jnp
from jax import lax
from jax.experimental import pallas as pl
from jax.experimental.pallas import tpu as pltpu

F32 = jnp.float32
BF16 = jnp.bfloat16
I32 = jnp.int32

A_HEAD_DIM = 128
A_KV_HEADS = 4
IDX_HEADS = 16
IDX_DIM = 64
DSA_TOPK_MAX = 256
RET_KEY_DIM = 256
RET_VAL_DIM = 256
RET_CHUNK = 128
RET_THETA = 10000.0
MLA_V = 128
MLA_NOPE = 128
MLA_ROPE = 64
ROPE_THETA = 500000.0
Q_BLOCK = 128
LN_EPS = 1e-5
RMS_EPS = 1e-6
DEPTH = 2
ALPHA = (2.0 * DEPTH) ** 0.25

LANE = 128
V7X_VMEM_BYTES = 64 * 1024 * 1024
VMEM_LIMIT = V7X_VMEM_BYTES - 8 * 1024 * 1024
MASKED = -1e30
INT_MIN = -(2 ** 31)

NT_DIMS = (((1,), (1,)), ((), ()))
TN_DIMS = (((0,), (0,)), ((), ()))


def _tile(n, pref, mult=LANE):
    if n <= pref:
        return n
    t = (pref // mult) * mult
    while t > mult and n % t:
        t -= mult
    assert n % t == 0, (n, pref, mult)
    return t


def _params(*sem):
    return pltpu.CompilerParams(dimension_semantics=sem, vmem_limit_bytes=VMEM_LIMIT)


def _rope_cos_sin(T, rot_dim, theta):
    inv = theta ** (-jnp.arange(0, rot_dim, 2, dtype=F32) / rot_dim)
    ang = jnp.arange(T, dtype=F32)[:, None] * inv[None, :]
    return jnp.cos(ang), jnp.sin(ang)


def _lane_tables(cos, sin, head_dim, scale=1.0):
    T, half = cos.shape
    rest = head_dim - 2 * half
    zh = jnp.zeros((T, half), F32)
    c = jnp.concatenate([cos, cos, jnp.ones((T, rest), F32)], 1)
    sa = jnp.concatenate([-sin, zh, jnp.zeros((T, rest), F32)], 1)
    sb = jnp.concatenate([zh, sin, jnp.zeros((T, rest), F32)], 1)
    reps = LANE // head_dim
    return tuple(jnp.tile(t * scale, (1, reps)) for t in (c, sa, sb))


def _proj_kernel(*refs, slab_pat, mode, half, scale, with_tab, two_inputs):
    x_ref, w_ref = refs[:2]
    acc = jnp.dot(x_ref[...], w_ref[...], preferred_element_type=F32)
    refs = refs[2:]
    if two_inputs:
        acc = acc + jnp.dot(refs[0][...], refs[1][...], preferred_element_type=F32)
        refs = refs[2:]
    if with_tab:
        c_ref, sa_ref, sb_ref, o_ref = refs
    else:
        (o_ref,) = refs
    for s, p in enumerate(slab_pat):
        a = acc[:, s * LANE:(s + 1) * LANE]
        if p < 0:
            out = a if scale == 1.0 else a * scale
        else:
            c = c_ref[:, p * LANE:(p + 1) * LANE]
            sa = sa_ref[:, p * LANE:(p + 1) * LANE]
            if mode == "lane":
                sb = sb_ref[:, p * LANE:(p + 1) * LANE]
                out = a * c + pltpu.roll(a, LANE - half, 1) * sa + pltpu.roll(a, half, 1) * sb
            else:
                q = s ^ 1
                out = a * c + acc[:, q * LANE:(q + 1) * LANE] * sa
        o_ref[:, s * LANE:(s + 1) * LANE] = out.astype(o_ref.dtype)


def _proj(x, w, *, out_dtype, bm=1024, bn=1024, tabs=None, slab_pat=None, mode="lane", half=0,
          scale=1.0, seq_len=None, second=None, name="proj"):
    M, K = x.shape
    N = w.shape[1]
    bm = _tile(M, bm) if seq_len is None else _tile(seq_len, bm)
    bn = _tile(N, bn)
    if slab_pat is None:
        slab_pat = (-1,) * (bn // LANE)
    assert len(slab_pat) == bn // LANE
    in_specs = [pl.BlockSpec((bm, K), lambda i, j: (i, 0)),
                pl.BlockSpec((K, bn), lambda i, j: (0, j))]
    args = [x, w]
    if second is not None:
        x2, w2 = second
        in_specs += [pl.BlockSpec((bm, x2.shape[1]), lambda i, j: (i, 0)),
                     pl.BlockSpec((x2.shape[1], bn), lambda i, j: (0, j))]
        args += [x2, w2]
    if tabs is not None:
        tb = seq_len // bm
        tw = tabs[0].shape[1]
        in_specs += [pl.BlockSpec((bm, tw), lambda i, j: (i % tb, 0))] * 3
        args += list(tabs)
    kern = functools.partial(_proj_kernel, slab_pat=tuple(slab_pat), mode=mode, half=half,
                             scale=scale, with_tab=tabs is not None, two_inputs=second is not None)
    return pl.pallas_call(
        kern,
        grid=(M // bm, N // bn),
        in_specs=in_specs,
        out_specs=pl.BlockSpec((bm, bn), lambda i, j: (i, j)),
        out_shape=jax.ShapeDtypeStruct((M, N), out_dtype),
        compiler_params=_params("parallel", "parallel"),
        name=name,
    )(*args)


def _mm_ksplit_kernel(x_ref, w_ref, o_ref):
    part = jnp.dot(x_ref[...], w_ref[...], preferred_element_type=F32)

    @pl.when(pl.program_id(2) == 0)
    def _():
        o_ref[...] = part

    @pl.when(pl.program_id(2) > 0)
    def _():
        o_ref[...] += part


def _mm_ksplit(x, w, *, bm=1024, bn=1024, bk=2048, name="mm_ksplit"):
    M, K = x.shape
    N = w.shape[1]
    bm, bn, bk = _tile(M, bm), _tile(N, bn), _tile(K, bk)
    return pl.pallas_call(
        _mm_ksplit_kernel,
        grid=(M // bm, N // bn, K // bk),
        in_specs=[pl.BlockSpec((bm, bk), lambda i, j, k: (i, k)),
                  pl.BlockSpec((bk, bn), lambda i, j, k: (k, j))],
        out_specs=pl.BlockSpec((bm, bn), lambda i, j, k: (i, j)),
        out_shape=jax.ShapeDtypeStruct((M, N), F32),
        compiler_params=_params("parallel", "parallel", "arbitrary"),
        name=name,
    )(x, w)


def _swiglu_up_kernel(x_ref, w1_ref, w3_ref, o_ref):
    x = x_ref[...]
    a = jnp.dot(x, w1_ref[...], preferred_element_type=F32)
    b = jnp.dot(x, w3_ref[...], preferred_element_type=F32)
    o_ref[...] = (a * jax.nn.sigmoid(a) * b).astype(o_ref.dtype)


def _swiglu_up(x, w1, w3, *, bm=1024, bn=512):
    M, K = x.shape
    N = w1.shape[1]
    bm, bn = _tile(M, bm), _tile(N, bn)
    return pl.pallas_call(
        _swiglu_up_kernel,
        grid=(M // bm, N // bn),
        in_specs=[pl.BlockSpec((bm, K), lambda i, j: (i, 0)),
                  pl.BlockSpec((K, bn), lambda i, j: (0, j)),
                  pl.BlockSpec((K, bn), lambda i, j: (0, j))],
        out_specs=pl.BlockSpec((bm, bn), lambda i, j: (i, j)),
        out_shape=jax.ShapeDtypeStruct((M, N), BF16),
        compiler_params=_params("parallel", "parallel"),
        name="swiglu_up",
    )(x, w1, w3)


def _layer_norm_rows(z, g, b):
    mu = jnp.mean(z, axis=-1, keepdims=True)
    zc = z - mu
    var = jnp.mean(zc * zc, axis=-1, keepdims=True)
    return zc * lax.rsqrt(var + LN_EPS) * g + b


def _pack_bf16_pairs(x):
    half = x.shape[1] // 2
    hi = pltpu.bitcast(x[:, :half].astype(BF16).astype(F32), I32)
    lo = pltpu.bitcast(x[:, half:].astype(BF16).astype(F32), I32)
    return hi | lax.shift_right_logical(lo, 16)


def _unpack_bf16_pairs(w):
    hi = pltpu.bitcast(w & jnp.int32(-65536), F32).astype(BF16)
    lo = pltpu.bitcast(lax.shift_left(w, 16), F32).astype(BF16)
    return hi, lo


def _add_ln_kernel(x_ref, y_ref, g_ref, b_ref, of_ref, o2_ref, *, packed):
    out = _layer_norm_rows(ALPHA * x_ref[...] + y_ref[...], g_ref[...], b_ref[...])
    of_ref[...] = out
    o2_ref[...] = _pack_bf16_pairs(out) if packed else out.astype(o2_ref.dtype)


def _add_ln(x, y, g, b, *, packed=False, bm=256):
    M, D = x.shape
    bm = _tile(M, bm, 8)
    row = pl.BlockSpec((bm, D), lambda i: (i, 0))
    vec = pl.BlockSpec((1, D), lambda i: (0, 0))
    second = jax.ShapeDtypeStruct((M, D // 2), I32) if packed else jax.ShapeDtypeStruct((M, D), BF16)
    return pl.pallas_call(
        functools.partial(_add_ln_kernel, packed=packed),
        grid=(M // bm,),
        in_specs=[row, row, vec, vec],
        out_specs=[row, pl.BlockSpec((bm, second.shape[1]), lambda i: (i, 0))],
        out_shape=[jax.ShapeDtypeStruct((M, D), F32), second],
        compiler_params=_params("parallel"),
        name="add_ln",
    )(x, y, g.reshape(1, D), b.reshape(1, D))


LOG2E = 1.4426950408889634
SOFTMAX_ROW_BLOCK = 512


def _online_softmax(s, m, l):
    rows = s.shape[0]
    rb = min(SOFTMAX_ROW_BLOCK, rows)
    ms, ls, scales, ps = [], [], [], []
    for r in range(rows // rb):
        sl = slice(r * rb, (r + 1) * rb)
        s_r = s[sl]
        m_r = jnp.maximum(m[sl], jnp.max(s_r, axis=1, keepdims=True))
        a_r = jnp.exp2(m[sl] - m_r)
        p_r = jnp.exp2((s_r - m_r).astype(BF16))
        ms.append(m_r)
        ls.append(a_r * l[sl] + jnp.sum(p_r.astype(F32), axis=1, keepdims=True))
        scales.append(a_r)
        ps.append(p_r)
    cat = lambda parts: jnp.concatenate(parts, axis=0)
    return cat(ms), cat(ls), cat(scales), cat(ps)


def _dsa_kernel(q_ref, k_ref, v_ref, iq_ref, ik_ref, o_ref, keys_ref, bias_ref, *, topk, ck, n_rep, idx_bits):
    blk = pl.program_id(1)
    n_chunks = (blk * Q_BLOCK + Q_BLOCK + ck - 1) // ck
    row = lax.broadcasted_iota(I32, (Q_BLOCK, ck), 0) + blk * Q_BLOCK
    lane = lax.broadcasted_iota(I32, (Q_BLOCK, ck), 1)
    w_off = IDX_HEADS * IDX_DIM + IDX_DIM
    iq = iq_ref[0]
    wi = iq[:, w_off:w_off + IDX_HEADS]
    q_idx = jnp.concatenate([iq[:, h * IDX_DIM:(h + 1) * IDX_DIM].astype(BF16) for h in range(IDX_HEADS)], axis=0)

    def score_body(c, carry):
        off = pl.multiple_of(c * ck, ck)
        kc = ik_ref[0, pl.ds(off, ck), :][:, :IDX_DIM].astype(BF16)
        lg = lax.dot_general(q_idx, kc, NT_DIMS, preferred_element_type=F32)
        s = jnp.zeros((Q_BLOCK, ck), F32)
        for h in range(IDX_HEADS):
            s = s + wi[:, h:h + 1] * jnp.maximum(lg[h * Q_BLOCK:(h + 1) * Q_BLOCK], 0.0)
        bits = pltpu.bitcast(s, I32)
        key = bits ^ ((bits >> 31) & 0x7FFFFFFF)
        keys_ref[c] = jnp.where(lane + off <= row, key, INT_MIN)
        return carry

    lax.fori_loop(0, n_chunks, score_body, 0)

    def count(indicator):
        def body(c, acc):
            part = indicator(keys_ref[c], lane + c * ck)
            for j in range(ck // LANE):
                acc = acc + part[:, j * LANE:(j + 1) * LANE]
            return acc
        acc = lax.fori_loop(0, n_chunks, body, jnp.zeros((Q_BLOCK, LANE), F32))
        return jnp.sum(acc, axis=1, keepdims=True)

    kf = float(topk)
    ok = count(lambda kc, idx: jnp.where(kc >= 0, 1.0, 0.0)) >= kf
    thr = jnp.where(ok, 0, INT_MIN).astype(I32)

    def bit_body(i, thr):
        cand = thr + lax.shift_left(jnp.int32(1), 30 - i)
        ok = count(lambda kc, idx: jnp.where(kc >= cand, 1.0, 0.0)) >= kf
        return jnp.where(ok, cand, thr)

    thr = lax.fori_loop(0, 31, bit_body, thr)

    n_gt = count(lambda kc, idx: jnp.where(kc > thr, 1.0, 0.0))
    n_ge = count(lambda kc, idx: jnp.where(kc >= thr, 1.0, 0.0))
    need = kf - n_gt
    has_thr = thr > INT_MIN
    surplus = jnp.where(has_thr, n_ge - n_gt - need, 0.0)

    def tie_search():
        def tie_body(i, last):
            cand = last + lax.shift_left(jnp.int32(1), idx_bits - 1 - i)
            ok = count(lambda kc, idx: jnp.where(kc == thr, jnp.where(idx < cand, 1.0, 0.0), 0.0)) < need
            return jnp.where(ok, cand, last)
        return lax.fori_loop(0, idx_bits, tie_body, jnp.zeros((Q_BLOCK, 1), I32))

    last = lax.cond(jnp.max(surplus) > 0.0, tie_search, lambda: jnp.full((Q_BLOCK, 1), 2 ** idx_bits, I32))
    last = jnp.where(has_thr, last, -1)

    def bias_body(c, carry):
        kc = keys_ref[c]
        tie_bias = jnp.where(lane + c * ck <= last, 0.0, MASKED)
        bias_ref[c] = jnp.where(kc == thr, tie_bias, jnp.where(kc > thr, 0.0, MASKED))
        return carry

    lax.fori_loop(0, n_chunks, bias_body, 0)

    q = q_ref[0]
    rows = n_rep * Q_BLOCK
    qgs = [jnp.concatenate([q[:, (g * n_rep + r) * A_HEAD_DIM:(g * n_rep + r + 1) * A_HEAD_DIM]
                            for r in range(n_rep)], axis=0) for g in range(A_KV_HEADS)]

    def att_body(c, carry, n_sub):
        width = n_sub * ck
        off = pl.multiple_of(c * width, width)
        b = jnp.concatenate([bias_ref[c * n_sub + t] for t in range(n_sub)], axis=1)
        bias = jnp.concatenate([b] * n_rep, axis=0)
        out = []
        for g in range(A_KV_HEADS):
            m, l, acc = carry[g]
            kc = k_ref[0, pl.ds(off, width), g * A_HEAD_DIM:(g + 1) * A_HEAD_DIM]
            vc = v_ref[0, pl.ds(off, width), g * A_HEAD_DIM:(g + 1) * A_HEAD_DIM]
            s = lax.dot_general(qgs[g], kc, NT_DIMS, preferred_element_type=F32) + bias
            m, l, a, p = _online_softmax(s, m, l)
            out.append((m, l, a * acc + jnp.dot(p, vc, preferred_element_type=F32)))
        return tuple(out)

    one = (jnp.full((rows, 1), MASKED, F32), jnp.zeros((rows, 1), F32), jnp.zeros((rows, A_HEAD_DIM), F32))
    carry = (one,) * A_KV_HEADS
    if k_ref.shape[1] >= 2 * ck:
        carry = lax.fori_loop(0, n_chunks // 2, functools.partial(att_body, n_sub=2), carry)
        carry = lax.cond(n_chunks % 2 == 1, lambda c: att_body(n_chunks - 1, c, 1), lambda c: c, carry)
    else:
        carry = att_body(0, carry, 1)
    for g in range(A_KV_HEADS):
        _, l, acc = carry[g]
        o = acc / l
        for r in range(n_rep):
            col = (g * n_rep + r) * A_HEAD_DIM
            o_ref[0, :, col:col + A_HEAD_DIM] = o[r * Q_BLOCK:(r + 1) * Q_BLOCK].astype(o_ref.dtype)


def _dsa(q, k, v, idx, *, B, T):
    a_heads = q.shape[-1] // A_HEAD_DIM
    n_rep = a_heads // A_KV_HEADS
    topk = min(DSA_TOPK_MAX, T // 4)
    ck = _tile(T, 512)
    idx_w = idx.shape[-1]
    kv_w = k.shape[-1]
    kern = functools.partial(_dsa_kernel, topk=topk, ck=ck, n_rep=n_rep, idx_bits=max(1, (T - 1).bit_length()))
    return pl.pallas_call(
        kern,
        grid=(B, T // Q_BLOCK),
        in_specs=[pl.BlockSpec((1, Q_BLOCK, q.shape[-1]), lambda b, i: (b, i, 0)),
                  pl.BlockSpec((1, T, kv_w), lambda b, i: (b, 0, 0)),
                  pl.BlockSpec((1, T, kv_w), lambda b, i: (b, 0, 0)),
                  pl.BlockSpec((1, Q_BLOCK, idx_w), lambda b, i: (b, i, 0)),
                  pl.BlockSpec((1, T, LANE), lambda b, i: (b, 0, IDX_HEADS * IDX_DIM // LANE))],
        out_specs=pl.BlockSpec((1, Q_BLOCK, q.shape[-1]), lambda b, i: (b, i, 0)),
        out_shape=jax.ShapeDtypeStruct(q.shape, BF16),
        scratch_shapes=[pltpu.VMEM((T // ck, Q_BLOCK, ck), I32), pltpu.VMEM((T // ck, Q_BLOCK, ck), F32)],
        compiler_params=_params("parallel", "parallel"),
        name="dsa",
    )(q, k, v, idx, idx)


def _retention_kernel(q_ref, k_ref, v_ref, g_ref, gn_ref, din_ref, qd_ref, kd_ref, cd_ref, o_ref, state_ref, *, n_sub):
    @pl.when(pl.program_id(2) == 0)
    def _():
        state_ref[...] = jnp.zeros_like(state_ref)

    din = din_ref[0]
    qd = qd_ref[0]
    kd = kd_ref[0]
    cd = cd_ref[0]
    gn = gn_ref[...]
    for s in range(n_sub):
        sl = pl.ds(s * RET_CHUNK, RET_CHUNK)
        qc = q_ref[sl, :]
        kc = k_ref[sl, :]
        vc = v_ref[sl, :]
        st = state_ref[...]
        inner = lax.dot_general(qc, kc, NT_DIMS, preferred_element_type=F32) * din
        o = (jnp.dot(inner.astype(BF16), vc, preferred_element_type=F32)
             + jnp.dot(qc, st.astype(BF16), preferred_element_type=F32) * qd)
        vk = (vc.astype(F32) * kd).astype(BF16)
        state_ref[...] = st * cd + lax.dot_general(kc, vk, TN_DIMS, preferred_element_type=F32)
        mu = jnp.mean(o, axis=-1, keepdims=True)
        oc = o - mu
        var = jnp.mean(oc * oc, axis=-1, keepdims=True)
        gate = g_ref[sl, :].astype(F32)
        o_ref[sl, :] = (gate * jax.nn.sigmoid(gate) * (oc * lax.rsqrt(var + LN_EPS) * gn)).astype(o_ref.dtype)


def _retention(qk, pv, gn_g, *, B, T, heads, v_blk0):
    N = qk.shape[0]
    C = RET_CHUNK
    rb = _tile(T, 512)
    n_sub = rb // C
    nr = T // rb
    log_gamma = jnp.log(1.0 - 2.0 ** (-5.0 - jnp.arange(heads, dtype=F32)))
    pos = jnp.arange(C, dtype=F32)
    diff = pos[:, None] - pos[None, :]
    din = jnp.exp(jnp.where(diff[None] >= 0, log_gamma[:, None, None] * diff[None], -jnp.inf))
    qd = jnp.exp(log_gamma[:, None] * (pos[None] + 1.0))[:, :, None]
    kd = jnp.exp(log_gamma[:, None] * (C - 1.0 - pos[None]))[:, :, None]
    cd = jnp.exp(log_gamma * C)[:, None, None]
    W = RET_VAL_DIM
    blk = lambda off: pl.BlockSpec((rb, W), lambda b, h, r: (b * nr + r, off + h))
    per_head = lambda shape: pl.BlockSpec((1,) + shape, lambda b, h, r: (h, 0, 0))
    return pl.pallas_call(
        functools.partial(_retention_kernel, n_sub=n_sub),
        grid=(B, heads, nr),
        in_specs=[blk(0), blk(heads), blk(v_blk0), blk(v_blk0 + heads),
                  pl.BlockSpec((1, W), lambda b, h, r: (0, h)),
                  per_head((C, C)), per_head((C, 1)), per_head((C, 1)), per_head((1, 1))],
        out_specs=pl.BlockSpec((rb, W), lambda b, h, r: (b * nr + r, h)),
        out_shape=jax.ShapeDtypeStruct((N, heads * W), BF16),
        scratch_shapes=[pltpu.VMEM((RET_KEY_DIM, RET_VAL_DIM), F32)],
        compiler_params=_params("parallel", "parallel", "arbitrary"),
        name="retention",
    )(qk, qk, pv, pv, gn_g.reshape(1, heads * W), din, qd, kd, cd)


def _mla_down_kernel(x_ref, w_ref, qg_ref, kvg_ref, c_ref, sa_ref, sb_ref, cq_ref, ckv_ref, kr_ref, *, q_rank, kv_rank):
    acc = jnp.dot(x_ref[...], w_ref[...], preferred_element_type=F32)

    def rms(a, g):
        return a * lax.rsqrt(jnp.mean(a * a, axis=-1, keepdims=True) + RMS_EPS) * g

    cq_ref[...] = rms(acc[:, :q_rank], qg_ref[...]).astype(cq_ref.dtype)
    ckv_ref[...] = rms(acc[:, q_rank:q_rank + kv_rank], kvg_ref[...]).astype(ckv_ref.dtype)
    kr = acc[:, q_rank + kv_rank:]
    half = MLA_ROPE // 2
    kr = kr * c_ref[...] + pltpu.roll(kr, LANE - half, 1) * sa_ref[...] + pltpu.roll(kr, half, 1) * sb_ref[...]
    kr_ref[...] = kr.astype(kr_ref.dtype)


def _mla_down(x, w, q_g, kv_g, tabs, *, T, q_rank, kv_rank, bm=512):
    M, K = x.shape
    Nw = w.shape[1]
    bm = _tile(T, bm)
    tb = T // bm
    row = lambda n: pl.BlockSpec((bm, n), lambda i: (i, 0))
    tab = pl.BlockSpec((bm, LANE), lambda i: (i % tb, 0))
    return pl.pallas_call(
        functools.partial(_mla_down_kernel, q_rank=q_rank, kv_rank=kv_rank),
        grid=(M // bm,),
        in_specs=[row(K), pl.BlockSpec((K, Nw), lambda i: (0, 0)),
                  pl.BlockSpec((1, q_rank), lambda i: (0, 0)), pl.BlockSpec((1, kv_rank), lambda i: (0, 0)),
                  tab, tab, tab],
        out_specs=[row(q_rank), row(kv_rank), row(LANE)],
        out_shape=[jax.ShapeDtypeStruct((M, q_rank), BF16), jax.ShapeDtypeStruct((M, kv_rank), BF16),
                   jax.ShapeDtypeStruct((M, LANE), BF16)],
        compiler_params=_params("parallel"),
        name="mla_down",
    )(x, w, q_g.reshape(1, q_rank), kv_g.reshape(1, kv_rank), *tabs)


MLA_HEADS_PER_STEP = 4


def _mla_attn_kernel(qn_ref, qr_ref, kn_ref, kr_ref, v_ref, o_ref, *, tq):
    i = pl.program_id(2)
    lane = lax.broadcasted_iota(I32, (tq, LANE), 1)
    qs = []
    for j in range(MLA_HEADS_PER_STEP):
        lo = (j % 2) * MLA_ROPE
        own = jnp.where((lane >= lo) & (lane < lo + MLA_ROPE), 1.0, 0.0)
        pair = qr_ref[:, (j // 2) * LANE:(j // 2 + 1) * LANE].astype(F32)
        qr = (pair * own).astype(BF16)
        qs.append(jnp.concatenate([qn_ref[:, j * LANE:(j + 1) * LANE], qr], axis=1))

    def step(c, carry, masked, width):
        off = pl.multiple_of(c * width, width)
        kr = kr_ref[pl.ds(off, width), :]
        out = []
        for j in range(MLA_HEADS_PER_STEP):
            m, l, acc = carry[j]
            k = jnp.concatenate([kn_ref[pl.ds(off, width), j * LANE:(j + 1) * LANE], kr], axis=1)
            s = lax.dot_general(qs[j], k, NT_DIMS, preferred_element_type=F32)
            if masked:
                r_io = lax.broadcasted_iota(I32, (tq, width), 0)
                c_io = lax.broadcasted_iota(I32, (tq, width), 1)
                s = jnp.where(c_io <= r_io, s, MASKED)
            m, l, a, p = _online_softmax(s, m, l)
            v = v_ref[pl.ds(off, width), j * MLA_V:(j + 1) * MLA_V]
            out.append((m, l, a * acc + jnp.dot(p, v, preferred_element_type=F32)))
        return tuple(out)

    one = (jnp.full((tq, 1), MASKED, F32), jnp.zeros((tq, 1), F32), jnp.zeros((tq, MLA_V), F32))
    carry = (one,) * MLA_HEADS_PER_STEP
    if kn_ref.shape[0] >= 2 * tq:
        carry = lax.fori_loop(0, i // 2, functools.partial(step, masked=False, width=2 * tq), carry)
        carry = lax.cond(i % 2 == 1, lambda c: step(i - 1, c, False, tq), lambda c: c, carry)
    carry = step(i, carry, True, tq)
    for j in range(MLA_HEADS_PER_STEP):
        _, l, acc = carry[j]
        o_ref[:, j * MLA_V:(j + 1) * MLA_V] = (acc / l).astype(o_ref.dtype)


def _mla_attn(qn, qr, kv, kr, *, B, T, heads):
    N = qn.shape[0]
    tq = _tile(T, 512)
    nq = T // tq
    hp = MLA_HEADS_PER_STEP
    w = hp * LANE
    return pl.pallas_call(
        functools.partial(_mla_attn_kernel, tq=tq),
        grid=(B, heads // hp, nq),
        in_specs=[pl.BlockSpec((tq, w), lambda b, h, i: (b * nq + i, h)),
                  pl.BlockSpec((tq, hp * MLA_ROPE), lambda b, h, i: (b * nq + i, h)),
                  pl.BlockSpec((T, w), lambda b, h, i: (b, h)),
                  pl.BlockSpec((T, LANE), lambda b, h, i: (b, 0)),
                  pl.BlockSpec((T, w), lambda b, h, i: (b, heads // hp + h))],
        out_specs=pl.BlockSpec((tq, w), lambda b, h, i: (b * nq + i, h)),
        out_shape=jax.ShapeDtypeStruct((N, heads * MLA_V), BF16),
        compiler_params=_params("parallel", "parallel", "parallel"),
        name="mla_attn",
    )(qn, qr, kv, kr, kv)


def _router_kernel(x_ref, r_ref, meta_ref, cnt_ref, carry_ref, *, n_exp):
    @pl.when(pl.program_id(0) == 0)
    def _():
        carry_ref[...] = jnp.zeros_like(carry_ref)

    bm = x_ref.shape[0]
    logits = jnp.dot(x_ref[...], r_ref[...], preferred_element_type=F32, precision=lax.Precision.HIGHEST)
    lane = lax.broadcasted_iota(I32, (bm, LANE), 1).astype(F32)
    logits = jnp.where(lane < n_exp, logits, -jnp.inf)
    m1 = jnp.max(logits, axis=1, keepdims=True)
    i1 = jnp.min(jnp.where(logits == m1, lane, float(LANE)), axis=1, keepdims=True)
    rest = jnp.where(lane == i1, -jnp.inf, logits)
    m2 = jnp.max(rest, axis=1, keepdims=True)
    i2 = jnp.min(jnp.where(rest == m2, lane, float(LANE)), axis=1, keepdims=True)
    e = jnp.exp(m2 - m1)
    g1 = 1.0 / (1.0 + e)
    g2 = e / (1.0 + e)
    sel = jnp.where(lane == i1, 1.0, jnp.where(lane == i2, 1.0, 0.0))
    r_io = lax.broadcasted_iota(I32, (bm, bm), 0)
    c_io = lax.broadcasted_iota(I32, (bm, bm), 1)
    below = jnp.where(c_io < r_io, 1.0, 0.0).astype(BF16)
    carry = carry_ref[0:1, :]
    rank = jnp.dot(below, sel.astype(BF16), preferred_element_type=F32) + carry
    r1 = jnp.sum(jnp.where(lane == i1, rank, 0.0), axis=1, keepdims=True)
    r2 = jnp.sum(jnp.where(lane == i2, rank, 0.0), axis=1, keepdims=True)
    meta = jnp.where(lane == 0, i1, 0.0)
    meta = jnp.where(lane == 1, i2, meta)
    meta = jnp.where(lane == 2, g1, meta)
    meta = jnp.where(lane == 3, g2, meta)
    meta = jnp.where(lane == 4, r1, meta)
    meta = jnp.where(lane == 5, r2, meta)
    meta_ref[...] = meta
    total = carry + jnp.sum(sel, axis=0, keepdims=True)
    carry_ref[...] = jnp.broadcast_to(total, carry_ref.shape)
    cnt_ref[...] = jnp.broadcast_to(total, cnt_ref.shape)


def _router(x, router, *, bm=512):
    M, D = x.shape
    n_exp = router.shape[1]
    bm = _tile(M, bm)
    r_pad = jnp.zeros((D, LANE), F32).at[:, :n_exp].set(router)
    return pl.pallas_call(
        functools.partial(_router_kernel, n_exp=n_exp),
        grid=(M // bm,),
        in_specs=[pl.BlockSpec((bm, D), lambda i: (i, 0)), pl.BlockSpec((D, LANE), lambda i: (0, 0))],
        out_specs=[pl.BlockSpec((bm, LANE), lambda i: (i, 0)), pl.BlockSpec((8, LANE), lambda i: (0, 0))],
        out_shape=[jax.ShapeDtypeStruct((M, LANE), F32), jax.ShapeDtypeStruct((8, LANE), F32)],
        scratch_shapes=[pltpu.VMEM((8, LANE), F32)],
        compiler_params=_params("arbitrary"),
        name="router",
    )(x, r_pad)


def _moe_gather_kernel(tok_ref, nv_ref, x_hbm, o_ref, stage_ref, sem, *, tm):
    r = pl.program_id(0)
    nv = nv_ref[0]

    half = stage_ref.shape[1]

    def row_copy(j, tok):
        return pltpu.make_async_copy(x_hbm.at[pl.ds(tok, 1)], stage_ref.at[pl.ds(j, 1)], sem)

    def gather_start(tile):
        def body(j, carry):
            row_copy(j, tok_ref[tile * tm + j]).start()
            return carry
        lax.fori_loop(0, tm, body, 0)

    def gather_wait():
        def body(j, carry):
            row_copy(j, 0).wait()
            return carry
        lax.fori_loop(0, tm, body, 0)

    @pl.when(r < nv)
    def _():
        @pl.when(r == 0)
        def _():
            gather_start(0)

        gather_wait()
        hi, lo = _unpack_bf16_pairs(stage_ref[...])
        o_ref[:, :half] = hi
        o_ref[:, half:] = lo

        @pl.when(r + 1 < nv)
        def _():
            gather_start(r + 1)


def _moe_up_kernel(te_ref, nv_ref, xs_ref, w1_ref, w3_ref, o_ref, wb1_ref, wb3_ref):
    r = pl.program_id(1)

    @pl.when(r < nv_ref[0])
    def _():
        @pl.when((r == 0) | (te_ref[r] != te_ref[jnp.maximum(r - 1, 0)]))
        def _():
            wb1_ref[...] = w1_ref[0].astype(BF16)
            wb3_ref[...] = w3_ref[0].astype(BF16)

        xb = xs_ref[...]
        a = jnp.dot(xb, wb1_ref[...], preferred_element_type=F32)
        b = jnp.dot(xb, wb3_ref[...], preferred_element_type=F32)
        o_ref[...] = (a * jax.nn.sigmoid(a) * b).astype(o_ref.dtype)


def _moe_down_kernel(te_ref, nv_ref, h_ref, w2_ref, o_ref, wb_ref):
    r = pl.program_id(1)

    @pl.when(r < nv_ref[0])
    def _():
        @pl.when((r == 0) | (te_ref[r] != te_ref[jnp.maximum(r - 1, 0)]))
        def _():
            wb_ref[...] = w2_ref[0].astype(BF16)

        o_ref[...] = jnp.dot(h_ref[...], wb_ref[...], preferred_element_type=F32)


def _experts(x, w1, w3, w2, tile_expert, n_valid, row_tok, *, tm, tf=512, tn=512):
    D = 2 * x.shape[1]
    P = row_tok.shape[0]
    E, _, F = w1.shape
    tf, tn = _tile(F, tf), _tile(D, tn)
    n_tiles = P // tm
    row = lambda r, nv: jnp.minimum(r, nv[0] - 1)

    xs = pl.pallas_call(
        functools.partial(_moe_gather_kernel, tm=tm),
        grid_spec=pltpu.PrefetchScalarGridSpec(
            num_scalar_prefetch=2, grid=(n_tiles,),
            in_specs=[pl.BlockSpec(memory_space=pl.ANY)],
            out_specs=pl.BlockSpec((tm, D), lambda r, tok, nv: (row(r, nv), 0)),
            scratch_shapes=[pltpu.VMEM((tm, D // 2), I32), pltpu.SemaphoreType.DMA(())]),
        out_shape=jax.ShapeDtypeStruct((P, D), BF16),
        compiler_params=_params("arbitrary"),
        name="moe_gather",
    )(row_tok, n_valid, x)

    hid = pl.pallas_call(
        _moe_up_kernel,
        grid_spec=pltpu.PrefetchScalarGridSpec(
            num_scalar_prefetch=2, grid=(F // tf, n_tiles),
            in_specs=[pl.BlockSpec((tm, D), lambda f, r, te, nv: (row(r, nv), 0)),
                      pl.BlockSpec((1, D, tf), lambda f, r, te, nv: (te[r], 0, f)),
                      pl.BlockSpec((1, D, tf), lambda f, r, te, nv: (te[r], 0, f))],
            out_specs=pl.BlockSpec((tm, tf), lambda f, r, te, nv: (row(r, nv), f)),
            scratch_shapes=[pltpu.VMEM((D, tf), BF16), pltpu.VMEM((D, tf), BF16)]),
        out_shape=jax.ShapeDtypeStruct((P, F), BF16),
        compiler_params=_params("arbitrary", "arbitrary"),
        name="moe_up",
    )(tile_expert, n_valid, xs, w1, w3)

    return pl.pallas_call(
        _moe_down_kernel,
        grid_spec=pltpu.PrefetchScalarGridSpec(
            num_scalar_prefetch=2, grid=(D // tn, n_tiles),
            in_specs=[pl.BlockSpec((tm, F), lambda n, r, te, nv: (row(r, nv), 0)),
                      pl.BlockSpec((1, F, tn), lambda n, r, te, nv: (te[r], 0, n))],
            out_specs=pl.BlockSpec((tm, tn), lambda n, r, te, nv: (row(r, nv), n)),
            scratch_shapes=[pltpu.VMEM((F, tn), BF16)]),
        out_shape=jax.ShapeDtypeStruct((P, D), F32),
        compiler_params=_params("arbitrary", "arbitrary"),
        name="moe_down",
    )(tile_expert, n_valid, hid, w2)


def _combine_kernel(dest_ref, x_ref, meta_ref, g_ref, b_ref, ys_hbm, o_ref, buf_ref, sem):
    bm = x_ref.shape[0]
    i = pl.program_id(0)
    n = pl.num_programs(0)

    def copy(slot, j, k, d):
        return pltpu.make_async_copy(ys_hbm.at[pl.ds(d, 1)], buf_ref.at[slot, k, pl.ds(j, 1)], sem.at[slot, k])

    def gather_start(blk):
        slot = blk % 2

        def body(j, carry):
            for k in range(2):
                copy(slot, j, k, dest_ref[2 * (blk * bm + j) + k]).start()
            return carry
        lax.fori_loop(0, bm, body, 0)

    @pl.when(i == 0)
    def _():
        gather_start(0)

    @pl.when(i + 1 < n)
    def _():
        gather_start(i + 1)

    slot = i % 2

    def wait(j, carry):
        for k in range(2):
            copy(slot, j, k, 0).wait()
        return carry

    lax.fori_loop(0, bm, wait, 0)
    meta = meta_ref[...]
    y = meta[:, 2:3] * buf_ref[slot, 0] + meta[:, 3:4] * buf_ref[slot, 1]
    o_ref[...] = _layer_norm_rows(ALPHA * x_ref[...] + y, g_ref[...], b_ref[...])


def _combine(x, meta, ys, dest, g, b, *, bm=256):
    M, D = x.shape
    bm = _tile(M, bm, 8)
    row = lambda n: pl.BlockSpec((bm, n), lambda i, d: (i, 0))
    vec = pl.BlockSpec((1, D), lambda i, d: (0, 0))
    return pl.pallas_call(
        _combine_kernel,
        grid_spec=pltpu.PrefetchScalarGridSpec(
            num_scalar_prefetch=1, grid=(M // bm,),
            in_specs=[row(D), row(LANE), vec, vec, pl.BlockSpec(memory_space=pl.ANY)],
            out_specs=row(D),
            scratch_shapes=[pltpu.VMEM((2, 2, bm, D), F32), pltpu.SemaphoreType.DMA((2, 2))]),
        out_shape=jax.ShapeDtypeStruct((M, D), F32),
        compiler_params=_params("arbitrary"),
        name="moe_combine",
    )(dest, x, meta, g.reshape(1, D), b.reshape(1, D), ys)


def _even_layer(x, xb, w_in, ret_gn_g, w_out, ln1_g, ln1_b, w1, w3, w2, ln2_g, ln2_b, *, B, T):
    N, D = x.shape
    a_heads = D // 2 // A_HEAD_DIM
    r_heads = D // 2 // RET_VAL_DIM
    qa_w, kv_w = a_heads * A_HEAD_DIM, A_KV_HEADS * A_HEAD_DIM
    qi_w = IDX_HEADS * IDX_DIM
    rk_w, rv_w = r_heads * RET_KEY_DIM, r_heads * RET_VAL_DIM
    sizes = (qa_w, kv_w, kv_w, qi_w, IDX_DIM, IDX_HEADS, rk_w, rk_w, rv_w, rv_w)
    offs = [0]
    for s in sizes:
        offs.append(offs[-1] + s)
    col = lambda a, b_: w_in[:, offs[a]:offs[b_]]
    w_qa, w_ka, w_va = col(0, 1), col(1, 2), col(2, 3)
    w_qi, w_ki, w_wi = col(3, 4), col(4, 5), col(5, 6)
    w_qb, w_kb, w_vb, w_gb = col(6, 7), col(7, 8), col(8, 9), col(9, 10)

    cos_a, sin_a = _rope_cos_sin(T, A_HEAD_DIM // 4, ROPE_THETA)
    tab_q = _lane_tables(cos_a, sin_a, A_HEAD_DIM, A_HEAD_DIM ** -0.5 * LOG2E)
    tab_k = _lane_tables(cos_a, sin_a, A_HEAD_DIM)
    cos_i, sin_i = _rope_cos_sin(T, IDX_DIM // 4, ROPE_THETA)
    tab_i = _lane_tables(cos_i, sin_i, IDX_DIM)
    pass_c = jnp.ones((T, LANE - IDX_DIM), F32)
    pass_s = jnp.zeros((T, LANE - IDX_DIM), F32)
    tab_idx = tuple(jnp.concatenate([t, t[:, :IDX_DIM], p], 1)
                    for t, p in zip(tab_i, (pass_c, pass_s, pass_s)))
    inv = 1.0 / (RET_THETA ** jnp.linspace(0.0, 1.0, RET_KEY_DIM // 2, dtype=F32))
    ang = jnp.arange(T, dtype=F32)[:, None] * inv[None, :]
    cos_r, sin_r = jnp.cos(ang), jnp.sin(ang)
    sin_pair = jnp.concatenate([-sin_r, sin_r], 1)
    tab_r = (jnp.concatenate([cos_r, cos_r], 1), sin_pair, sin_pair)

    idx_pad = LANE - IDX_DIM - IDX_HEADS
    w_idx = jnp.concatenate([w_qi, w_ki, w_wi * (IDX_DIM ** -0.5 * IDX_HEADS ** -0.5),
                             jnp.zeros((D, idx_pad), F32)], 1).astype(BF16)
    w_rqk = jnp.concatenate([w_qb, w_kb * RET_KEY_DIM ** -0.5], 1).astype(BF16)
    w_pv = jnp.concatenate([w_va, w_vb, w_gb], 1).astype(BF16)
    qa = _proj(xb, w_qa.astype(BF16), out_dtype=BF16, tabs=tab_q, half=A_HEAD_DIM // 8,
               seq_len=T, name="proj_qa", **_pat(qa_w, 1024, 0))
    ka = _proj(xb, w_ka.astype(BF16), out_dtype=BF16, tabs=tab_k, half=A_HEAD_DIM // 8,
               seq_len=T, name="proj_ka", **_pat(kv_w, 1024, 0))
    n_idx = w_idx.shape[1]
    idx = _proj(xb, w_idx, out_dtype=F32, tabs=tab_idx, half=IDX_DIM // 8, seq_len=T, name="proj_idx",
                bm=512, bn=n_idx, slab_pat=(0,) * (qi_w // LANE) + (1,))
    rqk = _proj(xb, w_rqk, out_dtype=BF16, tabs=tab_r, mode="pair", seq_len=T, name="proj_ret_qk",
                bn=1024, slab_pat=(0, 1) * (_tile(2 * rk_w, 1024) // (2 * LANE)))
    pv = _proj(xb, w_pv, out_dtype=BF16, bn=768, seq_len=T, name="proj_v")

    ya = _dsa(qa.reshape(B, T, qa_w), ka.reshape(B, T, kv_w), pv.reshape(B, T, -1), idx.reshape(B, T, n_idx),
              B=B, T=T)
    yb = _retention(rqk, pv, ret_gn_g, B=B, T=T, heads=r_heads, v_blk0=kv_w // RET_VAL_DIM)
    w_out_b = w_out.astype(BF16)
    y = _proj(ya.reshape(N, qa_w), w_out_b[:qa_w], second=(yb, w_out_b[qa_w:]), out_dtype=F32, name="proj_out0")
    x1, x1b = _add_ln(x, y, ln1_g, ln1_b)
    hid = _swiglu_up(x1b, w1.astype(BF16), w3.astype(BF16))
    y = _mm_ksplit(hid, w2.astype(BF16), bk=3584, name="ffn_down")
    return _add_ln(x1, y, ln2_g, ln2_b)


def _pat(width, bn, p):
    bn = _tile(width, bn)
    return dict(bn=bn, slab_pat=(p,) * (bn // LANE))


def _odd_layer(x, xb, w_dq_dkv, q_norm_g, w_uq, kv_norm_g, w_ukv, w_out, ln1_g, ln1_b,
               router, we1, we3, we2, ln2_g, ln2_b, *, B, T):
    N, D = x.shape
    heads = D // MLA_V
    q_rank, kv_rank = q_norm_g.shape[0], kv_norm_g.shape[0]
    scale = (MLA_NOPE + MLA_ROPE) ** -0.5 * LOG2E
    cos_c, sin_c = _rope_cos_sin(T, MLA_ROPE, ROPE_THETA)
    tab_kr = _lane_tables(cos_c, sin_c, MLA_ROPE)
    tab_qr = _lane_tables(cos_c, sin_c, MLA_ROPE, scale)

    w_kr = w_dq_dkv[:, q_rank + kv_rank:]
    w_down = jnp.concatenate([w_dq_dkv[:, :q_rank + kv_rank], w_kr, w_kr], 1).astype(BF16)
    cq, ckv, kr = _mla_down(xb, w_down, q_norm_g, kv_norm_g, tab_kr, T=T, q_rank=q_rank, kv_rank=kv_rank)
    w_uq3 = w_uq.reshape(q_rank, heads, MLA_NOPE + MLA_ROPE)
    w_qn = w_uq3[:, :, :MLA_NOPE].reshape(q_rank, heads * MLA_NOPE).astype(BF16)
    w_qr = w_uq3[:, :, MLA_NOPE:].reshape(q_rank, heads * MLA_ROPE).astype(BF16)
    w_kv3 = w_ukv.reshape(kv_rank, heads, MLA_NOPE + MLA_V)
    w_kv = jnp.concatenate([w_kv3[:, :, :MLA_NOPE].reshape(kv_rank, heads * MLA_NOPE),
                            w_kv3[:, :, MLA_NOPE:].reshape(kv_rank, heads * MLA_V)], 1).astype(BF16)
    qn = _proj(cq, w_qn, out_dtype=BF16, scale=scale, seq_len=T, name="proj_q_nope")
    qr = _proj(cq, w_qr, out_dtype=BF16, tabs=tab_qr, half=MLA_ROPE // 2, seq_len=T, name="proj_q_rope",
               **_pat(heads * MLA_ROPE, 1024, 0))
    kv = _proj(ckv, w_kv, out_dtype=BF16, seq_len=T, name="proj_kv")
    att = _mla_attn(qn, qr, kv, kr, B=B, T=T, heads=heads)
    y = _proj(att, w_out.astype(BF16), out_dtype=F32, name="proj_out1")
    x1, x1_packed = _add_ln(x, y, ln1_g, ln1_b, packed=True)

    E = router.shape[1]
    F = we1.shape[2]
    tm = _tile(N, 512)
    meta, cnt = _router(x1, router)
    counts = cnt[0, :E].astype(I32)
    padded = (counts + tm - 1) // tm * tm
    ends = jnp.cumsum(padded)
    starts = ends - padded
    i1, i2 = meta[:, 0].astype(I32), meta[:, 1].astype(I32)
    dest = jnp.stack([starts[i1] + meta[:, 4].astype(I32), starts[i2] + meta[:, 5].astype(I32)], 1).reshape(-1)
    n_rows = 2 * N + E * tm
    n_tiles = n_rows // tm
    n_valid = (ends[-1] // tm).astype(I32).reshape(1)
    tile_start = jnp.arange(n_tiles, dtype=I32) * tm
    tile_expert = jnp.minimum(jnp.sum(tile_start[:, None] >= ends[None, :], axis=1), E - 1).astype(I32)
    tile_expert = jnp.where(jnp.arange(n_tiles) < n_valid[0], tile_expert, tile_expert[jnp.maximum(n_valid[0] - 1, 0)])
    row_tok = jnp.zeros((n_rows,), I32).at[dest].set(jnp.repeat(jnp.arange(N, dtype=I32), 2))
    ys = _experts(x1_packed, we1, we3, we2, tile_expert, n_valid, row_tok, tm=tm)
    return _combine(x1, meta, ys, dest, ln2_g, ln2_b)


def kernel(x, l0_w_in, l0_ret_gn_g, l0_w_out, l0_ln1_g, l0_ln1_b, l0_ffn_w1, l0_ffn_w3, l0_ffn_w2, l0_ln2_g, l0_ln2_b, l1_w_dq_dkv, l1_q_norm_g, l1_w_uq, l1_kv_norm_g, l1_w_ukv, l1_w_out, l1_ln1_g, l1_ln1_b, l1_router, l1_moe_w1, l1_moe_w3, l1_moe_w2, l1_ln2_g, l1_ln2_b):
    B, T, D = x.shape
    x2 = x.reshape(B * T, D)
    h, hb = _even_layer(x2, x2.astype(BF16), l0_w_in, l0_ret_gn_g, l0_w_out, l0_ln1_g, l0_ln1_b,
                        l0_ffn_w1, l0_ffn_w3, l0_ffn_w2, l0_ln2_g, l0_ln2_b, B=B, T=T)
    out = _odd_layer(h, hb, l1_w_dq_dkv, l1_q_norm_g, l1_w_uq, l1_kv_norm_g, l1_w_ukv, l1_w_out,
                     l1_ln1_g, l1_ln1_b, l1_router, l1_moe_w1, l1_moe_w3, l1_moe_w2, l1_ln2_g, l1_ln2_b, B=B, T=T)
    return out.reshape(B, T, D)
```

```python
import functools

import jax
import jax.numpy as jnp
from jax import lax
from jax.experimental import pallas as pl
from jax.experimental.pallas import tpu as pltpu

F32 = jnp.float32
BF16 = jnp.bfloat16
I32 = jnp.int32

A_HEAD_DIM = 128
A_KV_HEADS = 4
IDX_HEADS = 16
IDX_DIM = 64
DSA_TOPK_MAX = 256
RET_KEY_DIM = 256
RET_VAL_DIM = 256
RET_CHUNK = 128
RET_THETA = 10000.0
MLA_V = 128
MLA_NOPE = 128
MLA_ROPE = 64
ROPE_THETA = 500000.0
Q_BLOCK = 128
LN_EPS = 1e-5
RMS_EPS = 1e-6
DEPTH = 2
ALPHA = (2.0 * DEPTH) ** 0.25

LANE = 128
V7X_VMEM_BYTES = 64 * 1024 * 1024
VMEM_LIMIT = V7X_VMEM_BYTES - 8 * 1024 * 1024
MASKED = -1e30
INT_MIN = -(2 ** 31)

NT_DIMS = (((1,), (1,)), ((), ()))
TN_DIMS = (((0,), (0,)), ((), ()))


def _tile(n, pref, mult=LANE):
    if n <= pref:
        return n
    t = (pref // mult) * mult
    while t > mult and n % t:
        t -= mult
    assert n % t == 0, (n, pref, mult)
    return t


def _params(*sem):
    return pltpu.CompilerParams(dimension_semantics=sem, vmem_limit_bytes=VMEM_LIMIT)


def _rope_cos_sin(T, rot_dim, theta):
    inv = theta ** (-jnp.arange(0, rot_dim, 2, dtype=F32) / rot_dim)
    ang = jnp.arange(T, dtype=F32)[:, None] * inv[None, :]
    return jnp.cos(ang), jnp.sin(ang)


def _lane_tables(cos, sin, head_dim, scale=1.0):
    T, half = cos.shape
    rest = head_dim - 2 * half
    zh = jnp.zeros((T, half), F32)
    c = jnp.concatenate([cos, cos, jnp.ones((T, rest), F32)], 1)
    sa = jnp.concatenate([-sin, zh, jnp.zeros((T, rest), F32)], 1)
    sb = jnp.concatenate([zh, sin, jnp.zeros((T, rest), F32)], 1)
    reps = LANE // head_dim
    return tuple(jnp.tile(t * scale, (1, reps)) for t in (c, sa, sb))


def _proj_kernel(*refs, slab_pat, mode, half, scale, with_tab, two_inputs):
    x_ref, w_ref = refs[:2]
    acc = jnp.dot(x_ref[...], w_ref[...], preferred_element_type=F32)
    refs = refs[2:]
    if two_inputs:
        acc = acc + jnp.dot(refs[0][...], refs[1][...], preferred_element_type=F32)
        refs = refs[2:]
    if with_tab:
        c_ref, sa_ref, sb_ref, o_ref = refs
    else:
        (o_ref,) = refs
    for s, p in enumerate(slab_pat):
        a = acc[:, s * LANE:(s + 1) * LANE]
        if p < 0:
            out = a if scale == 1.0 else a * scale
        else:
            c = c_ref[:, p * LANE:(p + 1) * LANE]
            sa = sa_ref[:, p * LANE:(p + 1) * LANE]
            if mode == "lane":
                sb = sb_ref[:, p * LANE:(p + 1) * LANE]
                out = a * c + pltpu.roll(a, LANE - half, 1) * sa + pltpu.roll(a, half, 1) * sb
            else:
                q = s ^ 1
                out = a * c + acc[:, q * LANE:(q + 1) * LANE] * sa
        o_ref[:, s * LANE:(s + 1) * LANE] = out.astype(o_ref.dtype)


def _proj(x, w, *, out_dtype, bm=1024, bn=1024, tabs=None, slab_pat=None, mode="lane", half=0,
          scale=1.0, seq_len=None, second=None, name="proj"):
    M, K = x.shape
    N = w.shape[1]
    bm = _tile(M, bm) if seq_len is None else _tile(seq_len, bm)
    bn = _tile(N, bn)
    if slab_pat is None:
        slab_pat = (-1,) * (bn // LANE)
    assert len(slab_pat) == bn // LANE
    in_specs = [pl.BlockSpec((bm, K), lambda i, j: (i, 0)),
                pl.BlockSpec((K, bn), lambda i, j: (0, j))]
    args = [x, w]
    if second is not None:
        x2, w2 = second
        in_specs += [pl.BlockSpec((bm, x2.shape[1]), lambda i, j: (i, 0)),
                     pl.BlockSpec((x2.shape[1], bn), lambda i, j: (0, j))]
        args += [x2, w2]
    if tabs is not None:
        tb = seq_len // bm
        tw = tabs[0].shape[1]
        in_specs += [pl.BlockSpec((bm, tw), lambda i, j: (i % tb, 0))] * 3
        args += list(tabs)
    kern = functools.partial(_proj_kernel, slab_pat=tuple(slab_pat), mode=mode, half=half,
                             scale=scale, with_tab=tabs is not None, two_inputs=second is not None)
    return pl.pallas_call(
        kern,
        grid=(M // bm, N // bn),
        in_specs=in_specs,
        out_specs=pl.BlockSpec((bm, bn), lambda i, j: (i, j)),
        out_shape=jax.ShapeDtypeStruct((M, N), out_dtype),
        compiler_params=_params("parallel", "parallel"),
        name=name,
    )(*args)


def _mm_ksplit_kernel(x_ref, w_ref, o_ref):
    part = jnp.dot(x_ref[...], w_ref[...], preferred_element_type=F32)

    @pl.when(pl.program_id(2) == 0)
    def _():
        o_ref[...] = part

    @pl.when(pl.program_id(2) > 0)
    def _():
        o_ref[...] += part


def _mm_ksplit(x, w, *, bm=1024, bn=1024, bk=2048, name="mm_ksplit"):
    M, K = x.shape
    N = w.shape[1]
    bm, bn, bk = _tile(M, bm), _tile(N, bn), _tile(K, bk)
    return pl.pallas_call(
        _mm_ksplit_kernel,
        grid=(M // bm, N // bn, K // bk),
        in_specs=[pl.BlockSpec((bm, bk), lambda i, j, k: (i, k)),
                  pl.BlockSpec((bk, bn), lambda i, j, k: (k, j))],
        out_specs=pl.BlockSpec((bm, bn), lambda i, j, k: (i, j)),
        out_shape=jax.ShapeDtypeStruct((M, N), F32),
        compiler_params=_params("parallel", "parallel", "arbitrary"),
        name=name,
    )(x, w)


def _swiglu_up_kernel(x_ref, w1_ref, w3_ref, o_ref):
    x = x_ref[...]
    a = jnp.dot(x, w1_ref[...], preferred_element_type=F32)
    b = jnp.dot(x, w3_ref[...], preferred_element_type=F32)
    o_ref[...] = (a * jax.nn.sigmoid(a) * b).astype(o_ref.dtype)


def _swiglu_up(x, w1, w3, *, bm=1024, bn=512):
    M, K = x.shape
    N = w1.shape[1]
    bm, bn = _tile(M, bm), _tile(N, bn)
    return pl.pallas_call(
        _swiglu_up_kernel,
        grid=(M // bm, N // bn),
        in_specs=[pl.BlockSpec((bm, K), lambda i, j: (i, 0)),
                  pl.BlockSpec((K, bn), lambda i, j: (0, j)),
                  pl.BlockSpec((K, bn), lambda i, j: (0, j))],
        out_specs=pl.BlockSpec((bm, bn), lambda i, j: (i, j)),
        out_shape=jax.ShapeDtypeStruct((M, N), BF16),
        compiler_params=_params("parallel", "parallel"),
        name="swiglu_up",
    )(x, w1, w3)


def _layer_norm_rows(z, g, b):
    mu = jnp.mean(z, axis=-1, keepdims=True)
    zc = z - mu
    var = jnp.mean(zc * zc, axis=-1, keepdims=True)
    return zc * lax.rsqrt(var + LN_EPS) * g + b


def _pack_bf16_pairs(x):
    half = x.shape[1] // 2
    hi = pltpu.bitcast(x[:, :half].astype(BF16).astype(F32), I32)
    lo = pltpu.bitcast(x[:, half:].astype(BF16).astype(F32), I32)
    return hi | lax.shift_right_logical(lo, 16)


def _unpack_bf16_pairs(w):
    hi = pltpu.bitcast(w & jnp.int32(-65536), F32).astype(BF16)
    lo = pltpu.bitcast(lax.shift_left(w, 16), F32).astype(BF16)
    return hi, lo


def _add_ln_kernel(x_ref, y_ref, g_ref, b_ref, of_ref, o2_ref, *, packed):
    out = _layer_norm_rows(ALPHA * x_ref[...] + y_ref[...], g_ref[...], b_ref[...])
    of_ref[...] = out
    o2_ref[...] = _pack_bf16_pairs(out) if packed else out.astype(o2_ref.dtype)


def _add_ln(x, y, g, b, *, packed=False, bm=256):
    M, D = x.shape
    bm = _tile(M, bm, 8)
    row = pl.BlockSpec((bm, D), lambda i: (i, 0))
    vec = pl.BlockSpec((1, D), lambda i: (0, 0))
    second = jax.ShapeDtypeStruct((M, D // 2), I32) if packed else jax.ShapeDtypeStruct((M, D), BF16)
    return pl.pallas_call(
        functools.partial(_add_ln_kernel, packed=packed),
        grid=(M // bm,),
        in_specs=[row, row, vec, vec],
        out_specs=[row, pl.BlockSpec((bm, second.shape[1]), lambda i: (i, 0))],
        out_shape=[jax.ShapeDtypeStruct((M, D), F32), second],
        compiler_params=_params("parallel"),
        name="add_ln",
    )(x, y, g.reshape(1, D), b.reshape(1, D))


LOG2E = 1.4426950408889634
SOFTMAX_ROW_BLOCK = 512


def _online_softmax(s, m, l):
    rows = s.shape[0]
    rb = min(SOFTMAX_ROW_BLOCK, rows)
    ms, ls, scales, ps = [], [], [], []
    for r in range(rows // rb):
        sl = slice(r * rb, (r + 1) * rb)
        s_r = s[sl]
        m_r = jnp.maximum(m[sl], jnp.max(s_r, axis=1, keepdims=True))
        a_r = jnp.exp2(m[sl] - m_r)
        p_r = jnp.exp2((s_r - m_r).astype(BF16))
        ms.append(m_r)
        ls.append(a_r * l[sl] + jnp.sum(p_r.astype(F32), axis=1, keepdims=True))
        scales.append(a_r)
        ps.append(p_r)
    cat = lambda parts: jnp.concatenate(parts, axis=0)
    return cat(ms), cat(ls), cat(scales), cat(ps)


def _dsa_kernel(q_ref, k_ref, v_ref, iq_ref, ik_ref, o_ref, keys_ref, bias_ref, *, topk, ck, n_rep, idx_bits):
    blk = pl.program_id(1)
    n_chunks = (blk * Q_BLOCK + Q_BLOCK + ck - 1) // ck
    row = lax.broadcasted_iota(I32, (Q_BLOCK, ck), 0) + blk * Q_BLOCK
    lane = lax.broadcasted_iota(I32, (Q_BLOCK, ck), 1)
    w_off = IDX_HEADS * IDX_DIM + IDX_DIM
    iq = iq_ref[0]
    wi = iq[:, w_off:w_off + IDX_HEADS]
    q_idx = jnp.concatenate([iq[:, h * IDX_DIM:(h + 1) * IDX_DIM].astype(BF16) for h in range(IDX_HEADS)], axis=0)

    def score_body(c, carry):
        off = pl.multiple_of(c * ck, ck)
        kc = ik_ref[0, pl.ds(off, ck), :][:, :IDX_DIM].astype(BF16)
        lg = lax.dot_general(q_idx, kc, NT_DIMS, preferred_element_type=F32)
        s = jnp.zeros((Q_BLOCK, ck), F32)
        for h in range(IDX_HEADS):
            s = s + wi[:, h:h + 1] * jnp.maximum(lg[h * Q_BLOCK:(h + 1) * Q_BLOCK], 0.0)
        bits = pltpu.bitcast(s, I32)
        key = bits ^ ((bits >> 31) & 0x7FFFFFFF)
        keys_ref[c] = jnp.where(lane + off <= row, key, INT_MIN)
        return carry

    lax.fori_loop(0, n_chunks, score_body, 0)

    def count(indicator):
        def body(c, acc):
            part = indicator(keys_ref[c], lane + c * ck)
            for j in range(ck // LANE):
                acc = acc + part[:, j * LANE:(j + 1) * LANE]
            return acc
        acc = lax.fori_loop(0, n_chunks, body, jnp.zeros((Q_BLOCK, LANE), F32))
        return jnp.sum(acc, axis=1, keepdims=True)

    kf = float(topk)
    ok = count(lambda kc, idx: jnp.where(kc >= 0, 1.0, 0.0)) >= kf
    thr = jnp.where(ok, 0, INT_MIN).astype(I32)

    def bit_body(i, thr):
        cand = thr + lax.shift_left(jnp.int32(1), 30 - i)
        ok = count(lambda kc, idx: jnp.where(kc >= cand, 1.0, 0.0)) >= kf
        return jnp.where(ok, cand, thr)

    thr = lax.fori_loop(0, 31, bit_body, thr)

    n_gt = count(lambda kc, idx: jnp.where(kc > thr, 1.0, 0.0))
    n_ge = count(lambda kc, idx: jnp.where(kc >= thr, 1.0, 0.0))
    need = kf - n_gt
    has_thr = thr > INT_MIN
    surplus = jnp.where(has_thr, n_ge - n_gt - need, 0.0)

    def tie_search():
        def tie_body(i, last):
            cand = last + lax.shift_left(jnp.int32(1), idx_bits - 1 - i)
            ok = count(lambda kc, idx: jnp.where(kc == thr, jnp.where(idx < cand, 1.0, 0.0), 0.0)) < need
            return jnp.where(ok, cand, last)
        return lax.fori_loop(0, idx_bits, tie_body, jnp.zeros((Q_BLOCK, 1), I32))

    last = lax.cond(jnp.max(surplus) > 0.0, tie_search, lambda: jnp.full((Q_BLOCK, 1), 2 ** idx_bits, I32))
    last = jnp.where(has_thr, last, -1)

    def bias_body(c, carry):
        kc = keys_ref[c]
        tie_bias = jnp.where(lane + c * ck <= last, 0.0, MASKED)
        bias_ref[c] = jnp.where(kc == thr, tie_bias, jnp.where(kc > thr, 0.0, MASKED))
        return carry

    lax.fori_loop(0, n_chunks, bias_body, 0)

    q = q_ref[0]
    rows = n_rep * Q_BLOCK
    qgs = [jnp.concatenate([q[:, (g * n_rep + r) * A_HEAD_DIM:(g * n_rep + r + 1) * A_HEAD_DIM]
                            for r in range(n_rep)], axis=0) for g in range(A_KV_HEADS)]

    def att_body(c, carry, n_sub):
        width = n_sub * ck
        off = pl.multiple_of(c * width, width)
        b = jnp.concatenate([bias_ref[c * n_sub + t] for t in range(n_sub)], axis=1)
        bias = jnp.concatenate([b] * n_rep, axis=0)
        out = []
        for g in range(A_KV_HEADS):
            m, l, acc = carry[g]
            kc = k_ref[0, pl.ds(off, width), g * A_HEAD_DIM:(g + 1) * A_HEAD_DIM]
            vc = v_ref[0, pl.ds(off, width), g * A_HEAD_DIM:(g + 1) * A_HEAD_DIM]
            s = lax.dot_general(qgs[g], kc, NT_DIMS, preferred_element_type=F32) + bias
            m, l, a, p = _online_softmax(s, m, l)
            out.append((m, l, a * acc + jnp.dot(p, vc, preferred_element_type=F32)))
        return tuple(out)

    one = (jnp.full((rows, 1), MASKED, F32), jnp.zeros((rows, 1), F32), jnp.zeros((rows, A_HEAD_DIM), F32))
    carry = (one,) * A_KV_HEADS
    if k_ref.shape[1] >= 2 * ck:
        carry = lax.fori_loop(0, n_chunks // 2, functools.partial(att_body, n_sub=2), carry)
        carry = lax.cond(n_chunks % 2 == 1, lambda c: att_body(n_chunks - 1, c, 1), lambda c: c, carry)
    else:
        carry = att_body(0, carry, 1)
    for g in range(A_KV_HEADS):
        _, l, acc = carry[g]
        o = acc / l
        for r in range(n_rep):
            col = (g * n_rep + r) * A_HEAD_DIM
            o_ref[0, :, col:col + A_HEAD_DIM] = o[r * Q_BLOCK:(r + 1) * Q_BLOCK].astype(o_ref.dtype)


def _dsa(q, k, v, idx, *, B, T):
    a_heads = q.shape[-1] // A_HEAD_DIM
    n_rep = a_heads // A_KV_HEADS
    topk = min(DSA_TOPK_MAX, T // 4)
    ck = _tile(T, 512)
    idx_w = idx.shape[-1]
    kv_w = k.shape[-1]
    kern = functools.partial(_dsa_kernel, topk=topk, ck=ck, n_rep=n_rep, idx_bits=max(1, (T - 1).bit_length()))
    return pl.pallas_call(
        kern,
        grid=(B, T // Q_BLOCK),
        in_specs=[pl.BlockSpec((1, Q_BLOCK, q.shape[-1]), lambda b, i: (b, i, 0)),
                  pl.BlockSpec((1, T, kv_w), lambda b, i: (b, 0, 0)),
                  pl.BlockSpec((1, T, kv_w), lambda b, i: (b, 0, 0)),
                  pl.BlockSpec((1, Q_BLOCK, idx_w), lambda b, i: (b, i, 0)),
                  pl.BlockSpec((1, T, LANE), lambda b, i: (b, 0, IDX_HEADS * IDX_DIM // LANE))],
        out_specs=pl.BlockSpec((1, Q_BLOCK, q.shape[-1]), lambda b, i: (b, i, 0)),
        out_shape=jax.ShapeDtypeStruct(q.shape, BF16),
        scratch_shapes=[pltpu.VMEM((T // ck, Q_BLOCK, ck), I32), pltpu.VMEM((T // ck, Q_BLOCK, ck), F32)],
        compiler_params=_params("parallel", "parallel"),
        name="dsa",
    )(q, k, v, idx, idx)


def _retention_kernel(q_ref, k_ref, v_ref, g_ref, gn_ref, din_ref, qd_ref, kd_ref, cd_ref, o_ref, state_ref, *, n_sub):
    @pl.when(pl.program_id(2) == 0)
    def _():
        state_ref[...] = jnp.zeros_like(state_ref)

    din = din_ref[0]
    qd = qd_ref[0]
    kd = kd_ref[0]
    cd = cd_ref[0]
    gn = gn_ref[...]
    for s in range(n_sub):
        sl = pl.ds(s * RET_CHUNK, RET_CHUNK)
        qc = q_ref[sl, :]
        kc = k_ref[sl, :]
        vc = v_ref[sl, :]
        st = state_ref[...]
        inner = lax.dot_general(qc, kc, NT_DIMS, preferred_element_type=F32) * din
        o = (jnp.dot(inner.astype(BF16), vc, preferred_element_type=F32)
             + jnp.dot(qc, st.astype(BF16), preferred_element_type=F32) * qd)
        vk = (vc.astype(F32) * kd).astype(BF16)
        state_ref[...] = st * cd + lax.dot_general(kc, vk, TN_DIMS, preferred_element_type=F32)
        mu = jnp.mean(o, axis=-1, keepdims=True)
        oc = o - mu
        var = jnp.mean(oc * oc, axis=-1, keepdims=True)
        gate = g_ref[sl, :].astype(F32)
        o_ref[sl, :] = (gate * jax.nn.sigmoid(gate) * (oc * lax.rsqrt(var + LN_EPS) * gn)).astype(o_ref.dtype)


def _retention(qk, pv, gn_g, *, B, T, heads, v_blk0):
    N = qk.shape[0]
    C = RET_CHUNK
    rb = _tile(T, 512)
    n_sub = rb // C
    nr = T // rb
    log_gamma = jnp.log(1.0 - 2.0 ** (-5.0 - jnp.arange(heads, dtype=F32)))
    pos = jnp.arange(C, dtype=F32)
    diff = pos[:, None] - pos[None, :]
    din = jnp.exp(jnp.where(diff[None] >= 0, log_gamma[:, None, None] * diff[None], -jnp.inf))
    qd = jnp.exp(log_gamma[:, None] * (pos[None] + 1.0))[:, :, None]
    kd = jnp.exp(log_gamma[:, None] * (C - 1.0 - pos[None]))[:, :, None]
    cd = jnp.exp(log_gamma * C)[:, None, None]
    W = RET_VAL_DIM
    blk = lambda off: pl.BlockSpec((rb, W), lambda b, h, r: (b * nr + r, off + h))
    per_head = lambda shape: pl.BlockSpec((1,) + shape, lambda b, h, r: (h, 0, 0))
    return pl.pallas_call(
        functools.partial(_retention_kernel, n_sub=n_sub),
        grid=(B, heads, nr),
        in_specs=[blk(0), blk(heads), blk(v_blk0), blk(v_blk0 + heads),
                  pl.BlockSpec((1, W), lambda b, h, r: (0, h)),
                  per_head((C, C)), per_head((C, 1)), per_head((C, 1)), per_head((1, 1))],
        out_specs=pl.BlockSpec((rb, W), lambda b, h, r: (b * nr + r, h)),
        out_shape=jax.ShapeDtypeStruct((N, heads * W), BF16),
        scratch_shapes=[pltpu.VMEM((RET_KEY_DIM, RET_VAL_DIM), F32)],
        compiler_params=_params("parallel", "parallel", "arbitrary"),
        name="retention",
    )(qk, qk, pv, pv, gn_g.reshape(1, heads * W), din, qd, kd, cd)


def _mla_down_kernel(x_ref, w_ref, qg_ref, kvg_ref, c_ref, sa_ref, sb_ref, cq_ref, ckv_ref, kr_ref, *, q_rank, kv_rank):
    acc = jnp.dot(x_ref[...], w_ref[...], preferred_element_type=F32)

    def rms(a, g):
        return a * lax.rsqrt(jnp.mean(a * a, axis=-1, keepdims=True) + RMS_EPS) * g

    cq_ref[...] = rms(acc[:, :q_rank], qg_ref[...]).astype(cq_ref.dtype)
    ckv_ref[...] = rms(acc[:, q_rank:q_rank + kv_rank], kvg_ref[...]).astype(ckv_ref.dtype)
    kr = acc[:, q_rank + kv_rank:]
    half = MLA_ROPE // 2
    kr = kr * c_ref[...] + pltpu.roll(kr, LANE - half, 1) * sa_ref[...] + pltpu.roll(kr, half, 1) * sb_ref[...]
    kr_ref[...] = kr.astype(kr_ref.dtype)


def _mla_down(x, w, q_g, kv_g, tabs, *, T, q_rank, kv_rank, bm=512):
    M, K = x.shape
    Nw = w.shape[1]
    bm = _tile(T, bm)
    tb = T // bm
    row = lambda n: pl.BlockSpec((bm, n), lambda i: (i, 0))
    tab = pl.BlockSpec((bm, LANE), lambda i: (i % tb, 0))
    return pl.pallas_call(
        functools.partial(_mla_down_kernel, q_rank=q_rank, kv_rank=kv_rank),
        grid=(M // bm,),
        in_specs=[row(K), pl.BlockSpec((K, Nw), lambda i: (0, 0)),
                  pl.BlockSpec((1, q_rank), lambda i: (0, 0)), pl.BlockSpec((1, kv_rank), lambda i: (0, 0)),
                  tab, tab, tab],
        out_specs=[row(q_rank), row(kv_rank), row(LANE)],
        out_shape=[jax.ShapeDtypeStruct((M, q_rank), BF16), jax.ShapeDtypeStruct((M, kv_rank), BF16),
                   jax.ShapeDtypeStruct((M, LANE), BF16)],
        compiler_params=_params("parallel"),
        name="mla_down",
    )(x, w, q_g.reshape(1, q_rank), kv_g.reshape(1, kv_rank), *tabs)


MLA_HEADS_PER_STEP = 4


def _mla_attn_kernel(qn_ref, qr_ref, kn_ref, kr_ref, v_ref, o_ref, *, tq):
    i = pl.program_id(2)
    lane = lax.broadcasted_iota(I32, (tq, LANE), 1)
    qs = []
    for j in range(MLA_HEADS_PER_STEP):
        lo = (j % 2) * MLA_ROPE
        own = jnp.where((lane >= lo) & (lane < lo + MLA_ROPE), 1.0, 0.0)
        pair = qr_ref[:, (j // 2) * LANE:(j // 2 + 1) * LANE].astype(F32)
        qr = (pair * own).astype(BF16)
        qs.append(jnp.concatenate([qn_ref[:, j * LANE:(j + 1) * LANE], qr], axis=1))

    def step(c, carry, masked, width):
        off = pl.multiple_of(c * width, width)
        kr = kr_ref[pl.ds(off, width), :]
        out = []
        for j in range(MLA_HEADS_PER_STEP):
            m, l, acc = carry[j]
            k = jnp.concatenate([kn_ref[pl.ds(off, width), j * LANE:(j + 1) * LANE], kr], axis=1)
            s = lax.dot_general(qs[j], k, NT_DIMS, preferred_element_type=F32)
            if masked:
                r_io = lax.broadcasted_iota(I32, (tq, width), 0)
                c_io = lax.broadcasted_iota(I32, (tq, width), 1)
                s = jnp.where(c_io <= r_io, s, MASKED)
            m, l, a, p = _online_softmax(s, m, l)
            v = v_ref[pl.ds(off, width), j * MLA_V:(j + 1) * MLA_V]
            out.append((m, l, a * acc + jnp.dot(p, v, preferred_element_type=F32)))
        return tuple(out)

    one = (jnp.full((tq, 1), MASKED, F32), jnp.zeros((tq, 1), F32), jnp.zeros((tq, MLA_V), F32))
    carry = (one,) * MLA_HEADS_PER_STEP
    if kn_ref.shape[0] >= 2 * tq:
        carry = lax.fori_loop(0, i // 2, functools.partial(step, masked=False, width=2 * tq), carry)
        carry = lax.cond(i % 2 == 1, lambda c: step(i - 1, c, False, tq), lambda c: c, carry)
    carry = step(i, carry, True, tq)
    for j in range(MLA_HEADS_PER_STEP):
        _, l, acc = carry[j]
        o_ref[:, j * MLA_V:(j + 1) * MLA_V] = (acc / l).astype(o_ref.dtype)


def _mla_attn(qn, qr, kv, kr, *, B, T, heads):
    N = qn.shape[0]
    tq = _tile(T, 512)
    nq = T // tq
    hp = MLA_HEADS_PER_STEP
    w = hp * LANE
    return pl.pallas_call(
        functools.partial(_mla_attn_kernel, tq=tq),
        grid=(B, heads // hp, nq),
        in_specs=[pl.BlockSpec((tq, w), lambda b, h, i: (b * nq + i, h)),
                  pl.BlockSpec((tq, hp * MLA_ROPE), lambda b, h, i: (b * nq + i, h)),
                  pl.BlockSpec((T, w), lambda b, h, i: (b, h)),
                  pl.BlockSpec((T, LANE), lambda b, h, i: (b, 0)),
                  pl.BlockSpec((T, w), lambda b, h, i: (b, heads // hp + h))],
        out_specs=pl.BlockSpec((tq, w), lambda b, h, i: (b * nq + i, h)),
        out_shape=jax.ShapeDtypeStruct((N, heads * MLA_V), BF16),
        compiler_params=_params("parallel", "parallel", "parallel"),
        name="mla_attn",
    )(qn, qr, kv, kr, kv)


def _router_kernel(x_ref, r_ref, meta_ref, cnt_ref, carry_ref, *, n_exp):
    @pl.when(pl.program_id(0) == 0)
    def _():
        carry_ref[...] = jnp.zeros_like(carry_ref)

    bm = x_ref.shape[0]
    logits = jnp.dot(x_ref[...], r_ref[...], preferred_element_type=F32, precision=lax.Precision.HIGHEST)
    lane = lax.broadcasted_iota(I32, (bm, LANE), 1).astype(F32)
    logits = jnp.where(lane < n_exp, logits, -jnp.inf)
    m1 = jnp.max(logits, axis=1, keepdims=True)
    i1 = jnp.min(jnp.where(logits == m1, lane, float(LANE)), axis=1, keepdims=True)
    rest = jnp.where(lane == i1, -jnp.inf, logits)
    m2 = jnp.max(rest, axis=1, keepdims=True)
    i2 = jnp.min(jnp.where(rest == m2, lane, float(LANE)), axis=1, keepdims=True)
    e = jnp.exp(m2 - m1)
    g1 = 1.0 / (1.0 + e)
    g2 = e / (1.0 + e)
    sel = jnp.where(lane == i1, 1.0, jnp.where(lane == i2, 1.0, 0.0))
    r_io = lax.broadcasted_iota(I32, (bm, bm), 0)
    c_io = lax.broadcasted_iota(I32, (bm, bm), 1)
    below = jnp.where(c_io < r_io, 1.0, 0.0).astype(BF16)
    carry = carry_ref[0:1, :]
    rank = jnp.dot(below, sel.astype(BF16), preferred_element_type=F32) + carry
    r1 = jnp.sum(jnp.where(lane == i1, rank, 0.0), axis=1, keepdims=True)
    r2 = jnp.sum(jnp.where(lane == i2, rank, 0.0), axis=1, keepdims=True)
    meta = jnp.where(lane == 0, i1, 0.0)
    meta = jnp.where(lane == 1, i2, meta)
    meta = jnp.where(lane == 2, g1, meta)
    meta = jnp.where(lane == 3, g2, meta)
    meta = jnp.where(lane == 4, r1, meta)
    meta = jnp.where(lane == 5, r2, meta)
    meta_ref[...] = meta
    total = carry + jnp.sum(sel, axis=0, keepdims=True)
    carry_ref[...] = jnp.broadcast_to(total, carry_ref.shape)
    cnt_ref[...] = jnp.broadcast_to(total, cnt_ref.shape)


def _router(x, router, *, bm=512):
    M, D = x.shape
    n_exp = router.shape[1]
    bm = _tile(M, bm)
    r_pad = jnp.zeros((D, LANE), F32).at[:, :n_exp].set(router)
    return pl.pallas_call(
        functools.partial(_router_kernel, n_exp=n_exp),
        grid=(M // bm,),
        in_specs=[pl.BlockSpec((bm, D), lambda i: (i, 0)), pl.BlockSpec((D, LANE), lambda i: (0, 0))],
        out_specs=[pl.BlockSpec((bm, LANE), lambda i: (i, 0)), pl.BlockSpec((8, LANE), lambda i: (0, 0))],
        out_shape=[jax.ShapeDtypeStruct((M, LANE), F32), jax.ShapeDtypeStruct((8, LANE), F32)],
        scratch_shapes=[pltpu.VMEM((8, LANE), F32)],
        compiler_params=_params("arbitrary"),
        name="router",
    )(x, r_pad)


DMA_LOOP_UNROLL = 8


def _moe_gather_kernel(tok_ref, nv_ref, x_hbm, o_ref, stage_ref, sem, *, tm):
    r = pl.program_id(0)
    nv = nv_ref[0]

    half = stage_ref.shape[1]

    def row_copy(j, tok):
        return pltpu.make_async_copy(x_hbm.at[pl.ds(tok, 1)], stage_ref.at[pl.ds(j, 1)], sem)

    def gather_start(tile):
        def body(j, carry):
            row_copy(j, tok_ref[tile * tm + j]).start()
            return carry
        lax.fori_loop(0, tm, body, 0, unroll=DMA_LOOP_UNROLL)

    def gather_wait():
        def body(j, carry):
            row_copy(j, 0).wait()
            return carry
        lax.fori_loop(0, tm, body, 0, unroll=DMA_LOOP_UNROLL)

    @pl.when(r < nv)
    def _():
        @pl.when(r == 0)
        def _():
            gather_start(0)

        gather_wait()
        hi, lo = _unpack_bf16_pairs(stage_ref[...])
        o_ref[:, :half] = hi
        o_ref[:, half:] = lo

        @pl.when(r + 1 < nv)
        def _():
            gather_start(r + 1)


def _moe_up_kernel(te_ref, nv_ref, xs_ref, w1_ref, w3_ref, o_ref, wb1_ref, wb3_ref):
    r = pl.program_id(1)

    @pl.when(r < nv_ref[0])
    def _():
        @pl.when((r == 0) | (te_ref[r] != te_ref[jnp.maximum(r - 1, 0)]))
        def _():
            wb1_ref[...] = w1_ref[0].astype(BF16)
            wb3_ref[...] = w3_ref[0].astype(BF16)

        xb = xs_ref[...]
        a = jnp.dot(xb, wb1_ref[...], preferred_element_type=F32)
        b = jnp.dot(xb, wb3_ref[...], preferred_element_type=F32)
        o_ref[...] = (a * jax.nn.sigmoid(a) * b).astype(o_ref.dtype)


def _moe_down_kernel(te_ref, nv_ref, h_ref, w2_ref, o_ref, wb_ref):
    r = pl.program_id(1)

    @pl.when(r < nv_ref[0])
    def _():
        @pl.when((r == 0) | (te_ref[r] != te_ref[jnp.maximum(r - 1, 0)]))
        def _():
            wb_ref[...] = w2_ref[0].astype(BF16)

        o_ref[...] = jnp.dot(h_ref[...], wb_ref[...], preferred_element_type=F32)


def _experts(x, w1, w3, w2, tile_expert, n_valid, row_tok, *, tm, tf=512, tn=512):
    D = 2 * x.shape[1]
    P = row_tok.shape[0]
    E, _, F = w1.shape
    tf, tn = _tile(F, tf), _tile(D, tn)
    n_tiles = P // tm
    row = lambda r, nv: jnp.minimum(r, nv[0] - 1)

    xs = pl.pallas_call(
        functools.partial(_moe_gather_kernel, tm=tm),
        grid_spec=pltpu.PrefetchScalarGridSpec(
            num_scalar_prefetch=2, grid=(n_tiles,),
            in_specs=[pl.BlockSpec(memory_space=pl.ANY)],
            out_specs=pl.BlockSpec((tm, D), lambda r, tok, nv: (row(r, nv), 0)),
            scratch_shapes=[pltpu.VMEM((tm, D // 2), I32), pltpu.SemaphoreType.DMA(())]),
        out_shape=jax.ShapeDtypeStruct((P, D), BF16),
        compiler_params=_params("arbitrary"),
        name="moe_gather",
    )(row_tok, n_valid, x)

    hid = pl.pallas_call(
        _moe_up_kernel,
        grid_spec=pltpu.PrefetchScalarGridSpec(
            num_scalar_prefetch=2, grid=(F // tf, n_tiles),
            in_specs=[pl.BlockSpec((tm, D), lambda f, r, te, nv: (row(r, nv), 0)),
                      pl.BlockSpec((1, D, tf), lambda f, r, te, nv: (te[r], 0, f)),
                      pl.BlockSpec((1, D, tf), lambda f, r, te, nv: (te[r], 0, f))],
            out_specs=pl.BlockSpec((tm, tf), lambda f, r, te, nv: (row(r, nv), f)),
            scratch_shapes=[pltpu.VMEM((D, tf), BF16), pltpu.VMEM((D, tf), BF16)]),
        out_shape=jax.ShapeDtypeStruct((P, F), BF16),
        compiler_params=_params("arbitrary", "arbitrary"),
        name="moe_up",
    )(tile_expert, n_valid, xs, w1, w3)

    return pl.pallas_call(
        _moe_down_kernel,
        grid_spec=pltpu.PrefetchScalarGridSpec(
            num_scalar_prefetch=2, grid=(D // tn, n_tiles),
            in_specs=[pl.BlockSpec((tm, F), lambda n, r, te, nv: (row(r, nv), 0)),
                      pl.BlockSpec((1, F, tn), lambda n, r, te, nv: (te[r], 0, n))],
            out_specs=pl.BlockSpec((tm, tn), lambda n, r, te, nv: (row(r, nv), n)),
            scratch_shapes=[pltpu.VMEM((F, tn), BF16)]),
        out_shape=jax.ShapeDtypeStruct((P, D), F32),
        compiler_params=_params("arbitrary", "arbitrary"),
        name="moe_down",
    )(tile_expert, n_valid, hid, w2)


def _combine_kernel(dest_ref, x_ref, meta_ref, g_ref, b_ref, ys_hbm, o_ref, buf_ref, sem):
    bm = x_ref.shape[0]
    i = pl.program_id(0)
    n = pl.num_programs(0)

    def copy(slot, j, k, d):
        return pltpu.make_async_copy(ys_hbm.at[pl.ds(d, 1)], buf_ref.at[slot, k, pl.ds(j, 1)], sem.at[slot, k])

    def gather_start(blk):
        slot = blk % 2

        def body(j, carry):
            for k in range(2):
                copy(slot, j, k, dest_ref[2 * (blk * bm + j) + k]).start()
            return carry
        lax.fori_loop(0, bm, body, 0, unroll=DMA_LOOP_UNROLL)

    @pl.when(i == 0)
    def _():
        gather_start(0)

    @pl.when(i + 1 < n)
    def _():
        gather_start(i + 1)

    slot = i % 2

    def wait(j, carry):
        for k in range(2):
            copy(slot, j, k, 0).wait()
        return carry

    lax.fori_loop(0, bm, wait, 0, unroll=DMA_LOOP_UNROLL)
    meta = meta_ref[...]
    y = meta[:, 2:3] * buf_ref[slot, 0] + meta[:, 3:4] * buf_ref[slot, 1]
    o_ref[...] = _layer_norm_rows(ALPHA * x_ref[...] + y, g_ref[...], b_ref[...])


def _combine(x, meta, ys, dest, g, b, *, bm=256):
    M, D = x.shape
    bm = _tile(M, bm, 8)
    row = lambda n: pl.BlockSpec((bm, n), lambda i, d: (i, 0))
    vec = pl.BlockSpec((1, D), lambda i, d: (0, 0))
    return pl.pallas_call(
        _combine_kernel,
        grid_spec=pltpu.PrefetchScalarGridSpec(
            num_scalar_prefetch=1, grid=(M // bm,),
            in_specs=[row(D), row(LANE), vec, vec, pl.BlockSpec(memory_space=pl.ANY)],
            out_specs=row(D),
            scratch_shapes=[pltpu.VMEM((2, 2, bm, D), F32), pltpu.SemaphoreType.DMA((2, 2))]),
        out_shape=jax.ShapeDtypeStruct((M, D), F32),
        compiler_params=_params("arbitrary"),
        name="moe_combine",
    )(dest, x, meta, g.reshape(1, D), b.reshape(1, D), ys)


def _even_layer(x, xb, w_in, ret_gn_g, w_out, ln1_g, ln1_b, w1, w3, w2, ln2_g, ln2_b, *, B, T):
    N, D = x.shape
    a_heads = D // 2 // A_HEAD_DIM
    r_heads = D // 2 // RET_VAL_DIM
    qa_w, kv_w = a_heads * A_HEAD_DIM, A_KV_HEADS * A_HEAD_DIM
    qi_w = IDX_HEADS * IDX_DIM
    rk_w, rv_w = r_heads * RET_KEY_DIM, r_heads * RET_VAL_DIM
    sizes = (qa_w, kv_w, kv_w, qi_w, IDX_DIM, IDX_HEADS, rk_w, rk_w, rv_w, rv_w)
    offs = [0]
    for s in sizes:
        offs.append(offs[-1] + s)
    col = lambda a, b_: w_in[:, offs[a]:offs[b_]]
    w_qa, w_ka, w_va = col(0, 1), col(1, 2), col(2, 3)
    w_qi, w_ki, w_wi = col(3, 4), col(4, 5), col(5, 6)
    w_qb, w_kb, w_vb, w_gb = col(6, 7), col(7, 8), col(8, 9), col(9, 10)

    cos_a, sin_a = _rope_cos_sin(T, A_HEAD_DIM // 4, ROPE_THETA)
    tab_q = _lane_tables(cos_a, sin_a, A_HEAD_DIM, A_HEAD_DIM ** -0.5 * LOG2E)
    tab_k = _lane_tables(cos_a, sin_a, A_HEAD_DIM)
    cos_i, sin_i = _rope_cos_sin(T, IDX_DIM // 4, ROPE_THETA)
    tab_i = _lane_tables(cos_i, sin_i, IDX_DIM)
    pass_c = jnp.ones((T, LANE - IDX_DIM), F32)
    pass_s = jnp.zeros((T, LANE - IDX_DIM), F32)
    tab_idx = tuple(jnp.concatenate([t, t[:, :IDX_DIM], p], 1)
                    for t, p in zip(tab_i, (pass_c, pass_s, pass_s)))
    inv = 1.0 / (RET_THETA ** jnp.linspace(0.0, 1.0, RET_KEY_DIM // 2, dtype=F32))
    ang = jnp.arange(T, dtype=F32)[:, None] * inv[None, :]
    cos_r, sin_r = jnp.cos(ang), jnp.sin(ang)
    sin_pair = jnp.concatenate([-sin_r, sin_r], 1)
    tab_r = (jnp.concatenate([cos_r, cos_r], 1), sin_pair, sin_pair)

    idx_pad = LANE - IDX_DIM - IDX_HEADS
    w_idx = jnp.concatenate([w_qi, w_ki, w_wi * (IDX_DIM ** -0.5 * IDX_HEADS ** -0.5),
                             jnp.zeros((D, idx_pad), F32)], 1).astype(BF16)
    w_rqk = jnp.concatenate([w_qb, w_kb * RET_KEY_DIM ** -0.5], 1).astype(BF16)
    w_pv = jnp.concatenate([w_va, w_vb, w_gb], 1).astype(BF16)
    qa = _proj(xb, w_qa.astype(BF16), out_dtype=BF16, tabs=tab_q, half=A_HEAD_DIM // 8,
               seq_len=T, name="proj_qa", **_pat(qa_w, 1024, 0))
    ka = _proj(xb, w_ka.astype(BF16), out_dtype=BF16, tabs=tab_k, half=A_HEAD_DIM // 8,
               seq_len=T, name="proj_ka", **_pat(kv_w, 1024, 0))
    n_idx = w_idx.shape[1]
    idx = _proj(xb, w_idx, out_dtype=F32, tabs=tab_idx, half=IDX_DIM // 8, seq_len=T, name="proj_idx",
                bm=512, bn=n_idx, slab_pat=(0,) * (qi_w // LANE) + (1,))
    rqk = _proj(xb, w_rqk, out_dtype=BF16, tabs=tab_r, mode="pair", seq_len=T, name="proj_ret_qk",
                bn=1024, slab_pat=(0, 1) * (_tile(2 * rk_w, 1024) // (2 * LANE)))
    pv = _proj(xb, w_pv, out_dtype=BF16, bn=768, seq_len=T, name="proj_v")

    ya = _dsa(qa.reshape(B, T, qa_w), ka.reshape(B, T, kv_w), pv.reshape(B, T, -1), idx.reshape(B, T, n_idx),
              B=B, T=T)
    yb = _retention(rqk, pv, ret_gn_g, B=B, T=T, heads=r_heads, v_blk0=kv_w // RET_VAL_DIM)
    w_out_b = w_out.astype(BF16)
    y = _proj(ya.reshape(N, qa_w), w_out_b[:qa_w], second=(yb, w_out_b[qa_w:]), out_dtype=F32, name="proj_out0")
    x1, x1b = _add_ln(x, y, ln1_g, ln1_b)
    hid = _swiglu_up(x1b, w1.astype(BF16), w3.astype(BF16))
    y = _mm_ksplit(hid, w2.astype(BF16), bk=3584, name="ffn_down")
    return _add_ln(x1, y, ln2_g, ln2_b)


def _pat(width, bn, p):
    bn = _tile(width, bn)
    return dict(bn=bn, slab_pat=(p,) * (bn // LANE))


def _odd_layer(x, xb, w_dq_dkv, q_norm_g, w_uq, kv_norm_g, w_ukv, w_out, ln1_g, ln1_b,
               router, we1, we3, we2, ln2_g, ln2_b, *, B, T):
    N, D = x.shape
    heads = D // MLA_V
    q_rank, kv_rank = q_norm_g.shape[0], kv_norm_g.shape[0]
    scale = (MLA_NOPE + MLA_ROPE) ** -0.5 * LOG2E
    cos_c, sin_c = _rope_cos_sin(T, MLA_ROPE, ROPE_THETA)
    tab_kr = _lane_tables(cos_c, sin_c, MLA_ROPE)
    tab_qr = _lane_tables(cos_c, sin_c, MLA_ROPE, scale)

    w_kr = w_dq_dkv[:, q_rank + kv_rank:]
    w_down = jnp.concatenate([w_dq_dkv[:, :q_rank + kv_rank], w_kr, w_kr], 1).astype(BF16)
    cq, ckv, kr = _mla_down(xb, w_down, q_norm_g, kv_norm_g, tab_kr, T=T, q_rank=q_rank, kv_rank=kv_rank)
    w_uq3 = w_uq.reshape(q_rank, heads, MLA_NOPE + MLA_ROPE)
    w_qn = w_uq3[:, :, :MLA_NOPE].reshape(q_rank, heads * MLA_NOPE).astype(BF16)
    w_qr = w_uq3[:, :, MLA_NOPE:].reshape(q_rank, heads * MLA_ROPE).astype(BF16)
    w_kv3 = w_ukv.reshape(kv_rank, heads, MLA_NOPE + MLA_V)
    w_kv = jnp.concatenate([w_kv3[:, :, :MLA_NOPE].reshape(kv_rank, heads * MLA_NOPE),
                            w_kv3[:, :, MLA_NOPE:].reshape(kv_rank, heads * MLA_V)], 1).astype(BF16)
    qn = _proj(cq, w_qn, out_dtype=BF16, scale=scale, seq_len=T, name="proj_q_nope")
    qr = _proj(cq, w_qr, out_dtype=BF16, tabs=tab_qr, half=MLA_ROPE // 2, seq_len=T, name="proj_q_rope",
               **_pat(heads * MLA_ROPE, 1024, 0))
    kv = _proj(ckv, w_kv, out_dtype=BF16, seq_len=T, name="proj_kv")
    att = _mla_attn(qn, qr, kv, kr, B=B, T=T, heads=heads)
    y = _proj(att, w_out.astype(BF16), out_dtype=F32, name="proj_out1")
    x1, x1_packed = _add_ln(x, y, ln1_g, ln1_b, packed=True)

    E = router.shape[1]
    F = we1.shape[2]
    tm = _tile(N, 512)
    meta, cnt = _router(x1, router)
    counts = cnt[0, :E].astype(I32)
    padded = (counts + tm - 1) // tm * tm
    ends = jnp.cumsum(padded)
    starts = ends - padded
    i1, i2 = meta[:, 0].astype(I32), meta[:, 1].astype(I32)
    dest = jnp.stack([starts[i1] + meta[:, 4].astype(I32), starts[i2] + meta[:, 5].astype(I32)], 1).reshape(-1)
    n_rows = 2 * N + E * tm
    n_tiles = n_rows // tm
    n_valid = (ends[-1] // tm).astype(I32).reshape(1)
    tile_start = jnp.arange(n_tiles, dtype=I32) * tm
    tile_expert = jnp.minimum(jnp.sum(tile_start[:, None] >= ends[None, :], axis=1), E - 1).astype(I32)
    tile_expert = jnp.where(jnp.arange(n_tiles) < n_valid[0], tile_expert, tile_expert[jnp.maximum(n_valid[0] - 1, 0)])
    row_tok = jnp.zeros((n_rows,), I32).at[dest].set(jnp.repeat(jnp.arange(N, dtype=I32), 2))
    ys = _experts(x1_packed, we1, we3, we2, tile_expert, n_valid, row_tok, tm=tm)
    return _combine(x1, meta, ys, dest, ln2_g, ln2_b)


def kernel(x, l0_w_in, l0_ret_gn_g, l0_w_out, l0_ln1_g, l0_ln1_b, l0_ffn_w1, l0_ffn_w3, l0_ffn_w2, l0_ln2_g, l0_ln2_b, l1_w_dq_dkv, l1_q_norm_g, l1_w_uq, l1_kv_norm_g, l1_w_ukv, l1_w_out, l1_ln1_g, l1_ln1_b, l1_router, l1_moe_w1, l1_moe_w3, l1_moe_w2, l1_ln2_g, l1_ln2_b):
    B, T, D = x.shape
    x2 = x.reshape(B * T, D)
    h, hb = _even_layer(x2, x2.astype(BF16), l0_w_in, l0_ret_gn_g, l0_w_out, l0_ln1_g, l0_ln1_b,
                        l0_ffn_w1, l0_ffn_w3, l0_ffn_w2, l0_ln2_g, l0_ln2_b, B=B, T=T)
    out = _odd_layer(h, hb, l1_w_dq_dkv, l1_q_norm_g, l1_w_uq, l1_kv_norm_g, l1_w_ukv, l1_w_out,
                     l1_ln1_g, l1_ln1_b, l1_router, l1_moe_w1, l1_moe_w3, l1_moe_w2, l1_ln2_g, l1_ln2_b, B=B, T=T)
    return out.reshape(B, T, D)
```

```python
import functools

import jax
import jax.numpy as jnp
from jax import lax
from jax.experimental import pallas as pl
from jax.experimental.pallas import tpu as pltpu

F32 = jnp.float32
BF16 = jnp.bfloat16
I32 = jnp.int32

A_HEAD_DIM = 128
A_KV_HEADS = 4
IDX_HEADS = 16
IDX_DIM = 64
DSA_TOPK_MAX = 256
RET_KEY_DIM = 256
RET_VAL_DIM = 256
RET_CHUNK = 128
RET_THETA = 10000.0
MLA_V = 128
MLA_NOPE = 128
MLA_ROPE = 64
ROPE_THETA = 500000.0
Q_BLOCK = 128
LN_EPS = 1e-5
RMS_EPS = 1e-6
DEPTH = 2
ALPHA = (2.0 * DEPTH) ** 0.25

LANE = 128
V7X_VMEM_BYTES = 64 * 1024 * 1024
VMEM_LIMIT = V7X_VMEM_BYTES - 8 * 1024 * 1024
MASKED = -1e30
INT_MIN = -(2 ** 31)

NT_DIMS = (((1,), (1,)), ((), ()))
TN_DIMS = (((0,), (0,)), ((), ()))


def _tile(n, pref, mult=LANE):
    if n <= pref:
        return n
    t = (pref // mult) * mult
    while t > mult and n % t:
        t -= mult
    assert n % t == 0, (n, pref, mult)
    return t


def _params(*sem):
    return pltpu.CompilerParams(dimension_semantics=sem, vmem_limit_bytes=VMEM_LIMIT)


def _rope_cos_sin(T, rot_dim, theta):
    inv = theta ** (-jnp.arange(0, rot_dim, 2, dtype=F32) / rot_dim)
    ang = jnp.arange(T, dtype=F32)[:, None] * inv[None, :]
    return jnp.cos(ang), jnp.sin(ang)


def _lane_tables(cos, sin, head_dim, scale=1.0):
    T, half = cos.shape
    rest = head_dim - 2 * half
    zh = jnp.zeros((T, half), F32)
    c = jnp.concatenate([cos, cos, jnp.ones((T, rest), F32)], 1)
    sa = jnp.concatenate([-sin, zh, jnp.zeros((T, rest), F32)], 1)
    sb = jnp.concatenate([zh, sin, jnp.zeros((T, rest), F32)], 1)
    reps = LANE // head_dim
    return tuple(jnp.tile(t * scale, (1, reps)) for t in (c, sa, sb))


def _proj_kernel(*refs, slab_pat, mode, half, scale, with_tab, two_inputs):
    x_ref, w_ref = refs[:2]
    acc = jnp.dot(x_ref[...], w_ref[...], preferred_element_type=F32)
    refs = refs[2:]
    if two_inputs:
        acc = acc + jnp.dot(refs[0][...], refs[1][...], preferred_element_type=F32)
        refs = refs[2:]
    if with_tab:
        c_ref, sa_ref, sb_ref, o_ref = refs
    else:
        (o_ref,) = refs
    for s, p in enumerate(slab_pat):
        a = acc[:, s * LANE:(s + 1) * LANE]
        if p < 0:
            out = a if scale == 1.0 else a * scale
        else:
            c = c_ref[:, p * LANE:(p + 1) * LANE]
            sa = sa_ref[:, p * LANE:(p + 1) * LANE]
            if mode == "lane":
                sb = sb_ref[:, p * LANE:(p + 1) * LANE]
                out = a * c + pltpu.roll(a, LANE - half, 1) * sa + pltpu.roll(a, half, 1) * sb
            else:
                q = s ^ 1
                out = a * c + acc[:, q * LANE:(q + 1) * LANE] * sa
        o_ref[:, s * LANE:(s + 1) * LANE] = out.astype(o_ref.dtype)


def _proj(x, w, *, out_dtype, bm=1024, bn=1024, tabs=None, slab_pat=None, mode="lane", half=0,
          scale=1.0, seq_len=None, second=None, name="proj"):
    M, K = x.shape
    N = w.shape[1]
    bm = _tile(M, bm) if seq_len is None else _tile(seq_len, bm)
    bn = _tile(N, bn)
    if slab_pat is None:
        slab_pat = (-1,) * (bn // LANE)
    assert len(slab_pat) == bn // LANE
    in_specs = [pl.BlockSpec((bm, K), lambda i, j: (i, 0)),
                pl.BlockSpec((K, bn), lambda i, j: (0, j))]
    args = [x, w]
    if second is not None:
        x2, w2 = second
        in_specs += [pl.BlockSpec((bm, x2.shape[1]), lambda i, j: (i, 0)),
                     pl.BlockSpec((x2.shape[1], bn), lambda i, j: (0, j))]
        args += [x2, w2]
    if tabs is not None:
        tb = seq_len // bm
        tw = tabs[0].shape[1]
        in_specs += [pl.BlockSpec((bm, tw), lambda i, j: (i % tb, 0))] * 3
        args += list(tabs)
    kern = functools.partial(_proj_kernel, slab_pat=tuple(slab_pat), mode=mode, half=half,
                             scale=scale, with_tab=tabs is not None, two_inputs=second is not None)
    return pl.pallas_call(
        kern,
        grid=(M // bm, N // bn),
        in_specs=in_specs,
        out_specs=pl.BlockSpec((bm, bn), lambda i, j: (i, j)),
        out_shape=jax.ShapeDtypeStruct((M, N), out_dtype),
        compiler_params=_params("parallel", "parallel"),
        name=name,
    )(*args)


def _mm_ksplit_kernel(x_ref, w_ref, o_ref):
    part = jnp.dot(x_ref[...], w_ref[...], preferred_element_type=F32)

    @pl.when(pl.program_id(2) == 0)
    def _():
        o_ref[...] = part

    @pl.when(pl.program_id(2) > 0)
    def _():
        o_ref[...] += part


def _mm_ksplit(x, w, *, bm=1024, bn=1024, bk=2048, name="mm_ksplit"):
    M, K = x.shape
    N = w.shape[1]
    bm, bn, bk = _tile(M, bm), _tile(N, bn), _tile(K, bk)
    return pl.pallas_call(
        _mm_ksplit_kernel,
        grid=(M // bm, N // bn, K // bk),
        in_specs=[pl.BlockSpec((bm, bk), lambda i, j, k: (i, k)),
                  pl.BlockSpec((bk, bn), lambda i, j, k: (k, j))],
        out_specs=pl.BlockSpec((bm, bn), lambda i, j, k: (i, j)),
        out_shape=jax.ShapeDtypeStruct((M, N), F32),
        compiler_params=_params("parallel", "parallel", "arbitrary"),
        name=name,
    )(x, w)


def _swiglu_up_kernel(x_ref, w1_ref, w3_ref, o_ref):
    x = x_ref[...]
    a = jnp.dot(x, w1_ref[...], preferred_element_type=F32)
    b = jnp.dot(x, w3_ref[...], preferred_element_type=F32)
    o_ref[...] = (a * jax.nn.sigmoid(a) * b).astype(o_ref.dtype)


def _swiglu_up(x, w1, w3, *, bm=1024, bn=512):
    M, K = x.shape
    N = w1.shape[1]
    bm, bn = _tile(M, bm), _tile(N, bn)
    return pl.pallas_call(
        _swiglu_up_kernel,
        grid=(M // bm, N // bn),
        in_specs=[pl.BlockSpec((bm, K), lambda i, j: (i, 0)),
                  pl.BlockSpec((K, bn), lambda i, j: (0, j)),
                  pl.BlockSpec((K, bn), lambda i, j: (0, j))],
        out_specs=pl.BlockSpec((bm, bn), lambda i, j: (i, j)),
        out_shape=jax.ShapeDtypeStruct((M, N), BF16),
        compiler_params=_params("parallel", "parallel"),
        name="swiglu_up",
    )(x, w1, w3)


def _layer_norm_rows(z, g, b):
    mu = jnp.mean(z, axis=-1, keepdims=True)
    zc = z - mu
    var = jnp.mean(zc * zc, axis=-1, keepdims=True)
    return zc * lax.rsqrt(var + LN_EPS) * g + b


def _pack_bf16_pairs(x):
    half = x.shape[1] // 2
    hi = pltpu.bitcast(x[:, :half].astype(BF16).astype(F32), I32)
    lo = pltpu.bitcast(x[:, half:].astype(BF16).astype(F32), I32)
    return hi | lax.shift_right_logical(lo, 16)


def _unpack_bf16_pairs(w):
    hi = pltpu.bitcast(w & jnp.int32(-65536), F32).astype(BF16)
    lo = pltpu.bitcast(lax.shift_left(w, 16), F32).astype(BF16)
    return hi, lo


def _add_ln_kernel(x_ref, y_ref, g_ref, b_ref, of_ref, o2_ref, *, packed):
    out = _layer_norm_rows(ALPHA * x_ref[...] + y_ref[...], g_ref[...], b_ref[...])
    of_ref[...] = out
    o2_ref[...] = _pack_bf16_pairs(out) if packed else out.astype(o2_ref.dtype)


def _add_ln(x, y, g, b, *, packed=False, bm=256):
    M, D = x.shape
    bm = _tile(M, bm, 8)
    row = pl.BlockSpec((bm, D), lambda i: (i, 0))
    vec = pl.BlockSpec((1, D), lambda i: (0, 0))
    second = jax.ShapeDtypeStruct((M, D // 2), I32) if packed else jax.ShapeDtypeStruct((M, D), BF16)
    return pl.pallas_call(
        functools.partial(_add_ln_kernel, packed=packed),
        grid=(M // bm,),
        in_specs=[row, row, vec, vec],
        out_specs=[row, pl.BlockSpec((bm, second.shape[1]), lambda i: (i, 0))],
        out_shape=[jax.ShapeDtypeStruct((M, D), F32), second],
        compiler_params=_params("parallel"),
        name="add_ln",
    )(x, y, g.reshape(1, D), b.reshape(1, D))


LOG2E = 1.4426950408889634


def _online_softmax(s, m, l):
    m_new = jnp.maximum(m, jnp.max(s, axis=1, keepdims=True))
    a = jnp.exp2(m - m_new)
    p = jnp.exp2((s - m_new).astype(BF16))
    return m_new, a * l + jnp.sum(p.astype(F32), axis=1, keepdims=True), a, p


def _dsa_kernel(q_ref, k_ref, v_ref, iq_ref, ik_ref, o_ref, keys_ref, bias_ref, *, topk, ck, n_rep, idx_bits):
    blk = pl.program_id(1)
    n_chunks = (blk * Q_BLOCK + Q_BLOCK + ck - 1) // ck
    row = lax.broadcasted_iota(I32, (Q_BLOCK, ck), 0) + blk * Q_BLOCK
    lane = lax.broadcasted_iota(I32, (Q_BLOCK, ck), 1)
    w_off = IDX_HEADS * IDX_DIM + IDX_DIM
    iq = iq_ref[0]
    wi = iq[:, w_off:w_off + IDX_HEADS]
    q_idx = jnp.concatenate([iq[:, h * IDX_DIM:(h + 1) * IDX_DIM].astype(BF16) for h in range(IDX_HEADS)], axis=0)

    def score_body(c, carry):
        off = pl.multiple_of(c * ck, ck)
        kc = ik_ref[0, pl.ds(off, ck), :][:, :IDX_DIM].astype(BF16)
        lg = lax.dot_general(q_idx, kc, NT_DIMS, preferred_element_type=F32)
        s = jnp.zeros((Q_BLOCK, ck), F32)
        for h in range(IDX_HEADS):
            s = s + wi[:, h:h + 1] * jnp.maximum(lg[h * Q_BLOCK:(h + 1) * Q_BLOCK], 0.0)
        bits = pltpu.bitcast(s, I32)
        key = bits ^ ((bits >> 31) & 0x7FFFFFFF)
        keys_ref[c] = jnp.where(lane + off <= row, key, INT_MIN)
        return carry

    lax.fori_loop(0, n_chunks, score_body, 0)

    def count(indicator):
        def body(c, acc):
            part = indicator(keys_ref[c], lane + c * ck)
            for j in range(ck // LANE):
                acc = acc + part[:, j * LANE:(j + 1) * LANE]
            return acc
        acc = lax.fori_loop(0, n_chunks, body, jnp.zeros((Q_BLOCK, LANE), F32))
        return jnp.sum(acc, axis=1, keepdims=True)

    kf = float(topk)
    ok = count(lambda kc, idx: jnp.where(kc >= 0, 1.0, 0.0)) >= kf
    thr = jnp.where(ok, 0, INT_MIN).astype(I32)

    def bit_body(i, thr):
        cand = thr + lax.shift_left(jnp.int32(1), 30 - i)
        ok = count(lambda kc, idx: jnp.where(kc >= cand, 1.0, 0.0)) >= kf
        return jnp.where(ok, cand, thr)

    thr = lax.fori_loop(0, 31, bit_body, thr)

    n_gt = count(lambda kc, idx: jnp.where(kc > thr, 1.0, 0.0))
    n_ge = count(lambda kc, idx: jnp.where(kc >= thr, 1.0, 0.0))
    need = kf - n_gt
    has_thr = thr > INT_MIN
    surplus = jnp.where(has_thr, n_ge - n_gt - need, 0.0)

    def tie_search():
        def tie_body(i, last):
            cand = last + lax.shift_left(jnp.int32(1), idx_bits - 1 - i)
            ok = count(lambda kc, idx: jnp.where(kc == thr, jnp.where(idx < cand, 1.0, 0.0), 0.0)) < need
            return jnp.where(ok, cand, last)
        return lax.fori_loop(0, idx_bits, tie_body, jnp.zeros((Q_BLOCK, 1), I32))

    last = lax.cond(jnp.max(surplus) > 0.0, tie_search, lambda: jnp.full((Q_BLOCK, 1), 2 ** idx_bits, I32))
    last = jnp.where(has_thr, last, -1)

    def bias_body(c, carry):
        kc = keys_ref[c]
        tie_bias = jnp.where(lane + c * ck <= last, 0.0, MASKED)
        bias_ref[c] = jnp.where(kc == thr, tie_bias, jnp.where(kc > thr, 0.0, MASKED))
        return carry

    lax.fori_loop(0, n_chunks, bias_body, 0)

    q = q_ref[0]
    rows = n_rep * Q_BLOCK
    qgs = [jnp.concatenate([q[:, (g * n_rep + r) * A_HEAD_DIM:(g * n_rep + r + 1) * A_HEAD_DIM]
                            for r in range(n_rep)], axis=0) for g in range(A_KV_HEADS)]

    def att_body(c, carry, n_sub):
        width = n_sub * ck
        off = pl.multiple_of(c * width, width)
        b = jnp.concatenate([bias_ref[c * n_sub + t] for t in range(n_sub)], axis=1)
        bias = jnp.concatenate([b] * n_rep, axis=0)
        out = []
        for g in range(A_KV_HEADS):
            m, l, acc = carry[g]
            kc = k_ref[0, pl.ds(off, width), g * A_HEAD_DIM:(g + 1) * A_HEAD_DIM]
            vc = v_ref[0, pl.ds(off, width), g * A_HEAD_DIM:(g + 1) * A_HEAD_DIM]
            s = lax.dot_general(qgs[g], kc, NT_DIMS, preferred_element_type=F32) + bias
            m, l, a, p = _online_softmax(s, m, l)
            out.append((m, l, a * acc + jnp.dot(p, vc, preferred_element_type=F32)))
        return tuple(out)

    one = (jnp.full((rows, 1), MASKED, F32), jnp.zeros((rows, 1), F32), jnp.zeros((rows, A_HEAD_DIM), F32))
    carry = (one,) * A_KV_HEADS
    if k_ref.shape[1] >= 2 * ck:
        carry = lax.fori_loop(0, n_chunks // 2, functools.partial(att_body, n_sub=2), carry)
        carry = lax.cond(n_chunks % 2 == 1, lambda c: att_body(n_chunks - 1, c, 1), lambda c: c, carry)
    else:
        carry = att_body(0, carry, 1)
    for g in range(A_KV_HEADS):
        _, l, acc = carry[g]
        o = acc / l
        for r in range(n_rep):
            col = (g * n_rep + r) * A_HEAD_DIM
            o_ref[0, :, col:col + A_HEAD_DIM] = o[r * Q_BLOCK:(r + 1) * Q_BLOCK].astype(o_ref.dtype)


def _dsa(q, k, v, idx, *, B, T):
    a_heads = q.shape[-1] // A_HEAD_DIM
    n_rep = a_heads // A_KV_HEADS
    topk = min(DSA_TOPK_MAX, T // 4)
    ck = _tile(T, 512)
    idx_w = idx.shape[-1]
    kv_w = k.shape[-1]
    kern = functools.partial(_dsa_kernel, topk=topk, ck=ck, n_rep=n_rep, idx_bits=max(1, (T - 1).bit_length()))
    return pl.pallas_call(
        kern,
        grid=(B, T // Q_BLOCK),
        in_specs=[pl.BlockSpec((1, Q_BLOCK, q.shape[-1]), lambda b, i: (b, i, 0)),
                  pl.BlockSpec((1, T, kv_w), lambda b, i: (b, 0, 0)),
                  pl.BlockSpec((1, T, kv_w), lambda b, i: (b, 0, 0)),
                  pl.BlockSpec((1, Q_BLOCK, idx_w), lambda b, i: (b, i, 0)),
                  pl.BlockSpec((1, T, LANE), lambda b, i: (b, 0, IDX_HEADS * IDX_DIM // LANE))],
        out_specs=pl.BlockSpec((1, Q_BLOCK, q.shape[-1]), lambda b, i: (b, i, 0)),
        out_shape=jax.ShapeDtypeStruct(q.shape, BF16),
        scratch_shapes=[pltpu.VMEM((T // ck, Q_BLOCK, ck), I32), pltpu.VMEM((T // ck, Q_BLOCK, ck), F32)],
        compiler_params=_params("parallel", "parallel"),
        name="dsa",
    )(q, k, v, idx, idx)


RET_HEADS_PER_STEP = 2


def _retention_kernel(q_ref, k_ref, v_ref, g_ref, gn_ref, din_ref, qd_ref, kd_ref, cd_ref, o_ref, state_ref, *, n_sub):
    @pl.when(pl.program_id(2) == 0)
    def _():
        state_ref[...] = jnp.zeros_like(state_ref)

    W = RET_VAL_DIM
    for s in range(n_sub):
        sl = pl.ds(s * RET_CHUNK, RET_CHUNK)
        for j in range(RET_HEADS_PER_STEP):
            cols = slice(j * W, (j + 1) * W)
            qc = q_ref[sl, cols]
            kc = k_ref[sl, cols]
            vc = v_ref[sl, cols]
            st = state_ref[j]
            inner = lax.dot_general(qc, kc, NT_DIMS, preferred_element_type=F32) * din_ref[j]
            o = (jnp.dot(inner.astype(BF16), vc, preferred_element_type=F32)
                 + jnp.dot(qc, st.astype(BF16), preferred_element_type=F32) * qd_ref[j])
            vk = (vc.astype(F32) * kd_ref[j]).astype(BF16)
            state_ref[j] = st * cd_ref[j] + lax.dot_general(kc, vk, TN_DIMS, preferred_element_type=F32)
            mu = jnp.mean(o, axis=-1, keepdims=True)
            oc = o - mu
            var = jnp.mean(oc * oc, axis=-1, keepdims=True)
            gate = g_ref[sl, cols].astype(F32)
            normed = oc * lax.rsqrt(var + LN_EPS) * gn_ref[:, cols]
            o_ref[sl, cols] = (gate * jax.nn.sigmoid(gate) * normed).astype(o_ref.dtype)


def _retention(qk, pv, gn_g, *, B, T, heads, v_blk0):
    N = qk.shape[0]
    C = RET_CHUNK
    rb = _tile(T, 512)
    n_sub = rb // C
    nr = T // rb
    log_gamma = jnp.log(1.0 - 2.0 ** (-5.0 - jnp.arange(heads, dtype=F32)))
    pos = jnp.arange(C, dtype=F32)
    diff = pos[:, None] - pos[None, :]
    din = jnp.exp(jnp.where(diff[None] >= 0, log_gamma[:, None, None] * diff[None], -jnp.inf))
    qd = jnp.exp(log_gamma[:, None] * (pos[None] + 1.0))[:, :, None]
    kd = jnp.exp(log_gamma[:, None] * (C - 1.0 - pos[None]))[:, :, None]
    cd = jnp.exp(log_gamma * C)[:, None, None]
    hp = RET_HEADS_PER_STEP
    W = hp * RET_VAL_DIM
    assert heads % hp == 0 and v_blk0 % hp == 0
    blk = lambda off: pl.BlockSpec((rb, W), lambda b, h, r: (b * nr + r, off // hp + h))
    per_head = lambda shape: pl.BlockSpec((hp,) + shape, lambda b, h, r: (h, 0, 0))
    return pl.pallas_call(
        functools.partial(_retention_kernel, n_sub=n_sub),
        grid=(B, heads // hp, nr),
        in_specs=[blk(0), blk(heads), blk(v_blk0), blk(v_blk0 + heads),
                  pl.BlockSpec((1, W), lambda b, h, r: (0, h)),
                  per_head((C, C)), per_head((C, 1)), per_head((C, 1)), per_head((1, 1))],
        out_specs=pl.BlockSpec((rb, W), lambda b, h, r: (b * nr + r, h)),
        out_shape=jax.ShapeDtypeStruct((N, heads * RET_VAL_DIM), BF16),
        scratch_shapes=[pltpu.VMEM((hp, RET_KEY_DIM, RET_VAL_DIM), F32)],
        compiler_params=_params("parallel", "parallel", "arbitrary"),
        name="retention",
    )(qk, qk, pv, pv, gn_g.reshape(1, heads * RET_VAL_DIM), din, qd, kd, cd)


def _mla_down_kernel(x_ref, w_ref, qg_ref, kvg_ref, c_ref, sa_ref, sb_ref, cq_ref, ckv_ref, kr_ref, *, q_rank, kv_rank):
    acc = jnp.dot(x_ref[...], w_ref[...], preferred_element_type=F32)

    def rms(a, g):
        return a * lax.rsqrt(jnp.mean(a * a, axis=-1, keepdims=True) + RMS_EPS) * g

    cq_ref[...] = rms(acc[:, :q_rank], qg_ref[...]).astype(cq_ref.dtype)
    ckv_ref[...] = rms(acc[:, q_rank:q_rank + kv_rank], kvg_ref[...]).astype(ckv_ref.dtype)
    kr = acc[:, q_rank + kv_rank:]
    half = MLA_ROPE // 2
    kr = kr * c_ref[...] + pltpu.roll(kr, LANE - half, 1) * sa_ref[...] + pltpu.roll(kr, half, 1) * sb_ref[...]
    kr_ref[...] = kr.astype(kr_ref.dtype)


def _mla_down(x, w, q_g, kv_g, tabs, *, T, q_rank, kv_rank, bm=512):
    M, K = x.shape
    Nw = w.shape[1]
    bm = _tile(T, bm)
    tb = T // bm
    row = lambda n: pl.BlockSpec((bm, n), lambda i: (i, 0))
    tab = pl.BlockSpec((bm, LANE), lambda i: (i % tb, 0))
    return pl.pallas_call(
        functools.partial(_mla_down_kernel, q_rank=q_rank, kv_rank=kv_rank),
        grid=(M // bm,),
        in_specs=[row(K), pl.BlockSpec((K, Nw), lambda i: (0, 0)),
                  pl.BlockSpec((1, q_rank), lambda i: (0, 0)), pl.BlockSpec((1, kv_rank), lambda i: (0, 0)),
                  tab, tab, tab],
        out_specs=[row(q_rank), row(kv_rank), row(LANE)],
        out_shape=[jax.ShapeDtypeStruct((M, q_rank), BF16), jax.ShapeDtypeStruct((M, kv_rank), BF16),
                   jax.ShapeDtypeStruct((M, LANE), BF16)],
        compiler_params=_params("parallel"),
        name="mla_down",
    )(x, w, q_g.reshape(1, q_rank), kv_g.reshape(1, kv_rank), *tabs)


MLA_HEADS_PER_STEP = 4


def _mla_attn_kernel(qn_ref, qr_ref, kn_ref, kr_ref, v_ref, o_ref, *, tq):
    i = pl.program_id(2)
    lane = lax.broadcasted_iota(I32, (tq, LANE), 1)
    qs = []
    for j in range(MLA_HEADS_PER_STEP):
        lo = (j % 2) * MLA_ROPE
        own = jnp.where((lane >= lo) & (lane < lo + MLA_ROPE), 1.0, 0.0)
        pair = qr_ref[:, (j // 2) * LANE:(j // 2 + 1) * LANE].astype(F32)
        qr = (pair * own).astype(BF16)
        qs.append(jnp.concatenate([qn_ref[:, j * LANE:(j + 1) * LANE], qr], axis=1))

    def step(c, carry, masked, width):
        off = pl.multiple_of(c * width, width)
        kr = kr_ref[pl.ds(off, width), :]
        out = []
        for j in range(MLA_HEADS_PER_STEP):
            m, l, acc = carry[j]
            k = jnp.concatenate([kn_ref[pl.ds(off, width), j * LANE:(j + 1) * LANE], kr], axis=1)
            s = lax.dot_general(qs[j], k, NT_DIMS, preferred_element_type=F32)
            if masked:
                r_io = lax.broadcasted_iota(I32, (tq, width), 0)
                c_io = lax.broadcasted_iota(I32, (tq, width), 1)
                s = jnp.where(c_io <= r_io, s, MASKED)
            m, l, a, p = _online_softmax(s, m, l)
            v = v_ref[pl.ds(off, width), j * MLA_V:(j + 1) * MLA_V]
            out.append((m, l, a * acc + jnp.dot(p, v, preferred_element_type=F32)))
        return tuple(out)

    one = (jnp.full((tq, 1), MASKED, F32), jnp.zeros((tq, 1), F32), jnp.zeros((tq, MLA_V), F32))
    carry = (one,) * MLA_HEADS_PER_STEP
    if kn_ref.shape[0] >= 2 * tq:
        carry = lax.fori_loop(0, i // 2, functools.partial(step, masked=False, width=2 * tq), carry)
        carry = lax.cond(i % 2 == 1, lambda c: step(i - 1, c, False, tq), lambda c: c, carry)
    carry = step(i, carry, True, tq)
    for j in range(MLA_HEADS_PER_STEP):
        _, l, acc = carry[j]
        o_ref[:, j * MLA_V:(j + 1) * MLA_V] = (acc / l).astype(o_ref.dtype)


def _mla_attn(qn, qr, kv, kr, *, B, T, heads):
    N = qn.shape[0]
    tq = _tile(T, 512)
    nq = T // tq
    hp = MLA_HEADS_PER_STEP
    w = hp * LANE
    return pl.pallas_call(
        functools.partial(_mla_attn_kernel, tq=tq),
        grid=(B, heads // hp, nq),
        in_specs=[pl.BlockSpec((tq, w), lambda b, h, i: (b * nq + i, h)),
                  pl.BlockSpec((tq, hp * MLA_ROPE), lambda b, h, i: (b * nq + i, h)),
                  pl.BlockSpec((T, w), lambda b, h, i: (b, h)),
                  pl.BlockSpec((T, LANE), lambda b, h, i: (b, 0)),
                  pl.BlockSpec((T, w), lambda b, h, i: (b, heads // hp + h))],
        out_specs=pl.BlockSpec((tq, w), lambda b, h, i: (b * nq + i, h)),
        out_shape=jax.ShapeDtypeStruct((N, heads * MLA_V), BF16),
        compiler_params=_params("parallel", "parallel", "parallel"),
        name="mla_attn",
    )(qn, qr, kv, kr, kv)


def _router_kernel(x_ref, r_ref, meta_ref, cnt_ref, carry_ref, *, n_exp):
    @pl.when(pl.program_id(0) == 0)
    def _():
        carry_ref[...] = jnp.zeros_like(carry_ref)

    bm = x_ref.shape[0]
    logits = jnp.dot(x_ref[...], r_ref[...], preferred_element_type=F32, precision=lax.Precision.HIGHEST)
    lane = lax.broadcasted_iota(I32, (bm, LANE), 1).astype(F32)
    logits = jnp.where(lane < n_exp, logits, -jnp.inf)
    m1 = jnp.max(logits, axis=1, keepdims=True)
    i1 = jnp.min(jnp.where(logits == m1, lane, float(LANE)), axis=1, keepdims=True)
    rest = jnp.where(lane == i1, -jnp.inf, logits)
    m2 = jnp.max(rest, axis=1, keepdims=True)
    i2 = jnp.min(jnp.where(rest == m2, lane, float(LANE)), axis=1, keepdims=True)
    e = jnp.exp(m2 - m1)
    g1 = 1.0 / (1.0 + e)
    g2 = e / (1.0 + e)
    sel = jnp.where(lane == i1, 1.0, jnp.where(lane == i2, 1.0, 0.0))
    r_io = lax.broadcasted_iota(I32, (bm, bm), 0)
    c_io = lax.broadcasted_iota(I32, (bm, bm), 1)
    below = jnp.where(c_io < r_io, 1.0, 0.0).astype(BF16)
    carry = carry_ref[0:1, :]
    rank = jnp.dot(below, sel.astype(BF16), preferred_element_type=F32) + carry
    r1 = jnp.sum(jnp.where(lane == i1, rank, 0.0), axis=1, keepdims=True)
    r2 = jnp.sum(jnp.where(lane == i2, rank, 0.0), axis=1, keepdims=True)
    meta = jnp.where(lane == 0, i1, 0.0)
    meta = jnp.where(lane == 1, i2, meta)
    meta = jnp.where(lane == 2, g1, meta)
    meta = jnp.where(lane == 3, g2, meta)
    meta = jnp.where(lane == 4, r1, meta)
    meta = jnp.where(lane == 5, r2, meta)
    meta_ref[...] = meta
    total = carry + jnp.sum(sel, axis=0, keepdims=True)
    carry_ref[...] = jnp.broadcast_to(total, carry_ref.shape)
    cnt_ref[...] = jnp.broadcast_to(total, cnt_ref.shape)


def _router(x, router, *, bm=512):
    M, D = x.shape
    n_exp = router.shape[1]
    bm = _tile(M, bm)
    r_pad = jnp.zeros((D, LANE), F32).at[:, :n_exp].set(router)
    return pl.pallas_call(
        functools.partial(_router_kernel, n_exp=n_exp),
        grid=(M // bm,),
        in_specs=[pl.BlockSpec((bm, D), lambda i: (i, 0)), pl.BlockSpec((D, LANE), lambda i: (0, 0))],
        out_specs=[pl.BlockSpec((bm, LANE), lambda i: (i, 0)), pl.BlockSpec((8, LANE), lambda i: (0, 0))],
        out_shape=[jax.ShapeDtypeStruct((M, LANE), F32), jax.ShapeDtypeStruct((8, LANE), F32)],
        scratch_shapes=[pltpu.VMEM((8, LANE), F32)],
        compiler_params=_params("arbitrary"),
        name="router",
    )(x, r_pad)


DMA_LOOP_UNROLL = 8


def _moe_gather_kernel(tok_ref, nv_ref, x_hbm, o_ref, stage_ref, sem, *, tm):
    r = pl.program_id(0)
    nv = nv_ref[0]

    half = stage_ref.shape[1]

    def row_copy(j, tok):
        return pltpu.make_async_copy(x_hbm.at[pl.ds(tok, 1)], stage_ref.at[pl.ds(j, 1)], sem)

    def gather_start(tile):
        def body(j, carry):
            row_copy(j, tok_ref[tile * tm + j]).start()
            return carry
        lax.fori_loop(0, tm, body, 0, unroll=DMA_LOOP_UNROLL)

    def gather_wait():
        def body(j, carry):
            row_copy(j, 0).wait()
            return carry
        lax.fori_loop(0, tm, body, 0, unroll=DMA_LOOP_UNROLL)

    @pl.when(r < nv)
    def _():
        @pl.when(r == 0)
        def _():
            gather_start(0)

        gather_wait()
        hi, lo = _unpack_bf16_pairs(stage_ref[...])
        o_ref[:, :half] = hi
        o_ref[:, half:] = lo

        @pl.when(r + 1 < nv)
        def _():
            gather_start(r + 1)


def _moe_up_kernel(te_ref, nv_ref, xs_ref, w1_ref, w3_ref, o_ref, wb1_ref, wb3_ref):
    r = pl.program_id(1)

    @pl.when(r < nv_ref[0])
    def _():
        @pl.when((r == 0) | (te_ref[r] != te_ref[jnp.maximum(r - 1, 0)]))
        def _():
            wb1_ref[...] = w1_ref[0].astype(BF16)
            wb3_ref[...] = w3_ref[0].astype(BF16)

        xb = xs_ref[...]
        a = jnp.dot(xb, wb1_ref[...], preferred_element_type=F32)
        b = jnp.dot(xb, wb3_ref[...], preferred_element_type=F32)
        o_ref[...] = (a * jax.nn.sigmoid(a) * b).astype(o_ref.dtype)


def _moe_down_kernel(te_ref, nv_ref, h_ref, w2_ref, o_ref, wb_ref):
    r = pl.program_id(1)

    @pl.when(r < nv_ref[0])
    def _():
        @pl.when((r == 0) | (te_ref[r] != te_ref[jnp.maximum(r - 1, 0)]))
        def _():
            wb_ref[...] = w2_ref[0].astype(BF16)

        o_ref[...] = jnp.dot(h_ref[...], wb_ref[...], preferred_element_type=F32)


def _experts(x, w1, w3, w2, tile_expert, n_valid, row_tok, *, tm, tf=512, tn=512):
    D = 2 * x.shape[1]
    P = row_tok.shape[0]
    E, _, F = w1.shape
    tf, tn = _tile(F, tf), _tile(D, tn)
    n_tiles = P // tm
    row = lambda r, nv: jnp.minimum(r, nv[0] - 1)

    xs = pl.pallas_call(
        functools.partial(_moe_gather_kernel, tm=tm),
        grid_spec=pltpu.PrefetchScalarGridSpec(
            num_scalar_prefetch=2, grid=(n_tiles,),
            in_specs=[pl.BlockSpec(memory_space=pl.ANY)],
            out_specs=pl.BlockSpec((tm, D), lambda r, tok, nv: (row(r, nv), 0)),
            scratch_shapes=[pltpu.VMEM((tm, D // 2), I32), pltpu.SemaphoreType.DMA(())]),
        out_shape=jax.ShapeDtypeStruct((P, D), BF16),
        compiler_params=_params("arbitrary"),
        name="moe_gather",
    )(row_tok, n_valid, x)

    hid = pl.pallas_call(
        _moe_up_kernel,
        grid_spec=pltpu.PrefetchScalarGridSpec(
            num_scalar_prefetch=2, grid=(F // tf, n_tiles),
            in_specs=[pl.BlockSpec((tm, D), lambda f, r, te, nv: (row(r, nv), 0)),
                      pl.BlockSpec((1, D, tf), lambda f, r, te, nv: (te[r], 0, f)),
                      pl.BlockSpec((1, D, tf), lambda f, r, te, nv: (te[r], 0, f))],
            out_specs=pl.BlockSpec((tm, tf), lambda f, r, te, nv: (row(r, nv), f)),
            scratch_shapes=[pltpu.VMEM((D, tf), BF16), pltpu.VMEM((D, tf), BF16)]),
        out_shape=jax.ShapeDtypeStruct((P, F), BF16),
        compiler_params=_params("arbitrary", "arbitrary"),
        name="moe_up",
    )(tile_expert, n_valid, xs, w1, w3)

    return pl.pallas_call(
        _moe_down_kernel,
        grid_spec=pltpu.PrefetchScalarGridSpec(
            num_scalar_prefetch=2, grid=(D // tn, n_tiles),
            in_specs=[pl.BlockSpec((tm, F), lambda n, r, te, nv: (row(r, nv), 0)),
                      pl.BlockSpec((1, F, tn), lambda n, r, te, nv: (te[r], 0, n))],
            out_specs=pl.BlockSpec((tm, tn), lambda n, r, te, nv: (row(r, nv), n)),
            scratch_shapes=[pltpu.VMEM((F, tn), BF16)]),
        out_shape=jax.ShapeDtypeStruct((P, D), F32),
        compiler_params=_params("arbitrary", "arbitrary"),
        name="moe_down",
    )(tile_expert, n_valid, hid, w2)


def _combine_kernel(dest_ref, x_ref, meta_ref, g_ref, b_ref, ys_hbm, o_ref, buf_ref, sem):
    bm = x_ref.shape[0]
    i = pl.program_id(0)
    n = pl.num_programs(0)

    def copy(slot, j, k, d):
        return pltpu.make_async_copy(ys_hbm.at[pl.ds(d, 1)], buf_ref.at[slot, k, pl.ds(j, 1)], sem.at[slot, k])

    def gather_start(blk):
        slot = blk % 2

        def body(j, carry):
            for k in range(2):
                copy(slot, j, k, dest_ref[2 * (blk * bm + j) + k]).start()
            return carry
        lax.fori_loop(0, bm, body, 0, unroll=DMA_LOOP_UNROLL)

    @pl.when(i == 0)
    def _():
        gather_start(0)

    @pl.when(i + 1 < n)
    def _():
        gather_start(i + 1)

    slot = i % 2

    def wait(j, carry):
        for k in range(2):
            copy(slot, j, k, 0).wait()
        return carry

    lax.fori_loop(0, bm, wait, 0, unroll=DMA_LOOP_UNROLL)
    meta = meta_ref[...]
    y = meta[:, 2:3] * buf_ref[slot, 0] + meta[:, 3:4] * buf_ref[slot, 1]
    o_ref[...] = _layer_norm_rows(ALPHA * x_ref[...] + y, g_ref[...], b_ref[...])


def _combine(x, meta, ys, dest, g, b, *, bm=256):
    M, D = x.shape
    bm = _tile(M, bm, 8)
    row = lambda n: pl.BlockSpec((bm, n), lambda i, d: (i, 0))
    vec = pl.BlockSpec((1, D), lambda i, d: (0, 0))
    return pl.pallas_call(
        _combine_kernel,
        grid_spec=pltpu.PrefetchScalarGridSpec(
            num_scalar_prefetch=1, grid=(M // bm,),
            in_specs=[row(D), row(LANE), vec, vec, pl.BlockSpec(memory_space=pl.ANY)],
            out_specs=row(D),
            scratch_shapes=[pltpu.VMEM((2, 2, bm, D), F32), pltpu.SemaphoreType.DMA((2, 2))]),
        out_shape=jax.ShapeDtypeStruct((M, D), F32),
        compiler_params=_params("arbitrary"),
        name="moe_combine",
    )(dest, x, meta, g.reshape(1, D), b.reshape(1, D), ys)


def _even_layer(x, xb, w_in, ret_gn_g, w_out, ln1_g, ln1_b, w1, w3, w2, ln2_g, ln2_b, *, B, T):
    N, D = x.shape
    a_heads = D // 2 // A_HEAD_DIM
    r_heads = D // 2 // RET_VAL_DIM
    qa_w, kv_w = a_heads * A_HEAD_DIM, A_KV_HEADS * A_HEAD_DIM
    qi_w = IDX_HEADS * IDX_DIM
    rk_w, rv_w = r_heads * RET_KEY_DIM, r_heads * RET_VAL_DIM
    sizes = (qa_w, kv_w, kv_w, qi_w, IDX_DIM, IDX_HEADS, rk_w, rk_w, rv_w, rv_w)
    offs = [0]
    for s in sizes:
        offs.append(offs[-1] + s)
    col = lambda a, b_: w_in[:, offs[a]:offs[b_]]
    w_qa, w_ka, w_va = col(0, 1), col(1, 2), col(2, 3)
    w_qi, w_ki, w_wi = col(3, 4), col(4, 5), col(5, 6)
    w_qb, w_kb, w_vb, w_gb = col(6, 7), col(7, 8), col(8, 9), col(9, 10)

    cos_a, sin_a = _rope_cos_sin(T, A_HEAD_DIM // 4, ROPE_THETA)
    tab_q = _lane_tables(cos_a, sin_a, A_HEAD_DIM, A_HEAD_DIM ** -0.5 * LOG2E)
    tab_k = _lane_tables(cos_a, sin_a, A_HEAD_DIM)
    cos_i, sin_i = _rope_cos_sin(T, IDX_DIM // 4, ROPE_THETA)
    tab_i = _lane_tables(cos_i, sin_i, IDX_DIM)
    pass_c = jnp.ones((T, LANE - IDX_DIM), F32)
    pass_s = jnp.zeros((T, LANE - IDX_DIM), F32)
    tab_idx = tuple(jnp.concatenate([t, t[:, :IDX_DIM], p], 1)
                    for t, p in zip(tab_i, (pass_c, pass_s, pass_s)))
    inv = 1.0 / (RET_THETA ** jnp.linspace(0.0, 1.0, RET_KEY_DIM // 2, dtype=F32))
    ang = jnp.arange(T, dtype=F32)[:, None] * inv[None, :]
    cos_r, sin_r = jnp.cos(ang), jnp.sin(ang)
    sin_pair = jnp.concatenate([-sin_r, sin_r], 1)
    tab_r = (jnp.concatenate([cos_r, cos_r], 1), sin_pair, sin_pair)

    idx_pad = LANE - IDX_DIM - IDX_HEADS
    w_idx = jnp.concatenate([w_qi, w_ki, w_wi * (IDX_DIM ** -0.5 * IDX_HEADS ** -0.5),
                             jnp.zeros((D, idx_pad), F32)], 1).astype(BF16)
    w_rqk = jnp.concatenate([w_qb, w_kb * RET_KEY_DIM ** -0.5], 1).astype(BF16)
    w_pv = jnp.concatenate([w_va, w_vb, w_gb], 1).astype(BF16)
    qa = _proj(xb, w_qa.astype(BF16), out_dtype=BF16, tabs=tab_q, half=A_HEAD_DIM // 8,
               seq_len=T, name="proj_qa", **_pat(qa_w, 1024, 0))
    ka = _proj(xb, w_ka.astype(BF16), out_dtype=BF16, tabs=tab_k, half=A_HEAD_DIM // 8,
               seq_len=T, name="proj_ka", **_pat(kv_w, 1024, 0))
    n_idx = w_idx.shape[1]
    idx = _proj(xb, w_idx, out_dtype=F32, tabs=tab_idx, half=IDX_DIM // 8, seq_len=T, name="proj_idx",
                bm=512, bn=n_idx, slab_pat=(0,) * (qi_w // LANE) + (1,))
    rqk = _proj(xb, w_rqk, out_dtype=BF16, tabs=tab_r, mode="pair", seq_len=T, name="proj_ret_qk",
                bn=1024, slab_pat=(0, 1) * (_tile(2 * rk_w, 1024) // (2 * LANE)))
    pv = _proj(xb, w_pv, out_dtype=BF16, bn=768, seq_len=T, name="proj_v")

    ya = _dsa(qa.reshape(B, T, qa_w), ka.reshape(B, T, kv_w), pv.reshape(B, T, -1), idx.reshape(B, T, n_idx),
              B=B, T=T)
    yb = _retention(rqk, pv, ret_gn_g, B=B, T=T, heads=r_heads, v_blk0=kv_w // RET_VAL_DIM)
    w_out_b = w_out.astype(BF16)
    y = _proj(ya.reshape(N, qa_w), w_out_b[:qa_w], second=(yb, w_out_b[qa_w:]), out_dtype=F32, name="proj_out0")
    x1, x1b = _add_ln(x, y, ln1_g, ln1_b)
    hid = _swiglu_up(x1b, w1.astype(BF16), w3.astype(BF16))
    y = _mm_ksplit(hid, w2.astype(BF16), bk=3584, name="ffn_down")
    return _add_ln(x1, y, ln2_g, ln2_b)


def _pat(width, bn, p):
    bn = _tile(width, bn)
    return dict(bn=bn, slab_pat=(p,) * (bn // LANE))


def _odd_layer(x, xb, w_dq_dkv, q_norm_g, w_uq, kv_norm_g, w_ukv, w_out, ln1_g, ln1_b,
               router, we1, we3, we2, ln2_g, ln2_b, *, B, T):
    N, D = x.shape
    heads = D // MLA_V
    q_rank, kv_rank = q_norm_g.shape[0], kv_norm_g.shape[0]
    scale = (MLA_NOPE + MLA_ROPE) ** -0.5 * LOG2E
    cos_c, sin_c = _rope_cos_sin(T, MLA_ROPE, ROPE_THETA)
    tab_kr = _lane_tables(cos_c, sin_c, MLA_ROPE)
    tab_qr = _lane_tables(cos_c, sin_c, MLA_ROPE, scale)

    w_kr = w_dq_dkv[:, q_rank + kv_rank:]
    w_down = jnp.concatenate([w_dq_dkv[:, :q_rank + kv_rank], w_kr, w_kr], 1).astype(BF16)
    cq, ckv, kr = _mla_down(xb, w_down, q_norm_g, kv_norm_g, tab_kr, T=T, q_rank=q_rank, kv_rank=kv_rank)
    w_uq3 = w_uq.reshape(q_rank, heads, MLA_NOPE + MLA_ROPE)
    w_qn = w_uq3[:, :, :MLA_NOPE].reshape(q_rank, heads * MLA_NOPE).astype(BF16)
    w_qr = w_uq3[:, :, MLA_NOPE:].reshape(q_rank, heads * MLA_ROPE).astype(BF16)
    w_kv3 = w_ukv.reshape(kv_rank, heads, MLA_NOPE + MLA_V)
    w_kv = jnp.concatenate([w_kv3[:, :, :MLA_NOPE].reshape(kv_rank, heads * MLA_NOPE),
                            w_kv3[:, :, MLA_NOPE:].reshape(kv_rank, heads * MLA_V)], 1).astype(BF16)
    qn = _proj(cq, w_qn, out_dtype=BF16, scale=scale, seq_len=T, name="proj_q_nope")
    qr = _proj(cq, w_qr, out_dtype=BF16, tabs=tab_qr, half=MLA_ROPE // 2, seq_len=T, name="proj_q_rope",
               **_pat(heads * MLA_ROPE, 1024, 0))
    kv = _proj(ckv, w_kv, out_dtype=BF16, seq_len=T, name="proj_kv")
    att = _mla_attn(qn, qr, kv, kr, B=B, T=T, heads=heads)
    y = _proj(att, w_out.astype(BF16), out_dtype=F32, name="proj_out1")
    x1, x1_packed = _add_ln(x, y, ln1_g, ln1_b, packed=True)

    E = router.shape[1]
    tm = _tile(N, 512)
    meta, cnt = _router(x1, router)
    counts = cnt[0, :E].astype(I32)
    padded = (counts + tm - 1) // tm * tm
    ends = jnp.cumsum(padded)
    starts = ends - padded
    i1, i2 = meta[:, 0].astype(I32), meta[:, 1].astype(I32)
    dest = jnp.stack([starts[i1] + meta[:, 4].astype(I32), starts[i2] + meta[:, 5].astype(I32)], 1).reshape(-1)
    n_rows = 2 * N + E * tm
    n_tiles = n_rows // tm
    n_valid = (ends[-1] // tm).astype(I32).reshape(1)
    tile_start = jnp.arange(n_tiles, dtype=I32) * tm
    tile_expert = jnp.minimum(jnp.sum(tile_start[:, None] >= ends[None, :], axis=1), E - 1).astype(I32)
    tile_expert = jnp.where(jnp.arange(n_tiles) < n_valid[0], tile_expert, tile_expert[jnp.maximum(n_valid[0] - 1, 0)])
    row_tok = jnp.zeros((n_rows,), I32).at[dest].set(jnp.repeat(jnp.arange(N, dtype=I32), 2))
    ys = _experts(x1_packed, we1, we3, we2, tile_expert, n_valid, row_tok, tm=tm)
    return _combine(x1, meta, ys, dest, ln2_g, ln2_b)


def kernel(x, l0_w_in, l0_ret_gn_g, l0_w_out, l0_ln1_g, l0_ln1_b, l0_ffn_w1, l0_ffn_w3, l0_ffn_w2, l0_ln2_g, l0_ln2_b, l1_w_dq_dkv, l1_q_norm_g, l1_w_uq, l1_kv_norm_g, l1_w_ukv, l1_w_out, l1_ln1_g, l1_ln1_b, l1_router, l1_moe_w1, l1_moe_w3, l1_moe_w2, l1_ln2_g, l1_ln2_b):
    B, T, D = x.shape
    x2 = x.reshape(B * T, D)
    h, hb = _even_layer(x2, x2.astype(BF16), l0_w_in, l0_ret_gn_g, l0_w_out, l0_ln1_g, l0_ln1_b,
                        l0_ffn_w1, l0_ffn_w3, l0_ffn_w2, l0_ln2_g, l0_ln2_b, B=B, T=T)
    out = _odd_layer(h, hb, l1_w_dq_dkv, l1_q_norm_g, l1_w_uq, l1_kv_norm_g, l1_w_ukv, l1_w_out,
                     l1_ln1_g, l1_ln1_b, l1_router, l1_moe_w1, l1_moe_w3, l1_moe_w2, l1_ln2_g, l1_ln2_b, B=B, T=T)
    return out.reshape(B, T, D)
```

```python
import functools

import jax
import jax.numpy as jnp
from jax import lax
from jax.experimental import pallas as pl
from jax.experimental.pallas import tpu as pltpu

F32 = jnp.float32
BF16 = jnp.bfloat16
I32 = jnp.int32

A_HEAD_DIM = 128
A_KV_HEADS = 4
IDX_HEADS = 16
IDX_DIM = 64
DSA_TOPK_MAX = 256
RET_KEY_DIM = 256
RET_VAL_DIM = 256
RET_CHUNK = 128
RET_THETA = 10000.0
MLA_V = 128
MLA_NOPE = 128
MLA_ROPE = 64
ROPE_THETA = 500000.0
Q_BLOCK = 128
LN_EPS = 1e-5
RMS_EPS = 1e-6
DEPTH = 2
ALPHA = (2.0 * DEPTH) ** 0.25

LANE = 128
V7X_VMEM_BYTES = 64 * 1024 * 1024
VMEM_LIMIT = V7X_VMEM_BYTES - 8 * 1024 * 1024
MASKED = -1e30
INT_MIN = -(2 ** 31)

NT_DIMS = (((1,), (1,)), ((), ()))
TN_DIMS = (((0,), (0,)), ((), ()))


def _tile(n, pref, mult=LANE):
    if n <= pref:
        return n
    t = (pref // mult) * mult
    while t > mult and n % t:
        t -= mult
    assert n % t == 0, (n, pref, mult)
    return t


def _params(*sem):
    return pltpu.CompilerParams(dimension_semantics=sem, vmem_limit_bytes=VMEM_LIMIT)


def _rope_cos_sin(T, rot_dim, theta):
    inv = theta ** (-jnp.arange(0, rot_dim, 2, dtype=F32) / rot_dim)
    ang = jnp.arange(T, dtype=F32)[:, None] * inv[None, :]
    return jnp.cos(ang), jnp.sin(ang)


def _lane_tables(cos, sin, head_dim, scale=1.0):
    T, half = cos.shape
    rest = head_dim - 2 * half
    zh = jnp.zeros((T, half), F32)
    c = jnp.concatenate([cos, cos, jnp.ones((T, rest), F32)], 1)
    sa = jnp.concatenate([-sin, zh, jnp.zeros((T, rest), F32)], 1)
    sb = jnp.concatenate([zh, sin, jnp.zeros((T, rest), F32)], 1)
    reps = LANE // head_dim
    return tuple(jnp.tile(t * scale, (1, reps)) for t in (c, sa, sb))


def _proj_kernel(*refs, slab_pat, mode, half, scale, with_tab, two_inputs):
    x_ref, w_ref = refs[:2]
    acc = jnp.dot(x_ref[...], w_ref[...], preferred_element_type=F32)
    refs = refs[2:]
    if two_inputs:
        acc = acc + jnp.dot(refs[0][...], refs[1][...], preferred_element_type=F32)
        refs = refs[2:]
    if with_tab:
        c_ref, sa_ref, sb_ref, o_ref = refs
    else:
        (o_ref,) = refs
    for s, p in enumerate(slab_pat):
        a = acc[:, s * LANE:(s + 1) * LANE]
        if p < 0:
            out = a if scale == 1.0 else a * scale
        else:
            c = c_ref[:, p * LANE:(p + 1) * LANE]
            sa = sa_ref[:, p * LANE:(p + 1) * LANE]
            if mode == "lane":
                sb = sb_ref[:, p * LANE:(p + 1) * LANE]
                out = a * c + pltpu.roll(a, LANE - half, 1) * sa + pltpu.roll(a, half, 1) * sb
            else:
                q = s ^ 1
                out = a * c + acc[:, q * LANE:(q + 1) * LANE] * sa
        o_ref[:, s * LANE:(s + 1) * LANE] = out.astype(o_ref.dtype)


def _proj(x, w, *, out_dtype, bm=1024, bn=1024, tabs=None, slab_pat=None, mode="lane", half=0,
          scale=1.0, seq_len=None, second=None, name="proj"):
    M, K = x.shape
    N = w.shape[1]
    bm = _tile(M, bm) if seq_len is None else _tile(seq_len, bm)
    bn = _tile(N, bn)
    if slab_pat is None:
        slab_pat = (-1,) * (bn // LANE)
    assert len(slab_pat) == bn // LANE
    in_specs = [pl.BlockSpec((bm, K), lambda i, j: (i, 0)),
                pl.BlockSpec((K, bn), lambda i, j: (0, j))]
    args = [x, w]
    if second is not None:
        x2, w2 = second
        in_specs += [pl.BlockSpec((bm, x2.shape[1]), lambda i, j: (i, 0)),
                     pl.BlockSpec((x2.shape[1], bn), lambda i, j: (0, j))]
        args += [x2, w2]
    if tabs is not None:
        tb = seq_len // bm
        tw = tabs[0].shape[1]
        in_specs += [pl.BlockSpec((bm, tw), lambda i, j: (i % tb, 0))] * 3
        args += list(tabs)
    kern = functools.partial(_proj_kernel, slab_pat=tuple(slab_pat), mode=mode, half=half,
                             scale=scale, with_tab=tabs is not None, two_inputs=second is not None)
    return pl.pallas_call(
        kern,
        grid=(M // bm, N // bn),
        in_specs=in_specs,
        out_specs=pl.BlockSpec((bm, bn), lambda i, j: (i, j)),
        out_shape=jax.ShapeDtypeStruct((M, N), out_dtype),
        compiler_params=_params("parallel", "parallel"),
        name=name,
    )(*args)


def _mm_ksplit_kernel(x_ref, w_ref, o_ref):
    part = jnp.dot(x_ref[...], w_ref[...], preferred_element_type=F32)

    @pl.when(pl.program_id(2) == 0)
    def _():
        o_ref[...] = part

    @pl.when(pl.program_id(2) > 0)
    def _():
        o_ref[...] += part


def _mm_ksplit(x, w, *, bm=1024, bn=1024, bk=2048, name="mm_ksplit"):
    M, K = x.shape
    N = w.shape[1]
    bm, bn, bk = _tile(M, bm), _tile(N, bn), _tile(K, bk)
    return pl.pallas_call(
        _mm_ksplit_kernel,
        grid=(M // bm, N // bn, K // bk),
        in_specs=[pl.BlockSpec((bm, bk), lambda i, j, k: (i, k)),
                  pl.BlockSpec((bk, bn), lambda i, j, k: (k, j))],
        out_specs=pl.BlockSpec((bm, bn), lambda i, j, k: (i, j)),
        out_shape=jax.ShapeDtypeStruct((M, N), F32),
        compiler_params=_params("parallel", "parallel", "arbitrary"),
        name=name,
    )(x, w)


def _swiglu_up_kernel(x_ref, w1_ref, w3_ref, o_ref):
    x = x_ref[...]
    a = jnp.dot(x, w1_ref[...], preferred_element_type=F32)
    b = jnp.dot(x, w3_ref[...], preferred_element_type=F32)
    o_ref[...] = (a * jax.nn.sigmoid(a) * b).astype(o_ref.dtype)


def _swiglu_up(x, w1, w3, *, bm=1024, bn=512):
    M, K = x.shape
    N = w1.shape[1]
    bm, bn = _tile(M, bm), _tile(N, bn)
    return pl.pallas_call(
        _swiglu_up_kernel,
        grid=(M // bm, N // bn),
        in_specs=[pl.BlockSpec((bm, K), lambda i, j: (i, 0)),
                  pl.BlockSpec((K, bn), lambda i, j: (0, j)),
                  pl.BlockSpec((K, bn), lambda i, j: (0, j))],
        out_specs=pl.BlockSpec((bm, bn), lambda i, j: (i, j)),
        out_shape=jax.ShapeDtypeStruct((M, N), BF16),
        compiler_params=_params("parallel", "parallel"),
        name="swiglu_up",
    )(x, w1, w3)


def _layer_norm_rows(z, g, b):
    mu = jnp.mean(z, axis=-1, keepdims=True)
    zc = z - mu
    var = jnp.mean(zc * zc, axis=-1, keepdims=True)
    return zc * lax.rsqrt(var + LN_EPS) * g + b


def _pack_bf16_pairs(x):
    half = x.shape[1] // 2
    hi = pltpu.bitcast(x[:, :half].astype(BF16).astype(F32), I32)
    lo = pltpu.bitcast(x[:, half:].astype(BF16).astype(F32), I32)
    return hi | lax.shift_right_logical(lo, 16)


def _unpack_bf16_pairs(w):
    hi = pltpu.bitcast(w & jnp.int32(-65536), F32).astype(BF16)
    lo = pltpu.bitcast(lax.shift_left(w, 16), F32).astype(BF16)
    return hi, lo


def _add_ln_kernel(x_ref, y_ref, g_ref, b_ref, *rest, packed, with_router):
    out = _layer_norm_rows(ALPHA * x_ref[...] + y_ref[...], g_ref[...], b_ref[...])
    if with_router:
        r_ref, of_ref, o2_ref, lg_ref = rest
        lg_ref[...] = jnp.dot(out, r_ref[...], preferred_element_type=F32, precision=lax.Precision.HIGHEST)
    else:
        of_ref, o2_ref = rest
    of_ref[...] = out
    o2_ref[...] = _pack_bf16_pairs(out) if packed else out.astype(o2_ref.dtype)


def _add_ln(x, y, g, b, *, packed=False, router=None, bm=256):
    M, D = x.shape
    bm = _tile(M, bm, 8)
    row = pl.BlockSpec((bm, D), lambda i: (i, 0))
    vec = pl.BlockSpec((1, D), lambda i: (0, 0))
    second = jax.ShapeDtypeStruct((M, D // 2), I32) if packed else jax.ShapeDtypeStruct((M, D), BF16)
    in_specs, args = [row, row, vec, vec], [x, y, g.reshape(1, D), b.reshape(1, D)]
    out_specs = [row, pl.BlockSpec((bm, second.shape[1]), lambda i: (i, 0))]
    out_shape = [jax.ShapeDtypeStruct((M, D), F32), second]
    if router is not None:
        in_specs.append(pl.BlockSpec((D, LANE), lambda i: (0, 0)))
        args.append(jnp.zeros((D, LANE), F32).at[:, :router.shape[1]].set(router))
        out_specs.append(pl.BlockSpec((bm, LANE), lambda i: (i, 0)))
        out_shape.append(jax.ShapeDtypeStruct((M, LANE), F32))
    return pl.pallas_call(
        functools.partial(_add_ln_kernel, packed=packed, with_router=router is not None),
        grid=(M // bm,),
        in_specs=in_specs,
        out_specs=out_specs,
        out_shape=out_shape,
        compiler_params=_params("parallel"),
        name="add_ln",
    )(*args)


LOG2E = 1.4426950408889634


def _online_softmax(s, m, l):
    m_new = jnp.maximum(m, jnp.max(s, axis=1, keepdims=True))
    a = jnp.exp2(m - m_new)
    p = jnp.exp2((s - m_new).astype(BF16))
    return m_new, a * l + jnp.sum(p.astype(F32), axis=1, keepdims=True), a, p


def _dsa_kernel(q_ref, k_ref, v_ref, iq_ref, ik_ref, o_ref, keys_ref, bias_ref, *, topk, ck, n_rep, idx_bits):
    blk = pl.program_id(1)
    n_chunks = (blk * Q_BLOCK + Q_BLOCK + ck - 1) // ck
    row = lax.broadcasted_iota(I32, (Q_BLOCK, ck), 0) + blk * Q_BLOCK
    lane = lax.broadcasted_iota(I32, (Q_BLOCK, ck), 1)
    w_off = IDX_HEADS * IDX_DIM + IDX_DIM
    iq = iq_ref[0]
    wi = iq[:, w_off:w_off + IDX_HEADS]
    q_idx = jnp.concatenate([iq[:, h * IDX_DIM:(h + 1) * IDX_DIM].astype(BF16) for h in range(IDX_HEADS)], axis=0)

    def score_body(c, carry):
        off = pl.multiple_of(c * ck, ck)
        kc = ik_ref[0, pl.ds(off, ck), :][:, :IDX_DIM].astype(BF16)
        lg = lax.dot_general(q_idx, kc, NT_DIMS, preferred_element_type=F32)
        s = jnp.zeros((Q_BLOCK, ck), F32)
        for h in range(IDX_HEADS):
            s = s + wi[:, h:h + 1] * jnp.maximum(lg[h * Q_BLOCK:(h + 1) * Q_BLOCK], 0.0)
        bits = pltpu.bitcast(s, I32)
        key = bits ^ ((bits >> 31) & 0x7FFFFFFF)
        keys_ref[c] = jnp.where(lane + off <= row, key, INT_MIN)
        return carry

    lax.fori_loop(0, n_chunks, score_body, 0)

    def count(indicator):
        def body(c, acc):
            part = indicator(keys_ref[c], lane + c * ck)
            for j in range(ck // LANE):
                acc = acc + part[:, j * LANE:(j + 1) * LANE]
            return acc
        acc = lax.fori_loop(0, n_chunks, body, jnp.zeros((Q_BLOCK, LANE), F32))
        return jnp.sum(acc, axis=1, keepdims=True)

    kf = float(topk)
    ok = count(lambda kc, idx: jnp.where(kc >= 0, 1.0, 0.0)) >= kf
    thr = jnp.where(ok, 0, INT_MIN).astype(I32)

    def bit_body(i, thr):
        cand = thr + lax.shift_left(jnp.int32(1), 30 - i)
        ok = count(lambda kc, idx: jnp.where(kc >= cand, 1.0, 0.0)) >= kf
        return jnp.where(ok, cand, thr)

    thr = lax.fori_loop(0, 31, bit_body, thr)

    n_gt = count(lambda kc, idx: jnp.where(kc > thr, 1.0, 0.0))
    n_ge = count(lambda kc, idx: jnp.where(kc >= thr, 1.0, 0.0))
    need = kf - n_gt
    has_thr = thr > INT_MIN
    surplus = jnp.where(has_thr, n_ge - n_gt - need, 0.0)

    def tie_search():
        def tie_body(i, last):
            cand = last + lax.shift_left(jnp.int32(1), idx_bits - 1 - i)
            ok = count(lambda kc, idx: jnp.where(kc == thr, jnp.where(idx < cand, 1.0, 0.0), 0.0)) < need
            return jnp.where(ok, cand, last)
        return lax.fori_loop(0, idx_bits, tie_body, jnp.zeros((Q_BLOCK, 1), I32))

    last = lax.cond(jnp.max(surplus) > 0.0, tie_search, lambda: jnp.full((Q_BLOCK, 1), 2 ** idx_bits, I32))
    last = jnp.where(has_thr, last, -1)

    def bias_body(c, carry):
        kc = keys_ref[c]
        tie_bias = jnp.where(lane + c * ck <= last, 0.0, MASKED)
        bias_ref[c] = jnp.where(kc == thr, tie_bias, jnp.where(kc > thr, 0.0, MASKED))
        return carry

    lax.fori_loop(0, n_chunks, bias_body, 0)

    q = q_ref[0]
    rows = n_rep * Q_BLOCK
    qgs = [jnp.concatenate([q[:, (g * n_rep + r) * A_HEAD_DIM:(g * n_rep + r + 1) * A_HEAD_DIM]
                            for r in range(n_rep)], axis=0) for g in range(A_KV_HEADS)]

    def att_body(c, carry, n_sub):
        width = n_sub * ck
        off = pl.multiple_of(c * width, width)
        b = jnp.concatenate([bias_ref[c * n_sub + t] for t in range(n_sub)], axis=1)
        bias = jnp.concatenate([b] * n_rep, axis=0)
        out = []
        for g in range(A_KV_HEADS):
            m, l, acc = carry[g]
            kc = k_ref[0, pl.ds(off, width), g * A_HEAD_DIM:(g + 1) * A_HEAD_DIM]
            vc = v_ref[0, pl.ds(off, width), g * A_HEAD_DIM:(g + 1) * A_HEAD_DIM]
            s = lax.dot_general(qgs[g], kc, NT_DIMS, preferred_element_type=F32) + bias
            m, l, a, p = _online_softmax(s, m, l)
            out.append((m, l, a * acc + jnp.dot(p, vc, preferred_element_type=F32)))
        return tuple(out)

    one = (jnp.full((rows, 1), MASKED, F32), jnp.zeros((rows, 1), F32), jnp.zeros((rows, A_HEAD_DIM), F32))
    carry = (one,) * A_KV_HEADS
    if k_ref.shape[1] >= 2 * ck:
        carry = lax.fori_loop(0, n_chunks // 2, functools.partial(att_body, n_sub=2), carry)
        carry = lax.cond(n_chunks % 2 == 1, lambda c: att_body(n_chunks - 1, c, 1), lambda c: c, carry)
    else:
        carry = att_body(0, carry, 1)
    for g in range(A_KV_HEADS):
        _, l, acc = carry[g]
        o = acc / l
        for r in range(n_rep):
            col = (g * n_rep + r) * A_HEAD_DIM
            o_ref[0, :, col:col + A_HEAD_DIM] = o[r * Q_BLOCK:(r + 1) * Q_BLOCK].astype(o_ref.dtype)


def _dsa(q, k, v, idx, *, B, T):
    a_heads = q.shape[-1] // A_HEAD_DIM
    n_rep = a_heads // A_KV_HEADS
    topk = min(DSA_TOPK_MAX, T // 4)
    ck = _tile(T, 512)
    idx_w = idx.shape[-1]
    kv_w = k.shape[-1]
    kern = functools.partial(_dsa_kernel, topk=topk, ck=ck, n_rep=n_rep, idx_bits=max(1, (T - 1).bit_length()))
    return pl.pallas_call(
        kern,
        grid=(B, T // Q_BLOCK),
        in_specs=[pl.BlockSpec((1, Q_BLOCK, q.shape[-1]), lambda b, i: (b, i, 0)),
                  pl.BlockSpec((1, T, kv_w), lambda b, i: (b, 0, 0)),
                  pl.BlockSpec((1, T, kv_w), lambda b, i: (b, 0, 0)),
                  pl.BlockSpec((1, Q_BLOCK, idx_w), lambda b, i: (b, i, 0)),
                  pl.BlockSpec((1, T, LANE), lambda b, i: (b, 0, IDX_HEADS * IDX_DIM // LANE))],
        out_specs=pl.BlockSpec((1, Q_BLOCK, q.shape[-1]), lambda b, i: (b, i, 0)),
        out_shape=jax.ShapeDtypeStruct(q.shape, BF16),
        scratch_shapes=[pltpu.VMEM((T // ck, Q_BLOCK, ck), I32), pltpu.VMEM((T // ck, Q_BLOCK, ck), F32)],
        compiler_params=_params("parallel", "parallel"),
        name="dsa",
    )(q, k, v, idx, idx)


RET_HEADS_PER_STEP = 2


def _retention_kernel(q_ref, k_ref, v_ref, g_ref, gn_ref, din_ref, qd_ref, kd_ref, cd_ref, o_ref, state_ref, *, n_sub):
    @pl.when(pl.program_id(2) == 0)
    def _():
        state_ref[...] = jnp.zeros_like(state_ref)

    W = RET_VAL_DIM
    for s in range(n_sub):
        sl = pl.ds(s * RET_CHUNK, RET_CHUNK)
        for j in range(RET_HEADS_PER_STEP):
            cols = slice(j * W, (j + 1) * W)
            qc = q_ref[sl, cols]
            kc = k_ref[sl, cols]
            vc = v_ref[sl, cols]
            st = state_ref[j]
            inner = lax.dot_general(qc, kc, NT_DIMS, preferred_element_type=F32) * din_ref[j]
            o = (jnp.dot(inner.astype(BF16), vc, preferred_element_type=F32)
                 + jnp.dot(qc, st.astype(BF16), preferred_element_type=F32) * qd_ref[j])
            vk = (vc.astype(F32) * kd_ref[j]).astype(BF16)
            state_ref[j] = st * cd_ref[j] + lax.dot_general(kc, vk, TN_DIMS, preferred_element_type=F32)
            mu = jnp.mean(o, axis=-1, keepdims=True)
            oc = o - mu
            var = jnp.mean(oc * oc, axis=-1, keepdims=True)
            gate = g_ref[sl, cols].astype(F32)
            normed = oc * lax.rsqrt(var + LN_EPS) * gn_ref[:, cols]
            o_ref[sl, cols] = (gate * jax.nn.sigmoid(gate) * normed).astype(o_ref.dtype)


def _retention(qk, pv, gn_g, *, B, T, heads, v_blk0):
    N = qk.shape[0]
    C = RET_CHUNK
    rb = _tile(T, 512)
    n_sub = rb // C
    nr = T // rb
    log_gamma = jnp.log(1.0 - 2.0 ** (-5.0 - jnp.arange(heads, dtype=F32)))
    pos = jnp.arange(C, dtype=F32)
    diff = pos[:, None] - pos[None, :]
    din = jnp.exp(jnp.where(diff[None] >= 0, log_gamma[:, None, None] * diff[None], -jnp.inf))
    qd = jnp.exp(log_gamma[:, None] * (pos[None] + 1.0))[:, :, None]
    kd = jnp.exp(log_gamma[:, None] * (C - 1.0 - pos[None]))[:, :, None]
    cd = jnp.exp(log_gamma * C)[:, None, None]
    hp = RET_HEADS_PER_STEP
    W = hp * RET_VAL_DIM
    assert heads % hp == 0 and v_blk0 % hp == 0
    blk = lambda off: pl.BlockSpec((rb, W), lambda b, h, r: (b * nr + r, off // hp + h))
    per_head = lambda shape: pl.BlockSpec((hp,) + shape, lambda b, h, r: (h, 0, 0))
    return pl.pallas_call(
        functools.partial(_retention_kernel, n_sub=n_sub),
        grid=(B, heads // hp, nr),
        in_specs=[blk(0), blk(heads), blk(v_blk0), blk(v_blk0 + heads),
                  pl.BlockSpec((1, W), lambda b, h, r: (0, h)),
                  per_head((C, C)), per_head((C, 1)), per_head((C, 1)), per_head((1, 1))],
        out_specs=pl.BlockSpec((rb, W), lambda b, h, r: (b * nr + r, h)),
        out_shape=jax.ShapeDtypeStruct((N, heads * RET_VAL_DIM), BF16),
        scratch_shapes=[pltpu.VMEM((hp, RET_KEY_DIM, RET_VAL_DIM), F32)],
        compiler_params=_params("parallel", "parallel", "arbitrary"),
        name="retention",
    )(qk, qk, pv, pv, gn_g.reshape(1, heads * RET_VAL_DIM), din, qd, kd, cd)


def _mla_down_kernel(x_ref, w_ref, qg_ref, kvg_ref, c_ref, sa_ref, sb_ref, cq_ref, ckv_ref, kr_ref, *, q_rank, kv_rank):
    acc = jnp.dot(x_ref[...], w_ref[...], preferred_element_type=F32)

    def rms(a, g):
        return a * lax.rsqrt(jnp.mean(a * a, axis=-1, keepdims=True) + RMS_EPS) * g

    cq_ref[...] = rms(acc[:, :q_rank], qg_ref[...]).astype(cq_ref.dtype)
    ckv_ref[...] = rms(acc[:, q_rank:q_rank + kv_rank], kvg_ref[...]).astype(ckv_ref.dtype)
    kr = acc[:, q_rank + kv_rank:]
    half = MLA_ROPE // 2
    kr = kr * c_ref[...] + pltpu.roll(kr, LANE - half, 1) * sa_ref[...] + pltpu.roll(kr, half, 1) * sb_ref[...]
    kr_ref[...] = kr.astype(kr_ref.dtype)


def _mla_down(x, w, q_g, kv_g, tabs, *, T, q_rank, kv_rank, bm=512):
    M, K = x.shape
    Nw = w.shape[1]
    bm = _tile(T, bm)
    tb = T // bm
    row = lambda n: pl.BlockSpec((bm, n), lambda i: (i, 0))
    tab = pl.BlockSpec((bm, LANE), lambda i: (i % tb, 0))
    return pl.pallas_call(
        functools.partial(_mla_down_kernel, q_rank=q_rank, kv_rank=kv_rank),
        grid=(M // bm,),
        in_specs=[row(K), pl.BlockSpec((K, Nw), lambda i: (0, 0)),
                  pl.BlockSpec((1, q_rank), lambda i: (0, 0)), pl.BlockSpec((1, kv_rank), lambda i: (0, 0)),
                  tab, tab, tab],
        out_specs=[row(q_rank), row(kv_rank), row(LANE)],
        out_shape=[jax.ShapeDtypeStruct((M, q_rank), BF16), jax.ShapeDtypeStruct((M, kv_rank), BF16),
                   jax.ShapeDtypeStruct((M, LANE), BF16)],
        compiler_params=_params("parallel"),
        name="mla_down",
    )(x, w, q_g.reshape(1, q_rank), kv_g.reshape(1, kv_rank), *tabs)


MLA_HEADS_PER_STEP = 4


def _mla_attn_kernel(qn_ref, qr_ref, kn_ref, kr_ref, v_ref, o_ref, *, tq):
    i = pl.program_id(2)
    lane = lax.broadcasted_iota(I32, (tq, LANE), 1)
    qs = []
    for j in range(MLA_HEADS_PER_STEP):
        lo = (j % 2) * MLA_ROPE
        own = jnp.where((lane >= lo) & (lane < lo + MLA_ROPE), 1.0, 0.0)
        pair = qr_ref[:, (j // 2) * LANE:(j // 2 + 1) * LANE].astype(F32)
        qr = (pair * own).astype(BF16)
        qs.append(jnp.concatenate([qn_ref[:, j * LANE:(j + 1) * LANE], qr], axis=1))

    def step(c, carry, masked, width):
        off = pl.multiple_of(c * width, width)
        kr = kr_ref[pl.ds(off, width), :]
        out = []
        for j in range(MLA_HEADS_PER_STEP):
            m, l, acc = carry[j]
            k = jnp.concatenate([kn_ref[pl.ds(off, width), j * LANE:(j + 1) * LANE], kr], axis=1)
            s = lax.dot_general(qs[j], k, NT_DIMS, preferred_element_type=F32)
            if masked:
                r_io = lax.broadcasted_iota(I32, (tq, width), 0)
                c_io = lax.broadcasted_iota(I32, (tq, width), 1)
                s = jnp.where(c_io <= r_io, s, MASKED)
            m, l, a, p = _online_softmax(s, m, l)
            v = v_ref[pl.ds(off, width), j * MLA_V:(j + 1) * MLA_V]
            out.append((m, l, a * acc + jnp.dot(p, v, preferred_element_type=F32)))
        return tuple(out)

    one = (jnp.full((tq, 1), MASKED, F32), jnp.zeros((tq, 1), F32), jnp.zeros((tq, MLA_V), F32))
    carry = (one,) * MLA_HEADS_PER_STEP
    if kn_ref.shape[0] >= 2 * tq:
        carry = lax.fori_loop(0, i // 2, functools.partial(step, masked=False, width=2 * tq), carry)
        carry = lax.cond(i % 2 == 1, lambda c: step(i - 1, c, False, tq), lambda c: c, carry)
    carry = step(i, carry, True, tq)
    for j in range(MLA_HEADS_PER_STEP):
        _, l, acc = carry[j]
        o_ref[:, j * MLA_V:(j + 1) * MLA_V] = (acc / l).astype(o_ref.dtype)


def _mla_attn(qn, qr, kv, kr, *, B, T, heads):
    N = qn.shape[0]
    tq = _tile(T, 512)
    nq = T // tq
    hp = MLA_HEADS_PER_STEP
    w = hp * LANE
    return pl.pallas_call(
        functools.partial(_mla_attn_kernel, tq=tq),
        grid=(B, heads // hp, nq),
        in_specs=[pl.BlockSpec((tq, w), lambda b, h, i: (b * nq + i, h)),
                  pl.BlockSpec((tq, hp * MLA_ROPE), lambda b, h, i: (b * nq + i, h)),
                  pl.BlockSpec((T, w), lambda b, h, i: (b, h)),
                  pl.BlockSpec((T, LANE), lambda b, h, i: (b, 0)),
                  pl.BlockSpec((T, w), lambda b, h, i: (b, heads // hp + h))],
        out_specs=pl.BlockSpec((tq, w), lambda b, h, i: (b * nq + i, h)),
        out_shape=jax.ShapeDtypeStruct((N, heads * MLA_V), BF16),
        compiler_params=_params("parallel", "parallel", "parallel"),
        name="mla_attn",
    )(qn, qr, kv, kr, kv)


def _router_kernel(lg_ref, meta_ref, cnt_ref, carry_ref, *, n_exp):
    @pl.when(pl.program_id(0) == 0)
    def _():
        carry_ref[...] = jnp.zeros_like(carry_ref)

    bm = lg_ref.shape[0]
    lane = lax.broadcasted_iota(I32, (bm, LANE), 1).astype(F32)
    logits = jnp.where(lane < n_exp, lg_ref[...], -jnp.inf)
    m1 = jnp.max(logits, axis=1, keepdims=True)
    i1 = jnp.min(jnp.where(logits == m1, lane, float(LANE)), axis=1, keepdims=True)
    rest = jnp.where(lane == i1, -jnp.inf, logits)
    m2 = jnp.max(rest, axis=1, keepdims=True)
    i2 = jnp.min(jnp.where(rest == m2, lane, float(LANE)), axis=1, keepdims=True)
    e = jnp.exp(m2 - m1)
    g1 = 1.0 / (1.0 + e)
    g2 = e / (1.0 + e)
    sel = jnp.where(lane == i1, 1.0, jnp.where(lane == i2, 1.0, 0.0))
    r_io = lax.broadcasted_iota(I32, (bm, bm), 0)
    c_io = lax.broadcasted_iota(I32, (bm, bm), 1)
    below = jnp.where(c_io < r_io, 1.0, 0.0).astype(BF16)
    carry = carry_ref[0:1, :]
    rank = jnp.dot(below, sel.astype(BF16), preferred_element_type=F32) + carry
    r1 = jnp.sum(jnp.where(lane == i1, rank, 0.0), axis=1, keepdims=True)
    r2 = jnp.sum(jnp.where(lane == i2, rank, 0.0), axis=1, keepdims=True)
    meta = jnp.where(lane == 0, i1, 0.0)
    meta = jnp.where(lane == 1, i2, meta)
    meta = jnp.where(lane == 2, g1, meta)
    meta = jnp.where(lane == 3, g2, meta)
    meta = jnp.where(lane == 4, r1, meta)
    meta = jnp.where(lane == 5, r2, meta)
    meta_ref[...] = meta
    total = carry + jnp.sum(sel, axis=0, keepdims=True)
    carry_ref[...] = jnp.broadcast_to(total, carry_ref.shape)
    cnt_ref[...] = jnp.broadcast_to(total, cnt_ref.shape)


def _router(logits, n_exp, *, bm=512):
    M = logits.shape[0]
    bm = _tile(M, bm)
    return pl.pallas_call(
        functools.partial(_router_kernel, n_exp=n_exp),
        grid=(M // bm,),
        in_specs=[pl.BlockSpec((bm, LANE), lambda i: (i, 0))],
        out_specs=[pl.BlockSpec((bm, LANE), lambda i: (i, 0)), pl.BlockSpec((8, LANE), lambda i: (0, 0))],
        out_shape=[jax.ShapeDtypeStruct((M, LANE), F32), jax.ShapeDtypeStruct((8, LANE), F32)],
        scratch_shapes=[pltpu.VMEM((8, LANE), F32)],
        compiler_params=_params("arbitrary"),
        name="router",
    )(logits)


DMA_LOOP_UNROLL = 8


def _moe_gather_kernel(tok_ref, nv_ref, x_hbm, o_ref, stage_ref, sem, *, tm):
    r = pl.program_id(0)
    nv = nv_ref[0]

    half = stage_ref.shape[1]

    def row_copy(j, tok):
        return pltpu.make_async_copy(x_hbm.at[pl.ds(tok, 1)], stage_ref.at[pl.ds(j, 1)], sem)

    def gather_start(tile):
        def body(j, carry):
            row_copy(j, tok_ref[tile * tm + j]).start()
            return carry
        lax.fori_loop(0, tm, body, 0, unroll=DMA_LOOP_UNROLL)

    def gather_wait():
        def body(j, carry):
            row_copy(j, 0).wait()
            return carry
        lax.fori_loop(0, tm, body, 0, unroll=DMA_LOOP_UNROLL)

    @pl.when(r < nv)
    def _():
        @pl.when(r == 0)
        def _():
            gather_start(0)

        gather_wait()
        hi, lo = _unpack_bf16_pairs(stage_ref[...])
        o_ref[:, :half] = hi
        o_ref[:, half:] = lo

        @pl.when(r + 1 < nv)
        def _():
            gather_start(r + 1)


def _for_tile_rows(half_ref, r, tm, compute):
    @pl.when(half_ref[r] == 0)
    def _():
        compute(tm)

    @pl.when(half_ref[r] != 0)
    def _():
        compute(tm // 2)


def _moe_up_kernel(te_ref, nv_ref, half_ref, xs_ref, w1_ref, w3_ref, o_ref, wb1_ref, wb3_ref):
    r = pl.program_id(1)
    tm = xs_ref.shape[0]

    @pl.when(r < nv_ref[0])
    def _():
        @pl.when((r == 0) | (te_ref[r] != te_ref[jnp.maximum(r - 1, 0)]))
        def _():
            wb1_ref[...] = w1_ref[0].astype(BF16)
            wb3_ref[...] = w3_ref[0].astype(BF16)

        def compute(rows):
            xb = xs_ref[:rows, :]
            a = jnp.dot(xb, wb1_ref[...], preferred_element_type=F32)
            b = jnp.dot(xb, wb3_ref[...], preferred_element_type=F32)
            o_ref[:rows, :] = (a * jax.nn.sigmoid(a) * b).astype(o_ref.dtype)
            if rows < tm:
                o_ref[rows:, :] = jnp.zeros((tm - rows, o_ref.shape[1]), o_ref.dtype)

        _for_tile_rows(half_ref, r, tm, compute)


def _moe_down_kernel(te_ref, nv_ref, half_ref, h_ref, w2_ref, o_ref, wb_ref):
    r = pl.program_id(1)
    tm = h_ref.shape[0]

    @pl.when(r < nv_ref[0])
    def _():
        @pl.when((r == 0) | (te_ref[r] != te_ref[jnp.maximum(r - 1, 0)]))
        def _():
            wb_ref[...] = w2_ref[0].astype(BF16)

        def compute(rows):
            o_ref[:rows, :] = jnp.dot(h_ref[:rows, :], wb_ref[...], preferred_element_type=F32)
            if rows < tm:
                o_ref[rows:, :] = jnp.zeros((tm - rows, o_ref.shape[1]), o_ref.dtype)

        _for_tile_rows(half_ref, r, tm, compute)


def _experts(x, w1, w3, w2, tile_expert, n_valid, tile_half, row_tok, *, tm, tf=512, tn=512):
    D = 2 * x.shape[1]
    P = row_tok.shape[0]
    E, _, F = w1.shape
    tf, tn = _tile(F, tf), _tile(D, tn)
    n_tiles = P // tm
    row = lambda r, nv: jnp.minimum(r, nv[0] - 1)

    xs = pl.pallas_call(
        functools.partial(_moe_gather_kernel, tm=tm),
        grid_spec=pltpu.PrefetchScalarGridSpec(
            num_scalar_prefetch=2, grid=(n_tiles,),
            in_specs=[pl.BlockSpec(memory_space=pl.ANY)],
            out_specs=pl.BlockSpec((tm, D), lambda r, tok, nv: (row(r, nv), 0)),
            scratch_shapes=[pltpu.VMEM((tm, D // 2), I32), pltpu.SemaphoreType.DMA(())]),
        out_shape=jax.ShapeDtypeStruct((P, D), BF16),
        compiler_params=_params("arbitrary"),
        name="moe_gather",
    )(row_tok, n_valid, x)

    hid = pl.pallas_call(
        _moe_up_kernel,
        grid_spec=pltpu.PrefetchScalarGridSpec(
            num_scalar_prefetch=3, grid=(F // tf, n_tiles),
            in_specs=[pl.BlockSpec((tm, D), lambda f, r, te, nv, th: (row(r, nv), 0)),
                      pl.BlockSpec((1, D, tf), lambda f, r, te, nv, th: (te[r], 0, f)),
                      pl.BlockSpec((1, D, tf), lambda f, r, te, nv, th: (te[r], 0, f))],
            out_specs=pl.BlockSpec((tm, tf), lambda f, r, te, nv, th: (row(r, nv), f)),
            scratch_shapes=[pltpu.VMEM((D, tf), BF16), pltpu.VMEM((D, tf), BF16)]),
        out_shape=jax.ShapeDtypeStruct((P, F), BF16),
        compiler_params=_params("arbitrary", "arbitrary"),
        name="moe_up",
    )(tile_expert, n_valid, tile_half, xs, w1, w3)

    return pl.pallas_call(
        _moe_down_kernel,
        grid_spec=pltpu.PrefetchScalarGridSpec(
            num_scalar_prefetch=3, grid=(D // tn, n_tiles),
            in_specs=[pl.BlockSpec((tm, F), lambda n, r, te, nv, th: (row(r, nv), 0)),
                      pl.BlockSpec((1, F, tn), lambda n, r, te, nv, th: (te[r], 0, n))],
            out_specs=pl.BlockSpec((tm, tn), lambda n, r, te, nv, th: (row(r, nv), n)),
            scratch_shapes=[pltpu.VMEM((F, tn), BF16)]),
        out_shape=jax.ShapeDtypeStruct((P, D), F32),
        compiler_params=_params("arbitrary", "arbitrary"),
        name="moe_down",
    )(tile_expert, n_valid, tile_half, hid, w2)


def _combine_kernel(dest_ref, x_ref, meta_ref, g_ref, b_ref, ys_hbm, o_ref, buf_ref, sem):
    bm = x_ref.shape[0]
    i = pl.program_id(0)
    n = pl.num_programs(0)

    def copy(slot, j, k, d):
        return pltpu.make_async_copy(ys_hbm.at[pl.ds(d, 1)], buf_ref.at[slot, k, pl.ds(j, 1)], sem.at[slot, k])

    def gather_start(blk):
        slot = blk % 2

        def body(j, carry):
            for k in range(2):
                copy(slot, j, k, dest_ref[2 * (blk * bm + j) + k]).start()
            return carry
        lax.fori_loop(0, bm, body, 0, unroll=DMA_LOOP_UNROLL)

    @pl.when(i == 0)
    def _():
        gather_start(0)

    @pl.when(i + 1 < n)
    def _():
        gather_start(i + 1)

    slot = i % 2

    def wait(j, carry):
        for k in range(2):
            copy(slot, j, k, 0).wait()
        return carry

    lax.fori_loop(0, bm, wait, 0, unroll=DMA_LOOP_UNROLL)
    meta = meta_ref[...]
    y = meta[:, 2:3] * buf_ref[slot, 0] + meta[:, 3:4] * buf_ref[slot, 1]
    o_ref[...] = _layer_norm_rows(ALPHA * x_ref[...] + y, g_ref[...], b_ref[...])


def _combine(x, meta, ys, dest, g, b, *, bm=256):
    M, D = x.shape
    bm = _tile(M, bm, 8)
    row = lambda n: pl.BlockSpec((bm, n), lambda i, d: (i, 0))
    vec = pl.BlockSpec((1, D), lambda i, d: (0, 0))
    return pl.pallas_call(
        _combine_kernel,
        grid_spec=pltpu.PrefetchScalarGridSpec(
            num_scalar_prefetch=1, grid=(M // bm,),
            in_specs=[row(D), row(LANE), vec, vec, pl.BlockSpec(memory_space=pl.ANY)],
            out_specs=row(D),
            scratch_shapes=[pltpu.VMEM((2, 2, bm, D), F32), pltpu.SemaphoreType.DMA((2, 2))]),
        out_shape=jax.ShapeDtypeStruct((M, D), F32),
        compiler_params=_params("arbitrary"),
        name="moe_combine",
    )(dest, x, meta, g.reshape(1, D), b.reshape(1, D), ys)


def _even_layer(x, xb, w_in, ret_gn_g, w_out, ln1_g, ln1_b, w1, w3, w2, ln2_g, ln2_b, *, B, T):
    N, D = x.shape
    a_heads = D // 2 // A_HEAD_DIM
    r_heads = D // 2 // RET_VAL_DIM
    qa_w, kv_w = a_heads * A_HEAD_DIM, A_KV_HEADS * A_HEAD_DIM
    qi_w = IDX_HEADS * IDX_DIM
    rk_w, rv_w = r_heads * RET_KEY_DIM, r_heads * RET_VAL_DIM
    sizes = (qa_w, kv_w, kv_w, qi_w, IDX_DIM, IDX_HEADS, rk_w, rk_w, rv_w, rv_w)
    offs = [0]
    for s in sizes:
        offs.append(offs[-1] + s)
    col = lambda a, b_: w_in[:, offs[a]:offs[b_]]
    w_qa, w_ka, w_va = col(0, 1), col(1, 2), col(2, 3)
    w_qi, w_ki, w_wi = col(3, 4), col(4, 5), col(5, 6)
    w_qb, w_kb, w_vb, w_gb = col(6, 7), col(7, 8), col(8, 9), col(9, 10)

    cos_a, sin_a = _rope_cos_sin(T, A_HEAD_DIM // 4, ROPE_THETA)
    tab_q = _lane_tables(cos_a, sin_a, A_HEAD_DIM, A_HEAD_DIM ** -0.5 * LOG2E)
    tab_k = _lane_tables(cos_a, sin_a, A_HEAD_DIM)
    cos_i, sin_i = _rope_cos_sin(T, IDX_DIM // 4, ROPE_THETA)
    tab_i = _lane_tables(cos_i, sin_i, IDX_DIM)
    pass_c = jnp.ones((T, LANE - IDX_DIM), F32)
    pass_s = jnp.zeros((T, LANE - IDX_DIM), F32)
    tab_idx = tuple(jnp.concatenate([t, t[:, :IDX_DIM], p], 1)
                    for t, p in zip(tab_i, (pass_c, pass_s, pass_s)))
    inv = 1.0 / (RET_THETA ** jnp.linspace(0.0, 1.0, RET_KEY_DIM // 2, dtype=F32))
    ang = jnp.arange(T, dtype=F32)[:, None] * inv[None, :]
    cos_r, sin_r = jnp.cos(ang), jnp.sin(ang)
    sin_pair = jnp.concatenate([-sin_r, sin_r], 1)
    tab_r = (jnp.concatenate([cos_r, cos_r], 1), sin_pair, sin_pair)

    idx_pad = LANE - IDX_DIM - IDX_HEADS
    w_idx = jnp.concatenate([w_qi, w_ki, w_wi * (IDX_DIM ** -0.5 * IDX_HEADS ** -0.5),
                             jnp.zeros((D, idx_pad), F32)], 1).astype(BF16)
    w_rqk = jnp.concatenate([w_qb, w_kb * RET_KEY_DIM ** -0.5], 1).astype(BF16)
    w_pv = jnp.concatenate([w_va, w_vb, w_gb], 1).astype(BF16)
    qa = _proj(xb, w_qa.astype(BF16), out_dtype=BF16, tabs=tab_q, half=A_HEAD_DIM // 8,
               seq_len=T, name="proj_qa", **_pat(qa_w, 1024, 0))
    ka = _proj(xb, w_ka.astype(BF16), out_dtype=BF16, tabs=tab_k, half=A_HEAD_DIM // 8,
               seq_len=T, name="proj_ka", **_pat(kv_w, 1024, 0))
    n_idx = w_idx.shape[1]
    idx = _proj(xb, w_idx, out_dtype=F32, tabs=tab_idx, half=IDX_DIM // 8, seq_len=T, name="proj_idx",
                bm=512, bn=n_idx, slab_pat=(0,) * (qi_w // LANE) + (1,))
    rqk = _proj(xb, w_rqk, out_dtype=BF16, tabs=tab_r, mode="pair", seq_len=T, name="proj_ret_qk",
                bn=1024, slab_pat=(0, 1) * (_tile(2 * rk_w, 1024) // (2 * LANE)))
    pv = _proj(xb, w_pv, out_dtype=BF16, bn=768, seq_len=T, name="proj_v")

    ya = _dsa(qa.reshape(B, T, qa_w), ka.reshape(B, T, kv_w), pv.reshape(B, T, -1), idx.reshape(B, T, n_idx),
              B=B, T=T)
    yb = _retention(rqk, pv, ret_gn_g, B=B, T=T, heads=r_heads, v_blk0=kv_w // RET_VAL_DIM)
    w_out_b = w_out.astype(BF16)
    y = _proj(ya.reshape(N, qa_w), w_out_b[:qa_w], second=(yb, w_out_b[qa_w:]), out_dtype=F32, name="proj_out0")
    x1, x1b = _add_ln(x, y, ln1_g, ln1_b)
    hid = _swiglu_up(x1b, w1.astype(BF16), w3.astype(BF16))
    y = _mm_ksplit(hid, w2.astype(BF16), bk=3584, name="ffn_down")
    return _add_ln(x1, y, ln2_g, ln2_b)


def _pat(width, bn, p):
    bn = _tile(width, bn)
    return dict(bn=bn, slab_pat=(p,) * (bn // LANE))


def _odd_layer(x, xb, w_dq_dkv, q_norm_g, w_uq, kv_norm_g, w_ukv, w_out, ln1_g, ln1_b,
               router, we1, we3, we2, ln2_g, ln2_b, *, B, T):
    N, D = x.shape
    heads = D // MLA_V
    q_rank, kv_rank = q_norm_g.shape[0], kv_norm_g.shape[0]
    scale = (MLA_NOPE + MLA_ROPE) ** -0.5 * LOG2E
    cos_c, sin_c = _rope_cos_sin(T, MLA_ROPE, ROPE_THETA)
    tab_kr = _lane_tables(cos_c, sin_c, MLA_ROPE)
    tab_qr = _lane_tables(cos_c, sin_c, MLA_ROPE, scale)

    w_kr = w_dq_dkv[:, q_rank + kv_rank:]
    w_down = jnp.concatenate([w_dq_dkv[:, :q_rank + kv_rank], w_kr, w_kr], 1).astype(BF16)
    cq, ckv, kr = _mla_down(xb, w_down, q_norm_g, kv_norm_g, tab_kr, T=T, q_rank=q_rank, kv_rank=kv_rank)
    w_uq3 = w_uq.reshape(q_rank, heads, MLA_NOPE + MLA_ROPE)
    w_qn = w_uq3[:, :, :MLA_NOPE].reshape(q_rank, heads * MLA_NOPE).astype(BF16)
    w_qr = w_uq3[:, :, MLA_NOPE:].reshape(q_rank, heads * MLA_ROPE).astype(BF16)
    w_kv3 = w_ukv.reshape(kv_rank, heads, MLA_NOPE + MLA_V)
    w_kv = jnp.concatenate([w_kv3[:, :, :MLA_NOPE].reshape(kv_rank, heads * MLA_NOPE),
                            w_kv3[:, :, MLA_NOPE:].reshape(kv_rank, heads * MLA_V)], 1).astype(BF16)
    qn = _proj(cq, w_qn, out_dtype=BF16, scale=scale, seq_len=T, bn=2048, name="proj_q_nope")
    qr = _proj(cq, w_qr, out_dtype=BF16, tabs=tab_qr, half=MLA_ROPE // 2, seq_len=T, name="proj_q_rope",
               **_pat(heads * MLA_ROPE, 1024, 0))
    kv = _proj(ckv, w_kv, out_dtype=BF16, seq_len=T, bn=2048, name="proj_kv")
    att = _mla_attn(qn, qr, kv, kr, B=B, T=T, heads=heads)
    y = _proj(att, w_out.astype(BF16), out_dtype=F32, name="proj_out1")
    x1, x1_packed, logits = _add_ln(x, y, ln1_g, ln1_b, packed=True, router=router)

    E = router.shape[1]
    tm = _tile(N, 512)
    meta, cnt = _router(logits, E)
    counts = cnt[0, :E].astype(I32)
    padded = (counts + tm - 1) // tm * tm
    ends = jnp.cumsum(padded)
    starts = ends - padded
    i1, i2 = meta[:, 0].astype(I32), meta[:, 1].astype(I32)
    dest = jnp.stack([starts[i1] + meta[:, 4].astype(I32), starts[i2] + meta[:, 5].astype(I32)], 1).reshape(-1)
    n_rows = 2 * N + E * tm
    n_tiles = n_rows // tm
    n_valid = (ends[-1] // tm).astype(I32).reshape(1)
    tile_start = jnp.arange(n_tiles, dtype=I32) * tm
    tile_expert = jnp.minimum(jnp.sum(tile_start[:, None] >= ends[None, :], axis=1), E - 1).astype(I32)
    tile_expert = jnp.where(jnp.arange(n_tiles) < n_valid[0], tile_expert, tile_expert[jnp.maximum(n_valid[0] - 1, 0)])
    tile_rows = counts[tile_expert] - (tile_start - starts[tile_expert])
    tile_half = (tile_rows <= tm // 2).astype(I32)
    row_tok = jnp.zeros((n_rows,), I32).at[dest].set(jnp.repeat(jnp.arange(N, dtype=I32), 2))
    ys = _experts(x1_packed, we1, we3, we2, tile_expert, n_valid, tile_half, row_tok, tm=tm)
    return _combine(x1, meta, ys, dest, ln2_g, ln2_b)


def kernel(x, l0_w_in, l0_ret_gn_g, l0_w_out, l0_ln1_g, l0_ln1_b, l0_ffn_w1, l0_ffn_w3, l0_ffn_w2, l0_ln2_g, l0_ln2_b, l1_w_dq_dkv, l1_q_norm_g, l1_w_uq, l1_kv_norm_g, l1_w_ukv, l1_w_out, l1_ln1_g, l1_ln1_b, l1_router, l1_moe_w1, l1_moe_w3, l1_moe_w2, l1_ln2_g, l1_ln2_b):
    B, T, D = x.shape
    x2 = x.reshape(B * T, D)
    h, hb = _even_layer(x2, x2.astype(BF16), l0_w_in, l0_ret_gn_g, l0_w_out, l0_ln1_g, l0_ln1_b,
                        l0_ffn_w1, l0_ffn_w3, l0_ffn_w2, l0_ln2_g, l0_ln2_b, B=B, T=T)
    out = _odd_layer(h, hb, l1_w_dq_dkv, l1_q_norm_g, l1_w_uq, l1_kv_norm_g, l1_w_ukv, l1_w_out,
                     l1_ln1_g, l1_ln1_b, l1_router, l1_moe_w1, l1_moe_w3, l1_moe_w2, l1_ln2_g, l1_ln2_b, B=B, T=T)
    return out.reshape(B, T, D)
```

```python
import functools

import jax
import jax.numpy as jnp
from jax import lax
from jax.experimental import pallas as pl
from jax.experimental.pallas import tpu as pltpu

F32 = jnp.float32
BF16 = jnp.bfloat16
I32 = jnp.int32

A_HEAD_DIM = 128
A_KV_HEADS = 4
IDX_HEADS = 16
IDX_DIM = 64
DSA_TOPK_MAX = 256
RET_KEY_DIM = 256
RET_VAL_DIM = 256
RET_CHUNK = 128
RET_THETA = 10000.0
MLA_V = 128
MLA_NOPE = 128
MLA_ROPE = 64
ROPE_THETA = 500000.0
Q_BLOCK = 128
LN_EPS = 1e-5
RMS_EPS = 1e-6
DEPTH = 2
ALPHA = (2.0 * DEPTH) ** 0.25

LANE = 128
V7X_VMEM_BYTES = 64 * 1024 * 1024
VMEM_LIMIT = V7X_VMEM_BYTES - 8 * 1024 * 1024
MASKED = -1e30
INT_MIN = -(2 ** 31)

NT_DIMS = (((1,), (1,)), ((), ()))
TN_DIMS = (((0,), (0,)), ((), ()))


def _tile(n, pref, mult=LANE):
    if n <= pref:
        return n
    t = (pref // mult) * mult
    while t > mult and n % t:
        t -= mult
    assert n % t == 0, (n, pref, mult)
    return t


def _params(*sem):
    return pltpu.CompilerParams(dimension_semantics=sem, vmem_limit_bytes=VMEM_LIMIT)


def _rope_cos_sin(T, rot_dim, theta):
    inv = theta ** (-jnp.arange(0, rot_dim, 2, dtype=F32) / rot_dim)
    ang = jnp.arange(T, dtype=F32)[:, None] * inv[None, :]
    return jnp.cos(ang), jnp.sin(ang)


def _lane_tables(cos, sin, head_dim, scale=1.0):
    T, half = cos.shape
    rest = head_dim - 2 * half
    zh = jnp.zeros((T, half), F32)
    c = jnp.concatenate([cos, cos, jnp.ones((T, rest), F32)], 1)
    sa = jnp.concatenate([-sin, zh, jnp.zeros((T, rest), F32)], 1)
    sb = jnp.concatenate([zh, sin, jnp.zeros((T, rest), F32)], 1)
    reps = LANE // head_dim
    return tuple(jnp.tile(t * scale, (1, reps)) for t in (c, sa, sb))


def _proj_kernel(*refs, slab_pat, mode, half, scale, with_tab, two_inputs):
    x_ref, w_ref = refs[:2]
    acc = jnp.dot(x_ref[...], w_ref[...], preferred_element_type=F32)
    refs = refs[2:]
    if two_inputs:
        acc = acc + jnp.dot(refs[0][...], refs[1][...], preferred_element_type=F32)
        refs = refs[2:]
    if with_tab:
        c_ref, sa_ref, sb_ref, o_ref = refs
    else:
        (o_ref,) = refs
    for s, p in enumerate(slab_pat):
        a = acc[:, s * LANE:(s + 1) * LANE]
        if p < 0:
            out = a if scale == 1.0 else a * scale
        else:
            c = c_ref[:, p * LANE:(p + 1) * LANE]
            sa = sa_ref[:, p * LANE:(p + 1) * LANE]
            if mode == "lane":
                sb = sb_ref[:, p * LANE:(p + 1) * LANE]
                out = a * c + pltpu.roll(a, LANE - half, 1) * sa + pltpu.roll(a, half, 1) * sb
            else:
                q = s ^ 1
                out = a * c + acc[:, q * LANE:(q + 1) * LANE] * sa
        o_ref[:, s * LANE:(s + 1) * LANE] = out.astype(o_ref.dtype)


def _proj(x, w, *, out_dtype, bm=1024, bn=1024, tabs=None, slab_pat=None, mode="lane", half=0,
          scale=1.0, seq_len=None, second=None, name="proj"):
    M, K = x.shape
    N = w.shape[1]
    bm = _tile(M, bm) if seq_len is None else _tile(seq_len, bm)
    bn = _tile(N, bn)
    if slab_pat is None:
        slab_pat = (-1,) * (bn // LANE)
    assert len(slab_pat) == bn // LANE
    in_specs = [pl.BlockSpec((bm, K), lambda i, j: (i, 0)),
                pl.BlockSpec((K, bn), lambda i, j: (0, j))]
    args = [x, w]
    if second is not None:
        x2, w2 = second
        in_specs += [pl.BlockSpec((bm, x2.shape[1]), lambda i, j: (i, 0)),
                     pl.BlockSpec((x2.shape[1], bn), lambda i, j: (0, j))]
        args += [x2, w2]
    if tabs is not None:
        tb = seq_len // bm
        tw = tabs[0].shape[1]
        in_specs += [pl.BlockSpec((bm, tw), lambda i, j: (i % tb, 0))] * 3
        args += list(tabs)
    kern = functools.partial(_proj_kernel, slab_pat=tuple(slab_pat), mode=mode, half=half,
                             scale=scale, with_tab=tabs is not None, two_inputs=second is not None)
    return pl.pallas_call(
        kern,
        grid=(M // bm, N // bn),
        in_specs=in_specs,
        out_specs=pl.BlockSpec((bm, bn), lambda i, j: (i, j)),
        out_shape=jax.ShapeDtypeStruct((M, N), out_dtype),
        compiler_params=_params("parallel", "parallel"),
        name=name,
    )(*args)


def _mm_ksplit_kernel(x_ref, w_ref, o_ref):
    part = jnp.dot(x_ref[...], w_ref[...], preferred_element_type=F32)

    @pl.when(pl.program_id(2) == 0)
    def _():
        o_ref[...] = part

    @pl.when(pl.program_id(2) > 0)
    def _():
        o_ref[...] += part


def _mm_ksplit(x, w, *, bm=1024, bn=1024, bk=2048, name="mm_ksplit"):
    M, K = x.shape
    N = w.shape[1]
    bm, bn, bk = _tile(M, bm), _tile(N, bn), _tile(K, bk)
    return pl.pallas_call(
        _mm_ksplit_kernel,
        grid=(M // bm, N // bn, K // bk),
        in_specs=[pl.BlockSpec((bm, bk), lambda i, j, k: (i, k)),
                  pl.BlockSpec((bk, bn), lambda i, j, k: (k, j))],
        out_specs=pl.BlockSpec((bm, bn), lambda i, j, k: (i, j)),
        out_shape=jax.ShapeDtypeStruct((M, N), F32),
        compiler_params=_params("parallel", "parallel", "arbitrary"),
        name=name,
    )(x, w)


def _swiglu_up_kernel(x_ref, w1_ref, w3_ref, o_ref):
    x = x_ref[...]
    a = jnp.dot(x, w1_ref[...], preferred_element_type=F32)
    b = jnp.dot(x, w3_ref[...], preferred_element_type=F32)
    o_ref[...] = (a * jax.nn.sigmoid(a) * b).astype(o_ref.dtype)


def _swiglu_up(x, w1, w3, *, bm=1024, bn=512):
    M, K = x.shape
    N = w1.shape[1]
    bm, bn = _tile(M, bm), _tile(N, bn)
    return pl.pallas_call(
        _swiglu_up_kernel,
        grid=(M // bm, N // bn),
        in_specs=[pl.BlockSpec((bm, K), lambda i, j: (i, 0)),
                  pl.BlockSpec((K, bn), lambda i, j: (0, j)),
                  pl.BlockSpec((K, bn), lambda i, j: (0, j))],
        out_specs=pl.BlockSpec((bm, bn), lambda i, j: (i, j)),
        out_shape=jax.ShapeDtypeStruct((M, N), BF16),
        compiler_params=_params("parallel", "parallel"),
        name="swiglu_up",
    )(x, w1, w3)


def _layer_norm_rows(z, g, b):
    mu = jnp.mean(z, axis=-1, keepdims=True)
    zc = z - mu
    var = jnp.mean(zc * zc, axis=-1, keepdims=True)
    return zc * lax.rsqrt(var + LN_EPS) * g + b


def _pack_bf16_pairs(x):
    half = x.shape[1] // 2
    hi = pltpu.bitcast(x[:, :half].astype(BF16).astype(F32), I32)
    lo = pltpu.bitcast(x[:, half:].astype(BF16).astype(F32), I32)
    return hi | lax.shift_right_logical(lo, 16)


def _unpack_bf16_pairs(w):
    hi = pltpu.bitcast(w & jnp.int32(-65536), F32).astype(BF16)
    lo = pltpu.bitcast(lax.shift_left(w, 16), F32).astype(BF16)
    return hi, lo


def _add_ln_kernel(x_ref, y_ref, g_ref, b_ref, *rest, packed, with_router):
    out = _layer_norm_rows(ALPHA * x_ref[...] + y_ref[...], g_ref[...], b_ref[...])
    if with_router:
        r_ref, of_ref, o2_ref, lg_ref = rest
        lg_ref[...] = jnp.dot(out, r_ref[...], preferred_element_type=F32, precision=lax.Precision.HIGHEST)
    else:
        of_ref, o2_ref = rest
    of_ref[...] = out
    o2_ref[...] = _pack_bf16_pairs(out) if packed else out.astype(o2_ref.dtype)


def _add_ln(x, y, g, b, *, packed=False, router=None, bm=256):
    M, D = x.shape
    bm = _tile(M, bm, 8)
    row = pl.BlockSpec((bm, D), lambda i: (i, 0))
    vec = pl.BlockSpec((1, D), lambda i: (0, 0))
    second = jax.ShapeDtypeStruct((M, D // 2), I32) if packed else jax.ShapeDtypeStruct((M, D), BF16)
    in_specs, args = [row, row, vec, vec], [x, y, g.reshape(1, D), b.reshape(1, D)]
    out_specs = [row, pl.BlockSpec((bm, second.shape[1]), lambda i: (i, 0))]
    out_shape = [jax.ShapeDtypeStruct((M, D), F32), second]
    if router is not None:
        in_specs.append(pl.BlockSpec((D, LANE), lambda i: (0, 0)))
        args.append(jnp.zeros((D, LANE), F32).at[:, :router.shape[1]].set(router))
        out_specs.append(pl.BlockSpec((bm, LANE), lambda i: (i, 0)))
        out_shape.append(jax.ShapeDtypeStruct((M, LANE), F32))
    return pl.pallas_call(
        functools.partial(_add_ln_kernel, packed=packed, with_router=router is not None),
        grid=(M // bm,),
        in_specs=in_specs,
        out_specs=out_specs,
        out_shape=out_shape,
        compiler_params=_params("parallel"),
        name="add_ln",
    )(*args)


LOG2E = 1.4426950408889634


def _online_softmax(s, m, l):
    m_new = jnp.maximum(m, jnp.max(s, axis=1, keepdims=True))
    a = jnp.exp2(m - m_new)
    p = jnp.exp2((s - m_new).astype(BF16))
    return m_new, a * l + jnp.sum(p.astype(F32), axis=1, keepdims=True), a, p


def _dsa_kernel(q_ref, k_ref, v_ref, iq_ref, ik_ref, o_ref, keys_ref, bias_ref, *, topk, ck, n_rep, idx_bits):
    blk = pl.program_id(1)
    n_chunks = (blk * Q_BLOCK + Q_BLOCK + ck - 1) // ck
    row = lax.broadcasted_iota(I32, (Q_BLOCK, ck), 0) + blk * Q_BLOCK
    lane = lax.broadcasted_iota(I32, (Q_BLOCK, ck), 1)
    w_off = IDX_HEADS * IDX_DIM + IDX_DIM
    iq = iq_ref[0]
    wi = iq[:, w_off:w_off + IDX_HEADS]
    q_idx = jnp.concatenate([iq[:, h * IDX_DIM:(h + 1) * IDX_DIM].astype(BF16) for h in range(IDX_HEADS)], axis=0)

    def score_body(c, carry):
        off = pl.multiple_of(c * ck, ck)
        kc = ik_ref[0, pl.ds(off, ck), :][:, :IDX_DIM].astype(BF16)
        lg = lax.dot_general(q_idx, kc, NT_DIMS, preferred_element_type=F32)
        s = jnp.zeros((Q_BLOCK, ck), F32)
        for h in range(IDX_HEADS):
            s = s + wi[:, h:h + 1] * jnp.maximum(lg[h * Q_BLOCK:(h + 1) * Q_BLOCK], 0.0)
        bits = pltpu.bitcast(s, I32)
        key = bits ^ ((bits >> 31) & 0x7FFFFFFF)
        keys_ref[c] = jnp.where(lane + off <= row, key, INT_MIN)
        return carry

    lax.fori_loop(0, n_chunks, score_body, 0)

    def count(indicator):
        def body(c, acc):
            part = indicator(keys_ref[c], lane + c * ck)
            for j in range(ck // LANE):
                acc = acc + part[:, j * LANE:(j + 1) * LANE]
            return acc
        acc = lax.fori_loop(0, n_chunks, body, jnp.zeros((Q_BLOCK, LANE), F32))
        return jnp.sum(acc, axis=1, keepdims=True)

    kf = float(topk)
    ok = count(lambda kc, idx: jnp.where(kc >= 0, 1.0, 0.0)) >= kf
    thr = jnp.where(ok, 0, INT_MIN).astype(I32)

    def bit_body(i, thr):
        cand = thr + lax.shift_left(jnp.int32(1), 30 - i)
        ok = count(lambda kc, idx: jnp.where(kc >= cand, 1.0, 0.0)) >= kf
        return jnp.where(ok, cand, thr)

    thr = lax.fori_loop(0, 31, bit_body, thr)

    n_gt = count(lambda kc, idx: jnp.where(kc > thr, 1.0, 0.0))
    n_ge = count(lambda kc, idx: jnp.where(kc >= thr, 1.0, 0.0))
    need = kf - n_gt
    has_thr = thr > INT_MIN
    surplus = jnp.where(has_thr, n_ge - n_gt - need, 0.0)

    def tie_search():
        def tie_body(i, last):
            cand = last + lax.shift_left(jnp.int32(1), idx_bits - 1 - i)
            ok = count(lambda kc, idx: jnp.where(kc == thr, jnp.where(idx < cand, 1.0, 0.0), 0.0)) < need
            return jnp.where(ok, cand, last)
        return lax.fori_loop(0, idx_bits, tie_body, jnp.zeros((Q_BLOCK, 1), I32))

    last = lax.cond(jnp.max(surplus) > 0.0, tie_search, lambda: jnp.full((Q_BLOCK, 1), 2 ** idx_bits, I32))
    last = jnp.where(has_thr, last, -1)

    def bias_body(c, carry):
        kc = keys_ref[c]
        tie_bias = jnp.where(lane + c * ck <= last, 0.0, MASKED)
        bias_ref[c] = jnp.where(kc == thr, tie_bias, jnp.where(kc > thr, 0.0, MASKED))
        return carry

    lax.fori_loop(0, n_chunks, bias_body, 0)

    q = q_ref[0]
    rows = n_rep * Q_BLOCK
    qgs = [jnp.concatenate([q[:, (g * n_rep + r) * A_HEAD_DIM:(g * n_rep + r + 1) * A_HEAD_DIM]
                            for r in range(n_rep)], axis=0) for g in range(A_KV_HEADS)]

    def att_body(c, carry, n_sub):
        width = n_sub * ck
        off = pl.multiple_of(c * width, width)
        b = jnp.concatenate([bias_ref[c * n_sub + t] for t in range(n_sub)], axis=1)
        bias = jnp.concatenate([b] * n_rep, axis=0)
        out = []
        for g in range(A_KV_HEADS):
            m, l, acc = carry[g]
            kc = k_ref[0, pl.ds(off, width), g * A_HEAD_DIM:(g + 1) * A_HEAD_DIM]
            vc = v_ref[0, pl.ds(off, width), g * A_HEAD_DIM:(g + 1) * A_HEAD_DIM]
            s = lax.dot_general(qgs[g], kc, NT_DIMS, preferred_element_type=F32) + bias
            m, l, a, p = _online_softmax(s, m, l)
            out.append((m, l, a * acc + jnp.dot(p, vc, preferred_element_type=F32)))
        return tuple(out)

    one = (jnp.full((rows, 1), MASKED, F32), jnp.zeros((rows, 1), F32), jnp.zeros((rows, A_HEAD_DIM), F32))
    carry = (one,) * A_KV_HEADS
    if k_ref.shape[1] >= 2 * ck:
        carry = lax.fori_loop(0, n_chunks // 2, functools.partial(att_body, n_sub=2), carry)
        carry = lax.cond(n_chunks % 2 == 1, lambda c: att_body(n_chunks - 1, c, 1), lambda c: c, carry)
    else:
        carry = att_body(0, carry, 1)
    for g in range(A_KV_HEADS):
        _, l, acc = carry[g]
        o = acc / l
        for r in range(n_rep):
            col = (g * n_rep + r) * A_HEAD_DIM
            o_ref[0, :, col:col + A_HEAD_DIM] = o[r * Q_BLOCK:(r + 1) * Q_BLOCK].astype(o_ref.dtype)


def _dsa(q, k, v, idx, *, B, T):
    a_heads = q.shape[-1] // A_HEAD_DIM
    n_rep = a_heads // A_KV_HEADS
    topk = min(DSA_TOPK_MAX, T // 4)
    ck = _tile(T, 512)
    idx_w = idx.shape[-1]
    kv_w = k.shape[-1]
    kern = functools.partial(_dsa_kernel, topk=topk, ck=ck, n_rep=n_rep, idx_bits=max(1, (T - 1).bit_length()))
    return pl.pallas_call(
        kern,
        grid=(B, T // Q_BLOCK),
        in_specs=[pl.BlockSpec((1, Q_BLOCK, q.shape[-1]), lambda b, i: (b, i, 0)),
                  pl.BlockSpec((1, T, kv_w), lambda b, i: (b, 0, 0)),
                  pl.BlockSpec((1, T, kv_w), lambda b, i: (b, 0, 0)),
                  pl.BlockSpec((1, Q_BLOCK, idx_w), lambda b, i: (b, i, 0)),
                  pl.BlockSpec((1, T, LANE), lambda b, i: (b, 0, IDX_HEADS * IDX_DIM // LANE))],
        out_specs=pl.BlockSpec((1, Q_BLOCK, q.shape[-1]), lambda b, i: (b, i, 0)),
        out_shape=jax.ShapeDtypeStruct(q.shape, BF16),
        scratch_shapes=[pltpu.VMEM((T // ck, Q_BLOCK, ck), I32), pltpu.VMEM((T // ck, Q_BLOCK, ck), F32)],
        compiler_params=_params("parallel", "parallel"),
        name="dsa",
    )(q, k, v, idx, idx)


RET_HEADS_PER_STEP = 2


def _retention_kernel(q_ref, k_ref, v_ref, g_ref, gn_ref, din_ref, qd_ref, kd_ref, cd_ref, o_ref, state_ref, *, n_sub):
    @pl.when(pl.program_id(2) == 0)
    def _():
        state_ref[...] = jnp.zeros_like(state_ref)

    W = RET_VAL_DIM
    for s in range(n_sub):
        sl = pl.ds(s * RET_CHUNK, RET_CHUNK)
        for j in range(RET_HEADS_PER_STEP):
            cols = slice(j * W, (j + 1) * W)
            qc = q_ref[sl, cols]
            kc = k_ref[sl, cols]
            vc = v_ref[sl, cols]
            st = state_ref[j]
            inner = lax.dot_general(qc, kc, NT_DIMS, preferred_element_type=F32) * din_ref[j]
            o = (jnp.dot(inner.astype(BF16), vc, preferred_element_type=F32)
                 + jnp.dot(qc, st.astype(BF16), preferred_element_type=F32) * qd_ref[j])
            vk = (vc.astype(F32) * kd_ref[j]).astype(BF16)
            state_ref[j] = st * cd_ref[j] + lax.dot_general(kc, vk, TN_DIMS, preferred_element_type=F32)
            mu = jnp.mean(o, axis=-1, keepdims=True)
            oc = o - mu
            var = jnp.mean(oc * oc, axis=-1, keepdims=True)
            gate = g_ref[sl, cols].astype(F32)
            normed = oc * lax.rsqrt(var + LN_EPS) * gn_ref[:, cols]
            o_ref[sl, cols] = (gate * jax.nn.sigmoid(gate) * normed).astype(o_ref.dtype)


def _retention(qk, pv, gn_g, *, B, T, heads, v_blk0):
    N = qk.shape[0]
    C = RET_CHUNK
    rb = _tile(T, 512)
    n_sub = rb // C
    nr = T // rb
    log_gamma = jnp.log(1.0 - 2.0 ** (-5.0 - jnp.arange(heads, dtype=F32)))
    pos = jnp.arange(C, dtype=F32)
    diff = pos[:, None] - pos[None, :]
    din = jnp.exp(jnp.where(diff[None] >= 0, log_gamma[:, None, None] * diff[None], -jnp.inf))
    qd = jnp.exp(log_gamma[:, None] * (pos[None] + 1.0))[:, :, None]
    kd = jnp.exp(log_gamma[:, None] * (C - 1.0 - pos[None]))[:, :, None]
    cd = jnp.exp(log_gamma * C)[:, None, None]
    hp = RET_HEADS_PER_STEP
    W = hp * RET_VAL_DIM
    assert heads % hp == 0 and v_blk0 % hp == 0
    blk = lambda off: pl.BlockSpec((rb, W), lambda b, h, r: (b * nr + r, off // hp + h))
    per_head = lambda shape: pl.BlockSpec((hp,) + shape, lambda b, h, r: (h, 0, 0))
    return pl.pallas_call(
        functools.partial(_retention_kernel, n_sub=n_sub),
        grid=(B, heads // hp, nr),
        in_specs=[blk(0), blk(heads), blk(v_blk0), blk(v_blk0 + heads),
                  pl.BlockSpec((1, W), lambda b, h, r: (0, h)),
                  per_head((C, C)), per_head((C, 1)), per_head((C, 1)), per_head((1, 1))],
        out_specs=pl.BlockSpec((rb, W), lambda b, h, r: (b * nr + r, h)),
        out_shape=jax.ShapeDtypeStruct((N, heads * RET_VAL_DIM), BF16),
        scratch_shapes=[pltpu.VMEM((hp, RET_KEY_DIM, RET_VAL_DIM), F32)],
        compiler_params=_params("parallel", "parallel", "arbitrary"),
        name="retention",
    )(qk, qk, pv, pv, gn_g.reshape(1, heads * RET_VAL_DIM), din, qd, kd, cd)


def _mla_down_kernel(x_ref, w_ref, qg_ref, kvg_ref, c_ref, sa_ref, sb_ref, cq_ref, ckv_ref, kr_ref, *, q_rank, kv_rank):
    acc = jnp.dot(x_ref[...], w_ref[...], preferred_element_type=F32)

    def rms(a, g):
        return a * lax.rsqrt(jnp.mean(a * a, axis=-1, keepdims=True) + RMS_EPS) * g

    cq_ref[...] = rms(acc[:, :q_rank], qg_ref[...]).astype(cq_ref.dtype)
    ckv_ref[...] = rms(acc[:, q_rank:q_rank + kv_rank], kvg_ref[...]).astype(ckv_ref.dtype)
    kr = acc[:, q_rank + kv_rank:]
    half = MLA_ROPE // 2
    kr = kr * c_ref[...] + pltpu.roll(kr, LANE - half, 1) * sa_ref[...] + pltpu.roll(kr, half, 1) * sb_ref[...]
    kr_ref[...] = kr.astype(kr_ref.dtype)


def _mla_down(x, w, q_g, kv_g, tabs, *, T, q_rank, kv_rank, bm=512):
    M, K = x.shape
    Nw = w.shape[1]
    bm = _tile(T, bm)
    tb = T // bm
    row = lambda n: pl.BlockSpec((bm, n), lambda i: (i, 0))
    tab = pl.BlockSpec((bm, LANE), lambda i: (i % tb, 0))
    return pl.pallas_call(
        functools.partial(_mla_down_kernel, q_rank=q_rank, kv_rank=kv_rank),
        grid=(M // bm,),
        in_specs=[row(K), pl.BlockSpec((K, Nw), lambda i: (0, 0)),
                  pl.BlockSpec((1, q_rank), lambda i: (0, 0)), pl.BlockSpec((1, kv_rank), lambda i: (0, 0)),
                  tab, tab, tab],
        out_specs=[row(q_rank), row(kv_rank), row(LANE)],
        out_shape=[jax.ShapeDtypeStruct((M, q_rank), BF16), jax.ShapeDtypeStruct((M, kv_rank), BF16),
                   jax.ShapeDtypeStruct((M, LANE), BF16)],
        compiler_params=_params("parallel"),
        name="mla_down",
    )(x, w, q_g.reshape(1, q_rank), kv_g.reshape(1, kv_rank), *tabs)


MLA_HEADS_PER_STEP = 4


def _mla_attn_kernel(qn_ref, qr_ref, kn_ref, kr_ref, v_ref, o_ref, *, tq):
    i = pl.program_id(2)
    lane = lax.broadcasted_iota(I32, (tq, LANE), 1)
    qs = []
    for j in range(MLA_HEADS_PER_STEP):
        lo = (j % 2) * MLA_ROPE
        own = jnp.where((lane >= lo) & (lane < lo + MLA_ROPE), 1.0, 0.0)
        pair = qr_ref[:, (j // 2) * LANE:(j // 2 + 1) * LANE].astype(F32)
        qr = (pair * own).astype(BF16)
        qs.append(jnp.concatenate([qn_ref[:, j * LANE:(j + 1) * LANE], qr], axis=1))

    def step(c, carry, masked, width):
        off = pl.multiple_of(c * width, width)
        kr = kr_ref[pl.ds(off, width), :]
        out = []
        for j in range(MLA_HEADS_PER_STEP):
            m, l, acc = carry[j]
            k = jnp.concatenate([kn_ref[pl.ds(off, width), j * LANE:(j + 1) * LANE], kr], axis=1)
            s = lax.dot_general(qs[j], k, NT_DIMS, preferred_element_type=F32)
            if masked:
                r_io = lax.broadcasted_iota(I32, (tq, width), 0)
                c_io = lax.broadcasted_iota(I32, (tq, width), 1)
                s = jnp.where(c_io <= r_io, s, MASKED)
            m, l, a, p = _online_softmax(s, m, l)
            v = v_ref[pl.ds(off, width), j * MLA_V:(j + 1) * MLA_V]
            out.append((m, l, a * acc + jnp.dot(p, v, preferred_element_type=F32)))
        return tuple(out)

    one = (jnp.full((tq, 1), MASKED, F32), jnp.zeros((tq, 1), F32), jnp.zeros((tq, MLA_V), F32))
    carry = (one,) * MLA_HEADS_PER_STEP
    if kn_ref.shape[0] >= 2 * tq:
        carry = lax.fori_loop(0, i // 2, functools.partial(step, masked=False, width=2 * tq), carry)
        carry = lax.cond(i % 2 == 1, lambda c: step(i - 1, c, False, tq), lambda c: c, carry)
    carry = step(i, carry, True, tq)
    for j in range(MLA_HEADS_PER_STEP):
        _, l, acc = carry[j]
        o_ref[:, j * MLA_V:(j + 1) * MLA_V] = (acc / l).astype(o_ref.dtype)


def _mla_attn(qn, qr, kv, kr, *, B, T, heads):
    N = qn.shape[0]
    tq = _tile(T, 512)
    nq = T // tq
    hp = MLA_HEADS_PER_STEP
    w = hp * LANE
    return pl.pallas_call(
        functools.partial(_mla_attn_kernel, tq=tq),
        grid=(B, heads // hp, nq),
        in_specs=[pl.BlockSpec((tq, w), lambda b, h, i: (b * nq + i, h)),
                  pl.BlockSpec((tq, hp * MLA_ROPE), lambda b, h, i: (b * nq + i, h)),
                  pl.BlockSpec((T, w), lambda b, h, i: (b, h)),
                  pl.BlockSpec((T, LANE), lambda b, h, i: (b, 0)),
                  pl.BlockSpec((T, w), lambda b, h, i: (b, heads // hp + h))],
        out_specs=pl.BlockSpec((tq, w), lambda b, h, i: (b * nq + i, h)),
        out_shape=jax.ShapeDtypeStruct((N, heads * MLA_V), BF16),
        compiler_params=_params("parallel", "parallel", "parallel"),
        name="mla_attn",
    )(qn, qr, kv, kr, kv)


def _router_kernel(lg_ref, meta_ref, cnt_ref, carry_ref, *, n_exp):
    @pl.when(pl.program_id(0) == 0)
    def _():
        carry_ref[...] = jnp.zeros_like(carry_ref)

    bm = lg_ref.shape[0]
    lane = lax.broadcasted_iota(I32, (bm, LANE), 1).astype(F32)
    logits = jnp.where(lane < n_exp, lg_ref[...], -jnp.inf)
    m1 = jnp.max(logits, axis=1, keepdims=True)
    i1 = jnp.min(jnp.where(logits == m1, lane, float(LANE)), axis=1, keepdims=True)
    rest = jnp.where(lane == i1, -jnp.inf, logits)
    m2 = jnp.max(rest, axis=1, keepdims=True)
    i2 = jnp.min(jnp.where(rest == m2, lane, float(LANE)), axis=1, keepdims=True)
    e = jnp.exp(m2 - m1)
    g1 = 1.0 / (1.0 + e)
    g2 = e / (1.0 + e)
    sel = jnp.where(lane == i1, 1.0, jnp.where(lane == i2, 1.0, 0.0))
    r_io = lax.broadcasted_iota(I32, (bm, bm), 0)
    c_io = lax.broadcasted_iota(I32, (bm, bm), 1)
    below = jnp.where(c_io < r_io, 1.0, 0.0).astype(BF16)
    carry = carry_ref[0:1, :]
    rank = jnp.dot(below, sel.astype(BF16), preferred_element_type=F32) + carry
    r1 = jnp.sum(jnp.where(lane == i1, rank, 0.0), axis=1, keepdims=True)
    r2 = jnp.sum(jnp.where(lane == i2, rank, 0.0), axis=1, keepdims=True)
    meta = jnp.where(lane == 0, i1, 0.0)
    meta = jnp.where(lane == 1, i2, meta)
    meta = jnp.where(lane == 2, g1, meta)
    meta = jnp.where(lane == 3, g2, meta)
    meta = jnp.where(lane == 4, r1, meta)
    meta = jnp.where(lane == 5, r2, meta)
    meta_ref[...] = meta
    total = carry + jnp.sum(sel, axis=0, keepdims=True)
    carry_ref[...] = jnp.broadcast_to(total, carry_ref.shape)
    cnt_ref[...] = jnp.broadcast_to(total, cnt_ref.shape)


def _router(logits, n_exp, *, bm=512):
    M = logits.shape[0]
    bm = _tile(M, bm)
    return pl.pallas_call(
        functools.partial(_router_kernel, n_exp=n_exp),
        grid=(M // bm,),
        in_specs=[pl.BlockSpec((bm, LANE), lambda i: (i, 0))],
        out_specs=[pl.BlockSpec((bm, LANE), lambda i: (i, 0)), pl.BlockSpec((8, LANE), lambda i: (0, 0))],
        out_shape=[jax.ShapeDtypeStruct((M, LANE), F32), jax.ShapeDtypeStruct((8, LANE), F32)],
        scratch_shapes=[pltpu.VMEM((8, LANE), F32)],
        compiler_params=_params("arbitrary"),
        name="router",
    )(logits)


DMA_LOOP_UNROLL = 8


def _moe_gather_kernel(tok_ref, nv_ref, x_hbm, o_ref, stage_ref, sem, *, tm):
    r = pl.program_id(0)
    nv = nv_ref[0]

    half = stage_ref.shape[1]

    def row_copy(j, tok):
        return pltpu.make_async_copy(x_hbm.at[pl.ds(tok, 1)], stage_ref.at[pl.ds(j, 1)], sem)

    def gather_start(tile):
        def body(j, carry):
            row_copy(j, tok_ref[tile * tm + j]).start()
            return carry
        lax.fori_loop(0, tm, body, 0, unroll=DMA_LOOP_UNROLL)

    def gather_wait():
        def body(j, carry):
            row_copy(j, 0).wait()
            return carry
        lax.fori_loop(0, tm, body, 0, unroll=DMA_LOOP_UNROLL)

    @pl.when(r < nv)
    def _():
        @pl.when(r == 0)
        def _():
            gather_start(0)

        gather_wait()
        hi, lo = _unpack_bf16_pairs(stage_ref[...])
        o_ref[:, :half] = hi
        o_ref[:, half:] = lo

        @pl.when(r + 1 < nv)
        def _():
            gather_start(r + 1)


def _moe_up_kernel(te_ref, nv_ref, xs_ref, w1_ref, w3_ref, o_ref, wb1_ref, wb3_ref):
    r = pl.program_id(1)

    @pl.when(r < nv_ref[0])
    def _():
        @pl.when((r == 0) | (te_ref[r] != te_ref[jnp.maximum(r - 1, 0)]))
        def _():
            wb1_ref[...] = w1_ref[0].astype(BF16)
            wb3_ref[...] = w3_ref[0].astype(BF16)

        xb = xs_ref[...]
        a = jnp.dot(xb, wb1_ref[...], preferred_element_type=F32)
        b = jnp.dot(xb, wb3_ref[...], preferred_element_type=F32)
        o_ref[...] = (a * jax.nn.sigmoid(a) * b).astype(o_ref.dtype)


def _moe_down_kernel(te_ref, nv_ref, h_ref, w2_ref, o_ref, wb_ref):
    r = pl.program_id(1)

    @pl.when(r < nv_ref[0])
    def _():
        @pl.when((r == 0) | (te_ref[r] != te_ref[jnp.maximum(r - 1, 0)]))
        def _():
            wb_ref[...] = w2_ref[0].astype(BF16)

        o_ref[...] = jnp.dot(h_ref[...], wb_ref[...], preferred_element_type=F32)


def _experts(x, w1, w3, w2, tile_expert, n_valid, row_tok, *, tm, tf=512, tn=512):
    D = 2 * x.shape[1]
    P = row_tok.shape[0]
    E, _, F = w1.shape
    tf, tn = _tile(F, tf), _tile(D, tn)
    n_tiles = P // tm
    row = lambda r, nv: jnp.minimum(r, nv[0] - 1)

    xs = pl.pallas_call(
        functools.partial(_moe_gather_kernel, tm=tm),
        grid_spec=pltpu.PrefetchScalarGridSpec(
            num_scalar_prefetch=2, grid=(n_tiles,),
            in_specs=[pl.BlockSpec(memory_space=pl.ANY)],
            out_specs=pl.BlockSpec((tm, D), lambda r, tok, nv: (row(r, nv), 0)),
            scratch_shapes=[pltpu.VMEM((tm, D // 2), I32), pltpu.SemaphoreType.DMA(())]),
        out_shape=jax.ShapeDtypeStruct((P, D), BF16),
        compiler_params=_params("arbitrary"),
        name="moe_gather",
    )(row_tok, n_valid, x)

    hid = pl.pallas_call(
        _moe_up_kernel,
        grid_spec=pltpu.PrefetchScalarGridSpec(
            num_scalar_prefetch=2, grid=(F // tf, n_tiles),
            in_specs=[pl.BlockSpec((tm, D), lambda f, r, te, nv: (row(r, nv), 0)),
                      pl.BlockSpec((1, D, tf), lambda f, r, te, nv: (te[r], 0, f)),
                      pl.BlockSpec((1, D, tf), lambda f, r, te, nv: (te[r], 0, f))],
            out_specs=pl.BlockSpec((tm, tf), lambda f, r, te, nv: (row(r, nv), f)),
            scratch_shapes=[pltpu.VMEM((D, tf), BF16), pltpu.VMEM((D, tf), BF16)]),
        out_shape=jax.ShapeDtypeStruct((P, F), BF16),
        compiler_params=_params("arbitrary", "arbitrary"),
        name="moe_up",
    )(tile_expert, n_valid, xs, w1, w3)

    return pl.pallas_call(
        _moe_down_kernel,
        grid_spec=pltpu.PrefetchScalarGridSpec(
            num_scalar_prefetch=2, grid=(D // tn, n_tiles),
            in_specs=[pl.BlockSpec((tm, F), lambda n, r, te, nv: (row(r, nv), 0)),
                      pl.BlockSpec((1, F, tn), lambda n, r, te, nv: (te[r], 0, n))],
            out_specs=pl.BlockSpec((tm, tn), lambda n, r, te, nv: (row(r, nv), n)),
            scratch_shapes=[pltpu.VMEM((F, tn), BF16)]),
        out_shape=jax.ShapeDtypeStruct((P, D), F32),
        compiler_params=_params("arbitrary", "arbitrary"),
        name="moe_down",
    )(tile_expert, n_valid, hid, w2)


def _combine_kernel(dest_ref, x_ref, meta_ref, g_ref, b_ref, ys_hbm, o_ref, buf_ref, sem):
    bm = x_ref.shape[0]
    i = pl.program_id(0)
    n = pl.num_programs(0)

    def copy(slot, j, k, d):
        return pltpu.make_async_copy(ys_hbm.at[pl.ds(d, 1)], buf_ref.at[slot, k, pl.ds(j, 1)], sem.at[slot, k])

    def gather_start(blk):
        slot = blk % 2

        def body(j, carry):
            for k in range(2):
                copy(slot, j, k, dest_ref[2 * (blk * bm + j) + k]).start()
            return carry
        lax.fori_loop(0, bm, body, 0, unroll=DMA_LOOP_UNROLL)

    @pl.when(i == 0)
    def _():
        gather_start(0)

    @pl.when(i + 1 < n)
    def _():
        gather_start(i + 1)

    slot = i % 2

    def wait(j, carry):
        for k in range(2):
            copy(slot, j, k, 0).wait()
        return carry

    lax.fori_loop(0, bm, wait, 0, unroll=DMA_LOOP_UNROLL)
    meta = meta_ref[...]
    y = meta[:, 2:3] * buf_ref[slot, 0] + meta[:, 3:4] * buf_ref[slot, 1]
    o_ref[...] = _layer_norm_rows(ALPHA * x_ref[...] + y, g_ref[...], b_ref[...])


def _combine(x, meta, ys, dest, g, b, *, bm=256):
    M, D = x.shape
    bm = _tile(M, bm, 8)
    row = lambda n: pl.BlockSpec((bm, n), lambda i, d: (i, 0))
    vec = pl.BlockSpec((1, D), lambda i, d: (0, 0))
    return pl.pallas_call(
        _combine_kernel,
        grid_spec=pltpu.PrefetchScalarGridSpec(
            num_scalar_prefetch=1, grid=(M // bm,),
            in_specs=[row(D), row(LANE), vec, vec, pl.BlockSpec(memory_space=pl.ANY)],
            out_specs=row(D),
            scratch_shapes=[pltpu.VMEM((2, 2, bm, D), F32), pltpu.SemaphoreType.DMA((2, 2))]),
        out_shape=jax.ShapeDtypeStruct((M, D), F32),
        compiler_params=_params("arbitrary"),
        name="moe_combine",
    )(dest, x, meta, g.reshape(1, D), b.reshape(1, D), ys)


def _even_layer(x, xb, w_in, ret_gn_g, w_out, ln1_g, ln1_b, w1, w3, w2, ln2_g, ln2_b, *, B, T):
    N, D = x.shape
    a_heads = D // 2 // A_HEAD_DIM
    r_heads = D // 2 // RET_VAL_DIM
    qa_w, kv_w = a_heads * A_HEAD_DIM, A_KV_HEADS * A_HEAD_DIM
    qi_w = IDX_HEADS * IDX_DIM
    rk_w, rv_w = r_heads * RET_KEY_DIM, r_heads * RET_VAL_DIM
    sizes = (qa_w, kv_w, kv_w, qi_w, IDX_DIM, IDX_HEADS, rk_w, rk_w, rv_w, rv_w)
    offs = [0]
    for s in sizes:
        offs.append(offs[-1] + s)
    col = lambda a, b_: w_in[:, offs[a]:offs[b_]]
    w_qa, w_ka, w_va = col(0, 1), col(1, 2), col(2, 3)
    w_qi, w_ki, w_wi = col(3, 4), col(4, 5), col(5, 6)
    w_qb, w_kb, w_vb, w_gb = col(6, 7), col(7, 8), col(8, 9), col(9, 10)

    cos_a, sin_a = _rope_cos_sin(T, A_HEAD_DIM // 4, ROPE_THETA)
    tab_q = _lane_tables(cos_a, sin_a, A_HEAD_DIM, A_HEAD_DIM ** -0.5 * LOG2E)
    tab_k = _lane_tables(cos_a, sin_a, A_HEAD_DIM)
    cos_i, sin_i = _rope_cos_sin(T, IDX_DIM // 4, ROPE_THETA)
    tab_i = _lane_tables(cos_i, sin_i, IDX_DIM)
    pass_c = jnp.ones((T, LANE - IDX_DIM), F32)
    pass_s = jnp.zeros((T, LANE - IDX_DIM), F32)
    tab_idx = tuple(jnp.concatenate([t, t[:, :IDX_DIM], p], 1)
                    for t, p in zip(tab_i, (pass_c, pass_s, pass_s)))
    inv = 1.0 / (RET_THETA ** jnp.linspace(0.0, 1.0, RET_KEY_DIM // 2, dtype=F32))
    ang = jnp.arange(T, dtype=F32)[:, None] * inv[None, :]
    cos_r, sin_r = jnp.cos(ang), jnp.sin(ang)
    sin_pair = jnp.concatenate([-sin_r, sin_r], 1)
    tab_r = (jnp.concatenate([cos_r, cos_r], 1), sin_pair, sin_pair)

    idx_pad = LANE - IDX_DIM - IDX_HEADS
    w_idx = jnp.concatenate([w_qi, w_ki, w_wi * (IDX_DIM ** -0.5 * IDX_HEADS ** -0.5),
                             jnp.zeros((D, idx_pad), F32)], 1).astype(BF16)
    w_rqk = jnp.concatenate([w_qb, w_kb * RET_KEY_DIM ** -0.5], 1).astype(BF16)
    w_pv = jnp.concatenate([w_va, w_vb, w_gb], 1).astype(BF16)
    qa = _proj(xb, w_qa.astype(BF16), out_dtype=BF16, tabs=tab_q, half=A_HEAD_DIM // 8,
               seq_len=T, name="proj_qa", **_pat(qa_w, 1024, 0))
    ka = _proj(xb, w_ka.astype(BF16), out_dtype=BF16, tabs=tab_k, half=A_HEAD_DIM // 8,
               seq_len=T, name="proj_ka", **_pat(kv_w, 1024, 0))
    n_idx = w_idx.shape[1]
    idx = _proj(xb, w_idx, out_dtype=F32, tabs=tab_idx, half=IDX_DIM // 8, seq_len=T, name="proj_idx",
                bm=512, bn=n_idx, slab_pat=(0,) * (qi_w // LANE) + (1,))
    rqk = _proj(xb, w_rqk, out_dtype=BF16, tabs=tab_r, mode="pair", seq_len=T, name="proj_ret_qk",
                bn=1024, slab_pat=(0, 1) * (_tile(2 * rk_w, 1024) // (2 * LANE)))
    pv = _proj(xb, w_pv, out_dtype=BF16, bn=768, seq_len=T, name="proj_v")

    ya = _dsa(qa.reshape(B, T, qa_w), ka.reshape(B, T, kv_w), pv.reshape(B, T, -1), idx.reshape(B, T, n_idx),
              B=B, T=T)
    yb = _retention(rqk, pv, ret_gn_g, B=B, T=T, heads=r_heads, v_blk0=kv_w // RET_VAL_DIM)
    w_out_b = w_out.astype(BF16)
    y = _proj(ya.reshape(N, qa_w), w_out_b[:qa_w], second=(yb, w_out_b[qa_w:]), out_dtype=F32, name="proj_out0")
    x1, x1b = _add_ln(x, y, ln1_g, ln1_b)
    hid = _swiglu_up(x1b, w1.astype(BF16), w3.astype(BF16))
    y = _mm_ksplit(hid, w2.astype(BF16), bk=3584, name="ffn_down")
    return _add_ln(x1, y, ln2_g, ln2_b)


def _pat(width, bn, p):
    bn = _tile(width, bn)
    return dict(bn=bn, slab_pat=(p,) * (bn // LANE))


def _odd_layer(x, xb, w_dq_dkv, q_norm_g, w_uq, kv_norm_g, w_ukv, w_out, ln1_g, ln1_b,
               router, we1, we3, we2, ln2_g, ln2_b, *, B, T):
    N, D = x.shape
    heads = D // MLA_V
    q_rank, kv_rank = q_norm_g.shape[0], kv_norm_g.shape[0]
    scale = (MLA_NOPE + MLA_ROPE) ** -0.5 * LOG2E
    cos_c, sin_c = _rope_cos_sin(T, MLA_ROPE, ROPE_THETA)
    tab_kr = _lane_tables(cos_c, sin_c, MLA_ROPE)
    tab_qr = _lane_tables(cos_c, sin_c, MLA_ROPE, scale)

    w_kr = w_dq_dkv[:, q_rank + kv_rank:]
    w_down = jnp.concatenate([w_dq_dkv[:, :q_rank + kv_rank], w_kr, w_kr], 1).astype(BF16)
    cq, ckv, kr = _mla_down(xb, w_down, q_norm_g, kv_norm_g, tab_kr, T=T, q_rank=q_rank, kv_rank=kv_rank)
    w_uq3 = w_uq.reshape(q_rank, heads, MLA_NOPE + MLA_ROPE)
    w_qn = w_uq3[:, :, :MLA_NOPE].reshape(q_rank, heads * MLA_NOPE).astype(BF16)
    w_qr = w_uq3[:, :, MLA_NOPE:].reshape(q_rank, heads * MLA_ROPE).astype(BF16)
    w_kv3 = w_ukv.reshape(kv_rank, heads, MLA_NOPE + MLA_V)
    w_kv = jnp.concatenate([w_kv3[:, :, :MLA_NOPE].reshape(kv_rank, heads * MLA_NOPE),
                            w_kv3[:, :, MLA_NOPE:].reshape(kv_rank, heads * MLA_V)], 1).astype(BF16)
    qn = _proj(cq, w_qn, out_dtype=BF16, scale=scale, seq_len=T, bn=2048, name="proj_q_nope")
    qr = _proj(cq, w_qr, out_dtype=BF16, tabs=tab_qr, half=MLA_ROPE // 2, seq_len=T, name="proj_q_rope",
               **_pat(heads * MLA_ROPE, 1024, 0))
    kv = _proj(ckv, w_kv, out_dtype=BF16, seq_len=T, bn=2048, name="proj_kv")
    att = _mla_attn(qn, qr, kv, kr, B=B, T=T, heads=heads)
    y = _proj(att, w_out.astype(BF16), out_dtype=F32, name="proj_out1")
    x1, x1_packed, logits = _add_ln(x, y, ln1_g, ln1_b, packed=True, router=router)

    E = router.shape[1]
    tm = _tile(N, 512)
    meta, cnt = _router(logits, E)
    counts = cnt[0, :E].astype(I32)
    padded = (counts + tm - 1) // tm * tm
    ends = jnp.cumsum(padded)
    starts = ends - padded
    i1, i2 = meta[:, 0].astype(I32), meta[:, 1].astype(I32)
    dest = jnp.stack([starts[i1] + meta[:, 4].astype(I32), starts[i2] + meta[:, 5].astype(I32)], 1).reshape(-1)
    n_rows = 2 * N + E * tm
    n_tiles = n_rows // tm
    n_valid = (ends[-1] // tm).astype(I32).reshape(1)
    tile_start = jnp.arange(n_tiles, dtype=I32) * tm
    tile_expert = jnp.minimum(jnp.sum(tile_start[:, None] >= ends[None, :], axis=1), E - 1).astype(I32)
    tile_expert = jnp.where(jnp.arange(n_tiles) < n_valid[0], tile_expert, tile_expert[jnp.maximum(n_valid[0] - 1, 0)])
    row_tok = jnp.zeros((n_rows,), I32).at[dest].set(jnp.repeat(jnp.arange(N, dtype=I32), 2))
    ys = _experts(x1_packed, we1, we3, we2, tile_expert, n_valid, row_tok, tm=tm)
    return _combine(x1, meta, ys, dest, ln2_g, ln2_b)


def kernel(x, l0_w_in, l0_ret_gn_g, l0_w_out, l0_ln1_g, l0_ln1_b, l0_ffn_w1, l0_ffn_w3, l0_ffn_w2, l0_ln2_g, l0_ln2_b, l1_w_dq_dkv, l1_q_norm_g, l1_w_uq, l1_kv_norm_g, l1_w_ukv, l1_w_out, l1_ln1_g, l1_ln1_b, l1_router, l1_moe_w1, l1_moe_w3, l1_moe_w2, l1_ln2_g, l1_ln2_b):
    B, T, D = x.shape
    x2 = x.reshape(B * T, D)
    h, hb = _even_layer(x2, x2.astype(BF16), l0_w_in, l0_ret_gn_g, l0_w_out, l0_ln1_g, l0_ln1_b,
                        l0_ffn_w1, l0_ffn_w3, l0_ffn_w2, l0_ln2_g, l0_ln2_b, B=B, T=T)
    out = _odd_layer(h, hb, l1_w_dq_dkv, l1_q_norm_g, l1_w_uq, l1_kv_norm_g, l1_w_ukv, l1_w_out,
                     l1_ln1_g, l1_ln1_b, l1_router, l1_moe_w1, l1_moe_w3, l1_moe_w2, l1_ln2_g, l1_ln2_b, B=B, T=T)
    return out.reshape(B, T, D)
```

```python
import functools

import jax
import jax.numpy as jnp
from jax import lax
from jax.experimental import pallas as pl
from jax.experimental.pallas import tpu as pltpu

F32 = jnp.float32
BF16 = jnp.bfloat16
I32 = jnp.int32

A_HEAD_DIM = 128
A_KV_HEADS = 4
IDX_HEADS = 16
IDX_DIM = 64
DSA_TOPK_MAX = 256
RET_KEY_DIM = 256
RET_VAL_DIM = 256
RET_CHUNK = 128
RET_THETA = 10000.0
MLA_V = 128
MLA_NOPE = 128
MLA_ROPE = 64
ROPE_THETA = 500000.0
Q_BLOCK = 128
LN_EPS = 1e-5
RMS_EPS = 1e-6
DEPTH = 2
ALPHA = (2.0 * DEPTH) ** 0.25

LANE = 128
V7X_VMEM_BYTES = 64 * 1024 * 1024
VMEM_LIMIT = V7X_VMEM_BYTES - 8 * 1024 * 1024
MASKED = -1e30
INT_MIN = -(2 ** 31)

NT_DIMS = (((1,), (1,)), ((), ()))
TN_DIMS = (((0,), (0,)), ((), ()))


def _tile(n, pref, mult=LANE):
    if n <= pref:
        return n
    t = (pref // mult) * mult
    while t > mult and n % t:
        t -= mult
    assert n % t == 0, (n, pref, mult)
    return t


def _params(*sem):
    return pltpu.CompilerParams(dimension_semantics=sem, vmem_limit_bytes=VMEM_LIMIT)


def _rope_cos_sin(T, rot_dim, theta):
    inv = theta ** (-jnp.arange(0, rot_dim, 2, dtype=F32) / rot_dim)
    ang = jnp.arange(T, dtype=F32)[:, None] * inv[None, :]
    return jnp.cos(ang), jnp.sin(ang)


def _lane_tables(cos, sin, head_dim, scale=1.0):
    T, half = cos.shape
    rest = head_dim - 2 * half
    zh = jnp.zeros((T, half), F32)
    c = jnp.concatenate([cos, cos, jnp.ones((T, rest), F32)], 1)
    sa = jnp.concatenate([-sin, zh, jnp.zeros((T, rest), F32)], 1)
    sb = jnp.concatenate([zh, sin, jnp.zeros((T, rest), F32)], 1)
    reps = LANE // head_dim
    return tuple(jnp.tile(t * scale, (1, reps)) for t in (c, sa, sb))


def _proj_kernel(*refs, slab_pat, mode, half, scale, with_tab, two_inputs, w_rows_out=False):
    x_ref, w_ref = refs[:2]
    if w_rows_out:
        acc = lax.dot_general(x_ref[...], w_ref[...], NT_DIMS, preferred_element_type=F32)
    else:
        acc = jnp.dot(x_ref[...], w_ref[...], preferred_element_type=F32)
    refs = refs[2:]
    if two_inputs:
        acc = acc + jnp.dot(refs[0][...], refs[1][...], preferred_element_type=F32)
        refs = refs[2:]
    if with_tab:
        c_ref, sa_ref, sb_ref, o_ref = refs
    else:
        (o_ref,) = refs
    for s, p in enumerate(slab_pat):
        a = acc[:, s * LANE:(s + 1) * LANE]
        if p < 0:
            out = a if scale == 1.0 else a * scale
        else:
            c = c_ref[:, p * LANE:(p + 1) * LANE]
            sa = sa_ref[:, p * LANE:(p + 1) * LANE]
            if mode == "lane":
                sb = sb_ref[:, p * LANE:(p + 1) * LANE]
                out = a * c + pltpu.roll(a, LANE - half, 1) * sa + pltpu.roll(a, half, 1) * sb
            else:
                q = s ^ 1
                out = a * c + acc[:, q * LANE:(q + 1) * LANE] * sa
        o_ref[:, s * LANE:(s + 1) * LANE] = out.astype(o_ref.dtype)


def _proj(x, w, *, out_dtype, bm=1024, bn=1024, tabs=None, slab_pat=None, mode="lane", half=0,
          scale=1.0, seq_len=None, second=None, w_rows_out=False, name="proj"):
    M, K = x.shape
    N = w.shape[0] if w_rows_out else w.shape[1]
    bm = _tile(M, bm) if seq_len is None else _tile(seq_len, bm)
    bn = _tile(N, bn)
    if slab_pat is None:
        slab_pat = (-1,) * (bn // LANE)
    assert len(slab_pat) == bn // LANE
    w_spec = pl.BlockSpec((bn, K), lambda i, j: (j, 0)) if w_rows_out else pl.BlockSpec((K, bn), lambda i, j: (0, j))
    in_specs = [pl.BlockSpec((bm, K), lambda i, j: (i, 0)), w_spec]
    args = [x, w]
    if second is not None:
        x2, w2 = second
        in_specs += [pl.BlockSpec((bm, x2.shape[1]), lambda i, j: (i, 0)),
                     pl.BlockSpec((x2.shape[1], bn), lambda i, j: (0, j))]
        args += [x2, w2]
    if tabs is not None:
        tb = seq_len // bm
        tw = tabs[0].shape[1]
        in_specs += [pl.BlockSpec((bm, tw), lambda i, j: (i % tb, 0))] * 3
        args += list(tabs)
    kern = functools.partial(_proj_kernel, slab_pat=tuple(slab_pat), mode=mode, half=half,
                             scale=scale, with_tab=tabs is not None, two_inputs=second is not None,
                             w_rows_out=w_rows_out)
    return pl.pallas_call(
        kern,
        grid=(M // bm, N // bn),
        in_specs=in_specs,
        out_specs=pl.BlockSpec((bm, bn), lambda i, j: (i, j)),
        out_shape=jax.ShapeDtypeStruct((M, N), out_dtype),
        compiler_params=_params("parallel", "parallel"),
        name=name,
    )(*args)


def _mm_ksplit_kernel(x_ref, w_ref, o_ref):
    part = jnp.dot(x_ref[...], w_ref[...], preferred_element_type=F32)

    @pl.when(pl.program_id(2) == 0)
    def _():
        o_ref[...] = part

    @pl.when(pl.program_id(2) > 0)
    def _():
        o_ref[...] += part


def _mm_ksplit(x, w, *, bm=1024, bn=1024, bk=2048, name="mm_ksplit"):
    M, K = x.shape
    N = w.shape[1]
    bm, bn, bk = _tile(M, bm), _tile(N, bn), _tile(K, bk)
    return pl.pallas_call(
        _mm_ksplit_kernel,
        grid=(M // bm, N // bn, K // bk),
        in_specs=[pl.BlockSpec((bm, bk), lambda i, j, k: (i, k)),
                  pl.BlockSpec((bk, bn), lambda i, j, k: (k, j))],
        out_specs=pl.BlockSpec((bm, bn), lambda i, j, k: (i, j)),
        out_shape=jax.ShapeDtypeStruct((M, N), F32),
        compiler_params=_params("parallel", "parallel", "arbitrary"),
        name=name,
    )(x, w)


def _swiglu_up_kernel(x_ref, w1_ref, w3_ref, o_ref):
    x = x_ref[...]
    a = jnp.dot(x, w1_ref[...], preferred_element_type=F32)
    b = jnp.dot(x, w3_ref[...], preferred_element_type=F32)
    o_ref[...] = (a * jax.nn.sigmoid(a) * b).astype(o_ref.dtype)


def _swiglu_up(x, w1, w3, *, bm=1024, bn=512):
    M, K = x.shape
    N = w1.shape[1]
    bm, bn = _tile(M, bm), _tile(N, bn)
    return pl.pallas_call(
        _swiglu_up_kernel,
        grid=(M // bm, N // bn),
        in_specs=[pl.BlockSpec((bm, K), lambda i, j: (i, 0)),
                  pl.BlockSpec((K, bn), lambda i, j: (0, j)),
                  pl.BlockSpec((K, bn), lambda i, j: (0, j))],
        out_specs=pl.BlockSpec((bm, bn), lambda i, j: (i, j)),
        out_shape=jax.ShapeDtypeStruct((M, N), BF16),
        compiler_params=_params("parallel", "parallel"),
        name="swiglu_up",
    )(x, w1, w3)


def _layer_norm_rows(z, g, b):
    mu = jnp.mean(z, axis=-1, keepdims=True)
    zc = z - mu
    var = jnp.mean(zc * zc, axis=-1, keepdims=True)
    return zc * lax.rsqrt(var + LN_EPS) * g + b


def _pack_bf16_pairs(x):
    half = x.shape[1] // 2
    hi = pltpu.bitcast(x[:, :half].astype(BF16).astype(F32), I32)
    lo = pltpu.bitcast(x[:, half:].astype(BF16).astype(F32), I32)
    return hi | lax.shift_right_logical(lo, 16)


def _unpack_bf16_pairs(w):
    hi = pltpu.bitcast(w & jnp.int32(-65536), F32).astype(BF16)
    lo = pltpu.bitcast(lax.shift_left(w, 16), F32).astype(BF16)
    return hi, lo


def _add_ln_kernel(x_ref, y_ref, g_ref, b_ref, *rest, packed, with_router):
    out = _layer_norm_rows(ALPHA * x_ref[...] + y_ref[...], g_ref[...], b_ref[...])
    if with_router:
        r_ref, of_ref, o2_ref, lg_ref = rest
        lg_ref[...] = jnp.dot(out, r_ref[...], preferred_element_type=F32, precision=lax.Precision.HIGHEST)
    else:
        of_ref, o2_ref = rest
    of_ref[...] = out
    o2_ref[...] = _pack_bf16_pairs(out) if packed else out.astype(o2_ref.dtype)


def _add_ln(x, y, g, b, *, packed=False, router=None, bm=256):
    M, D = x.shape
    bm = _tile(M, bm, 8)
    row = pl.BlockSpec((bm, D), lambda i: (i, 0))
    vec = pl.BlockSpec((1, D), lambda i: (0, 0))
    second = jax.ShapeDtypeStruct((M, D // 2), I32) if packed else jax.ShapeDtypeStruct((M, D), BF16)
    in_specs, args = [row, row, vec, vec], [x, y, g.reshape(1, D), b.reshape(1, D)]
    out_specs = [row, pl.BlockSpec((bm, second.shape[1]), lambda i: (i, 0))]
    out_shape = [jax.ShapeDtypeStruct((M, D), F32), second]
    if router is not None:
        in_specs.append(pl.BlockSpec((D, LANE), lambda i: (0, 0)))
        args.append(jnp.zeros((D, LANE), F32).at[:, :router.shape[1]].set(router))
        out_specs.append(pl.BlockSpec((bm, LANE), lambda i: (i, 0)))
        out_shape.append(jax.ShapeDtypeStruct((M, LANE), F32))
    return pl.pallas_call(
        functools.partial(_add_ln_kernel, packed=packed, with_router=router is not None),
        grid=(M // bm,),
        in_specs=in_specs,
        out_specs=out_specs,
        out_shape=out_shape,
        compiler_params=_params("parallel"),
        name="add_ln",
    )(*args)


LOG2E = 1.4426950408889634


def _online_softmax(s, m, l):
    m_new = jnp.maximum(m, jnp.max(s, axis=1, keepdims=True))
    a = jnp.exp2(m - m_new)
    p = jnp.exp2((s - m_new).astype(BF16))
    return m_new, a * l + jnp.sum(p.astype(F32), axis=1, keepdims=True), a, p


def _dsa_kernel(q_ref, k_ref, v_ref, iq_ref, ik_ref, o_ref, keys_ref, bias_ref, *, topk, ck, n_rep, idx_bits):
    blk = pl.program_id(1)
    n_chunks = (blk * Q_BLOCK + Q_BLOCK + ck - 1) // ck
    row = lax.broadcasted_iota(I32, (Q_BLOCK, ck), 0) + blk * Q_BLOCK
    lane = lax.broadcasted_iota(I32, (Q_BLOCK, ck), 1)
    w_off = IDX_HEADS * IDX_DIM + IDX_DIM
    iq = iq_ref[0]
    wi = iq[:, w_off:w_off + IDX_HEADS]
    q_idx = jnp.concatenate([iq[:, h * IDX_DIM:(h + 1) * IDX_DIM].astype(BF16) for h in range(IDX_HEADS)], axis=0)

    def score_body(c, carry):
        off = pl.multiple_of(c * ck, ck)
        kc = ik_ref[0, pl.ds(off, ck), :][:, :IDX_DIM].astype(BF16)
        lg = lax.dot_general(q_idx, kc, NT_DIMS, preferred_element_type=F32)
        s = jnp.zeros((Q_BLOCK, ck), F32)
        for h in range(IDX_HEADS):
            s = s + wi[:, h:h + 1] * jnp.maximum(lg[h * Q_BLOCK:(h + 1) * Q_BLOCK], 0.0)
        bits = pltpu.bitcast(s, I32)
        key = bits ^ ((bits >> 31) & 0x7FFFFFFF)
        keys_ref[c] = jnp.where(lane + off <= row, key, INT_MIN)
        return carry

    lax.fori_loop(0, n_chunks, score_body, 0)

    def count(indicator):
        def body(c, acc):
            part = indicator(keys_ref[c], lane + c * ck)
            for j in range(ck // LANE):
                acc = acc + part[:, j * LANE:(j + 1) * LANE]
            return acc
        acc = lax.fori_loop(0, n_chunks, body, jnp.zeros((Q_BLOCK, LANE), F32))
        return jnp.sum(acc, axis=1, keepdims=True)

    kf = float(topk)
    ok = count(lambda kc, idx: jnp.where(kc >= 0, 1.0, 0.0)) >= kf
    thr = jnp.where(ok, 0, INT_MIN).astype(I32)

    def bit_body(i, thr):
        cand = thr + lax.shift_left(jnp.int32(1), 30 - i)
        ok = count(lambda kc, idx: jnp.where(kc >= cand, 1.0, 0.0)) >= kf
        return jnp.where(ok, cand, thr)

    thr = lax.fori_loop(0, 31, bit_body, thr)

    n_gt = count(lambda kc, idx: jnp.where(kc > thr, 1.0, 0.0))
    n_ge = count(lambda kc, idx: jnp.where(kc >= thr, 1.0, 0.0))
    need = kf - n_gt
    has_thr = thr > INT_MIN
    surplus = jnp.where(has_thr, n_ge - n_gt - need, 0.0)

    def tie_search():
        def tie_body(i, last):
            cand = last + lax.shift_left(jnp.int32(1), idx_bits - 1 - i)
            ok = count(lambda kc, idx: jnp.where(kc == thr, jnp.where(idx < cand, 1.0, 0.0), 0.0)) < need
            return jnp.where(ok, cand, last)
        return lax.fori_loop(0, idx_bits, tie_body, jnp.zeros((Q_BLOCK, 1), I32))

    last = lax.cond(jnp.max(surplus) > 0.0, tie_search, lambda: jnp.full((Q_BLOCK, 1), 2 ** idx_bits, I32))
    last = jnp.where(has_thr, last, -1)

    def bias_body(c, carry):
        kc = keys_ref[c]
        tie_bias = jnp.where(lane + c * ck <= last, 0.0, MASKED)
        bias_ref[c] = jnp.where(kc == thr, tie_bias, jnp.where(kc > thr, 0.0, MASKED))
        return carry

    lax.fori_loop(0, n_chunks, bias_body, 0)

    q = q_ref[0]
    rows = n_rep * Q_BLOCK
    qgs = [jnp.concatenate([q[:, (g * n_rep + r) * A_HEAD_DIM:(g * n_rep + r + 1) * A_HEAD_DIM]
                            for r in range(n_rep)], axis=0) for g in range(A_KV_HEADS)]

    def att_body(c, carry, n_sub):
        width = n_sub * ck
        off = pl.multiple_of(c * width, width)
        b = jnp.concatenate([bias_ref[c * n_sub + t] for t in range(n_sub)], axis=1)
        bias = jnp.concatenate([b] * n_rep, axis=0)
        out = []
        for g in range(A_KV_HEADS):
            m, l, acc = carry[g]
            kc = k_ref[0, pl.ds(off, width), g * A_HEAD_DIM:(g + 1) * A_HEAD_DIM]
            vc = v_ref[0, pl.ds(off, width), g * A_HEAD_DIM:(g + 1) * A_HEAD_DIM]
            s = lax.dot_general(qgs[g], kc, NT_DIMS, preferred_element_type=F32) + bias
            m, l, a, p = _online_softmax(s, m, l)
            out.append((m, l, a * acc + jnp.dot(p, vc, preferred_element_type=F32)))
        return tuple(out)

    one = (jnp.full((rows, 1), MASKED, F32), jnp.zeros((rows, 1), F32), jnp.zeros((rows, A_HEAD_DIM), F32))
    carry = (one,) * A_KV_HEADS
    if k_ref.shape[1] >= 2 * ck:
        carry = lax.fori_loop(0, n_chunks // 2, functools.partial(att_body, n_sub=2), carry)
        carry = lax.cond(n_chunks % 2 == 1, lambda c: att_body(n_chunks - 1, c, 1), lambda c: c, carry)
    else:
        carry = att_body(0, carry, 1)
    for g in range(A_KV_HEADS):
        _, l, acc = carry[g]
        o = acc / l
        for r in range(n_rep):
            col = (g * n_rep + r) * A_HEAD_DIM
            o_ref[0, :, col:col + A_HEAD_DIM] = o[r * Q_BLOCK:(r + 1) * Q_BLOCK].astype(o_ref.dtype)


def _dsa(q, k, v, idx, *, B, T):
    a_heads = q.shape[-1] // A_HEAD_DIM
    n_rep = a_heads // A_KV_HEADS
    topk = min(DSA_TOPK_MAX, T // 4)
    ck = _tile(T, 512)
    idx_w = idx.shape[-1]
    kv_w = k.shape[-1]
    kern = functools.partial(_dsa_kernel, topk=topk, ck=ck, n_rep=n_rep, idx_bits=max(1, (T - 1).bit_length()))
    return pl.pallas_call(
        kern,
        grid=(B, T // Q_BLOCK),
        in_specs=[pl.BlockSpec((1, Q_BLOCK, q.shape[-1]), lambda b, i: (b, i, 0)),
                  pl.BlockSpec((1, T, kv_w), lambda b, i: (b, 0, 0)),
                  pl.BlockSpec((1, T, kv_w), lambda b, i: (b, 0, 0)),
                  pl.BlockSpec((1, Q_BLOCK, idx_w), lambda b, i: (b, i, 0)),
                  pl.BlockSpec((1, T, LANE), lambda b, i: (b, 0, IDX_HEADS * IDX_DIM // LANE))],
        out_specs=pl.BlockSpec((1, Q_BLOCK, q.shape[-1]), lambda b, i: (b, i, 0)),
        out_shape=jax.ShapeDtypeStruct(q.shape, BF16),
        scratch_shapes=[pltpu.VMEM((T // ck, Q_BLOCK, ck), I32), pltpu.VMEM((T // ck, Q_BLOCK, ck), F32)],
        compiler_params=_params("parallel", "parallel"),
        name="dsa",
    )(q, k, v, idx, idx)


RET_HEADS_PER_STEP = 2


def _retention_kernel(q_ref, k_ref, v_ref, g_ref, gn_ref, din_ref, qd_ref, kd_ref, cd_ref, o_ref, state_ref, *, n_sub):
    @pl.when(pl.program_id(2) == 0)
    def _():
        state_ref[...] = jnp.zeros_like(state_ref)

    W = RET_VAL_DIM
    for s in range(n_sub):
        sl = pl.ds(s * RET_CHUNK, RET_CHUNK)
        for j in range(RET_HEADS_PER_STEP):
            cols = slice(j * W, (j + 1) * W)
            qc = q_ref[sl, cols]
            kc = k_ref[sl, cols]
            vc = v_ref[sl, cols]
            st = state_ref[j]
            inner = lax.dot_general(qc, kc, NT_DIMS, preferred_element_type=F32) * din_ref[j]
            o = (jnp.dot(inner.astype(BF16), vc, preferred_element_type=F32)
                 + jnp.dot(qc, st.astype(BF16), preferred_element_type=F32) * qd_ref[j])
            vk = (vc.astype(F32) * kd_ref[j]).astype(BF16)
            state_ref[j] = st * cd_ref[j] + lax.dot_general(kc, vk, TN_DIMS, preferred_element_type=F32)
            mu = jnp.mean(o, axis=-1, keepdims=True)
            oc = o - mu
            var = jnp.mean(oc * oc, axis=-1, keepdims=True)
            gate = g_ref[sl, cols].astype(F32)
            normed = oc * lax.rsqrt(var + LN_EPS) * gn_ref[:, cols]
            o_ref[sl, cols] = (gate * jax.nn.sigmoid(gate) * normed).astype(o_ref.dtype)


def _retention(qk, pv, gn_g, *, B, T, heads, v_blk0):
    N = qk.shape[0]
    C = RET_CHUNK
    rb = _tile(T, 512)
    n_sub = rb // C
    nr = T // rb
    log_gamma = jnp.log(1.0 - 2.0 ** (-5.0 - jnp.arange(heads, dtype=F32)))
    pos = jnp.arange(C, dtype=F32)
    diff = pos[:, None] - pos[None, :]
    din = jnp.exp(jnp.where(diff[None] >= 0, log_gamma[:, None, None] * diff[None], -jnp.inf))
    qd = jnp.exp(log_gamma[:, None] * (pos[None] + 1.0))[:, :, None]
    kd = jnp.exp(log_gamma[:, None] * (C - 1.0 - pos[None]))[:, :, None]
    cd = jnp.exp(log_gamma * C)[:, None, None]
    hp = RET_HEADS_PER_STEP
    W = hp * RET_VAL_DIM
    assert heads % hp == 0 and v_blk0 % hp == 0
    blk = lambda off: pl.BlockSpec((rb, W), lambda b, h, r: (b * nr + r, off // hp + h))
    per_head = lambda shape: pl.BlockSpec((hp,) + shape, lambda b, h, r: (h, 0, 0))
    return pl.pallas_call(
        functools.partial(_retention_kernel, n_sub=n_sub),
        grid=(B, heads // hp, nr),
        in_specs=[blk(0), blk(heads), blk(v_blk0), blk(v_blk0 + heads),
                  pl.BlockSpec((1, W), lambda b, h, r: (0, h)),
                  per_head((C, C)), per_head((C, 1)), per_head((C, 1)), per_head((1, 1))],
        out_specs=pl.BlockSpec((rb, W), lambda b, h, r: (b * nr + r, h)),
        out_shape=jax.ShapeDtypeStruct((N, heads * RET_VAL_DIM), BF16),
        scratch_shapes=[pltpu.VMEM((hp, RET_KEY_DIM, RET_VAL_DIM), F32)],
        compiler_params=_params("parallel", "parallel", "arbitrary"),
        name="retention",
    )(qk, qk, pv, pv, gn_g.reshape(1, heads * RET_VAL_DIM), din, qd, kd, cd)


def _mla_down_kernel(x_ref, w_ref, qg_ref, kvg_ref, c_ref, sa_ref, sb_ref, cq_ref, ckv_ref, kr_ref, *, q_rank, kv_rank):
    acc = jnp.dot(x_ref[...], w_ref[...], preferred_element_type=F32)

    def rms(a, g):
        return a * lax.rsqrt(jnp.mean(a * a, axis=-1, keepdims=True) + RMS_EPS) * g

    cq_ref[...] = rms(acc[:, :q_rank], qg_ref[...]).astype(cq_ref.dtype)
    ckv_ref[...] = rms(acc[:, q_rank:q_rank + kv_rank], kvg_ref[...]).astype(ckv_ref.dtype)
    kr = acc[:, q_rank + kv_rank:]
    half = MLA_ROPE // 2
    kr = kr * c_ref[...] + pltpu.roll(kr, LANE - half, 1) * sa_ref[...] + pltpu.roll(kr, half, 1) * sb_ref[...]
    kr_ref[...] = kr.astype(kr_ref.dtype)


def _mla_down(x, w, q_g, kv_g, tabs, *, T, q_rank, kv_rank, bm=512):
    M, K = x.shape
    Nw = w.shape[1]
    bm = _tile(T, bm)
    tb = T // bm
    row = lambda n: pl.BlockSpec((bm, n), lambda i: (i, 0))
    tab = pl.BlockSpec((bm, LANE), lambda i: (i % tb, 0))
    return pl.pallas_call(
        functools.partial(_mla_down_kernel, q_rank=q_rank, kv_rank=kv_rank),
        grid=(M // bm,),
        in_specs=[row(K), pl.BlockSpec((K, Nw), lambda i: (0, 0)),
                  pl.BlockSpec((1, q_rank), lambda i: (0, 0)), pl.BlockSpec((1, kv_rank), lambda i: (0, 0)),
                  tab, tab, tab],
        out_specs=[row(q_rank), row(kv_rank), row(LANE)],
        out_shape=[jax.ShapeDtypeStruct((M, q_rank), BF16), jax.ShapeDtypeStruct((M, kv_rank), BF16),
                   jax.ShapeDtypeStruct((M, LANE), BF16)],
        compiler_params=_params("parallel"),
        name="mla_down",
    )(x, w, q_g.reshape(1, q_rank), kv_g.reshape(1, kv_rank), *tabs)


MLA_HEADS_PER_STEP = 4


def _mla_attn_kernel(qn_ref, qr_ref, kn_ref, kr_ref, v_ref, o_ref, *, tq):
    i = pl.program_id(2)
    lane = lax.broadcasted_iota(I32, (tq, LANE), 1)
    qs = []
    for j in range(MLA_HEADS_PER_STEP):
        lo = (j % 2) * MLA_ROPE
        own = jnp.where((lane >= lo) & (lane < lo + MLA_ROPE), 1.0, 0.0)
        pair = qr_ref[:, (j // 2) * LANE:(j // 2 + 1) * LANE].astype(F32)
        qr = (pair * own).astype(BF16)
        qs.append(jnp.concatenate([qn_ref[:, j * LANE:(j + 1) * LANE], qr], axis=1))

    def step(c, carry, masked, width):
        off = pl.multiple_of(c * width, width)
        kr = kr_ref[pl.ds(off, width), :]
        out = []
        for j in range(MLA_HEADS_PER_STEP):
            m, l, acc = carry[j]
            k = jnp.concatenate([kn_ref[pl.ds(off, width), j * LANE:(j + 1) * LANE], kr], axis=1)
            s = lax.dot_general(qs[j], k, NT_DIMS, preferred_element_type=F32)
            if masked:
                r_io = lax.broadcasted_iota(I32, (tq, width), 0)
                c_io = lax.broadcasted_iota(I32, (tq, width), 1)
                s = jnp.where(c_io <= r_io, s, MASKED)
            m, l, a, p = _online_softmax(s, m, l)
            v = v_ref[pl.ds(off, width), j * MLA_V:(j + 1) * MLA_V]
            out.append((m, l, a * acc + jnp.dot(p, v, preferred_element_type=F32)))
        return tuple(out)

    one = (jnp.full((tq, 1), MASKED, F32), jnp.zeros((tq, 1), F32), jnp.zeros((tq, MLA_V), F32))
    carry = (one,) * MLA_HEADS_PER_STEP
    if kn_ref.shape[0] >= 2 * tq:
        carry = lax.fori_loop(0, i // 2, functools.partial(step, masked=False, width=2 * tq), carry)
        carry = lax.cond(i % 2 == 1, lambda c: step(i - 1, c, False, tq), lambda c: c, carry)
    carry = step(i, carry, True, tq)
    for j in range(MLA_HEADS_PER_STEP):
        _, l, acc = carry[j]
        o_ref[:, j * MLA_V:(j + 1) * MLA_V] = (acc / l).astype(o_ref.dtype)


def _mla_attn(qn, qr, kv, kr, *, B, T, heads):
    N = qn.shape[0]
    tq = _tile(T, 512)
    nq = T // tq
    hp = MLA_HEADS_PER_STEP
    w = hp * LANE
    return pl.pallas_call(
        functools.partial(_mla_attn_kernel, tq=tq),
        grid=(B, heads // hp, nq),
        in_specs=[pl.BlockSpec((tq, w), lambda b, h, i: (b * nq + i, h)),
                  pl.BlockSpec((tq, hp * MLA_ROPE), lambda b, h, i: (b * nq + i, h)),
                  pl.BlockSpec((T, w), lambda b, h, i: (b, h)),
                  pl.BlockSpec((T, LANE), lambda b, h, i: (b, 0)),
                  pl.BlockSpec((T, w), lambda b, h, i: (b, heads // hp + h))],
        out_specs=pl.BlockSpec((tq, w), lambda b, h, i: (b * nq + i, h)),
        out_shape=jax.ShapeDtypeStruct((N, heads * MLA_V), BF16),
        compiler_params=_params("parallel", "parallel", "parallel"),
        name="mla_attn",
    )(qn, qr, kv, kr, kv)


def _router_kernel(lg_ref, meta_ref, cnt_ref, carry_ref, *, n_exp):
    @pl.when(pl.program_id(0) == 0)
    def _():
        carry_ref[...] = jnp.zeros_like(carry_ref)

    bm = lg_ref.shape[0]
    lane = lax.broadcasted_iota(I32, (bm, LANE), 1).astype(F32)
    logits = jnp.where(lane < n_exp, lg_ref[...], -jnp.inf)
    m1 = jnp.max(logits, axis=1, keepdims=True)
    i1 = jnp.min(jnp.where(logits == m1, lane, float(LANE)), axis=1, keepdims=True)
    rest = jnp.where(lane == i1, -jnp.inf, logits)
    m2 = jnp.max(rest, axis=1, keepdims=True)
    i2 = jnp.min(jnp.where(rest == m2, lane, float(LANE)), axis=1, keepdims=True)
    e = jnp.exp(m2 - m1)
    g1 = 1.0 / (1.0 + e)
    g2 = e / (1.0 + e)
    sel = jnp.where(lane == i1, 1.0, jnp.where(lane == i2, 1.0, 0.0))
    r_io = lax.broadcasted_iota(I32, (bm, bm), 0)
    c_io = lax.broadcasted_iota(I32, (bm, bm), 1)
    below = jnp.where(c_io < r_io, 1.0, 0.0).astype(BF16)
    carry = carry_ref[0:1, :]
    rank = jnp.dot(below, sel.astype(BF16), preferred_element_type=F32) + carry
    r1 = jnp.sum(jnp.where(lane == i1, rank, 0.0), axis=1, keepdims=True)
    r2 = jnp.sum(jnp.where(lane == i2, rank, 0.0), axis=1, keepdims=True)
    meta = jnp.where(lane == 0, i1, 0.0)
    meta = jnp.where(lane == 1, i2, meta)
    meta = jnp.where(lane == 2, g1, meta)
    meta = jnp.where(lane == 3, g2, meta)
    meta = jnp.where(lane == 4, r1, meta)
    meta = jnp.where(lane == 5, r2, meta)
    meta_ref[...] = meta
    total = carry + jnp.sum(sel, axis=0, keepdims=True)
    carry_ref[...] = jnp.broadcast_to(total, carry_ref.shape)
    cnt_ref[...] = jnp.broadcast_to(total, cnt_ref.shape)


def _router(logits, n_exp, *, bm=512):
    M = logits.shape[0]
    bm = _tile(M, bm)
    return pl.pallas_call(
        functools.partial(_router_kernel, n_exp=n_exp),
        grid=(M // bm,),
        in_specs=[pl.BlockSpec((bm, LANE), lambda i: (i, 0))],
        out_specs=[pl.BlockSpec((bm, LANE), lambda i: (i, 0)), pl.BlockSpec((8, LANE), lambda i: (0, 0))],
        out_shape=[jax.ShapeDtypeStruct((M, LANE), F32), jax.ShapeDtypeStruct((8, LANE), F32)],
        scratch_shapes=[pltpu.VMEM((8, LANE), F32)],
        compiler_params=_params("arbitrary"),
        name="router",
    )(logits)


DMA_LOOP_UNROLL = 8


def _moe_gather_kernel(tok_ref, nv_ref, x_hbm, o_ref, stage_ref, sem, *, tm):
    r = pl.program_id(0)
    nv = nv_ref[0]

    half = stage_ref.shape[1]

    def row_copy(j, tok):
        return pltpu.make_async_copy(x_hbm.at[pl.ds(tok, 1)], stage_ref.at[pl.ds(j, 1)], sem)

    def gather_start(tile):
        def body(j, carry):
            row_copy(j, tok_ref[tile * tm + j]).start()
            return carry
        lax.fori_loop(0, tm, body, 0, unroll=DMA_LOOP_UNROLL)

    def gather_wait():
        def body(j, carry):
            row_copy(j, 0).wait()
            return carry
        lax.fori_loop(0, tm, body, 0, unroll=DMA_LOOP_UNROLL)

    @pl.when(r < nv)
    def _():
        @pl.when(r == 0)
        def _():
            gather_start(0)

        gather_wait()
        hi, lo = _unpack_bf16_pairs(stage_ref[...])
        o_ref[:, :half] = hi
        o_ref[:, half:] = lo

        @pl.when(r + 1 < nv)
        def _():
            gather_start(r + 1)


def _moe_up_kernel(te_ref, nv_ref, xs_ref, w1_ref, w3_ref, o_ref, wb1_ref, wb3_ref):
    r = pl.program_id(1)

    @pl.when(r < nv_ref[0])
    def _():
        @pl.when((r == 0) | (te_ref[r] != te_ref[jnp.maximum(r - 1, 0)]))
        def _():
            wb1_ref[...] = w1_ref[0].astype(BF16)
            wb3_ref[...] = w3_ref[0].astype(BF16)

        xb = xs_ref[...]
        a = jnp.dot(xb, wb1_ref[...], preferred_element_type=F32)
        b = jnp.dot(xb, wb3_ref[...], preferred_element_type=F32)
        o_ref[...] = (a * jax.nn.sigmoid(a) * b).astype(o_ref.dtype)


def _moe_down_kernel(te_ref, nv_ref, h_ref, w2_ref, o_ref, wb_ref):
    r = pl.program_id(1)

    @pl.when(r < nv_ref[0])
    def _():
        @pl.when((r == 0) | (te_ref[r] != te_ref[jnp.maximum(r - 1, 0)]))
        def _():
            wb_ref[...] = w2_ref[0].astype(BF16)

        o_ref[...] = jnp.dot(h_ref[...], wb_ref[...], preferred_element_type=F32)


def _experts(x, w1, w3, w2, tile_expert, n_valid, row_tok, *, tm, tf=512, tn=512):
    D = 2 * x.shape[1]
    P = row_tok.shape[0]
    E, _, F = w1.shape
    tf, tn = _tile(F, tf), _tile(D, tn)
    n_tiles = P // tm
    row = lambda r, nv: jnp.minimum(r, nv[0] - 1)

    xs = pl.pallas_call(
        functools.partial(_moe_gather_kernel, tm=tm),
        grid_spec=pltpu.PrefetchScalarGridSpec(
            num_scalar_prefetch=2, grid=(n_tiles,),
            in_specs=[pl.BlockSpec(memory_space=pl.ANY)],
            out_specs=pl.BlockSpec((tm, D), lambda r, tok, nv: (row(r, nv), 0)),
            scratch_shapes=[pltpu.VMEM((tm, D // 2), I32), pltpu.SemaphoreType.DMA(())]),
        out_shape=jax.ShapeDtypeStruct((P, D), BF16),
        compiler_params=_params("arbitrary"),
        name="moe_gather",
    )(row_tok, n_valid, x)

    hid = pl.pallas_call(
        _moe_up_kernel,
        grid_spec=pltpu.PrefetchScalarGridSpec(
            num_scalar_prefetch=2, grid=(F // tf, n_tiles),
            in_specs=[pl.BlockSpec((tm, D), lambda f, r, te, nv: (row(r, nv), 0)),
                      pl.BlockSpec((1, D, tf), lambda f, r, te, nv: (te[r], 0, f)),
                      pl.BlockSpec((1, D, tf), lambda f, r, te, nv: (te[r], 0, f))],
            out_specs=pl.BlockSpec((tm, tf), lambda f, r, te, nv: (row(r, nv), f)),
            scratch_shapes=[pltpu.VMEM((D, tf), BF16), pltpu.VMEM((D, tf), BF16)]),
        out_shape=jax.ShapeDtypeStruct((P, F), BF16),
        compiler_params=_params("arbitrary", "arbitrary"),
        name="moe_up",
    )(tile_expert, n_valid, xs, w1, w3)

    return pl.pallas_call(
        _moe_down_kernel,
        grid_spec=pltpu.PrefetchScalarGridSpec(
            num_scalar_prefetch=2, grid=(D // tn, n_tiles),
            in_specs=[pl.BlockSpec((tm, F), lambda n, r, te, nv: (row(r, nv), 0)),
                      pl.BlockSpec((1, F, tn), lambda n, r, te, nv: (te[r], 0, n))],
            out_specs=pl.BlockSpec((tm, tn), lambda n, r, te, nv: (row(r, nv), n)),
            scratch_shapes=[pltpu.VMEM((F, tn), BF16)]),
        out_shape=jax.ShapeDtypeStruct((P, D), F32),
        compiler_params=_params("arbitrary", "arbitrary"),
        name="moe_down",
    )(tile_expert, n_valid, hid, w2)


def _combine_kernel(dest_ref, x_ref, meta_ref, g_ref, b_ref, ys_hbm, o_ref, buf_ref, sem):
    bm = x_ref.shape[0]
    i = pl.program_id(0)
    n = pl.num_programs(0)

    def copy(slot, j, k, d):
        return pltpu.make_async_copy(ys_hbm.at[pl.ds(d, 1)], buf_ref.at[slot, k, pl.ds(j, 1)], sem.at[slot, k])

    def gather_start(blk):
        slot = blk % 2

        def body(j, carry):
            for k in range(2):
                copy(slot, j, k, dest_ref[2 * (blk * bm + j) + k]).start()
            return carry
        lax.fori_loop(0, bm, body, 0, unroll=DMA_LOOP_UNROLL)

    @pl.when(i == 0)
    def _():
        gather_start(0)

    @pl.when(i + 1 < n)
    def _():
        gather_start(i + 1)

    slot = i % 2

    def wait(j, carry):
        for k in range(2):
            copy(slot, j, k, 0).wait()
        return carry

    lax.fori_loop(0, bm, wait, 0, unroll=DMA_LOOP_UNROLL)
    meta = meta_ref[...]
    y = meta[:, 2:3] * buf_ref[slot, 0] + meta[:, 3:4] * buf_ref[slot, 1]
    o_ref[...] = _layer_norm_rows(ALPHA * x_ref[...] + y, g_ref[...], b_ref[...])


def _combine(x, meta, ys, dest, g, b, *, bm=256):
    M, D = x.shape
    bm = _tile(M, bm, 8)
    row = lambda n: pl.BlockSpec((bm, n), lambda i, d: (i, 0))
    vec = pl.BlockSpec((1, D), lambda i, d: (0, 0))
    return pl.pallas_call(
        _combine_kernel,
        grid_spec=pltpu.PrefetchScalarGridSpec(
            num_scalar_prefetch=1, grid=(M // bm,),
            in_specs=[row(D), row(LANE), vec, vec, pl.BlockSpec(memory_space=pl.ANY)],
            out_specs=row(D),
            scratch_shapes=[pltpu.VMEM((2, 2, bm, D), F32), pltpu.SemaphoreType.DMA((2, 2))]),
        out_shape=jax.ShapeDtypeStruct((M, D), F32),
        compiler_params=_params("arbitrary"),
        name="moe_combine",
    )(dest, x, meta, g.reshape(1, D), b.reshape(1, D), ys)


def _even_layer(x, xb, w_in, ret_gn_g, w_out, ln1_g, ln1_b, w1, w3, w2, ln2_g, ln2_b, *, B, T):
    N, D = x.shape
    a_heads = D // 2 // A_HEAD_DIM
    r_heads = D // 2 // RET_VAL_DIM
    qa_w, kv_w = a_heads * A_HEAD_DIM, A_KV_HEADS * A_HEAD_DIM
    qi_w = IDX_HEADS * IDX_DIM
    rk_w, rv_w = r_heads * RET_KEY_DIM, r_heads * RET_VAL_DIM
    sizes = (qa_w, kv_w, kv_w, qi_w, IDX_DIM, IDX_HEADS, rk_w, rk_w, rv_w, rv_w)
    offs = [0]
    for s in sizes:
        offs.append(offs[-1] + s)
    w_in_t = w_in.T
    col = lambda a, b_: w_in_t[offs[a]:offs[b_]]
    w_qa, w_ka, w_va = col(0, 1), col(1, 2), col(2, 3)
    w_qi, w_ki, w_wi = col(3, 4), col(4, 5), col(5, 6)
    w_qb, w_kb, w_vb, w_gb = col(6, 7), col(7, 8), col(8, 9), col(9, 10)

    cos_a, sin_a = _rope_cos_sin(T, A_HEAD_DIM // 4, ROPE_THETA)
    tab_q = _lane_tables(cos_a, sin_a, A_HEAD_DIM, A_HEAD_DIM ** -0.5 * LOG2E)
    tab_k = _lane_tables(cos_a, sin_a, A_HEAD_DIM)
    cos_i, sin_i = _rope_cos_sin(T, IDX_DIM // 4, ROPE_THETA)
    tab_i = _lane_tables(cos_i, sin_i, IDX_DIM)
    pass_c = jnp.ones((T, LANE - IDX_DIM), F32)
    pass_s = jnp.zeros((T, LANE - IDX_DIM), F32)
    tab_idx = tuple(jnp.concatenate([t, t[:, :IDX_DIM], p], 1)
                    for t, p in zip(tab_i, (pass_c, pass_s, pass_s)))
    inv = 1.0 / (RET_THETA ** jnp.linspace(0.0, 1.0, RET_KEY_DIM // 2, dtype=F32))
    ang = jnp.arange(T, dtype=F32)[:, None] * inv[None, :]
    cos_r, sin_r = jnp.cos(ang), jnp.sin(ang)
    sin_pair = jnp.concatenate([-sin_r, sin_r], 1)
    tab_r = (jnp.concatenate([cos_r, cos_r], 1), sin_pair, sin_pair)

    idx_pad = LANE - IDX_DIM - IDX_HEADS
    w_idx = jnp.concatenate([w_qi, w_ki, w_wi * (IDX_DIM ** -0.5 * IDX_HEADS ** -0.5),
                             jnp.zeros((idx_pad, D), F32)], 0).astype(BF16)
    w_rqk = jnp.concatenate([w_qb, w_kb * RET_KEY_DIM ** -0.5], 0).astype(BF16)
    w_pv = jnp.concatenate([w_va, w_vb, w_gb], 0).astype(BF16)
    qa = _proj(xb, w_qa.astype(BF16), out_dtype=BF16, tabs=tab_q, half=A_HEAD_DIM // 8, w_rows_out=True,
               seq_len=T, name="proj_qa", **_pat(qa_w, 1024, 0))
    ka = _proj(xb, w_ka.astype(BF16), out_dtype=BF16, tabs=tab_k, half=A_HEAD_DIM // 8, w_rows_out=True,
               seq_len=T, name="proj_ka", **_pat(kv_w, 1024, 0))
    n_idx = w_idx.shape[0]
    idx = _proj(xb, w_idx, out_dtype=F32, tabs=tab_idx, half=IDX_DIM // 8, seq_len=T, name="proj_idx",
                w_rows_out=True, bm=512, bn=n_idx, slab_pat=(0,) * (qi_w // LANE) + (1,))
    rqk = _proj(xb, w_rqk, out_dtype=BF16, tabs=tab_r, mode="pair", seq_len=T, name="proj_ret_qk",
                w_rows_out=True, bn=1024, slab_pat=(0, 1) * (_tile(2 * rk_w, 1024) // (2 * LANE)))
    pv = _proj(xb, w_pv, out_dtype=BF16, bn=768, seq_len=T, w_rows_out=True, name="proj_v")

    ya = _dsa(qa.reshape(B, T, qa_w), ka.reshape(B, T, kv_w), pv.reshape(B, T, -1), idx.reshape(B, T, n_idx),
              B=B, T=T)
    yb = _retention(rqk, pv, ret_gn_g, B=B, T=T, heads=r_heads, v_blk0=kv_w // RET_VAL_DIM)
    w_out_b = w_out.astype(BF16)
    y = _proj(ya.reshape(N, qa_w), w_out_b[:qa_w], second=(yb, w_out_b[qa_w:]), out_dtype=F32, name="proj_out0")
    x1, x1b = _add_ln(x, y, ln1_g, ln1_b)
    hid = _swiglu_up(x1b, w1.astype(BF16), w3.astype(BF16))
    y = _mm_ksplit(hid, w2.astype(BF16), bk=3584, name="ffn_down")
    return _add_ln(x1, y, ln2_g, ln2_b)


def _pat(width, bn, p):
    bn = _tile(width, bn)
    return dict(bn=bn, slab_pat=(p,) * (bn // LANE))


def _odd_layer(x, xb, w_dq_dkv, q_norm_g, w_uq, kv_norm_g, w_ukv, w_out, ln1_g, ln1_b,
               router, we1, we3, we2, ln2_g, ln2_b, *, B, T):
    N, D = x.shape
    heads = D // MLA_V
    q_rank, kv_rank = q_norm_g.shape[0], kv_norm_g.shape[0]
    scale = (MLA_NOPE + MLA_ROPE) ** -0.5 * LOG2E
    cos_c, sin_c = _rope_cos_sin(T, MLA_ROPE, ROPE_THETA)
    tab_kr = _lane_tables(cos_c, sin_c, MLA_ROPE)
    tab_qr = _lane_tables(cos_c, sin_c, MLA_ROPE, scale)

    w_kr = w_dq_dkv[:, q_rank + kv_rank:]
    w_down = jnp.concatenate([w_dq_dkv[:, :q_rank + kv_rank], w_kr, w_kr], 1).astype(BF16)
    cq, ckv, kr = _mla_down(xb, w_down, q_norm_g, kv_norm_g, tab_kr, T=T, q_rank=q_rank, kv_rank=kv_rank)
    w_uq3 = w_uq.reshape(q_rank, heads, MLA_NOPE + MLA_ROPE)
    w_qn = w_uq3[:, :, :MLA_NOPE].reshape(q_rank, heads * MLA_NOPE).astype(BF16)
    w_qr = w_uq3[:, :, MLA_NOPE:].reshape(q_rank, heads * MLA_ROPE).astype(BF16)
    w_kv3 = w_ukv.reshape(kv_rank, heads, MLA_NOPE + MLA_V)
    w_kv = jnp.concatenate([w_kv3[:, :, :MLA_NOPE].reshape(kv_rank, heads * MLA_NOPE),
                            w_kv3[:, :, MLA_NOPE:].reshape(kv_rank, heads * MLA_V)], 1).astype(BF16)
    qn = _proj(cq, w_qn, out_dtype=BF16, scale=scale, seq_len=T, bn=2048, name="proj_q_nope")
    qr = _proj(cq, w_qr, out_dtype=BF16, tabs=tab_qr, half=MLA_ROPE // 2, seq_len=T, name="proj_q_rope",
               **_pat(heads * MLA_ROPE, 1024, 0))
    kv = _proj(ckv, w_kv, out_dtype=BF16, seq_len=T, bn=2048, name="proj_kv")
    att = _mla_attn(qn, qr, kv, kr, B=B, T=T, heads=heads)
    y = _proj(att, w_out.astype(BF16), out_dtype=F32, name="proj_out1")
    x1, x1_packed, logits = _add_ln(x, y, ln1_g, ln1_b, packed=True, router=router)

    E = router.shape[1]
    tm = _tile(N, 512)
    meta, cnt = _router(logits, E)
    counts = cnt[0, :E].astype(I32)
    padded = (counts + tm - 1) // tm * tm
    ends = jnp.cumsum(padded)
    starts = ends - padded
    i1, i2 = meta[:, 0].astype(I32), meta[:, 1].astype(I32)
    dest = jnp.stack([starts[i1] + meta[:, 4].astype(I32), starts[i2] + meta[:, 5].astype(I32)], 1).reshape(-1)
    n_rows = 2 * N + E * tm
    n_tiles = n_rows // tm
    n_valid = (ends[-1] // tm).astype(I32).reshape(1)
    tile_start = jnp.arange(n_tiles, dtype=I32) * tm
    tile_expert = jnp.minimum(jnp.sum(tile_start[:, None] >= ends[None, :], axis=1), E - 1).astype(I32)
    tile_expert = jnp.where(jnp.arange(n_tiles) < n_valid[0], tile_expert, tile_expert[jnp.maximum(n_valid[0] - 1, 0)])
    row_tok = jnp.zeros((n_rows,), I32).at[dest].set(jnp.repeat(jnp.arange(N, dtype=I32), 2))
    ys = _experts(x1_packed, we1, we3, we2, tile_expert, n_valid, row_tok, tm=tm)
    return _combine(x1, meta, ys, dest, ln2_g, ln2_b)


def kernel(x, l0_w_in, l0_ret_gn_g, l0_w_out, l0_ln1_g, l0_ln1_b, l0_ffn_w1, l0_ffn_w3, l0_ffn_w2, l0_ln2_g, l0_ln2_b, l1_w_dq_dkv, l1_q_norm_g, l1_w_uq, l1_kv_norm_g, l1_w_ukv, l1_w_out, l1_ln1_g, l1_ln1_b, l1_router, l1_moe_w1, l1_moe_w3, l1_moe_w2, l1_ln2_g, l1_ln2_b):
    B, T, D = x.shape
    x2 = x.reshape(B * T, D)
    h, hb = _even_layer(x2, x2.astype(BF16), l0_w_in, l0_ret_gn_g, l0_w_out, l0_ln1_g, l0_ln1_b,
                        l0_ffn_w1, l0_ffn_w3, l0_ffn_w2, l0_ln2_g, l0_ln2_b, B=B, T=T)
    out = _odd_layer(h, hb, l1_w_dq_dkv, l1_q_norm_g, l1_w_uq, l1_kv_norm_g, l1_w_ukv, l1_w_out,
                     l1_ln1_g, l1_ln1_b, l1_router, l1_moe_w1, l1_moe_w3, l1_moe_w2, l1_ln2_g, l1_ln2_b, B=B, T=T)
    return out.reshape(B, T, D)
```

```python
import functools

import jax
import jax.numpy as jnp
from jax import lax
from jax.experimental import pallas as pl
from jax.experimental.pallas import tpu as pltpu

F32 = jnp.float32
BF16 = jnp.bfloat16
I32 = jnp.int32

A_HEAD_DIM = 128
A_KV_HEADS = 4
IDX_HEADS = 16
IDX_DIM = 64
DSA_TOPK_MAX = 256
RET_KEY_DIM = 256
RET_VAL_DIM = 256
RET_CHUNK = 128
RET_THETA = 10000.0
MLA_V = 128
MLA_NOPE = 128
MLA_ROPE = 64
ROPE_THETA = 500000.0
Q_BLOCK = 128
LN_EPS = 1e-5
RMS_EPS = 1e-6
DEPTH = 2
ALPHA = (2.0 * DEPTH) ** 0.25

LANE = 128
V7X_VMEM_BYTES = 64 * 1024 * 1024
VMEM_LIMIT = V7X_VMEM_BYTES - 8 * 1024 * 1024
MASKED = -1e30
INT_MIN = -(2 ** 31)

NT_DIMS = (((1,), (1,)), ((), ()))
TN_DIMS = (((0,), (0,)), ((), ()))


def _tile(n, pref, mult=LANE):
    if n <= pref:
        return n
    t = (pref // mult) * mult
    while t > mult and n % t:
        t -= mult
    assert n % t == 0, (n, pref, mult)
    return t


def _params(*sem):
    return pltpu.CompilerParams(dimension_semantics=sem, vmem_limit_bytes=VMEM_LIMIT)


def _rope_cos_sin(T, rot_dim, theta):
    inv = theta ** (-jnp.arange(0, rot_dim, 2, dtype=F32) / rot_dim)
    ang = jnp.arange(T, dtype=F32)[:, None] * inv[None, :]
    return jnp.cos(ang), jnp.sin(ang)


def _lane_tables(cos, sin, head_dim, scale=1.0):
    T, half = cos.shape
    rest = head_dim - 2 * half
    zh = jnp.zeros((T, half), F32)
    c = jnp.concatenate([cos, cos, jnp.ones((T, rest), F32)], 1)
    sa = jnp.concatenate([-sin, zh, jnp.zeros((T, rest), F32)], 1)
    sb = jnp.concatenate([zh, sin, jnp.zeros((T, rest), F32)], 1)
    reps = LANE // head_dim
    return tuple(jnp.tile(t * scale, (1, reps)) for t in (c, sa, sb))


def _proj_kernel(*refs, slab_pat, mode, half, scale, with_tab, two_inputs, w_rows_out=False):
    x_ref, w_ref = refs[:2]
    if w_rows_out:
        acc = lax.dot_general(x_ref[...], w_ref[...], NT_DIMS, preferred_element_type=F32)
    else:
        acc = jnp.dot(x_ref[...], w_ref[...], preferred_element_type=F32)
    refs = refs[2:]
    if two_inputs:
        acc = acc + jnp.dot(refs[0][...], refs[1][...], preferred_element_type=F32)
        refs = refs[2:]
    if with_tab:
        c_ref, sa_ref, sb_ref, o_ref = refs
    else:
        (o_ref,) = refs
    for s, p in enumerate(slab_pat):
        a = acc[:, s * LANE:(s + 1) * LANE]
        if p < 0:
            out = a if scale == 1.0 else a * scale
        else:
            c = c_ref[:, p * LANE:(p + 1) * LANE]
            sa = sa_ref[:, p * LANE:(p + 1) * LANE]
            if mode == "lane":
                sb = sb_ref[:, p * LANE:(p + 1) * LANE]
                out = a * c + pltpu.roll(a, LANE - half, 1) * sa + pltpu.roll(a, half, 1) * sb
            else:
                q = s ^ 1
                out = a * c + acc[:, q * LANE:(q + 1) * LANE] * sa
        o_ref[:, s * LANE:(s + 1) * LANE] = out.astype(o_ref.dtype)


def _proj(x, w, *, out_dtype, bm=1024, bn=1024, tabs=None, slab_pat=None, mode="lane", half=0,
          scale=1.0, seq_len=None, second=None, w_rows_out=False, name="proj"):
    M, K = x.shape
    N = w.shape[0] if w_rows_out else w.shape[1]
    bm = _tile(M, bm) if seq_len is None else _tile(seq_len, bm)
    bn = _tile(N, bn)
    if slab_pat is None:
        slab_pat = (-1,) * (bn // LANE)
    assert len(slab_pat) == bn // LANE
    w_spec = pl.BlockSpec((bn, K), lambda i, j: (j, 0)) if w_rows_out else pl.BlockSpec((K, bn), lambda i, j: (0, j))
    in_specs = [pl.BlockSpec((bm, K), lambda i, j: (i, 0)), w_spec]
    args = [x, w]
    if second is not None:
        x2, w2 = second
        in_specs += [pl.BlockSpec((bm, x2.shape[1]), lambda i, j: (i, 0)),
                     pl.BlockSpec((x2.shape[1], bn), lambda i, j: (0, j))]
        args += [x2, w2]
    if tabs is not None:
        tb = seq_len // bm
        tw = tabs[0].shape[1]
        in_specs += [pl.BlockSpec((bm, tw), lambda i, j: (i % tb, 0))] * 3
        args += list(tabs)
    kern = functools.partial(_proj_kernel, slab_pat=tuple(slab_pat), mode=mode, half=half,
                             scale=scale, with_tab=tabs is not None, two_inputs=second is not None,
                             w_rows_out=w_rows_out)
    return pl.pallas_call(
        kern,
        grid=(M // bm, N // bn),
        in_specs=in_specs,
        out_specs=pl.BlockSpec((bm, bn), lambda i, j: (i, j)),
        out_shape=jax.ShapeDtypeStruct((M, N), out_dtype),
        compiler_params=_params("parallel", "parallel"),
        name=name,
    )(*args)


def _mm_ksplit_kernel(x_ref, w_ref, o_ref):
    part = jnp.dot(x_ref[...], w_ref[...], preferred_element_type=F32)

    @pl.when(pl.program_id(2) == 0)
    def _():
        o_ref[...] = part

    @pl.when(pl.program_id(2) > 0)
    def _():
        o_ref[...] += part


def _mm_ksplit(x, w, *, bm=1024, bn=1024, bk=2048, name="mm_ksplit"):
    M, K = x.shape
    N = w.shape[1]
    bm, bn, bk = _tile(M, bm), _tile(N, bn), _tile(K, bk)
    return pl.pallas_call(
        _mm_ksplit_kernel,
        grid=(M // bm, N // bn, K // bk),
        in_specs=[pl.BlockSpec((bm, bk), lambda i, j, k: (i, k)),
                  pl.BlockSpec((bk, bn), lambda i, j, k: (k, j))],
        out_specs=pl.BlockSpec((bm, bn), lambda i, j, k: (i, j)),
        out_shape=jax.ShapeDtypeStruct((M, N), F32),
        compiler_params=_params("parallel", "parallel", "arbitrary"),
        name=name,
    )(x, w)


def _swiglu_up_kernel(x_ref, w1_ref, w3_ref, o_ref):
    x = x_ref[...]
    a = jnp.dot(x, w1_ref[...], preferred_element_type=F32)
    b = jnp.dot(x, w3_ref[...], preferred_element_type=F32)
    o_ref[...] = (a * jax.nn.sigmoid(a) * b).astype(o_ref.dtype)


def _swiglu_up(x, w1, w3, *, bm=1024, bn=512):
    M, K = x.shape
    N = w1.shape[1]
    bm, bn = _tile(M, bm), _tile(N, bn)
    return pl.pallas_call(
        _swiglu_up_kernel,
        grid=(M // bm, N // bn),
        in_specs=[pl.BlockSpec((bm, K), lambda i, j: (i, 0)),
                  pl.BlockSpec((K, bn), lambda i, j: (0, j)),
                  pl.BlockSpec((K, bn), lambda i, j: (0, j))],
        out_specs=pl.BlockSpec((bm, bn), lambda i, j: (i, j)),
        out_shape=jax.ShapeDtypeStruct((M, N), BF16),
        compiler_params=_params("parallel", "parallel"),
        name="swiglu_up",
    )(x, w1, w3)


def _layer_norm_rows(z, g, b):
    mu = jnp.mean(z, axis=-1, keepdims=True)
    zc = z - mu
    var = jnp.mean(zc * zc, axis=-1, keepdims=True)
    return zc * lax.rsqrt(var + LN_EPS) * g + b


def _pack_bf16_pairs(x):
    half = x.shape[1] // 2
    hi = pltpu.bitcast(x[:, :half].astype(BF16).astype(F32), I32)
    lo = pltpu.bitcast(x[:, half:].astype(BF16).astype(F32), I32)
    return hi | lax.shift_right_logical(lo, 16)


def _unpack_bf16_pairs(w):
    hi = pltpu.bitcast(w & jnp.int32(-65536), F32).astype(BF16)
    lo = pltpu.bitcast(lax.shift_left(w, 16), F32).astype(BF16)
    return hi, lo


def _add_ln_kernel(x_ref, y_ref, g_ref, b_ref, *rest, packed, with_router):
    out = _layer_norm_rows(ALPHA * x_ref[...] + y_ref[...], g_ref[...], b_ref[...])
    if with_router:
        r_ref, of_ref, o2_ref, lg_ref = rest
        lg_ref[...] = jnp.dot(out, r_ref[...], preferred_element_type=F32, precision=lax.Precision.HIGHEST)
    else:
        of_ref, o2_ref = rest
    of_ref[...] = out
    o2_ref[...] = _pack_bf16_pairs(out) if packed else out.astype(o2_ref.dtype)


def _add_ln(x, y, g, b, *, packed=False, router=None, bm=256):
    M, D = x.shape
    bm = _tile(M, bm, 8)
    row = pl.BlockSpec((bm, D), lambda i: (i, 0))
    vec = pl.BlockSpec((1, D), lambda i: (0, 0))
    second = jax.ShapeDtypeStruct((M, D // 2), I32) if packed else jax.ShapeDtypeStruct((M, D), BF16)
    in_specs, args = [row, row, vec, vec], [x, y, g.reshape(1, D), b.reshape(1, D)]
    out_specs = [row, pl.BlockSpec((bm, second.shape[1]), lambda i: (i, 0))]
    out_shape = [jax.ShapeDtypeStruct((M, D), F32), second]
    if router is not None:
        in_specs.append(pl.BlockSpec((D, LANE), lambda i: (0, 0)))
        args.append(jnp.zeros((D, LANE), F32).at[:, :router.shape[1]].set(router))
        out_specs.append(pl.BlockSpec((bm, LANE), lambda i: (i, 0)))
        out_shape.append(jax.ShapeDtypeStruct((M, LANE), F32))
    return pl.pallas_call(
        functools.partial(_add_ln_kernel, packed=packed, with_router=router is not None),
        grid=(M // bm,),
        in_specs=in_specs,
        out_specs=out_specs,
        out_shape=out_shape,
        compiler_params=_params("parallel"),
        name="add_ln",
    )(*args)


LOG2E = 1.4426950408889634


def _online_softmax(s, m, l):
    m_new = jnp.maximum(m, jnp.max(s, axis=1, keepdims=True))
    a = jnp.exp2(m - m_new)
    p = jnp.exp2((s - m_new).astype(BF16))
    return m_new, a * l + jnp.sum(p.astype(F32), axis=1, keepdims=True), a, p


def _dsa_kernel(q_ref, k_ref, v_ref, iq_ref, ik_ref, o_ref, keys_ref, bias_ref, *, topk, ck, n_rep, idx_bits):
    blk = pl.program_id(1)
    n_chunks = (blk * Q_BLOCK + Q_BLOCK + ck - 1) // ck
    row = lax.broadcasted_iota(I32, (Q_BLOCK, ck), 0) + blk * Q_BLOCK
    lane = lax.broadcasted_iota(I32, (Q_BLOCK, ck), 1)
    w_off = IDX_HEADS * IDX_DIM + IDX_DIM
    iq = iq_ref[0]
    wi = iq[:, w_off:w_off + IDX_HEADS]
    q_idx = jnp.concatenate([iq[:, h * IDX_DIM:(h + 1) * IDX_DIM].astype(BF16) for h in range(IDX_HEADS)], axis=0)

    def score_body(c, carry):
        off = pl.multiple_of(c * ck, ck)
        kc = ik_ref[0, pl.ds(off, ck), :][:, :IDX_DIM].astype(BF16)
        lg = lax.dot_general(q_idx, kc, NT_DIMS, preferred_element_type=F32)
        s = jnp.zeros((Q_BLOCK, ck), F32)
        for h in range(IDX_HEADS):
            s = s + wi[:, h:h + 1] * jnp.maximum(lg[h * Q_BLOCK:(h + 1) * Q_BLOCK], 0.0)
        bits = pltpu.bitcast(s, I32)
        key = bits ^ ((bits >> 31) & 0x7FFFFFFF)
        keys_ref[c] = jnp.where(lane + off <= row, key, INT_MIN)
        return carry

    lax.fori_loop(0, n_chunks, score_body, 0)

    def count(indicator):
        def body(c, acc):
            part = indicator(keys_ref[c], lane + c * ck)
            for j in range(ck // LANE):
                acc = acc + part[:, j * LANE:(j + 1) * LANE]
            return acc
        acc = lax.fori_loop(0, n_chunks, body, jnp.zeros((Q_BLOCK, LANE), F32))
        return jnp.sum(acc, axis=1, keepdims=True)

    kf = float(topk)
    ok = count(lambda kc, idx: jnp.where(kc >= 0, 1.0, 0.0)) >= kf
    thr = jnp.where(ok, 0, INT_MIN).astype(I32)

    def bit_body(i, thr):
        cand = thr + lax.shift_left(jnp.int32(1), 30 - i)
        ok = count(lambda kc, idx: jnp.where(kc >= cand, 1.0, 0.0)) >= kf
        return jnp.where(ok, cand, thr)

    thr = lax.fori_loop(0, 31, bit_body, thr)

    n_gt = count(lambda kc, idx: jnp.where(kc > thr, 1.0, 0.0))
    n_ge = count(lambda kc, idx: jnp.where(kc >= thr, 1.0, 0.0))
    need = kf - n_gt
    has_thr = thr > INT_MIN
    surplus = jnp.where(has_thr, n_ge - n_gt - need, 0.0)

    def tie_search():
        def tie_body(i, last):
            cand = last + lax.shift_left(jnp.int32(1), idx_bits - 1 - i)
            ok = count(lambda kc, idx: jnp.where(kc == thr, jnp.where(idx < cand, 1.0, 0.0), 0.0)) < need
            return jnp.where(ok, cand, last)
        return lax.fori_loop(0, idx_bits, tie_body, jnp.zeros((Q_BLOCK, 1), I32))

    last = lax.cond(jnp.max(surplus) > 0.0, tie_search, lambda: jnp.full((Q_BLOCK, 1), 2 ** idx_bits, I32))
    last = jnp.where(has_thr, last, -1)

    def bias_body(c, carry):
        kc = keys_ref[c]
        tie_bias = jnp.where(lane + c * ck <= last, 0.0, MASKED)
        bias_ref[c] = jnp.where(kc == thr, tie_bias, jnp.where(kc > thr, 0.0, MASKED))
        return carry

    lax.fori_loop(0, n_chunks, bias_body, 0)

    q = q_ref[0]
    rows = n_rep * Q_BLOCK
    qgs = [jnp.concatenate([q[:, (g * n_rep + r) * A_HEAD_DIM:(g * n_rep + r + 1) * A_HEAD_DIM]
                            for r in range(n_rep)], axis=0) for g in range(A_KV_HEADS)]

    def att_body(c, carry, n_sub):
        width = n_sub * ck
        off = pl.multiple_of(c * width, width)
        b = jnp.concatenate([bias_ref[c * n_sub + t] for t in range(n_sub)], axis=1)
        bias = jnp.concatenate([b] * n_rep, axis=0)
        out = []
        for g in range(A_KV_HEADS):
            m, l, acc = carry[g]
            kc = k_ref[0, pl.ds(off, width), g * A_HEAD_DIM:(g + 1) * A_HEAD_DIM]
            vc = v_ref[0, pl.ds(off, width), g * A_HEAD_DIM:(g + 1) * A_HEAD_DIM]
            s = lax.dot_general(qgs[g], kc, NT_DIMS, preferred_element_type=F32) + bias
            m, l, a, p = _online_softmax(s, m, l)
            out.append((m, l, a * acc + jnp.dot(p, vc, preferred_element_type=F32)))
        return tuple(out)

    one = (jnp.full((rows, 1), MASKED, F32), jnp.zeros((rows, 1), F32), jnp.zeros((rows, A_HEAD_DIM), F32))
    carry = (one,) * A_KV_HEADS
    if k_ref.shape[1] >= 2 * ck:
        carry = lax.fori_loop(0, n_chunks // 2, functools.partial(att_body, n_sub=2), carry)
        carry = lax.cond(n_chunks % 2 == 1, lambda c: att_body(n_chunks - 1, c, 1), lambda c: c, carry)
    else:
        carry = att_body(0, carry, 1)
    for g in range(A_KV_HEADS):
        _, l, acc = carry[g]
        o = acc / l
        for r in range(n_rep):
            col = (g * n_rep + r) * A_HEAD_DIM
            o_ref[0, :, col:col + A_HEAD_DIM] = o[r * Q_BLOCK:(r + 1) * Q_BLOCK].astype(o_ref.dtype)


def _dsa(q, k, v, idx, *, B, T):
    a_heads = q.shape[-1] // A_HEAD_DIM
    n_rep = a_heads // A_KV_HEADS
    topk = min(DSA_TOPK_MAX, T // 4)
    ck = _tile(T, 512)
    idx_w = idx.shape[-1]
    kv_w = k.shape[-1]
    kern = functools.partial(_dsa_kernel, topk=topk, ck=ck, n_rep=n_rep, idx_bits=max(1, (T - 1).bit_length()))
    return pl.pallas_call(
        kern,
        grid=(B, T // Q_BLOCK),
        in_specs=[pl.BlockSpec((1, Q_BLOCK, q.shape[-1]), lambda b, i: (b, i, 0)),
                  pl.BlockSpec((1, T, kv_w), lambda b, i: (b, 0, 0)),
                  pl.BlockSpec((1, T, kv_w), lambda b, i: (b, 0, 0)),
                  pl.BlockSpec((1, Q_BLOCK, idx_w), lambda b, i: (b, i, 0)),
                  pl.BlockSpec((1, T, LANE), lambda b, i: (b, 0, IDX_HEADS * IDX_DIM // LANE))],
        out_specs=pl.BlockSpec((1, Q_BLOCK, q.shape[-1]), lambda b, i: (b, i, 0)),
        out_shape=jax.ShapeDtypeStruct(q.shape, BF16),
        scratch_shapes=[pltpu.VMEM((T // ck, Q_BLOCK, ck), I32), pltpu.VMEM((T // ck, Q_BLOCK, ck), F32)],
        compiler_params=_params("parallel", "parallel"),
        name="dsa",
    )(q, k, v, idx, idx)


RET_HEADS_PER_STEP = 2


def _retention_kernel(q_ref, k_ref, v_ref, g_ref, gn_ref, din_ref, qd_ref, kd_ref, cd_ref, o_ref, state_ref, *, n_sub):
    @pl.when(pl.program_id(2) == 0)
    def _():
        state_ref[...] = jnp.zeros_like(state_ref)

    W = RET_VAL_DIM
    for s in range(n_sub):
        sl = pl.ds(s * RET_CHUNK, RET_CHUNK)
        for j in range(RET_HEADS_PER_STEP):
            cols = slice(j * W, (j + 1) * W)
            qc = q_ref[sl, cols]
            kc = k_ref[sl, cols]
            vc = v_ref[sl, cols]
            st = state_ref[j]
            inner = lax.dot_general(qc, kc, NT_DIMS, preferred_element_type=F32) * din_ref[j]
            o = (jnp.dot(inner.astype(BF16), vc, preferred_element_type=F32)
                 + jnp.dot(qc, st.astype(BF16), preferred_element_type=F32) * qd_ref[j])
            vk = (vc.astype(F32) * kd_ref[j]).astype(BF16)
            state_ref[j] = st * cd_ref[j] + lax.dot_general(kc, vk, TN_DIMS, preferred_element_type=F32)
            mu = jnp.mean(o, axis=-1, keepdims=True)
            oc = o - mu
            var = jnp.mean(oc * oc, axis=-1, keepdims=True)
            gate = g_ref[sl, cols].astype(F32)
            normed = oc * lax.rsqrt(var + LN_EPS) * gn_ref[:, cols]
            o_ref[sl, cols] = (gate * jax.nn.sigmoid(gate) * normed).astype(o_ref.dtype)


def _retention(qk, pv, gn_g, *, B, T, heads, v_blk0):
    N = qk.shape[0]
    C = RET_CHUNK
    rb = _tile(T, 512)
    n_sub = rb // C
    nr = T // rb
    log_gamma = jnp.log(1.0 - 2.0 ** (-5.0 - jnp.arange(heads, dtype=F32)))
    pos = jnp.arange(C, dtype=F32)
    diff = pos[:, None] - pos[None, :]
    din = jnp.exp(jnp.where(diff[None] >= 0, log_gamma[:, None, None] * diff[None], -jnp.inf))
    qd = jnp.exp(log_gamma[:, None] * (pos[None] + 1.0))[:, :, None]
    kd = jnp.exp(log_gamma[:, None] * (C - 1.0 - pos[None]))[:, :, None]
    cd = jnp.exp(log_gamma * C)[:, None, None]
    hp = RET_HEADS_PER_STEP
    W = hp * RET_VAL_DIM
    assert heads % hp == 0 and v_blk0 % hp == 0
    blk = lambda off: pl.BlockSpec((rb, W), lambda b, h, r: (b * nr + r, off // hp + h))
    per_head = lambda shape: pl.BlockSpec((hp,) + shape, lambda b, h, r: (h, 0, 0))
    return pl.pallas_call(
        functools.partial(_retention_kernel, n_sub=n_sub),
        grid=(B, heads // hp, nr),
        in_specs=[blk(0), blk(heads), blk(v_blk0), blk(v_blk0 + heads),
                  pl.BlockSpec((1, W), lambda b, h, r: (0, h)),
                  per_head((C, C)), per_head((C, 1)), per_head((C, 1)), per_head((1, 1))],
        out_specs=pl.BlockSpec((rb, W), lambda b, h, r: (b * nr + r, h)),
        out_shape=jax.ShapeDtypeStruct((N, heads * RET_VAL_DIM), BF16),
        scratch_shapes=[pltpu.VMEM((hp, RET_KEY_DIM, RET_VAL_DIM), F32)],
        compiler_params=_params("parallel", "parallel", "arbitrary"),
        name="retention",
    )(qk, qk, pv, pv, gn_g.reshape(1, heads * RET_VAL_DIM), din, qd, kd, cd)


def _mla_down_kernel(x_ref, w_ref, qg_ref, kvg_ref, c_ref, sa_ref, sb_ref, cq_ref, ckv_ref, kr_ref, *, q_rank, kv_rank):
    acc = jnp.dot(x_ref[...], w_ref[...], preferred_element_type=F32)

    def rms(a, g):
        return a * lax.rsqrt(jnp.mean(a * a, axis=-1, keepdims=True) + RMS_EPS) * g

    cq_ref[...] = rms(acc[:, :q_rank], qg_ref[...]).astype(cq_ref.dtype)
    ckv_ref[...] = rms(acc[:, q_rank:q_rank + kv_rank], kvg_ref[...]).astype(ckv_ref.dtype)
    kr = acc[:, q_rank + kv_rank:]
    half = MLA_ROPE // 2
    kr = kr * c_ref[...] + pltpu.roll(kr, LANE - half, 1) * sa_ref[...] + pltpu.roll(kr, half, 1) * sb_ref[...]
    kr_ref[...] = kr.astype(kr_ref.dtype)


def _mla_down(x, w, q_g, kv_g, tabs, *, T, q_rank, kv_rank, bm=512):
    M, K = x.shape
    Nw = w.shape[1]
    bm = _tile(T, bm)
    tb = T // bm
    row = lambda n: pl.BlockSpec((bm, n), lambda i: (i, 0))
    tab = pl.BlockSpec((bm, LANE), lambda i: (i % tb, 0))
    return pl.pallas_call(
        functools.partial(_mla_down_kernel, q_rank=q_rank, kv_rank=kv_rank),
        grid=(M // bm,),
        in_specs=[row(K), pl.BlockSpec((K, Nw), lambda i: (0, 0)),
                  pl.BlockSpec((1, q_rank), lambda i: (0, 0)), pl.BlockSpec((1, kv_rank), lambda i: (0, 0)),
                  tab, tab, tab],
        out_specs=[row(q_rank), row(kv_rank), row(LANE)],
        out_shape=[jax.ShapeDtypeStruct((M, q_rank), BF16), jax.ShapeDtypeStruct((M, kv_rank), BF16),
                   jax.ShapeDtypeStruct((M, LANE), BF16)],
        compiler_params=_params("parallel"),
        name="mla_down",
    )(x, w, q_g.reshape(1, q_rank), kv_g.reshape(1, kv_rank), *tabs)


MLA_HEADS_PER_STEP = 4


def _mla_attn_kernel(qn_ref, qr_ref, kn_ref, kr_ref, v_ref, o_ref, *, tq):
    i = pl.program_id(2)
    lane = lax.broadcasted_iota(I32, (tq, LANE), 1)
    qs = []
    for j in range(MLA_HEADS_PER_STEP):
        lo = (j % 2) * MLA_ROPE
        own = jnp.where((lane >= lo) & (lane < lo + MLA_ROPE), 1.0, 0.0)
        pair = qr_ref[:, (j // 2) * LANE:(j // 2 + 1) * LANE].astype(F32)
        qr = (pair * own).astype(BF16)
        qs.append(jnp.concatenate([qn_ref[:, j * LANE:(j + 1) * LANE], qr], axis=1))

    def step(c, carry, masked, width):
        off = pl.multiple_of(c * width, width)
        kr = kr_ref[pl.ds(off, width), :]
        out = []
        for j in range(MLA_HEADS_PER_STEP):
            m, l, acc = carry[j]
            k = jnp.concatenate([kn_ref[pl.ds(off, width), j * LANE:(j + 1) * LANE], kr], axis=1)
            s = lax.dot_general(qs[j], k, NT_DIMS, preferred_element_type=F32)
            if masked:
                r_io = lax.broadcasted_iota(I32, (tq, width), 0)
                c_io = lax.broadcasted_iota(I32, (tq, width), 1)
                s = jnp.where(c_io <= r_io, s, MASKED)
            m, l, a, p = _online_softmax(s, m, l)
            v = v_ref[pl.ds(off, width), j * MLA_V:(j + 1) * MLA_V]
            out.append((m, l, a * acc + jnp.dot(p, v, preferred_element_type=F32)))
        return tuple(out)

    one = (jnp.full((tq, 1), MASKED, F32), jnp.zeros((tq, 1), F32), jnp.zeros((tq, MLA_V), F32))
    carry = (one,) * MLA_HEADS_PER_STEP
    if kn_ref.shape[0] >= 2 * tq:
        carry = lax.fori_loop(0, i // 2, functools.partial(step, masked=False, width=2 * tq), carry)
        carry = lax.cond(i % 2 == 1, lambda c: step(i - 1, c, False, tq), lambda c: c, carry)
    carry = step(i, carry, True, tq)
    for j in range(MLA_HEADS_PER_STEP):
        _, l, acc = carry[j]
        o_ref[:, j * MLA_V:(j + 1) * MLA_V] = (acc / l).astype(o_ref.dtype)


def _mla_attn(qn, qr, kv, kr, *, B, T, heads):
    N = qn.shape[0]
    tq = _tile(T, 512)
    nq = T // tq
    hp = MLA_HEADS_PER_STEP
    w = hp * LANE
    return pl.pallas_call(
        functools.partial(_mla_attn_kernel, tq=tq),
        grid=(B, heads // hp, nq),
        in_specs=[pl.BlockSpec((tq, w), lambda b, h, i: (b * nq + i, h)),
                  pl.BlockSpec((tq, hp * MLA_ROPE), lambda b, h, i: (b * nq + i, h)),
                  pl.BlockSpec((T, w), lambda b, h, i: (b, h)),
                  pl.BlockSpec((T, LANE), lambda b, h, i: (b, 0)),
                  pl.BlockSpec((T, w), lambda b, h, i: (b, heads // hp + h))],
        out_specs=pl.BlockSpec((tq, w), lambda b, h, i: (b * nq + i, h)),
        out_shape=jax.ShapeDtypeStruct((N, heads * MLA_V), BF16),
        compiler_params=_params("parallel", "parallel", "parallel"),
        name="mla_attn",
    )(qn, qr, kv, kr, kv)


def _router_kernel(lg_ref, meta_ref, cnt_ref, carry_ref, *, n_exp):
    @pl.when(pl.program_id(0) == 0)
    def _():
        carry_ref[...] = jnp.zeros_like(carry_ref)

    bm = lg_ref.shape[0]
    lane = lax.broadcasted_iota(I32, (bm, LANE), 1).astype(F32)
    logits = jnp.where(lane < n_exp, lg_ref[...], -jnp.inf)
    m1 = jnp.max(logits, axis=1, keepdims=True)
    i1 = jnp.min(jnp.where(logits == m1, lane, float(LANE)), axis=1, keepdims=True)
    rest = jnp.where(lane == i1, -jnp.inf, logits)
    m2 = jnp.max(rest, axis=1, keepdims=True)
    i2 = jnp.min(jnp.where(rest == m2, lane, float(LANE)), axis=1, keepdims=True)
    e = jnp.exp(m2 - m1)
    g1 = 1.0 / (1.0 + e)
    g2 = e / (1.0 + e)
    sel = jnp.where(lane == i1, 1.0, jnp.where(lane == i2, 1.0, 0.0))
    r_io = lax.broadcasted_iota(I32, (bm, bm), 0)
    c_io = lax.broadcasted_iota(I32, (bm, bm), 1)
    below = jnp.where(c_io < r_io, 1.0, 0.0).astype(BF16)
    carry = carry_ref[0:1, :]
    rank = jnp.dot(below, sel.astype(BF16), preferred_element_type=F32) + carry
    r1 = jnp.sum(jnp.where(lane == i1, rank, 0.0), axis=1, keepdims=True)
    r2 = jnp.sum(jnp.where(lane == i2, rank, 0.0), axis=1, keepdims=True)
    meta = jnp.where(lane == 0, i1, 0.0)
    meta = jnp.where(lane == 1, i2, meta)
    meta = jnp.where(lane == 2, g1, meta)
    meta = jnp.where(lane == 3, g2, meta)
    meta = jnp.where(lane == 4, r1, meta)
    meta = jnp.where(lane == 5, r2, meta)
    meta_ref[...] = meta
    total = carry + jnp.sum(sel, axis=0, keepdims=True)
    carry_ref[...] = jnp.broadcast_to(total, carry_ref.shape)
    cnt_ref[...] = jnp.broadcast_to(total, cnt_ref.shape)


def _router(logits, n_exp, *, bm=512):
    M = logits.shape[0]
    bm = _tile(M, bm)
    return pl.pallas_call(
        functools.partial(_router_kernel, n_exp=n_exp),
        grid=(M // bm,),
        in_specs=[pl.BlockSpec((bm, LANE), lambda i: (i, 0))],
        out_specs=[pl.BlockSpec((bm, LANE), lambda i: (i, 0)), pl.BlockSpec((8, LANE), lambda i: (0, 0))],
        out_shape=[jax.ShapeDtypeStruct((M, LANE), F32), jax.ShapeDtypeStruct((8, LANE), F32)],
        scratch_shapes=[pltpu.VMEM((8, LANE), F32)],
        compiler_params=_params("arbitrary"),
        name="router",
    )(logits)


DMA_LOOP_UNROLL = 8


def _moe_gather_kernel(tok_ref, nv_ref, x_hbm, o_ref, stage_ref, sem, *, tm):
    r = pl.program_id(0)
    nv = nv_ref[0]

    half = stage_ref.shape[1]

    def row_copy(j, tok):
        return pltpu.make_async_copy(x_hbm.at[pl.ds(tok, 1)], stage_ref.at[pl.ds(j, 1)], sem)

    def gather_start(tile):
        def body(j, carry):
            row_copy(j, tok_ref[tile * tm + j]).start()
            return carry
        lax.fori_loop(0, tm, body, 0, unroll=DMA_LOOP_UNROLL)

    def gather_wait():
        def body(j, carry):
            row_copy(j, 0).wait()
            return carry
        lax.fori_loop(0, tm, body, 0, unroll=DMA_LOOP_UNROLL)

    @pl.when(r < nv)
    def _():
        @pl.when(r == 0)
        def _():
            gather_start(0)

        gather_wait()
        hi, lo = _unpack_bf16_pairs(stage_ref[...])
        o_ref[:, :half] = hi
        o_ref[:, half:] = lo

        @pl.when(r + 1 < nv)
        def _():
            gather_start(r + 1)

    _zero_unused_tile(r, nv, o_ref)


def _zero_unused_tile(r, n_valid, o_ref):
    @pl.when(r >= n_valid)
    def _():
        o_ref[...] = jnp.zeros(o_ref.shape, o_ref.dtype)


def _moe_up_kernel(te_ref, nv_ref, xs_ref, w1_ref, w3_ref, o_ref, wb1_ref, wb3_ref):
    r = pl.program_id(1)

    @pl.when(r < nv_ref[0])
    def _():
        @pl.when((r == 0) | (te_ref[r] != te_ref[jnp.maximum(r - 1, 0)]))
        def _():
            wb1_ref[...] = w1_ref[0].astype(BF16)
            wb3_ref[...] = w3_ref[0].astype(BF16)

        xb = xs_ref[...]
        a = jnp.dot(xb, wb1_ref[...], preferred_element_type=F32)
        b = jnp.dot(xb, wb3_ref[...], preferred_element_type=F32)
        o_ref[...] = (a * jax.nn.sigmoid(a) * b).astype(o_ref.dtype)

    _zero_unused_tile(r, nv_ref[0], o_ref)


def _moe_down_kernel(te_ref, nv_ref, h_ref, w2_ref, o_ref, wb_ref):
    r = pl.program_id(1)

    @pl.when(r < nv_ref[0])
    def _():
        @pl.when((r == 0) | (te_ref[r] != te_ref[jnp.maximum(r - 1, 0)]))
        def _():
            wb_ref[...] = w2_ref[0].astype(BF16)

        o_ref[...] = jnp.dot(h_ref[...], wb_ref[...], preferred_element_type=F32)

    _zero_unused_tile(r, nv_ref[0], o_ref)


def _experts(x, w1, w3, w2, tile_expert, n_valid, row_tok, *, tm, tf=512, tn=512):
    D = 2 * x.shape[1]
    P = row_tok.shape[0]
    E, _, F = w1.shape
    tf, tn = _tile(F, tf), _tile(D, tn)
    n_tiles = P // tm
    row = lambda r, nv: jnp.minimum(r, nv[0] - 1)

    xs = pl.pallas_call(
        functools.partial(_moe_gather_kernel, tm=tm),
        grid_spec=pltpu.PrefetchScalarGridSpec(
            num_scalar_prefetch=2, grid=(n_tiles,),
            in_specs=[pl.BlockSpec(memory_space=pl.ANY)],
            out_specs=pl.BlockSpec((tm, D), lambda r, tok, nv: (r, 0)),
            scratch_shapes=[pltpu.VMEM((tm, D // 2), I32), pltpu.SemaphoreType.DMA(())]),
        out_shape=jax.ShapeDtypeStruct((P, D), BF16),
        compiler_params=_params("arbitrary"),
        name="moe_gather",
    )(row_tok, n_valid, x)

    hid = pl.pallas_call(
        _moe_up_kernel,
        grid_spec=pltpu.PrefetchScalarGridSpec(
            num_scalar_prefetch=2, grid=(F // tf, n_tiles),
            in_specs=[pl.BlockSpec((tm, D), lambda f, r, te, nv: (row(r, nv), 0)),
                      pl.BlockSpec((1, D, tf), lambda f, r, te, nv: (te[r], 0, f)),
                      pl.BlockSpec((1, D, tf), lambda f, r, te, nv: (te[r], 0, f))],
            out_specs=pl.BlockSpec((tm, tf), lambda f, r, te, nv: (r, f)),
            scratch_shapes=[pltpu.VMEM((D, tf), BF16), pltpu.VMEM((D, tf), BF16)]),
        out_shape=jax.ShapeDtypeStruct((P, F), BF16),
        compiler_params=_params("arbitrary", "arbitrary"),
        name="moe_up",
    )(tile_expert, n_valid, xs, w1, w3)

    return pl.pallas_call(
        _moe_down_kernel,
        grid_spec=pltpu.PrefetchScalarGridSpec(
            num_scalar_prefetch=2, grid=(D // tn, n_tiles),
            in_specs=[pl.BlockSpec((tm, F), lambda n, r, te, nv: (row(r, nv), 0)),
                      pl.BlockSpec((1, F, tn), lambda n, r, te, nv: (te[r], 0, n))],
            out_specs=pl.BlockSpec((tm, tn), lambda n, r, te, nv: (r, n)),
            scratch_shapes=[pltpu.VMEM((F, tn), BF16)]),
        out_shape=jax.ShapeDtypeStruct((P, D), F32),
        compiler_params=_params("arbitrary", "arbitrary"),
        name="moe_down",
    )(tile_expert, n_valid, hid, w2)


def _combine_kernel(dest_ref, x_ref, meta_ref, g_ref, b_ref, ys_hbm, o_ref, buf_ref, sem):
    bm = x_ref.shape[0]
    i = pl.program_id(0)
    n = pl.num_programs(0)

    def copy(slot, j, k, d):
        return pltpu.make_async_copy(ys_hbm.at[pl.ds(d, 1)], buf_ref.at[slot, k, pl.ds(j, 1)], sem.at[slot, k])

    def gather_start(blk):
        slot = blk % 2

        def body(j, carry):
            for k in range(2):
                copy(slot, j, k, dest_ref[2 * (blk * bm + j) + k]).start()
            return carry
        lax.fori_loop(0, bm, body, 0, unroll=DMA_LOOP_UNROLL)

    @pl.when(i == 0)
    def _():
        gather_start(0)

    @pl.when(i + 1 < n)
    def _():
        gather_start(i + 1)

    slot = i % 2

    def wait(j, carry):
        for k in range(2):
            copy(slot, j, k, 0).wait()
        return carry

    lax.fori_loop(0, bm, wait, 0, unroll=DMA_LOOP_UNROLL)
    meta = meta_ref[...]
    y = meta[:, 2:3] * buf_ref[slot, 0] + meta[:, 3:4] * buf_ref[slot, 1]
    o_ref[...] = _layer_norm_rows(ALPHA * x_ref[...] + y, g_ref[...], b_ref[...])


def _combine(x, meta, ys, dest, g, b, *, bm=256):
    M, D = x.shape
    bm = _tile(M, bm, 8)
    row = lambda n: pl.BlockSpec((bm, n), lambda i, d: (i, 0))
    vec = pl.BlockSpec((1, D), lambda i, d: (0, 0))
    return pl.pallas_call(
        _combine_kernel,
        grid_spec=pltpu.PrefetchScalarGridSpec(
            num_scalar_prefetch=1, grid=(M // bm,),
            in_specs=[row(D), row(LANE), vec, vec, pl.BlockSpec(memory_space=pl.ANY)],
            out_specs=row(D),
            scratch_shapes=[pltpu.VMEM((2, 2, bm, D), F32), pltpu.SemaphoreType.DMA((2, 2))]),
        out_shape=jax.ShapeDtypeStruct((M, D), F32),
        compiler_params=_params("arbitrary"),
        name="moe_combine",
    )(dest, x, meta, g.reshape(1, D), b.reshape(1, D), ys)


def _even_layer(x, xb, w_in, ret_gn_g, w_out, ln1_g, ln1_b, w1, w3, w2, ln2_g, ln2_b, *, B, T):
    N, D = x.shape
    a_heads = D // 2 // A_HEAD_DIM
    r_heads = D // 2 // RET_VAL_DIM
    qa_w, kv_w = a_heads * A_HEAD_DIM, A_KV_HEADS * A_HEAD_DIM
    qi_w = IDX_HEADS * IDX_DIM
    rk_w, rv_w = r_heads * RET_KEY_DIM, r_heads * RET_VAL_DIM
    sizes = (qa_w, kv_w, kv_w, qi_w, IDX_DIM, IDX_HEADS, rk_w, rk_w, rv_w, rv_w)
    offs = [0]
    for s in sizes:
        offs.append(offs[-1] + s)
    w_in_t = w_in.T
    col = lambda a, b_: w_in_t[offs[a]:offs[b_]]
    w_qa, w_ka, w_va = col(0, 1), col(1, 2), col(2, 3)
    w_qi, w_ki, w_wi = col(3, 4), col(4, 5), col(5, 6)
    w_qb, w_kb, w_vb, w_gb = col(6, 7), col(7, 8), col(8, 9), col(9, 10)

    cos_a, sin_a = _rope_cos_sin(T, A_HEAD_DIM // 4, ROPE_THETA)
    tab_q = _lane_tables(cos_a, sin_a, A_HEAD_DIM, A_HEAD_DIM ** -0.5 * LOG2E)
    tab_k = _lane_tables(cos_a, sin_a, A_HEAD_DIM)
    cos_i, sin_i = _rope_cos_sin(T, IDX_DIM // 4, ROPE_THETA)
    tab_i = _lane_tables(cos_i, sin_i, IDX_DIM)
    pass_c = jnp.ones((T, LANE - IDX_DIM), F32)
    pass_s = jnp.zeros((T, LANE - IDX_DIM), F32)
    tab_idx = tuple(jnp.concatenate([t, t[:, :IDX_DIM], p], 1)
                    for t, p in zip(tab_i, (pass_c, pass_s, pass_s)))
    inv = 1.0 / (RET_THETA ** jnp.linspace(0.0, 1.0, RET_KEY_DIM // 2, dtype=F32))
    ang = jnp.arange(T, dtype=F32)[:, None] * inv[None, :]
    cos_r, sin_r = jnp.cos(ang), jnp.sin(ang)
    sin_pair = jnp.concatenate([-sin_r, sin_r], 1)
    tab_r = (jnp.concatenate([cos_r, cos_r], 1), sin_pair, sin_pair)

    idx_pad = LANE - IDX_DIM - IDX_HEADS
    w_idx = jnp.concatenate([w_qi, w_ki, w_wi * (IDX_DIM ** -0.5 * IDX_HEADS ** -0.5),
                             jnp.zeros((idx_pad, D), F32)], 0).astype(BF16)
    w_rqk = jnp.concatenate([w_qb, w_kb * RET_KEY_DIM ** -0.5], 0).astype(BF16)
    w_pv = jnp.concatenate([w_va, w_vb, w_gb], 0).astype(BF16)
    qa = _proj(xb, w_qa.astype(BF16), out_dtype=BF16, tabs=tab_q, half=A_HEAD_DIM // 8, w_rows_out=True,
               seq_len=T, name="proj_qa", **_pat(qa_w, 1024, 0))
    ka = _proj(xb, w_ka.astype(BF16), out_dtype=BF16, tabs=tab_k, half=A_HEAD_DIM // 8, w_rows_out=True,
               seq_len=T, name="proj_ka", **_pat(kv_w, 1024, 0))
    n_idx = w_idx.shape[0]
    idx = _proj(xb, w_idx, out_dtype=F32, tabs=tab_idx, half=IDX_DIM // 8, seq_len=T, name="proj_idx",
                w_rows_out=True, bm=512, bn=n_idx, slab_pat=(0,) * (qi_w // LANE) + (1,))
    rqk = _proj(xb, w_rqk, out_dtype=BF16, tabs=tab_r, mode="pair", seq_len=T, name="proj_ret_qk",
                w_rows_out=True, bn=1024, slab_pat=(0, 1) * (_tile(2 * rk_w, 1024) // (2 * LANE)))
    pv = _proj(xb, w_pv, out_dtype=BF16, bn=768, seq_len=T, w_rows_out=True, name="proj_v")

    ya = _dsa(qa.reshape(B, T, qa_w), ka.reshape(B, T, kv_w), pv.reshape(B, T, -1), idx.reshape(B, T, n_idx),
              B=B, T=T)
    yb = _retention(rqk, pv, ret_gn_g, B=B, T=T, heads=r_heads, v_blk0=kv_w // RET_VAL_DIM)
    w_out_b = w_out.astype(BF16)
    y = _proj(ya.reshape(N, qa_w), w_out_b[:qa_w], second=(yb, w_out_b[qa_w:]), out_dtype=F32, name="proj_out0")
    x1, x1b = _add_ln(x, y, ln1_g, ln1_b)
    hid = _swiglu_up(x1b, w1.astype(BF16), w3.astype(BF16))
    y = _mm_ksplit(hid, w2.astype(BF16), bk=3584, name="ffn_down")
    return _add_ln(x1, y, ln2_g, ln2_b)


def _pat(width, bn, p):
    bn = _tile(width, bn)
    return dict(bn=bn, slab_pat=(p,) * (bn // LANE))


def _odd_layer(x, xb, w_dq_dkv, q_norm_g, w_uq, kv_norm_g, w_ukv, w_out, ln1_g, ln1_b,
               router, we1, we3, we2, ln2_g, ln2_b, *, B, T):
    N, D = x.shape
    heads = D // MLA_V
    q_rank, kv_rank = q_norm_g.shape[0], kv_norm_g.shape[0]
    scale = (MLA_NOPE + MLA_ROPE) ** -0.5 * LOG2E
    cos_c, sin_c = _rope_cos_sin(T, MLA_ROPE, ROPE_THETA)
    tab_kr = _lane_tables(cos_c, sin_c, MLA_ROPE)
    tab_qr = _lane_tables(cos_c, sin_c, MLA_ROPE, scale)

    w_kr = w_dq_dkv[:, q_rank + kv_rank:]
    w_down = jnp.concatenate([w_dq_dkv[:, :q_rank + kv_rank], w_kr, w_kr], 1).astype(BF16)
    cq, ckv, kr = _mla_down(xb, w_down, q_norm_g, kv_norm_g, tab_kr, T=T, q_rank=q_rank, kv_rank=kv_rank)
    w_uq3 = w_uq.reshape(q_rank, heads, MLA_NOPE + MLA_ROPE)
    w_qn = w_uq3[:, :, :MLA_NOPE].reshape(q_rank, heads * MLA_NOPE).astype(BF16)
    w_qr = w_uq3[:, :, MLA_NOPE:].reshape(q_rank, heads * MLA_ROPE).astype(BF16)
    w_kv3 = w_ukv.reshape(kv_rank, heads, MLA_NOPE + MLA_V)
    w_kv = jnp.concatenate([w_kv3[:, :, :MLA_NOPE].reshape(kv_rank, heads * MLA_NOPE),
                            w_kv3[:, :, MLA_NOPE:].reshape(kv_rank, heads * MLA_V)], 1).astype(BF16)
    qn = _proj(cq, w_qn, out_dtype=BF16, scale=scale, seq_len=T, bn=2048, name="proj_q_nope")
    qr = _proj(cq, w_qr, out_dtype=BF16, tabs=tab_qr, half=MLA_ROPE // 2, seq_len=T, name="proj_q_rope",
               **_pat(heads * MLA_ROPE, 1024, 0))
    kv = _proj(ckv, w_kv, out_dtype=BF16, seq_len=T, bn=2048, name="proj_kv")
    att = _mla_attn(qn, qr, kv, kr, B=B, T=T, heads=heads)
    y = _proj(att, w_out.astype(BF16), out_dtype=F32, name="proj_out1")
    x1, x1_packed, logits = _add_ln(x, y, ln1_g, ln1_b, packed=True, router=router)

    E = router.shape[1]
    tm = _tile(N, 512)
    meta, cnt = _router(logits, E)
    counts = cnt[0, :E].astype(I32)
    padded = (counts + tm - 1) // tm * tm
    ends = jnp.cumsum(padded)
    starts = ends - padded
    i1, i2 = meta[:, 0].astype(I32), meta[:, 1].astype(I32)
    dest = jnp.stack([starts[i1] + meta[:, 4].astype(I32), starts[i2] + meta[:, 5].astype(I32)], 1).reshape(-1)
    n_rows = 2 * N + E * tm
    n_tiles = n_rows // tm
    n_valid = (ends[-1] // tm).astype(I32).reshape(1)
    tile_start = jnp.arange(n_tiles, dtype=I32) * tm
    tile_expert = jnp.minimum(jnp.sum(tile_start[:, None] >= ends[None, :], axis=1), E - 1).astype(I32)
    tile_expert = jnp.where(jnp.arange(n_tiles) < n_valid[0], tile_expert, tile_expert[jnp.maximum(n_valid[0] - 1, 0)])
    row_tok = jnp.zeros((n_rows,), I32).at[dest].set(jnp.repeat(jnp.arange(N, dtype=I32), 2))
    ys = _experts(x1_packed, we1, we3, we2, tile_expert, n_valid, row_tok, tm=tm)
    return _combine(x1, meta, ys, dest, ln2_g, ln2_b)


def kernel(x, l0_w_in, l0_ret_gn_g, l0_w_out, l0_ln1_g, l0_ln1_b, l0_ffn_w1, l0_ffn_w3, l0_ffn_w2, l0_ln2_g, l0_ln2_b, l1_w_dq_dkv, l1_q_norm_g, l1_w_uq, l1_kv_norm_g, l1_w_ukv, l1_w_out, l1_ln1_g, l1_ln1_b, l1_router, l1_moe_w1, l1_moe_w3, l1_moe_w2, l1_ln2_g, l1_ln2_b):
    B, T, D = x.shape
    x2 = x.reshape(B * T, D)
    h, hb = _even_layer(x2, x2.astype(BF16), l0_w_in, l0_ret_gn_g, l0_w_out, l0_ln1_g, l0_ln1_b,
                        l0_ffn_w1, l0_ffn_w3, l0_ffn_w2, l0_ln2_g, l0_ln2_b, B=B, T=T)
    out = _odd_layer(h, hb, l1_w_dq_dkv, l1_q_norm_g, l1_w_uq, l1_kv_norm_g, l1_w_ukv, l1_w_out,
                     l1_ln1_g, l1_ln1_b, l1_router, l1_moe_w1, l1_moe_w3, l1_moe_w2, l1_ln2_g, l1_ln2_b, B=B, T=T)
    return out.reshape(B, T, D)
```

```python
import functools

import jax
import jax.numpy as jnp
from jax import lax
from jax.experimental import pallas as pl
from jax.experimental.pallas import tpu as pltpu

F32 = jnp.float32
BF16 = jnp.bfloat16
I32 = jnp.int32

A_HEAD_DIM = 128
A_KV_HEADS = 4
IDX_HEADS = 16
IDX_DIM = 64
DSA_TOPK_MAX = 256
RET_KEY_DIM = 256
RET_VAL_DIM = 256
RET_CHUNK = 128
RET_THETA = 10000.0
MLA_V = 128
MLA_NOPE = 128
MLA_ROPE = 64
ROPE_THETA = 500000.0
Q_BLOCK = 128
LN_EPS = 1e-5
RMS_EPS = 1e-6
DEPTH = 2
ALPHA = (2.0 * DEPTH) ** 0.25

LANE = 128
V7X_VMEM_BYTES = 64 * 1024 * 1024
VMEM_LIMIT = V7X_VMEM_BYTES - 8 * 1024 * 1024
MASKED = -1e30
INT_MIN = -(2 ** 31)

NT_DIMS = (((1,), (1,)), ((), ()))
TN_DIMS = (((0,), (0,)), ((), ()))


def _tile(n, pref, mult=LANE):
    if n <= pref:
        return n
    t = (pref // mult) * mult
    while t > mult and n % t:
        t -= mult
    assert n % t == 0, (n, pref, mult)
    return t


def _params(*sem):
    return pltpu.CompilerParams(dimension_semantics=sem, vmem_limit_bytes=VMEM_LIMIT)


def _rope_cos_sin(T, rot_dim, theta):
    inv = theta ** (-jnp.arange(0, rot_dim, 2, dtype=F32) / rot_dim)
    ang = jnp.arange(T, dtype=F32)[:, None] * inv[None, :]
    return jnp.cos(ang), jnp.sin(ang)


def _lane_tables(cos, sin, head_dim, scale=1.0):
    T, half = cos.shape
    rest = head_dim - 2 * half
    zh = jnp.zeros((T, half), F32)
    c = jnp.concatenate([cos, cos, jnp.ones((T, rest), F32)], 1)
    sa = jnp.concatenate([-sin, zh, jnp.zeros((T, rest), F32)], 1)
    sb = jnp.concatenate([zh, sin, jnp.zeros((T, rest), F32)], 1)
    reps = LANE // head_dim
    return tuple(jnp.tile(t * scale, (1, reps)) for t in (c, sa, sb))


def _proj_kernel(*refs, slab_pat, mode, half, scale, with_tab, two_inputs, w_rows_out=False):
    x_ref, w_ref = refs[:2]
    if w_rows_out:
        acc = lax.dot_general(x_ref[...], w_ref[...], NT_DIMS, preferred_element_type=F32)
    else:
        acc = jnp.dot(x_ref[...], w_ref[...], preferred_element_type=F32)
    refs = refs[2:]
    if two_inputs:
        acc = acc + jnp.dot(refs[0][...], refs[1][...], preferred_element_type=F32)
        refs = refs[2:]
    if with_tab:
        c_ref, sa_ref, sb_ref, o_ref = refs
    else:
        (o_ref,) = refs
    for s, p in enumerate(slab_pat):
        a = acc[:, s * LANE:(s + 1) * LANE]
        if p < 0:
            out = a if scale == 1.0 else a * scale
        else:
            c = c_ref[:, p * LANE:(p + 1) * LANE]
            sa = sa_ref[:, p * LANE:(p + 1) * LANE]
            if mode == "lane":
                sb = sb_ref[:, p * LANE:(p + 1) * LANE]
                out = a * c + pltpu.roll(a, LANE - half, 1) * sa + pltpu.roll(a, half, 1) * sb
            else:
                q = s ^ 1
                out = a * c + acc[:, q * LANE:(q + 1) * LANE] * sa
        o_ref[:, s * LANE:(s + 1) * LANE] = out.astype(o_ref.dtype)


def _proj(x, w, *, out_dtype, bm=1024, bn=1024, tabs=None, slab_pat=None, mode="lane", half=0,
          scale=1.0, seq_len=None, second=None, w_rows_out=False, name="proj"):
    M, K = x.shape
    N = w.shape[0] if w_rows_out else w.shape[1]
    bm = _tile(M, bm) if seq_len is None else _tile(seq_len, bm)
    bn = _tile(N, bn)
    if slab_pat is None:
        slab_pat = (-1,) * (bn // LANE)
    assert len(slab_pat) == bn // LANE
    w_spec = pl.BlockSpec((bn, K), lambda i, j: (j, 0)) if w_rows_out else pl.BlockSpec((K, bn), lambda i, j: (0, j))
    in_specs = [pl.BlockSpec((bm, K), lambda i, j: (i, 0)), w_spec]
    args = [x, w]
    if second is not None:
        x2, w2 = second
        in_specs += [pl.BlockSpec((bm, x2.shape[1]), lambda i, j: (i, 0)),
                     pl.BlockSpec((x2.shape[1], bn), lambda i, j: (0, j))]
        args += [x2, w2]
    if tabs is not None:
        tb = seq_len // bm
        tw = tabs[0].shape[1]
        in_specs += [pl.BlockSpec((bm, tw), lambda i, j: (i % tb, 0))] * 3
        args += list(tabs)
    kern = functools.partial(_proj_kernel, slab_pat=tuple(slab_pat), mode=mode, half=half,
                             scale=scale, with_tab=tabs is not None, two_inputs=second is not None,
                             w_rows_out=w_rows_out)
    return pl.pallas_call(
        kern,
        grid=(M // bm, N // bn),
        in_specs=in_specs,
        out_specs=pl.BlockSpec((bm, bn), lambda i, j: (i, j)),
        out_shape=jax.ShapeDtypeStruct((M, N), out_dtype),
        compiler_params=_params("parallel", "parallel"),
        name=name,
    )(*args)


def _mm_ksplit_kernel(x_ref, w_ref, o_ref):
    part = jnp.dot(x_ref[...], w_ref[...], preferred_element_type=F32)

    @pl.when(pl.program_id(2) == 0)
    def _():
        o_ref[...] = part

    @pl.when(pl.program_id(2) > 0)
    def _():
        o_ref[...] += part


def _mm_ksplit(x, w, *, bm=1024, bn=1024, bk=2048, name="mm_ksplit"):
    M, K = x.shape
    N = w.shape[1]
    bm, bn, bk = _tile(M, bm), _tile(N, bn), _tile(K, bk)
    return pl.pallas_call(
        _mm_ksplit_kernel,
        grid=(M // bm, N // bn, K // bk),
        in_specs=[pl.BlockSpec((bm, bk), lambda i, j, k: (i, k)),
                  pl.BlockSpec((bk, bn), lambda i, j, k: (k, j))],
        out_specs=pl.BlockSpec((bm, bn), lambda i, j, k: (i, j)),
        out_shape=jax.ShapeDtypeStruct((M, N), F32),
        compiler_params=_params("parallel", "parallel", "arbitrary"),
        name=name,
    )(x, w)


def _swiglu_up_kernel(x_ref, w1_ref, w3_ref, o_ref):
    x = x_ref[...]
    a = jnp.dot(x, w1_ref[...], preferred_element_type=F32)
    b = jnp.dot(x, w3_ref[...], preferred_element_type=F32)
    o_ref[...] = (a * jax.nn.sigmoid(a) * b).astype(o_ref.dtype)


def _swiglu_up(x, w1, w3, *, bm=1024, bn=512):
    M, K = x.shape
    N = w1.shape[1]
    bm, bn = _tile(M, bm), _tile(N, bn)
    return pl.pallas_call(
        _swiglu_up_kernel,
        grid=(M // bm, N // bn),
        in_specs=[pl.BlockSpec((bm, K), lambda i, j: (i, 0)),
                  pl.BlockSpec((K, bn), lambda i, j: (0, j)),
                  pl.BlockSpec((K, bn), lambda i, j: (0, j))],
        out_specs=pl.BlockSpec((bm, bn), lambda i, j: (i, j)),
        out_shape=jax.ShapeDtypeStruct((M, N), BF16),
        compiler_params=_params("parallel", "parallel"),
        name="swiglu_up",
    )(x, w1, w3)


def _layer_norm_rows(z, g, b):
    mu = jnp.mean(z, axis=-1, keepdims=True)
    zc = z - mu
    var = jnp.mean(zc * zc, axis=-1, keepdims=True)
    return zc * lax.rsqrt(var + LN_EPS) * g + b


def _pack_bf16_pairs(x):
    half = x.shape[1] // 2
    hi = pltpu.bitcast(x[:, :half].astype(BF16).astype(F32), I32)
    lo = pltpu.bitcast(x[:, half:].astype(BF16).astype(F32), I32)
    return hi | lax.shift_right_logical(lo, 16)


def _unpack_bf16_pairs(w):
    hi = pltpu.bitcast(w & jnp.int32(-65536), F32).astype(BF16)
    lo = pltpu.bitcast(lax.shift_left(w, 16), F32).astype(BF16)
    return hi, lo


def _add_ln_kernel(x_ref, y_ref, g_ref, b_ref, *rest, packed, with_router):
    out = _layer_norm_rows(ALPHA * x_ref[...] + y_ref[...], g_ref[...], b_ref[...])
    if with_router:
        r_ref, of_ref, o2_ref, lg_ref = rest
        lg_ref[...] = jnp.dot(out, r_ref[...], preferred_element_type=F32, precision=lax.Precision.HIGHEST)
    else:
        of_ref, o2_ref = rest
    of_ref[...] = out
    o2_ref[...] = _pack_bf16_pairs(out) if packed else out.astype(o2_ref.dtype)


def _add_ln(x, y, g, b, *, packed=False, router=None, bm=256):
    M, D = x.shape
    bm = _tile(M, bm, 8)
    row = pl.BlockSpec((bm, D), lambda i: (i, 0))
    vec = pl.BlockSpec((1, D), lambda i: (0, 0))
    second = jax.ShapeDtypeStruct((M, D // 2), I32) if packed else jax.ShapeDtypeStruct((M, D), BF16)
    in_specs, args = [row, row, vec, vec], [x, y, g.reshape(1, D), b.reshape(1, D)]
    out_specs = [row, pl.BlockSpec((bm, second.shape[1]), lambda i: (i, 0))]
    out_shape = [jax.ShapeDtypeStruct((M, D), F32), second]
    if router is not None:
        in_specs.append(pl.BlockSpec((D, LANE), lambda i: (0, 0)))
        args.append(jnp.zeros((D, LANE), F32).at[:, :router.shape[1]].set(router))
        out_specs.append(pl.BlockSpec((bm, LANE), lambda i: (i, 0)))
        out_shape.append(jax.ShapeDtypeStruct((M, LANE), F32))
    return pl.pallas_call(
        functools.partial(_add_ln_kernel, packed=packed, with_router=router is not None),
        grid=(M // bm,),
        in_specs=in_specs,
        out_specs=out_specs,
        out_shape=out_shape,
        compiler_params=_params("parallel"),
        name="add_ln",
    )(*args)


LOG2E = 1.4426950408889634


def _online_softmax(s, m, l):
    m_new = jnp.maximum(m, jnp.max(s, axis=1, keepdims=True))
    a = jnp.exp2(m - m_new)
    p = jnp.exp2((s - m_new).astype(BF16))
    return m_new, a * l + jnp.sum(p.astype(F32), axis=1, keepdims=True), a, p


def _dsa_kernel(q_ref, k_ref, v_ref, iq_ref, ik_ref, o_ref, keys_ref, bias_ref, *, topk, ck, n_rep, idx_bits):
    blk = pl.program_id(1)
    n_chunks = (blk * Q_BLOCK + Q_BLOCK + ck - 1) // ck
    row = lax.broadcasted_iota(I32, (Q_BLOCK, ck), 0) + blk * Q_BLOCK
    lane = lax.broadcasted_iota(I32, (Q_BLOCK, ck), 1)
    w_off = IDX_HEADS * IDX_DIM + IDX_DIM
    iq = iq_ref[0]
    wi = iq[:, w_off:w_off + IDX_HEADS]
    q_idx = jnp.concatenate([iq[:, h * IDX_DIM:(h + 1) * IDX_DIM].astype(BF16) for h in range(IDX_HEADS)], axis=0)

    def score_body(c, carry):
        off = pl.multiple_of(c * ck, ck)
        kc = ik_ref[0, pl.ds(off, ck), :][:, :IDX_DIM].astype(BF16)
        lg = lax.dot_general(q_idx, kc, NT_DIMS, preferred_element_type=F32)
        s = jnp.zeros((Q_BLOCK, ck), F32)
        for h in range(IDX_HEADS):
            s = s + wi[:, h:h + 1] * jnp.maximum(lg[h * Q_BLOCK:(h + 1) * Q_BLOCK], 0.0)
        bits = pltpu.bitcast(s, I32)
        key = bits ^ ((bits >> 31) & 0x7FFFFFFF)
        keys_ref[c] = jnp.where(lane + off <= row, key, INT_MIN)
        return carry

    lax.fori_loop(0, n_chunks, score_body, 0)

    def count(indicator):
        def body(c, acc):
            part = indicator(keys_ref[c], lane + c * ck)
            for j in range(ck // LANE):
                acc = acc + part[:, j * LANE:(j + 1) * LANE]
            return acc
        acc = lax.fori_loop(0, n_chunks, body, jnp.zeros((Q_BLOCK, LANE), F32))
        return jnp.sum(acc, axis=1, keepdims=True)

    kf = float(topk)
    ok = count(lambda kc, idx: jnp.where(kc >= 0, 1.0, 0.0)) >= kf
    thr = jnp.where(ok, 0, INT_MIN).astype(I32)

    def bit_body(i, thr):
        cand = thr + lax.shift_left(jnp.int32(1), 30 - i)
        ok = count(lambda kc, idx: jnp.where(kc >= cand, 1.0, 0.0)) >= kf
        return jnp.where(ok, cand, thr)

    thr = lax.fori_loop(0, 31, bit_body, thr)

    n_gt = count(lambda kc, idx: jnp.where(kc > thr, 1.0, 0.0))
    n_ge = count(lambda kc, idx: jnp.where(kc >= thr, 1.0, 0.0))
    need = kf - n_gt
    has_thr = thr > INT_MIN
    surplus = jnp.where(has_thr, n_ge - n_gt - need, 0.0)

    def tie_search():
        def tie_body(i, last):
            cand = last + lax.shift_left(jnp.int32(1), idx_bits - 1 - i)
            ok = count(lambda kc, idx: jnp.where(kc == thr, jnp.where(idx < cand, 1.0, 0.0), 0.0)) < need
            return jnp.where(ok, cand, last)
        return lax.fori_loop(0, idx_bits, tie_body, jnp.zeros((Q_BLOCK, 1), I32))

    last = lax.cond(jnp.max(surplus) > 0.0, tie_search, lambda: jnp.full((Q_BLOCK, 1), 2 ** idx_bits, I32))
    last = jnp.where(has_thr, last, -1)

    def bias_body(c, carry):
        kc = keys_ref[c]
        tie_bias = jnp.where(lane + c * ck <= last, 0.0, MASKED)
        bias_ref[c] = jnp.where(kc == thr, tie_bias, jnp.where(kc > thr, 0.0, MASKED))
        return carry

    lax.fori_loop(0, n_chunks, bias_body, 0)

    q = q_ref[0]
    rows = n_rep * Q_BLOCK
    qgs = [jnp.concatenate([q[:, (g * n_rep + r) * A_HEAD_DIM:(g * n_rep + r + 1) * A_HEAD_DIM]
                            for r in range(n_rep)], axis=0) for g in range(A_KV_HEADS)]

    def att_body(c, carry, n_sub):
        width = n_sub * ck
        off = pl.multiple_of(c * width, width)
        b = jnp.concatenate([bias_ref[c * n_sub + t] for t in range(n_sub)], axis=1)
        bias = jnp.concatenate([b] * n_rep, axis=0)
        out = []
        for g in range(A_KV_HEADS):
            m, l, acc = carry[g]
            kc = k_ref[0, pl.ds(off, width), g * A_HEAD_DIM:(g + 1) * A_HEAD_DIM]
            vc = v_ref[0, pl.ds(off, width), g * A_HEAD_DIM:(g + 1) * A_HEAD_DIM]
            s = lax.dot_general(qgs[g], kc, NT_DIMS, preferred_element_type=F32) + bias
            m, l, a, p = _online_softmax(s, m, l)
            out.append((m, l, a * acc + jnp.dot(p, vc, preferred_element_type=F32)))
        return tuple(out)

    one = (jnp.full((rows, 1), MASKED, F32), jnp.zeros((rows, 1), F32), jnp.zeros((rows, A_HEAD_DIM), F32))
    carry = (one,) * A_KV_HEADS
    if k_ref.shape[1] >= 2 * ck:
        carry = lax.fori_loop(0, n_chunks // 2, functools.partial(att_body, n_sub=2), carry)
        carry = lax.cond(n_chunks % 2 == 1, lambda c: att_body(n_chunks - 1, c, 1), lambda c: c, carry)
    else:
        carry = att_body(0, carry, 1)
    for g in range(A_KV_HEADS):
        _, l, acc = carry[g]
        o = acc / l
        for r in range(n_rep):
            col = (g * n_rep + r) * A_HEAD_DIM
            o_ref[0, :, col:col + A_HEAD_DIM] = o[r * Q_BLOCK:(r + 1) * Q_BLOCK].astype(o_ref.dtype)


def _dsa(q, k, v, idx, *, B, T):
    a_heads = q.shape[-1] // A_HEAD_DIM
    n_rep = a_heads // A_KV_HEADS
    topk = min(DSA_TOPK_MAX, T // 4)
    ck = _tile(T, 512)
    idx_w = idx.shape[-1]
    kv_w = k.shape[-1]
    kern = functools.partial(_dsa_kernel, topk=topk, ck=ck, n_rep=n_rep, idx_bits=max(1, (T - 1).bit_length()))
    return pl.pallas_call(
        kern,
        grid=(B, T // Q_BLOCK),
        in_specs=[pl.BlockSpec((1, Q_BLOCK, q.shape[-1]), lambda b, i: (b, i, 0)),
                  pl.BlockSpec((1, T, kv_w), lambda b, i: (b, 0, 0)),
                  pl.BlockSpec((1, T, kv_w), lambda b, i: (b, 0, 0)),
                  pl.BlockSpec((1, Q_BLOCK, idx_w), lambda b, i: (b, i, 0)),
                  pl.BlockSpec((1, T, LANE), lambda b, i: (b, 0, IDX_HEADS * IDX_DIM // LANE))],
        out_specs=pl.BlockSpec((1, Q_BLOCK, q.shape[-1]), lambda b, i: (b, i, 0)),
        out_shape=jax.ShapeDtypeStruct(q.shape, BF16),
        scratch_shapes=[pltpu.VMEM((T // ck, Q_BLOCK, ck), I32), pltpu.VMEM((T // ck, Q_BLOCK, ck), F32)],
        compiler_params=_params("parallel", "parallel"),
        name="dsa",
    )(q, k, v, idx, idx)


RET_HEADS_PER_STEP = 2


def _retention_kernel(q_ref, k_ref, v_ref, g_ref, gn_ref, din_ref, qd_ref, kd_ref, cd_ref, o_ref, state_ref, *, n_sub):
    @pl.when(pl.program_id(2) == 0)
    def _():
        state_ref[...] = jnp.zeros_like(state_ref)

    W = RET_VAL_DIM
    for s in range(n_sub):
        sl = pl.ds(s * RET_CHUNK, RET_CHUNK)
        for j in range(RET_HEADS_PER_STEP):
            cols = slice(j * W, (j + 1) * W)
            qc = q_ref[sl, cols]
            kc = k_ref[sl, cols]
            vc = v_ref[sl, cols]
            st = state_ref[j]
            inner = lax.dot_general(qc, kc, NT_DIMS, preferred_element_type=F32) * din_ref[j]
            o = (jnp.dot(inner.astype(BF16), vc, preferred_element_type=F32)
                 + jnp.dot(qc, st.astype(BF16), preferred_element_type=F32) * qd_ref[j])
            vk = (vc.astype(F32) * kd_ref[j]).astype(BF16)
            state_ref[j] = st * cd_ref[j] + lax.dot_general(kc, vk, TN_DIMS, preferred_element_type=F32)
            mu = jnp.mean(o, axis=-1, keepdims=True)
            oc = o - mu
            var = jnp.mean(oc * oc, axis=-1, keepdims=True)
            gate = g_ref[sl, cols].astype(F32)
            normed = oc * lax.rsqrt(var + LN_EPS) * gn_ref[:, cols]
            o_ref[sl, cols] = (gate * jax.nn.sigmoid(gate) * normed).astype(o_ref.dtype)


def _retention(qk, pv, gn_g, *, B, T, heads, v_blk0):
    N = qk.shape[0]
    C = RET_CHUNK
    rb = _tile(T, 512)
    n_sub = rb // C
    nr = T // rb
    log_gamma = jnp.log(1.0 - 2.0 ** (-5.0 - jnp.arange(heads, dtype=F32)))
    pos = jnp.arange(C, dtype=F32)
    diff = pos[:, None] - pos[None, :]
    din = jnp.exp(jnp.where(diff[None] >= 0, log_gamma[:, None, None] * diff[None], -jnp.inf))
    qd = jnp.exp(log_gamma[:, None] * (pos[None] + 1.0))[:, :, None]
    kd = jnp.exp(log_gamma[:, None] * (C - 1.0 - pos[None]))[:, :, None]
    cd = jnp.exp(log_gamma * C)[:, None, None]
    hp = RET_HEADS_PER_STEP
    W = hp * RET_VAL_DIM
    assert heads % hp == 0 and v_blk0 % hp == 0
    blk = lambda off: pl.BlockSpec((rb, W), lambda b, h, r: (b * nr + r, off // hp + h))
    per_head = lambda shape: pl.BlockSpec((hp,) + shape, lambda b, h, r: (h, 0, 0))
    return pl.pallas_call(
        functools.partial(_retention_kernel, n_sub=n_sub),
        grid=(B, heads // hp, nr),
        in_specs=[blk(0), blk(heads), blk(v_blk0), blk(v_blk0 + heads),
                  pl.BlockSpec((1, W), lambda b, h, r: (0, h)),
                  per_head((C, C)), per_head((C, 1)), per_head((C, 1)), per_head((1, 1))],
        out_specs=pl.BlockSpec((rb, W), lambda b, h, r: (b * nr + r, h)),
        out_shape=jax.ShapeDtypeStruct((N, heads * RET_VAL_DIM), BF16),
        scratch_shapes=[pltpu.VMEM((hp, RET_KEY_DIM, RET_VAL_DIM), F32)],
        compiler_params=_params("parallel", "parallel", "arbitrary"),
        name="retention",
    )(qk, qk, pv, pv, gn_g.reshape(1, heads * RET_VAL_DIM), din, qd, kd, cd)


def _mla_down_kernel(x_ref, w_ref, qg_ref, kvg_ref, c_ref, sa_ref, sb_ref, cq_ref, ckv_ref, kr_ref, *, q_rank, kv_rank):
    acc = jnp.dot(x_ref[...], w_ref[...], preferred_element_type=F32)

    def rms(a, g):
        return a * lax.rsqrt(jnp.mean(a * a, axis=-1, keepdims=True) + RMS_EPS) * g

    cq_ref[...] = rms(acc[:, :q_rank], qg_ref[...]).astype(cq_ref.dtype)
    ckv_ref[...] = rms(acc[:, q_rank:q_rank + kv_rank], kvg_ref[...]).astype(ckv_ref.dtype)
    kr = acc[:, q_rank + kv_rank:]
    half = MLA_ROPE // 2
    kr = kr * c_ref[...] + pltpu.roll(kr, LANE - half, 1) * sa_ref[...] + pltpu.roll(kr, half, 1) * sb_ref[...]
    kr_ref[...] = kr.astype(kr_ref.dtype)


def _mla_down(x, w, q_g, kv_g, tabs, *, T, q_rank, kv_rank, bm=512):
    M, K = x.shape
    Nw = w.shape[1]
    bm = _tile(T, bm)
    tb = T // bm
    row = lambda n: pl.BlockSpec((bm, n), lambda i: (i, 0))
    tab = pl.BlockSpec((bm, LANE), lambda i: (i % tb, 0))
    return pl.pallas_call(
        functools.partial(_mla_down_kernel, q_rank=q_rank, kv_rank=kv_rank),
        grid=(M // bm,),
        in_specs=[row(K), pl.BlockSpec((K, Nw), lambda i: (0, 0)),
                  pl.BlockSpec((1, q_rank), lambda i: (0, 0)), pl.BlockSpec((1, kv_rank), lambda i: (0, 0)),
                  tab, tab, tab],
        out_specs=[row(q_rank), row(kv_rank), row(LANE)],
        out_shape=[jax.ShapeDtypeStruct((M, q_rank), BF16), jax.ShapeDtypeStruct((M, kv_rank), BF16),
                   jax.ShapeDtypeStruct((M, LANE), BF16)],
        compiler_params=_params("parallel"),
        name="mla_down",
    )(x, w, q_g.reshape(1, q_rank), kv_g.reshape(1, kv_rank), *tabs)


MLA_HEADS_PER_STEP = 4


def _mla_attn_kernel(qn_ref, qr_ref, kn_ref, kr_ref, v_ref, o_ref, *, tq):
    i = pl.program_id(2)
    lane = lax.broadcasted_iota(I32, (tq, LANE), 1)
    qs = []
    for j in range(MLA_HEADS_PER_STEP):
        lo = (j % 2) * MLA_ROPE
        own = jnp.where((lane >= lo) & (lane < lo + MLA_ROPE), 1.0, 0.0)
        pair = qr_ref[:, (j // 2) * LANE:(j // 2 + 1) * LANE].astype(F32)
        qr = (pair * own).astype(BF16)
        qs.append(jnp.concatenate([qn_ref[:, j * LANE:(j + 1) * LANE], qr], axis=1))

    def step(c, carry, masked, width):
        off = pl.multiple_of(c * width, width)
        kr = kr_ref[pl.ds(off, width), :]
        out = []
        for j in range(MLA_HEADS_PER_STEP):
            m, l, acc = carry[j]
            k = jnp.concatenate([kn_ref[pl.ds(off, width), j * LANE:(j + 1) * LANE], kr], axis=1)
            s = lax.dot_general(qs[j], k, NT_DIMS, preferred_element_type=F32)
            if masked:
                r_io = lax.broadcasted_iota(I32, (tq, width), 0)
                c_io = lax.broadcasted_iota(I32, (tq, width), 1)
                s = jnp.where(c_io <= r_io, s, MASKED)
            m, l, a, p = _online_softmax(s, m, l)
            v = v_ref[pl.ds(off, width), j * MLA_V:(j + 1) * MLA_V]
            out.append((m, l, a * acc + jnp.dot(p, v, preferred_element_type=F32)))
        return tuple(out)

    one = (jnp.full((tq, 1), MASKED, F32), jnp.zeros((tq, 1), F32), jnp.zeros((tq, MLA_V), F32))
    carry = (one,) * MLA_HEADS_PER_STEP
    if kn_ref.shape[0] >= 2 * tq:
        carry = lax.fori_loop(0, i // 2, functools.partial(step, masked=False, width=2 * tq), carry)
        carry = lax.cond(i % 2 == 1, lambda c: step(i - 1, c, False, tq), lambda c: c, carry)
    carry = step(i, carry, True, tq)
    for j in range(MLA_HEADS_PER_STEP):
        _, l, acc = carry[j]
        o_ref[:, j * MLA_V:(j + 1) * MLA_V] = (acc / l).astype(o_ref.dtype)


def _mla_attn(qn, qr, kv, kr, *, B, T, heads):
    N = qn.shape[0]
    tq = _tile(T, 512)
    nq = T // tq
    hp = MLA_HEADS_PER_STEP
    w = hp * LANE
    return pl.pallas_call(
        functools.partial(_mla_attn_kernel, tq=tq),
        grid=(B, heads // hp, nq),
        in_specs=[pl.BlockSpec((tq, w), lambda b, h, i: (b * nq + i, h)),
                  pl.BlockSpec((tq, hp * MLA_ROPE), lambda b, h, i: (b * nq + i, h)),
                  pl.BlockSpec((T, w), lambda b, h, i: (b, h)),
                  pl.BlockSpec((T, LANE), lambda b, h, i: (b, 0)),
                  pl.BlockSpec((T, w), lambda b, h, i: (b, heads // hp + h))],
        out_specs=pl.BlockSpec((tq, w), lambda b, h, i: (b * nq + i, h)),
        out_shape=jax.ShapeDtypeStruct((N, heads * MLA_V), BF16),
        compiler_params=_params("parallel", "parallel", "parallel"),
        name="mla_attn",
    )(qn, qr, kv, kr, kv)


def _router_kernel(lg_ref, meta_ref, cnt_ref, carry_ref, *, n_exp):
    @pl.when(pl.program_id(0) == 0)
    def _():
        carry_ref[...] = jnp.zeros_like(carry_ref)

    bm = lg_ref.shape[0]
    lane = lax.broadcasted_iota(I32, (bm, LANE), 1).astype(F32)
    logits = jnp.where(lane < n_exp, lg_ref[...], -jnp.inf)
    m1 = jnp.max(logits, axis=1, keepdims=True)
    i1 = jnp.min(jnp.where(logits == m1, lane, float(LANE)), axis=1, keepdims=True)
    rest = jnp.where(lane == i1, -jnp.inf, logits)
    m2 = jnp.max(rest, axis=1, keepdims=True)
    i2 = jnp.min(jnp.where(rest == m2, lane, float(LANE)), axis=1, keepdims=True)
    e = jnp.exp(m2 - m1)
    g1 = 1.0 / (1.0 + e)
    g2 = e / (1.0 + e)
    sel = jnp.where(lane == i1, 1.0, jnp.where(lane == i2, 1.0, 0.0))
    r_io = lax.broadcasted_iota(I32, (bm, bm), 0)
    c_io = lax.broadcasted_iota(I32, (bm, bm), 1)
    below = jnp.where(c_io < r_io, 1.0, 0.0).astype(BF16)
    carry = carry_ref[0:1, :]
    rank = jnp.dot(below, sel.astype(BF16), preferred_element_type=F32) + carry
    r1 = jnp.sum(jnp.where(lane == i1, rank, 0.0), axis=1, keepdims=True)
    r2 = jnp.sum(jnp.where(lane == i2, rank, 0.0), axis=1, keepdims=True)
    meta = jnp.where(lane == 0, i1, 0.0)
    meta = jnp.where(lane == 1, i2, meta)
    meta = jnp.where(lane == 2, g1, meta)
    meta = jnp.where(lane == 3, g2, meta)
    meta = jnp.where(lane == 4, r1, meta)
    meta = jnp.where(lane == 5, r2, meta)
    meta_ref[...] = meta
    total = carry + jnp.sum(sel, axis=0, keepdims=True)
    carry_ref[...] = jnp.broadcast_to(total, carry_ref.shape)
    cnt_ref[...] = jnp.broadcast_to(total, cnt_ref.shape)


def _router(logits, n_exp, *, bm=512):
    M = logits.shape[0]
    bm = _tile(M, bm)
    return pl.pallas_call(
        functools.partial(_router_kernel, n_exp=n_exp),
        grid=(M // bm,),
        in_specs=[pl.BlockSpec((bm, LANE), lambda i: (i, 0))],
        out_specs=[pl.BlockSpec((bm, LANE), lambda i: (i, 0)), pl.BlockSpec((8, LANE), lambda i: (0, 0))],
        out_shape=[jax.ShapeDtypeStruct((M, LANE), F32), jax.ShapeDtypeStruct((8, LANE), F32)],
        scratch_shapes=[pltpu.VMEM((8, LANE), F32)],
        compiler_params=_params("arbitrary"),
        name="router",
    )(logits)


DMA_LOOP_UNROLL = 8


def _moe_gather_kernel(tok_ref, nv_ref, x_hbm, o_ref, stage_ref, sem, *, tm):
    r = pl.program_id(0)
    nv = nv_ref[0]

    half = stage_ref.shape[1]

    def row_copy(j, tok):
        return pltpu.make_async_copy(x_hbm.at[pl.ds(tok, 1)], stage_ref.at[pl.ds(j, 1)], sem)

    def gather_start(tile):
        def body(g, carry):
            for u in range(DMA_LOOP_UNROLL):
                j = g * DMA_LOOP_UNROLL + u
                row_copy(j, tok_ref[tile * tm + j]).start(priority=u % 2)
            return carry
        lax.fori_loop(0, tm // DMA_LOOP_UNROLL, body, 0)

    def gather_wait():
        def body(j, carry):
            row_copy(j, 0).wait()
            return carry
        lax.fori_loop(0, tm, body, 0, unroll=DMA_LOOP_UNROLL)

    @pl.when(r < nv)
    def _():
        @pl.when(r == 0)
        def _():
            gather_start(0)

        gather_wait()
        hi, lo = _unpack_bf16_pairs(stage_ref[...])
        o_ref[:, :half] = hi
        o_ref[:, half:] = lo

        @pl.when(r + 1 < nv)
        def _():
            gather_start(r + 1)

    _zero_unused_tile(r, nv, o_ref)


def _zero_unused_tile(r, n_valid, o_ref):
    @pl.when(r >= n_valid)
    def _():
        o_ref[...] = jnp.zeros(o_ref.shape, o_ref.dtype)


def _moe_up_kernel(te_ref, nv_ref, xs_ref, w1_ref, w3_ref, o_ref, wb1_ref, wb3_ref):
    r = pl.program_id(1)

    @pl.when(r < nv_ref[0])
    def _():
        @pl.when((r == 0) | (te_ref[r] != te_ref[jnp.maximum(r - 1, 0)]))
        def _():
            wb1_ref[...] = w1_ref[0].astype(BF16)
            wb3_ref[...] = w3_ref[0].astype(BF16)

        xb = xs_ref[...]
        a = jnp.dot(xb, wb1_ref[...], preferred_element_type=F32)
        b = jnp.dot(xb, wb3_ref[...], preferred_element_type=F32)
        o_ref[...] = (a * jax.nn.sigmoid(a) * b).astype(o_ref.dtype)

    _zero_unused_tile(r, nv_ref[0], o_ref)


def _moe_down_kernel(te_ref, nv_ref, h_ref, w2_ref, o_ref, wb_ref):
    r = pl.program_id(1)

    @pl.when(r < nv_ref[0])
    def _():
        @pl.when((r == 0) | (te_ref[r] != te_ref[jnp.maximum(r - 1, 0)]))
        def _():
            wb_ref[...] = w2_ref[0].astype(BF16)

        o_ref[...] = jnp.dot(h_ref[...], wb_ref[...], preferred_element_type=F32)

    _zero_unused_tile(r, nv_ref[0], o_ref)


def _experts(x, w1, w3, w2, tile_expert, n_valid, row_tok, *, tm, tf=512, tn=512):
    D = 2 * x.shape[1]
    P = row_tok.shape[0]
    E, _, F = w1.shape
    tf, tn = _tile(F, tf), _tile(D, tn)
    n_tiles = P // tm
    row = lambda r, nv: jnp.minimum(r, nv[0] - 1)

    xs = pl.pallas_call(
        functools.partial(_moe_gather_kernel, tm=tm),
        grid_spec=pltpu.PrefetchScalarGridSpec(
            num_scalar_prefetch=2, grid=(n_tiles,),
            in_specs=[pl.BlockSpec(memory_space=pl.ANY)],
            out_specs=pl.BlockSpec((tm, D), lambda r, tok, nv: (r, 0)),
            scratch_shapes=[pltpu.VMEM((tm, D // 2), I32), pltpu.SemaphoreType.DMA(())]),
        out_shape=jax.ShapeDtypeStruct((P, D), BF16),
        compiler_params=_params("arbitrary"),
        name="moe_gather",
    )(row_tok, n_valid, x)

    hid = pl.pallas_call(
        _moe_up_kernel,
        grid_spec=pltpu.PrefetchScalarGridSpec(
            num_scalar_prefetch=2, grid=(F // tf, n_tiles),
            in_specs=[pl.BlockSpec((tm, D), lambda f, r, te, nv: (row(r, nv), 0)),
                      pl.BlockSpec((1, D, tf), lambda f, r, te, nv: (te[r], 0, f)),
                      pl.BlockSpec((1, D, tf), lambda f, r, te, nv: (te[r], 0, f))],
            out_specs=pl.BlockSpec((tm, tf), lambda f, r, te, nv: (r, f)),
            scratch_shapes=[pltpu.VMEM((D, tf), BF16), pltpu.VMEM((D, tf), BF16)]),
        out_shape=jax.ShapeDtypeStruct((P, F), BF16),
        compiler_params=_params("arbitrary", "arbitrary"),
        name="moe_up",
    )(tile_expert, n_valid, xs, w1, w3)

    return pl.pallas_call(
        _moe_down_kernel,
        grid_spec=pltpu.PrefetchScalarGridSpec(
            num_scalar_prefetch=2, grid=(D // tn, n_tiles),
            in_specs=[pl.BlockSpec((tm, F), lambda n, r, te, nv: (row(r, nv), 0)),
                      pl.BlockSpec((1, F, tn), lambda n, r, te, nv: (te[r], 0, n))],
            out_specs=pl.BlockSpec((tm, tn), lambda n, r, te, nv: (r, n)),
            scratch_shapes=[pltpu.VMEM((F, tn), BF16)]),
        out_shape=jax.ShapeDtypeStruct((P, D), F32),
        compiler_params=_params("arbitrary", "arbitrary"),
        name="moe_down",
    )(tile_expert, n_valid, hid, w2)


def _combine_kernel(dest_ref, x_ref, meta_ref, g_ref, b_ref, ys_hbm, o_ref, buf_ref, sem):
    bm = x_ref.shape[0]
    i = pl.program_id(0)
    n = pl.num_programs(0)

    def copy(slot, j, k, d):
        return pltpu.make_async_copy(ys_hbm.at[pl.ds(d, 1)], buf_ref.at[slot, k, pl.ds(j, 1)], sem.at[slot, k])

    def gather_start(blk):
        slot = blk % 2

        def body(j, carry):
            for k in range(2):
                copy(slot, j, k, dest_ref[2 * (blk * bm + j) + k]).start(priority=k)
            return carry
        lax.fori_loop(0, bm, body, 0, unroll=DMA_LOOP_UNROLL)

    @pl.when(i == 0)
    def _():
        gather_start(0)

    @pl.when(i + 1 < n)
    def _():
        gather_start(i + 1)

    slot = i % 2

    def wait(j, carry):
        for k in range(2):
            copy(slot, j, k, 0).wait()
        return carry

    lax.fori_loop(0, bm, wait, 0, unroll=DMA_LOOP_UNROLL)
    meta = meta_ref[...]
    y = meta[:, 2:3] * buf_ref[slot, 0] + meta[:, 3:4] * buf_ref[slot, 1]
    o_ref[...] = _layer_norm_rows(ALPHA * x_ref[...] + y, g_ref[...], b_ref[...])


def _combine(x, meta, ys, dest, g, b, *, bm=256):
    M, D = x.shape
    bm = _tile(M, bm, 8)
    row = lambda n: pl.BlockSpec((bm, n), lambda i, d: (i, 0))
    vec = pl.BlockSpec((1, D), lambda i, d: (0, 0))
    return pl.pallas_call(
        _combine_kernel,
        grid_spec=pltpu.PrefetchScalarGridSpec(
            num_scalar_prefetch=1, grid=(M // bm,),
            in_specs=[row(D), row(LANE), vec, vec, pl.BlockSpec(memory_space=pl.ANY)],
            out_specs=row(D),
            scratch_shapes=[pltpu.VMEM((2, 2, bm, D), F32), pltpu.SemaphoreType.DMA((2, 2))]),
        out_shape=jax.ShapeDtypeStruct((M, D), F32),
        compiler_params=_params("arbitrary"),
        name="moe_combine",
    )(dest, x, meta, g.reshape(1, D), b.reshape(1, D), ys)


def _even_layer(x, xb, w_in, ret_gn_g, w_out, ln1_g, ln1_b, w1, w3, w2, ln2_g, ln2_b, *, B, T):
    N, D = x.shape
    a_heads = D // 2 // A_HEAD_DIM
    r_heads = D // 2 // RET_VAL_DIM
    qa_w, kv_w = a_heads * A_HEAD_DIM, A_KV_HEADS * A_HEAD_DIM
    qi_w = IDX_HEADS * IDX_DIM
    rk_w, rv_w = r_heads * RET_KEY_DIM, r_heads * RET_VAL_DIM
    sizes = (qa_w, kv_w, kv_w, qi_w, IDX_DIM, IDX_HEADS, rk_w, rk_w, rv_w, rv_w)
    offs = [0]
    for s in sizes:
        offs.append(offs[-1] + s)
    w_in_t = w_in.T
    col = lambda a, b_: w_in_t[offs[a]:offs[b_]]
    w_qa, w_ka, w_va = col(0, 1), col(1, 2), col(2, 3)
    w_qi, w_ki, w_wi = col(3, 4), col(4, 5), col(5, 6)
    w_qb, w_kb, w_vb, w_gb = col(6, 7), col(7, 8), col(8, 9), col(9, 10)

    cos_a, sin_a = _rope_cos_sin(T, A_HEAD_DIM // 4, ROPE_THETA)
    tab_q = _lane_tables(cos_a, sin_a, A_HEAD_DIM, A_HEAD_DIM ** -0.5 * LOG2E)
    tab_k = _lane_tables(cos_a, sin_a, A_HEAD_DIM)
    cos_i, sin_i = _rope_cos_sin(T, IDX_DIM // 4, ROPE_THETA)
    tab_i = _lane_tables(cos_i, sin_i, IDX_DIM)
    pass_c = jnp.ones((T, LANE - IDX_DIM), F32)
    pass_s = jnp.zeros((T, LANE - IDX_DIM), F32)
    tab_idx = tuple(jnp.concatenate([t, t[:, :IDX_DIM], p], 1)
                    for t, p in zip(tab_i, (pass_c, pass_s, pass_s)))
    inv = 1.0 / (RET_THETA ** jnp.linspace(0.0, 1.0, RET_KEY_DIM // 2, dtype=F32))
    ang = jnp.arange(T, dtype=F32)[:, None] * inv[None, :]
    cos_r, sin_r = jnp.cos(ang), jnp.sin(ang)
    sin_pair = jnp.concatenate([-sin_r, sin_r], 1)
    tab_r = (jnp.concatenate([cos_r, cos_r], 1), sin_pair, sin_pair)

    idx_pad = LANE - IDX_DIM - IDX_HEADS
    w_idx = jnp.concatenate([w_qi, w_ki, w_wi * (IDX_DIM ** -0.5 * IDX_HEADS ** -0.5),
                             jnp.zeros((idx_pad, D), F32)], 0).astype(BF16)
    w_rqk = jnp.concatenate([w_qb, w_kb * RET_KEY_DIM ** -0.5], 0).astype(BF16)
    w_pv = jnp.concatenate([w_va, w_vb, w_gb], 0).astype(BF16)
    qa = _proj(xb, w_qa.astype(BF16), out_dtype=BF16, tabs=tab_q, half=A_HEAD_DIM // 8, w_rows_out=True,
               seq_len=T, name="proj_qa", **_pat(qa_w, 1024, 0))
    ka = _proj(xb, w_ka.astype(BF16), out_dtype=BF16, tabs=tab_k, half=A_HEAD_DIM // 8, w_rows_out=True,
               seq_len=T, name="proj_ka", **_pat(kv_w, 1024, 0))
    n_idx = w_idx.shape[0]
    idx = _proj(xb, w_idx, out_dtype=F32, tabs=tab_idx, half=IDX_DIM // 8, seq_len=T, name="proj_idx",
                w_rows_out=True, bm=512, bn=n_idx, slab_pat=(0,) * (qi_w // LANE) + (1,))
    rqk = _proj(xb, w_rqk, out_dtype=BF16, tabs=tab_r, mode="pair", seq_len=T, name="proj_ret_qk",
                w_rows_out=True, bn=1024, slab_pat=(0, 1) * (_tile(2 * rk_w, 1024) // (2 * LANE)))
    pv = _proj(xb, w_pv, out_dtype=BF16, bn=768, seq_len=T, w_rows_out=True, name="proj_v")

    ya = _dsa(qa.reshape(B, T, qa_w), ka.reshape(B, T, kv_w), pv.reshape(B, T, -1), idx.reshape(B, T, n_idx),
              B=B, T=T)
    yb = _retention(rqk, pv, ret_gn_g, B=B, T=T, heads=r_heads, v_blk0=kv_w // RET_VAL_DIM)
    w_out_b = w_out.astype(BF16)
    y = _proj(ya.reshape(N, qa_w), w_out_b[:qa_w], second=(yb, w_out_b[qa_w:]), out_dtype=F32, name="proj_out0")
    x1, x1b = _add_ln(x, y, ln1_g, ln1_b)
    hid = _swiglu_up(x1b, w1.astype(BF16), w3.astype(BF16))
    y = _mm_ksplit(hid, w2.astype(BF16), bk=3584, name="ffn_down")
    return _add_ln(x1, y, ln2_g, ln2_b)


def _pat(width, bn, p):
    bn = _tile(width, bn)
    return dict(bn=bn, slab_pat=(p,) * (bn // LANE))


def _odd_layer(x, xb, w_dq_dkv, q_norm_g, w_uq, kv_norm_g, w_ukv, w_out, ln1_g, ln1_b,
               router, we1, we3, we2, ln2_g, ln2_b, *, B, T):
    N, D = x.shape
    heads = D // MLA_V
    q_rank, kv_rank = q_norm_g.shape[0], kv_norm_g.shape[0]
    scale = (MLA_NOPE + MLA_ROPE) ** -0.5 * LOG2E
    cos_c, sin_c = _rope_cos_sin(T, MLA_ROPE, ROPE_THETA)
    tab_kr = _lane_tables(cos_c, sin_c, MLA_ROPE)
    tab_qr = _lane_tables(cos_c, sin_c, MLA_ROPE, scale)

    w_kr = w_dq_dkv[:, q_rank + kv_rank:]
    w_down = jnp.concatenate([w_dq_dkv[:, :q_rank + kv_rank], w_kr, w_kr], 1).astype(BF16)
    cq, ckv, kr = _mla_down(xb, w_down, q_norm_g, kv_norm_g, tab_kr, T=T, q_rank=q_rank, kv_rank=kv_rank)
    w_uq3 = w_uq.reshape(q_rank, heads, MLA_NOPE + MLA_ROPE)
    w_qn = w_uq3[:, :, :MLA_NOPE].reshape(q_rank, heads * MLA_NOPE).astype(BF16)
    w_qr = w_uq3[:, :, MLA_NOPE:].reshape(q_rank, heads * MLA_ROPE).astype(BF16)
    w_kv3 = w_ukv.reshape(kv_rank, heads, MLA_NOPE + MLA_V)
    w_kv = jnp.concatenate([w_kv3[:, :, :MLA_NOPE].reshape(kv_rank, heads * MLA_NOPE),
                            w_kv3[:, :, MLA_NOPE:].reshape(kv_rank, heads * MLA_V)], 1).astype(BF16)
    qn = _proj(cq, w_qn, out_dtype=BF16, scale=scale, seq_len=T, bn=2048, name="proj_q_nope")
    qr = _proj(cq, w_qr, out_dtype=BF16, tabs=tab_qr, half=MLA_ROPE // 2, seq_len=T, name="proj_q_rope",
               **_pat(heads * MLA_ROPE, 1024, 0))
    kv = _proj(ckv, w_kv, out_dtype=BF16, seq_len=T, bn=2048, name="proj_kv")
    att = _mla_attn(qn, qr, kv, kr, B=B, T=T, heads=heads)
    y = _proj(att, w_out.astype(BF16), out_dtype=F32, name="proj_out1")
    x1, x1_packed, logits = _add_ln(x, y, ln1_g, ln1_b, packed=True, router=router)

    E = router.shape[1]
    tm = _tile(N, 512)
    meta, cnt = _router(logits, E)
    counts = cnt[0, :E].astype(I32)
    padded = (counts + tm - 1) // tm * tm
    ends = jnp.cumsum(padded)
    starts = ends - padded
    i1, i2 = meta[:, 0].astype(I32), meta[:, 1].astype(I32)
    dest = jnp.stack([starts[i1] + meta[:, 4].astype(I32), starts[i2] + meta[:, 5].astype(I32)], 1).reshape(-1)
    n_rows = 2 * N + E * tm
    n_tiles = n_rows // tm
    n_valid = (ends[-1] // tm).astype(I32).reshape(1)
    tile_start = jnp.arange(n_tiles, dtype=I32) * tm
    tile_expert = jnp.minimum(jnp.sum(tile_start[:, None] >= ends[None, :], axis=1), E - 1).astype(I32)
    tile_expert = jnp.where(jnp.arange(n_tiles) < n_valid[0], tile_expert, tile_expert[jnp.maximum(n_valid[0] - 1, 0)])
    row_tok = jnp.zeros((n_rows,), I32).at[dest].set(jnp.repeat(jnp.arange(N, dtype=I32), 2))
    ys = _experts(x1_packed, we1, we3, we2, tile_expert, n_valid, row_tok, tm=tm)
    return _combine(x1, meta, ys, dest, ln2_g, ln2_b)


def kernel(x, l0_w_in, l0_ret_gn_g, l0_w_out, l0_ln1_g, l0_ln1_b, l0_ffn_w1, l0_ffn_w3, l0_ffn_w2, l0_ln2_g, l0_ln2_b, l1_w_dq_dkv, l1_q_norm_g, l1_w_uq, l1_kv_norm_g, l1_w_ukv, l1_w_out, l1_ln1_g, l1_ln1_b, l1_router, l1_moe_w1, l1_moe_w3, l1_moe_w2, l1_ln2_g, l1_ln2_b):
    B, T, D = x.shape
    x2 = x.reshape(B * T, D)
    h, hb = _even_layer(x2, x2.astype(BF16), l0_w_in, l0_ret_gn_g, l0_w_out, l0_ln1_g, l0_ln1_b,
                        l0_ffn_w1, l0_ffn_w3, l0_ffn_w2, l0_ln2_g, l0_ln2_b, B=B, T=T)
    out = _odd_layer(h, hb, l1_w_dq_dkv, l1_q_norm_g, l1_w_uq, l1_kv_norm_g, l1_w_ukv, l1_w_out,
                     l1_ln1_g, l1_ln1_b, l1_router, l1_moe_w1, l1_moe_w3, l1_moe_w2, l1_ln2_g, l1_ln2_b, B=B, T=T)
    return out.reshape(B, T, D)
```
